```python
import math
import jax, jax.numpy as jnp
from jax import lax
import numpy as np

D_MODEL = 1024
BATCH = 8
SEQ = 2048
DEPTH = 1

HEAD_DIM = 64
N_HEADS = 8
N_KV_HEADS = 2
WINDOW = 128
BLOCK = 128
NUM_BUCKETS = 32
MAX_DISTANCE = 128
CONV_CH = 512
CONV_WIDTH = 31
D_FF = int(math.ceil(8 * D_MODEL / 3 / 256) * 256)
Q_COLS = N_HEADS * HEAD_DIM
KV_COLS = N_KV_HEADS * HEAD_DIM
CONV_COLS = 2 * CONV_CH
GATE_COLS = 2 * D_MODEL
IN_COLS = Q_COLS + 2 * KV_COLS + CONV_COLS + GATE_COLS
ALPHA = (2.0 * DEPTH) ** 0.25
BETA = (8.0 * DEPTH) ** -0.25
LN_EPS = 1e-5
NEG_INF = -1e30

kernel_name = "hybrid_gated_conformer_swa_encoder"


def layer_norm(x, g, b):
    xf = x.astype(jnp.float32)
    mu = jnp.mean(xf, axis=-1, keepdims=True)
    var = jnp.mean(jnp.square(xf - mu), axis=-1, keepdims=True)
    y = (xf - mu) * lax.rsqrt(var + LN_EPS) * g.astype(jnp.float32) + b.astype(jnp.float32)
    return y.astype(x.dtype)


def _t5_buckets(rel):
    nb = NUM_BUCKETS // 2
    ret = (rel > 0).astype(np.int32) * nb
    n = np.abs(rel)
    max_exact = nb // 2
    large = max_exact + (np.log(np.maximum(n, 1) / max_exact)
                         / np.log(MAX_DISTANCE / max_exact) * (nb - max_exact)).astype(np.int32)
    large = np.minimum(large, nb - 1)
    return (ret + np.where(n < max_exact, n, large)).astype(np.int32)


def windowed_gqa(q, k, v, rel_bias, sink):
    B, S, _ = q.shape
    nblk = S // BLOCK
    R = N_HEADS // N_KV_HEADS
    qb = q.reshape(B, nblk, BLOCK, N_KV_HEADS, R, HEAD_DIM)
    pad = ((0, 0), (BLOCK, BLOCK), (0, 0))
    kp = jnp.pad(k, pad).reshape(B, nblk + 2, BLOCK, N_KV_HEADS, HEAD_DIM)
    vp = jnp.pad(v, pad).reshape(B, nblk + 2, BLOCK, N_KV_HEADS, HEAD_DIM)
    kb = jnp.concatenate([kp[:, :-2], kp[:, 1:-1], kp[:, 2:]], axis=2)
    vb = jnp.concatenate([vp[:, :-2], vp[:, 1:-1], vp[:, 2:]], axis=2)

    qi = np.arange(BLOCK)[:, None]
    kj = np.arange(3 * BLOCK)[None, :]
    rel = kj - BLOCK - qi
    win = np.abs(rel) <= WINDOW
    kpos = (np.arange(nblk)[:, None] - 1) * BLOCK + np.arange(3 * BLOCK)[None, :]
    valid = (kpos >= 0) & (kpos < S)
    mask = jnp.asarray(win[None, :, :] & valid[:, None, :])
    buckets = jnp.asarray(_t5_buckets(rel))
    bias = jnp.transpose(rel_bias.astype(jnp.float32)[buckets], (2, 0, 1))
    bias = bias.reshape(N_KV_HEADS, R, BLOCK, 3 * BLOCK)

    scale = HEAD_DIM ** -0.5
    s = jnp.einsum('bnqgrd,bnkgd->bngrqk', qb.astype(jnp.float32), kb.astype(jnp.float32)) * scale
    s = s + bias[None, None]
    s = jnp.where(mask[None, :, None, None, :, :], s, NEG_INF)
    sink_col = jnp.broadcast_to(sink.astype(jnp.float32).reshape(1, 1, N_KV_HEADS, R, 1, 1),
                                s.shape[:-1] + (1,))
    p = jax.nn.softmax(jnp.concatenate([s, sink_col], axis=-1), axis=-1)[..., :-1]
    o = jnp.einsum('bngrqk,bnkgd->bnqgrd', p.astype(v.dtype), vb)
    return o.reshape(B, S, N_HEADS * HEAD_DIM)


def conformer_conv(u, conv_w, conv_b, ln_g, ln_b):
    glu = u[..., :CONV_CH] * jax.nn.sigmoid(u[..., CONV_CH:])
    dw = lax.conv_general_dilated(
        glu, conv_w[:, None, :].astype(glu.dtype), window_strides=(1,),
        padding=[((CONV_WIDTH - 1) // 2, (CONV_WIDTH - 1) // 2)],
        dimension_numbers=('NWC', 'WIO', 'NWC'), feature_group_count=CONV_CH)
    dw = dw + conv_b
    return jax.nn.silu(layer_norm(dw, ln_g, ln_b))


def setup_inputs(seed: int = 0) -> dict:
    key = jax.random.key(seed)
    ks = jax.random.split(key, 24)
    f32 = jnp.float32
    nrm = lambda k, shp, s: jax.random.normal(k, shp, f32) * s
    gain = lambda k, shp: 1.0 + nrm(k, shp, 0.02)
    w_in = nrm(ks[1], (DEPTH, D_MODEL, IN_COLS), D_MODEL ** -0.5)
    v0 = Q_COLS + KV_COLS
    w_in = w_in.at[:, :, v0:v0 + KV_COLS].multiply(BETA)
    return {
        "x": jax.random.normal(ks[0], (BATCH, SEQ, D_MODEL), f32),
        "ln_in_g": gain(ks[2], (D_MODEL,)),
        "ln_in_b": nrm(ks[3], (D_MODEL,), 0.02),
        "rel_bias": nrm(ks[4], (NUM_BUCKETS, N_HEADS), 0.1),
        "w_in": w_in,
        "b_gate": nrm(ks[5], (DEPTH, GATE_COLS), 0.02),
        "conv_w": nrm(ks[6], (DEPTH, CONV_WIDTH, CONV_CH), CONV_WIDTH ** -0.5),
        "conv_b": nrm(ks[7], (DEPTH, CONV_CH), 0.02),
        "conv_ln_g": gain(ks[8], (DEPTH, CONV_CH)),
        "conv_ln_b": nrm(ks[9], (DEPTH, CONV_CH), 0.02),
        "w_conv_out": nrm(ks[10], (DEPTH, CONV_CH, D_MODEL), BETA * CONV_CH ** -0.5),
        "w_attn_out": nrm(ks[11], (DEPTH, Q_COLS, D_MODEL), BETA * Q_COLS ** -0.5),
        "sink": nrm(ks[12], (DEPTH, N_HEADS), 0.5),
        "w_out": nrm(ks[13], (DEPTH, D_MODEL, D_MODEL), BETA * D_MODEL ** -0.5),
        "ln1_g": gain(ks[14], (DEPTH, D_MODEL)),
        "ln1_b": nrm(ks[15], (DEPTH, D_MODEL), 0.02),
        "w_gate": nrm(ks[16], (DEPTH, D_MODEL, D_FF), BETA * D_MODEL ** -0.5),
        "w_up": nrm(ks[17], (DEPTH, D_MODEL, D_FF), BETA * D_MODEL ** -0.5),
        "w_down": nrm(ks[18], (DEPTH, D_FF, D_MODEL), BETA * D_FF ** -0.5),
        "ln2_g": gain(ks[19], (DEPTH, D_MODEL)),
        "ln2_b": nrm(ks[20], (DEPTH, D_MODEL), 0.02),
    }


def reference(x, ln_in_g, ln_in_b, rel_bias, w_in, b_gate, conv_w, conv_b, conv_ln_g,
              conv_ln_b, w_conv_out, w_attn_out, sink, w_out, ln1_g, ln1_b,
              w_gate, w_up, w_down, ln2_g, ln2_b):
    x = layer_norm(x, ln_in_g, ln_in_b)
    o1 = Q_COLS
    o2 = o1 + KV_COLS
    o3 = o2 + KV_COLS
    o4 = o3 + CONV_COLS
    for l in range(DEPTH):
        u = jnp.einsum('bsd,dc->bsc', x, w_in[l])
        q, k, v = u[..., :o1], u[..., o1:o2], u[..., o2:o3]
        y_a = windowed_gqa(q, k, v, rel_bias, sink[l]) @ w_attn_out[l]
        y_c = conformer_conv(u[..., o3:o4], conv_w[l], conv_b[l],
                             conv_ln_g[l], conv_ln_b[l]) @ w_conv_out[l]
        gates = jax.nn.sigmoid(u[..., o4:] + b_gate[l])
        merged = gates[..., :D_MODEL] * y_a + gates[..., D_MODEL:] * y_c
        mix = merged @ w_out[l]
        x = layer_norm(ALPHA * x + mix, ln1_g[l], ln1_b[l])
        hid = jax.nn.silu(x @ w_gate[l]) * (x @ w_up[l])
        x = layer_norm(ALPHA * x + hid @ w_down[l], ln2_g[l], ln2_b[l])
    return x
```

```python
import functools
import math

import numpy as np
import jax
import jax.numpy as jnp
from jax import lax
from jax.experimental import pallas as pl
from jax.experimental.pallas import tpu as pltpu

D_MODEL = 1024
HEAD_DIM = 64
N_HEADS = 8
N_KV_HEADS = 2
WINDOW = 128
BLOCK = 128
NUM_BUCKETS = 32
MAX_DISTANCE = 128
CONV_CH = 512
CONV_WIDTH = 31
CONV_HALF = (CONV_WIDTH - 1) // 2
Q_COLS = N_HEADS * HEAD_DIM
KV_COLS = N_KV_HEADS * HEAD_DIM
CONV_COLS = 2 * CONV_CH
GATE_COLS = 2 * D_MODEL
QKV_COLS = Q_COLS + 2 * KV_COLS
DEPTH = 1
ALPHA = (2.0 * DEPTH) ** 0.25
LN_EPS = 1e-5
NEG_INF = -1e30
MASKED_BUCKET = NUM_BUCKETS

BAND = 3 * BLOCK
HALO = 16
CONV_ROWS = 32
SUBLANES = 8
LANES = 128
CONV_SLABS = CONV_CH // LANES

VMEM_LIMIT_BYTES = 56 * 1024 * 1024

TM_IN = 512
TS_MIX = 512
TM_FFN = 512

_ARB = "arbitrary"


def _layer_norm(x, g, b):
    mu = jnp.mean(x, axis=-1, keepdims=True)
    xc = x - mu
    var = jnp.mean(xc * xc, axis=-1, keepdims=True)
    return xc * lax.rsqrt(var + LN_EPS) * g + b


def _const_spec(shape):
    return pl.BlockSpec(shape, lambda *_: (0,) * len(shape), pipeline_mode=pl.Buffered(1))


def _smem_spec():
    return pl.BlockSpec(memory_space=pltpu.SMEM)


def _inproj_kernel(x_ref, g_ref, b_ref, w_ref, xn_ref, qkv_ref, cv_ref, gt_ref):
    xn = _layer_norm(x_ref[...], g_ref[...], b_ref[...])
    xn_ref[...] = xn
    xb = xn.astype(jnp.bfloat16)
    o1 = QKV_COLS
    o2 = o1 + CONV_COLS
    o3 = o2 + D_MODEL
    dot = functools.partial(jnp.dot, preferred_element_type=jnp.float32)
    qkv_ref[...] = dot(xb, w_ref[:, :o1]).astype(qkv_ref.dtype)
    cv_ref[...] = dot(xb, w_ref[:, o1:o2]).astype(cv_ref.dtype)
    gt_ref[:, :D_MODEL] = dot(xb, w_ref[:, o2:o3]).astype(gt_ref.dtype)
    gt_ref[:, D_MODEL:] = dot(xb, w_ref[:, o3:]).astype(gt_ref.dtype)


def _inproj(x2d, ln_g, ln_b, w_in_bf16):
    n_tok = x2d.shape[0]
    assert n_tok % TM_IN == 0
    row = lambda i: (i, 0)
    return pl.pallas_call(
        _inproj_kernel,
        grid=(n_tok // TM_IN,),
        in_specs=[
            pl.BlockSpec((TM_IN, D_MODEL), row),
            _const_spec((1, D_MODEL)),
            _const_spec((1, D_MODEL)),
            _const_spec(w_in_bf16.shape),
        ],
        out_specs=[
            pl.BlockSpec((TM_IN, D_MODEL), row),
            pl.BlockSpec((TM_IN, QKV_COLS), row),
            pl.BlockSpec((TM_IN, CONV_COLS), row),
            pl.BlockSpec((TM_IN, GATE_COLS), row),
        ],
        out_shape=[
            jax.ShapeDtypeStruct((n_tok, D_MODEL), jnp.float32),
            jax.ShapeDtypeStruct((n_tok, QKV_COLS), jnp.bfloat16),
            jax.ShapeDtypeStruct((n_tok, CONV_COLS), jnp.bfloat16),
            jax.ShapeDtypeStruct((n_tok, GATE_COLS), jnp.bfloat16),
        ],
        compiler_params=pltpu.CompilerParams(
            dimension_semantics=(_ARB,), vmem_limit_bytes=VMEM_LIMIT_BYTES),
        name="ln_inproj",
    )(x2d, ln_g, ln_b, w_in_bf16)


def _band_buckets():
    nb = NUM_BUCKETS // 2
    qi = np.arange(BLOCK)[:, None]
    kj = np.arange(BAND)[None, :]
    rel = kj - BLOCK - qi
    ret = (rel > 0).astype(np.int32) * nb
    n = np.abs(rel)
    max_exact = nb // 2
    large = max_exact + (np.log(np.maximum(n, 1) / max_exact)
                         / np.log(MAX_DISTANCE / max_exact) * (nb - max_exact)).astype(np.int32)
    large = np.minimum(large, nb - 1)
    buckets = (ret + np.where(n < max_exact, n, large)).astype(np.int32)
    return np.where(n <= WINDOW, buckets, MASKED_BUCKET).astype(np.int32)


def _mix_kernel(n_seq_blocks,
                relb_ref, sink_ref, bk_ref,
                q_ref, kv_ref, cv_ref, cvp_ref, cvn_ref, gt_ref, xn_ref,
                bgate_ref, convw_ref, convb_ref, clng_ref, clnb_ref,
                wao_ref, wco_ref, wout_ref, ln1g_ref, ln1b_ref,
                h_ref,
                bias_s, wb_s, glu_s, o_s, yc_s):
    b_id = pl.program_id(0)
    i_id = pl.program_id(1)
    n_tiles = pl.num_programs(1)
    ts = q_ref.shape[0]
    blocks_per_tile = ts // BLOCK
    dot = functools.partial(jnp.dot, preferred_element_type=jnp.float32)

    @pl.when((b_id == 0) & (i_id == 0))
    def _init():
        bk = bk_ref[...]
        col = lax.broadcasted_iota(jnp.int32, (BLOCK, BAND), 1)
        for h in range(N_HEADS):
            def body(b, acc):
                return jnp.where(bk == b, relb_ref[b, h], acc)
            mid = lax.fori_loop(0, NUM_BUCKETS, body,
                                jnp.full((BLOCK, BAND), NEG_INF, jnp.float32))
            bias_s[0, h] = jnp.where(col < BLOCK, NEG_INF, mid)
            bias_s[1, h] = mid
            bias_s[2, h] = jnp.where(col >= 2 * BLOCK, NEG_INF, mid)
        for k in range(CONV_WIDTH):
            wb_s[k] = jnp.broadcast_to(convw_ref[k:k + 1, :], (SUBLANES, CONV_CH))

    lane = lax.broadcasted_iota(jnp.int32, (BLOCK, LANES), 1)
    low_half = lane < HEAD_DIM
    for j in range(blocks_per_tile):
        n = i_id * blocks_per_tile + j
        variant = jnp.where(n == 0, 0, jnp.where(n == n_seq_blocks - 1, 2, 1))
        prev_r = pl.multiple_of(jnp.maximum(n - 1, 0) * BLOCK, BLOCK)
        own_r = pl.multiple_of(n * BLOCK, BLOCK)
        next_r = pl.multiple_of(jnp.minimum(n + 1, n_seq_blocks - 1) * BLOCK, BLOCK)
        kvb = jnp.concatenate([kv_ref[pl.ds(prev_r, BLOCK), :],
                               kv_ref[pl.ds(own_r, BLOCK), :],
                               kv_ref[pl.ds(next_r, BLOCK), :]], axis=0)
        kk = kvb[:, :LANES]
        vv = kvb[:, LANES:]
        kk_sw = pltpu.roll(kk, HEAD_DIM, axis=1)
        vv_sw = pltpu.roll(vv, HEAD_DIM, axis=1)

        q = q_ref[j * BLOCK:(j + 1) * BLOCK, :]
        q = q * jnp.asarray(HEAD_DIM ** -0.5, q.dtype)
        zero = jnp.zeros((BLOCK, LANES), q.dtype)
        q_lo, q_hi = [], []
        for p in range(N_HEADS // 2):
            q2 = q[:, p * LANES:(p + 1) * LANES]
            q_lo.append(jnp.where(low_half, q2, zero))
            q_hi.append(jnp.where(low_half, zero, q2))
        heads_a = (0, 2, 5, 7)
        heads_b = (1, 3, 4, 6)
        lhs_a = jnp.concatenate([q_lo[0], q_lo[1], q_hi[2], q_hi[3]], axis=0)
        lhs_b = jnp.concatenate([q_hi[0], q_hi[1], q_lo[2], q_lo[3]], axis=0)
        nt = (((1,), (1,)), ((), ()))
        s_a = lax.dot_general(lhs_a, kk, nt, preferred_element_type=jnp.float32)
        s_b = lax.dot_general(lhs_b, kk_sw, nt, preferred_element_type=jnp.float32)

        def softmax_rows(s_all, heads):
            ps, inv_ls = [], []
            for r, h in enumerate(heads):
                s = s_all[r * BLOCK:(r + 1) * BLOCK, :] + bias_s[variant, h]
                sink = sink_ref[h]
                m = jnp.maximum(jnp.max(s, axis=-1, keepdims=True), sink)
                p = jnp.exp(s - m)
                l = jnp.sum(p, axis=-1, keepdims=True) + jnp.exp(sink - m)
                ps.append(p.astype(jnp.bfloat16))
                inv_ls.append(1.0 / l)
            return jnp.concatenate(ps, axis=0), inv_ls

        p_a, inv_a = softmax_rows(s_a, heads_a)
        p_b, inv_b = softmax_rows(s_b, heads_b)
        r_a = dot(p_a, vv)
        r_b = dot(p_b, vv_sw)
        o_head = {}
        for r, h in enumerate(heads_a):
            o_head[h] = r_a[r * BLOCK:(r + 1) * BLOCK, :] * inv_a[r]
        for r, h in enumerate(heads_b):
            o_head[h] = r_b[r * BLOCK:(r + 1) * BLOCK, :] * inv_b[r]
        o_pairs = [jnp.where(low_half, o_head[2 * p], o_head[2 * p + 1])
                   for p in range(N_HEADS // 2)]
        o_s[j * BLOCK:(j + 1) * BLOCK, :] = jnp.concatenate(o_pairs, axis=1).astype(o_s.dtype)

    def glu(u):
        u = u.astype(jnp.float32)
        return u[:, :CONV_CH] * jax.nn.sigmoid(u[:, CONV_CH:])

    def store_glu(r0, val):
        for c in range(CONV_SLABS):
            glu_s[c, r0:r0 + val.shape[0], :] = val[:, c * LANES:(c + 1) * LANES]

    store_glu(0, jnp.where(i_id > 0, glu(cvp_ref[...]), 0.0))
    store_glu(HALO, glu(cv_ref[...]))
    store_glu(HALO + ts, jnp.where(i_id < n_tiles - 1, glu(cvn_ref[...]), 0.0))

    conv_b = convb_ref[...]
    cln_g = clng_ref[...]
    cln_b = clnb_ref[...]
    groups = CONV_ROWS // SUBLANES
    first_tap = HALO - CONV_HALF

    def conv_chunk(chunk, carry):
        r0 = pl.multiple_of(chunk * CONV_ROWS, CONV_ROWS)
        accs = [[jnp.zeros((SUBLANES, LANES), jnp.float32) for _ in range(CONV_SLABS)]
                for _ in range(groups)]
        for k in range(CONV_WIDTH):
            for c in range(CONV_SLABS):
                w = wb_s[k, :, c * LANES:(c + 1) * LANES]
                for g in range(groups):
                    tap = glu_s[c, pl.ds(r0 + g * SUBLANES + first_tap + k, SUBLANES), :]
                    accs[g][c] = accs[g][c] + tap * w
        dw = jnp.concatenate([jnp.concatenate(a, axis=1) for a in accs], axis=0) + conv_b
        y = _layer_norm(dw, cln_g, cln_b)
        yc_s[pl.ds(r0, CONV_ROWS), :] = (y * jax.nn.sigmoid(y)).astype(yc_s.dtype)
        return carry

    lax.fori_loop(0, ts // CONV_ROWS, conv_chunk, 0)

    y_a = dot(o_s[...], wao_ref[...])
    y_c = dot(yc_s[...], wco_ref[...])
    gates = jax.nn.sigmoid(gt_ref[...].astype(jnp.float32) + bgate_ref[...])
    merged = gates[:, :D_MODEL] * y_a + gates[:, D_MODEL:] * y_c
    mix = dot(merged.astype(jnp.bfloat16), wout_ref[...])
    h_ref[...] = _layer_norm(ALPHA * xn_ref[...] + mix, ln1g_ref[...], ln1b_ref[...])


def _mix(batch, seq, rel_bias, sink, qkv, cv, gt, xn, b_gate, conv_w, conv_b, cln_g, cln_b,
         w_ao, w_co, w_out, ln1_g, ln1_b):
    ts = TS_MIX
    assert seq % ts == 0 and ts % BLOCK == 0 and ts % HALO == 0 and seq // BLOCK >= 2
    n_seq_blocks = seq // BLOCK
    halo_blocks = ts // HALO
    last_halo = seq // HALO - 1
    buckets = jnp.asarray(_band_buckets())
    tile = lambda b, i: (b, i, 0)
    return pl.pallas_call(
        functools.partial(_mix_kernel, n_seq_blocks),
        grid=(batch, seq // ts),
        in_specs=[
            _smem_spec(),
            _smem_spec(),
            _const_spec((BLOCK, BAND)),
            pl.BlockSpec((None, ts, Q_COLS), tile),
            pl.BlockSpec((None, seq, 2 * KV_COLS), lambda b, i: (b, 0, Q_COLS // (2 * KV_COLS))),
            pl.BlockSpec((None, ts, CONV_COLS), tile),
            pl.BlockSpec((None, HALO, CONV_COLS),
                         lambda b, i: (b, jnp.maximum(i * halo_blocks - 1, 0), 0)),
            pl.BlockSpec((None, HALO, CONV_COLS),
                         lambda b, i: (b, jnp.minimum((i + 1) * halo_blocks, last_halo), 0)),
            pl.BlockSpec((None, ts, GATE_COLS), tile),
            pl.BlockSpec((None, ts, D_MODEL), tile),
            _const_spec((1, GATE_COLS)),
            _const_spec((CONV_WIDTH, CONV_CH)),
            _const_spec((1, CONV_CH)),
            _const_spec((1, CONV_CH)),
            _const_spec((1, CONV_CH)),
            _const_spec((Q_COLS, D_MODEL)),
            _const_spec((CONV_CH, D_MODEL)),
            _const_spec((D_MODEL, D_MODEL)),
            _const_spec((1, D_MODEL)),
            _const_spec((1, D_MODEL)),
        ],
        out_specs=pl.BlockSpec((None, ts, D_MODEL), tile),
        out_shape=jax.ShapeDtypeStruct((batch, seq, D_MODEL), jnp.float32),
        scratch_shapes=[
            pltpu.VMEM((3, N_HEADS, BLOCK, BAND), jnp.float32),
            pltpu.VMEM((CONV_WIDTH, SUBLANES, CONV_CH), jnp.float32),
            pltpu.VMEM((CONV_SLABS, ts + 2 * HALO, LANES), jnp.float32),
            pltpu.VMEM((ts, Q_COLS), jnp.bfloat16),
            pltpu.VMEM((ts, CONV_CH), jnp.bfloat16),
        ],
        compiler_params=pltpu.CompilerParams(
            dimension_semantics=(_ARB, _ARB), vmem_limit_bytes=VMEM_LIMIT_BYTES),
        name="token_mix",
    )(rel_bias, sink, buckets, qkv, qkv, cv, cv, cv, gt, xn, b_gate, conv_w, conv_b, cln_g, cln_b,
      w_ao, w_co, w_out, ln1_g, ln1_b)


def _ffn_kernel(h_ref, wg_ref, wu_ref, wd_ref, g_ref, b_ref, o_ref):
    dot = functools.partial(jnp.dot, preferred_element_type=jnp.float32)
    h = h_ref[...]
    hb = h.astype(jnp.bfloat16)
    gate = dot(hb, wg_ref[...])
    up = dot(hb, wu_ref[...])
    hid = (gate * jax.nn.sigmoid(gate) * up).astype(jnp.bfloat16)
    o_ref[...] = _layer_norm(ALPHA * h + dot(hid, wd_ref[...]), g_ref[...], b_ref[...])


def _ffn(h2d, w_gate, w_up, w_down, ln_g, ln_b):
    n_tok = h2d.shape[0]
    assert n_tok % TM_FFN == 0
    row = lambda i: (i, 0)
    return pl.pallas_call(
        _ffn_kernel,
        grid=(n_tok // TM_FFN,),
        in_specs=[
            pl.BlockSpec((TM_FFN, D_MODEL), row),
            _const_spec(w_gate.shape),
            _const_spec(w_up.shape),
            _const_spec(w_down.shape),
            _const_spec((1, D_MODEL)),
            _const_spec((1, D_MODEL)),
        ],
        out_specs=pl.BlockSpec((TM_FFN, D_MODEL), row),
        out_shape=jax.ShapeDtypeStruct((n_tok, D_MODEL), jnp.float32),
        compiler_params=pltpu.CompilerParams(
            dimension_semantics=(_ARB,), vmem_limit_bytes=VMEM_LIMIT_BYTES),
        name="swiglu_ffn",
    )(h2d, w_gate, w_up, w_down, ln_g, ln_b)


def kernel(x, ln_in_g, ln_in_b, rel_bias, w_in, b_gate, conv_w, conv_b, conv_ln_g, conv_ln_b,
           w_conv_out, w_attn_out, sink, w_out, ln1_g, ln1_b, w_gate, w_up, w_down, ln2_g, ln2_b):
    batch, seq, d_model = x.shape
    assert d_model == D_MODEL and w_in.shape[0] == DEPTH
    bf16 = jnp.bfloat16
    row = lambda v: v.reshape(1, -1)
    n_tok = batch * seq

    xn, qkv, cv, gt = _inproj(x.reshape(n_tok, D_MODEL), row(ln_in_g), row(ln_in_b),
                              w_in[0].astype(bf16))
    h = _mix(batch, seq, rel_bias, sink[0],
             qkv.reshape(batch, seq, QKV_COLS), cv.reshape(batch, seq, CONV_COLS),
             gt.reshape(batch, seq, GATE_COLS), xn.reshape(batch, seq, D_MODEL),
             row(b_gate[0]), conv_w[0], row(conv_b[0]), row(conv_ln_g[0]), row(conv_ln_b[0]),
             w_attn_out[0].astype(bf16), w_conv_out[0].astype(bf16), w_out[0].astype(bf16),
             row(ln1_g[0]), row(ln1_b[0]))
    out = _ffn(h.reshape(n_tok, D_MODEL), w_gate[0].astype(bf16), w_up[0].astype(bf16),
               w_down[0].astype(bf16), row(ln2_g[0]), row(ln2_b[0]))
    return out.reshape(batch, seq, D_MODEL)
```

```python
import functools
import math

import numpy as np
import jax
import jax.numpy as jnp
from jax import lax
from jax.experimental import pallas as pl
from jax.experimental.pallas import tpu as pltpu

D_MODEL = 1024
HEAD_DIM = 64
N_HEADS = 8
N_KV_HEADS = 2
WINDOW = 128
BLOCK = 128
NUM_BUCKETS = 32
MAX_DISTANCE = 128
CONV_CH = 512
CONV_WIDTH = 31
CONV_HALF = (CONV_WIDTH - 1) // 2
Q_COLS = N_HEADS * HEAD_DIM
KV_COLS = N_KV_HEADS * HEAD_DIM
CONV_COLS = 2 * CONV_CH
GATE_COLS = 2 * D_MODEL
QKV_COLS = Q_COLS + 2 * KV_COLS
DEPTH = 1
ALPHA = (2.0 * DEPTH) ** 0.25
LN_EPS = 1e-5
NEG_INF = -1e30
MASKED_BUCKET = NUM_BUCKETS

BAND = 3 * BLOCK
HALO = 16
CONV_ROWS = 32
SUBLANES = 8
LANES = 128
CONV_SLABS = CONV_CH // LANES

VMEM_LIMIT_BYTES = 56 * 1024 * 1024

TM_IN = 512
TS_MIX = 512
TM_FFN = 512

_ARB = "arbitrary"


def _layer_norm(x, g, b):
    mu = jnp.mean(x, axis=-1, keepdims=True)
    xc = x - mu
    var = jnp.mean(xc * xc, axis=-1, keepdims=True)
    return xc * lax.rsqrt(var + LN_EPS) * g + b


def _const_spec(shape):
    return pl.BlockSpec(shape, lambda *_: (0,) * len(shape), pipeline_mode=pl.Buffered(1))


def _smem_spec():
    return pl.BlockSpec(memory_space=pltpu.SMEM)


def _inproj_kernel(x_ref, g_ref, b_ref, w_ref, xn_ref, qkv_ref, cv_ref, gt_ref):
    xn = _layer_norm(x_ref[...], g_ref[...], b_ref[...])
    xn_ref[...] = xn
    xb = xn.astype(jnp.bfloat16)
    o1 = QKV_COLS
    o2 = o1 + CONV_COLS
    o3 = o2 + D_MODEL
    dot = functools.partial(jnp.dot, preferred_element_type=jnp.float32)
    qkv_ref[...] = dot(xb, w_ref[:, :o1]).astype(qkv_ref.dtype)
    cv_ref[...] = dot(xb, w_ref[:, o1:o2]).astype(cv_ref.dtype)
    gt_ref[:, :D_MODEL] = dot(xb, w_ref[:, o2:o3]).astype(gt_ref.dtype)
    gt_ref[:, D_MODEL:] = dot(xb, w_ref[:, o3:]).astype(gt_ref.dtype)


def _inproj(x2d, ln_g, ln_b, w_in_bf16):
    n_tok = x2d.shape[0]
    assert n_tok % TM_IN == 0
    row = lambda i: (i, 0)
    return pl.pallas_call(
        _inproj_kernel,
        grid=(n_tok // TM_IN,),
        in_specs=[
            pl.BlockSpec((TM_IN, D_MODEL), row),
            _const_spec((1, D_MODEL)),
            _const_spec((1, D_MODEL)),
            _const_spec(w_in_bf16.shape),
        ],
        out_specs=[
            pl.BlockSpec((TM_IN, D_MODEL), row),
            pl.BlockSpec((TM_IN, QKV_COLS), row),
            pl.BlockSpec((TM_IN, CONV_COLS), row),
            pl.BlockSpec((TM_IN, GATE_COLS), row),
        ],
        out_shape=[
            jax.ShapeDtypeStruct((n_tok, D_MODEL), jnp.float32),
            jax.ShapeDtypeStruct((n_tok, QKV_COLS), jnp.bfloat16),
            jax.ShapeDtypeStruct((n_tok, CONV_COLS), jnp.bfloat16),
            jax.ShapeDtypeStruct((n_tok, GATE_COLS), jnp.bfloat16),
        ],
        compiler_params=pltpu.CompilerParams(
            dimension_semantics=(_ARB,), vmem_limit_bytes=VMEM_LIMIT_BYTES),
        name="ln_inproj",
    )(x2d, ln_g, ln_b, w_in_bf16)


def _band_buckets():
    nb = NUM_BUCKETS // 2
    qi = np.arange(BLOCK)[:, None]
    kj = np.arange(BAND)[None, :]
    rel = kj - BLOCK - qi
    ret = (rel > 0).astype(np.int32) * nb
    n = np.abs(rel)
    max_exact = nb // 2
    large = max_exact + (np.log(np.maximum(n, 1) / max_exact)
                         / np.log(MAX_DISTANCE / max_exact) * (nb - max_exact)).astype(np.int32)
    large = np.minimum(large, nb - 1)
    buckets = (ret + np.where(n < max_exact, n, large)).astype(np.int32)
    return np.where(n <= WINDOW, buckets, MASKED_BUCKET).astype(np.int32)


def _mix_kernel(n_seq_blocks,
                relb_ref, sink_ref, bk_ref,
                q_ref, kv_ref, cv_ref, cvp_ref, cvn_ref, gt_ref, xn_ref,
                bgate_ref, convw_ref, convb_ref, clng_ref, clnb_ref,
                wao_ref, wco_ref, wout_ref, ln1g_ref, ln1b_ref,
                h_ref,
                bias_s, wb_s, glu_s, o_s, dw_s):
    b_id = pl.program_id(0)
    i_id = pl.program_id(1)
    n_tiles = pl.num_programs(1)
    ts = q_ref.shape[0]
    blocks_per_tile = ts // BLOCK
    dot = functools.partial(jnp.dot, preferred_element_type=jnp.float32)

    @pl.when((b_id == 0) & (i_id == 0))
    def _init():
        bk = bk_ref[...]
        col = lax.broadcasted_iota(jnp.int32, (BLOCK, BAND), 1)
        for h in range(N_HEADS):
            def body(b, acc):
                return jnp.where(bk == b, relb_ref[b, h], acc)
            mid = lax.fori_loop(0, NUM_BUCKETS, body,
                                jnp.full((BLOCK, BAND), NEG_INF, jnp.float32))
            bias_s[0, h] = jnp.where(col < BLOCK, NEG_INF, mid)
            bias_s[1, h] = mid
            bias_s[2, h] = jnp.where(col >= 2 * BLOCK, NEG_INF, mid)
        for k in range(CONV_WIDTH):
            wb_s[k] = jnp.broadcast_to(convw_ref[k:k + 1, :], (SUBLANES, CONV_CH))

    lane = lax.broadcasted_iota(jnp.int32, (BLOCK, LANES), 1)
    low_half = lane < HEAD_DIM
    for j in range(blocks_per_tile):
        n = i_id * blocks_per_tile + j
        variant = jnp.where(n == 0, 0, jnp.where(n == n_seq_blocks - 1, 2, 1))
        prev_r = pl.multiple_of(jnp.maximum(n - 1, 0) * BLOCK, BLOCK)
        own_r = pl.multiple_of(n * BLOCK, BLOCK)
        next_r = pl.multiple_of(jnp.minimum(n + 1, n_seq_blocks - 1) * BLOCK, BLOCK)
        kvb = jnp.concatenate([kv_ref[pl.ds(prev_r, BLOCK), :],
                               kv_ref[pl.ds(own_r, BLOCK), :],
                               kv_ref[pl.ds(next_r, BLOCK), :]], axis=0)
        kk = kvb[:, :LANES]
        vv = kvb[:, LANES:]
        kk_sw = pltpu.roll(kk, HEAD_DIM, axis=1)
        vv_sw = pltpu.roll(vv, HEAD_DIM, axis=1)

        q = q_ref[j * BLOCK:(j + 1) * BLOCK, :]
        q = q * jnp.asarray(HEAD_DIM ** -0.5, q.dtype)
        zero = jnp.zeros((BLOCK, LANES), q.dtype)
        q_lo, q_hi = [], []
        for p in range(N_HEADS // 2):
            q2 = q[:, p * LANES:(p + 1) * LANES]
            q_lo.append(jnp.where(low_half, q2, zero))
            q_hi.append(jnp.where(low_half, zero, q2))
        heads_a = (0, 2, 5, 7)
        heads_b = (1, 3, 4, 6)
        lhs_a = jnp.concatenate([q_lo[0], q_lo[1], q_hi[2], q_hi[3]], axis=0)
        lhs_b = jnp.concatenate([q_hi[0], q_hi[1], q_lo[2], q_lo[3]], axis=0)
        nt = (((1,), (1,)), ((), ()))
        s_a = lax.dot_general(lhs_a, kk, nt, preferred_element_type=jnp.float32)
        s_b = lax.dot_general(lhs_b, kk_sw, nt, preferred_element_type=jnp.float32)

        def softmax_rows(s_all, heads):
            ps, inv_ls = [], []
            for r, h in enumerate(heads):
                s = s_all[r * BLOCK:(r + 1) * BLOCK, :] + bias_s[variant, h]
                sink = sink_ref[h]
                m = jnp.maximum(jnp.max(s, axis=-1, keepdims=True), sink)
                p = jnp.exp(s - m)
                l = jnp.sum(p, axis=-1, keepdims=True) + jnp.exp(sink - m)
                ps.append(p.astype(jnp.bfloat16))
                inv_ls.append(1.0 / l)
            return jnp.concatenate(ps, axis=0), inv_ls

        p_a, inv_a = softmax_rows(s_a, heads_a)
        p_b, inv_b = softmax_rows(s_b, heads_b)
        r_a = dot(p_a, vv)
        r_b = dot(p_b, vv_sw)
        o_head = {}
        for r, h in enumerate(heads_a):
            o_head[h] = r_a[r * BLOCK:(r + 1) * BLOCK, :] * inv_a[r]
        for r, h in enumerate(heads_b):
            o_head[h] = r_b[r * BLOCK:(r + 1) * BLOCK, :] * inv_b[r]
        o_pairs = [jnp.where(low_half, o_head[2 * p], o_head[2 * p + 1])
                   for p in range(N_HEADS // 2)]
        o_s[j * BLOCK:(j + 1) * BLOCK, :] = jnp.concatenate(o_pairs, axis=1).astype(o_s.dtype)

    def glu(u):
        u = u.astype(jnp.float32)
        return u[:, :CONV_CH] * jax.nn.sigmoid(u[:, CONV_CH:])

    def store_glu(r0, val):
        for c in range(CONV_SLABS):
            glu_s[c, r0:r0 + val.shape[0], :] = val[:, c * LANES:(c + 1) * LANES]

    store_glu(0, jnp.where(i_id > 0, glu(cvp_ref[...]), 0.0))
    store_glu(HALO, glu(cv_ref[...]))
    store_glu(HALO + ts, jnp.where(i_id < n_tiles - 1, glu(cvn_ref[...]), 0.0))

    conv_b = convb_ref[...]
    cln_g = clng_ref[...]
    cln_b = clnb_ref[...]
    groups = CONV_ROWS // SUBLANES
    first_tap = HALO - CONV_HALF

    def conv_chunk(chunk, carry):
        r0 = pl.multiple_of(chunk * CONV_ROWS, CONV_ROWS)
        accs = [[jnp.zeros((SUBLANES, LANES), jnp.float32) for _ in range(CONV_SLABS)]
                for _ in range(groups)]
        for k in range(CONV_WIDTH):
            for c in range(CONV_SLABS):
                w = wb_s[k, :, c * LANES:(c + 1) * LANES]
                for g in range(groups):
                    tap = glu_s[c, pl.ds(r0 + g * SUBLANES + first_tap + k, SUBLANES), :]
                    accs[g][c] = accs[g][c] + tap * w
        dw_s[pl.ds(r0, CONV_ROWS), :] = jnp.concatenate(
            [jnp.concatenate(a, axis=1) for a in accs], axis=0)
        return carry

    lax.fori_loop(0, ts // CONV_ROWS, conv_chunk, 0)
    y = _layer_norm(dw_s[...] + conv_b, cln_g, cln_b)
    yc = (y * jax.nn.sigmoid(y)).astype(jnp.bfloat16)

    y_a = dot(o_s[...], wao_ref[...])
    y_c = dot(yc, wco_ref[...])
    gates = jax.nn.sigmoid(gt_ref[...].astype(jnp.float32) + bgate_ref[...])
    merged = gates[:, :D_MODEL] * y_a + gates[:, D_MODEL:] * y_c
    mix = dot(merged.astype(jnp.bfloat16), wout_ref[...])
    h_ref[...] = _layer_norm(ALPHA * xn_ref[...] + mix, ln1g_ref[...], ln1b_ref[...])


def _mix(batch, seq, rel_bias, sink, qkv, cv, gt, xn, b_gate, conv_w, conv_b, cln_g, cln_b,
         w_ao, w_co, w_out, ln1_g, ln1_b):
    ts = TS_MIX
    assert seq % ts == 0 and ts % BLOCK == 0 and ts % HALO == 0 and seq // BLOCK >= 2
    n_seq_blocks = seq // BLOCK
    halo_blocks = ts // HALO
    last_halo = seq // HALO - 1
    buckets = jnp.asarray(_band_buckets())
    tile = lambda b, i: (b, i, 0)
    return pl.pallas_call(
        functools.partial(_mix_kernel, n_seq_blocks),
        grid=(batch, seq // ts),
        in_specs=[
            _smem_spec(),
            _smem_spec(),
            _const_spec((BLOCK, BAND)),
            pl.BlockSpec((None, ts, Q_COLS), tile),
            pl.BlockSpec((None, seq, 2 * KV_COLS), lambda b, i: (b, 0, Q_COLS // (2 * KV_COLS))),
            pl.BlockSpec((None, ts, CONV_COLS), tile),
            pl.BlockSpec((None, HALO, CONV_COLS),
                         lambda b, i: (b, jnp.maximum(i * halo_blocks - 1, 0), 0)),
            pl.BlockSpec((None, HALO, CONV_COLS),
                         lambda b, i: (b, jnp.minimum((i + 1) * halo_blocks, last_halo), 0)),
            pl.BlockSpec((None, ts, GATE_COLS), tile),
            pl.BlockSpec((None, ts, D_MODEL), tile),
            _const_spec((1, GATE_COLS)),
            _const_spec((CONV_WIDTH, CONV_CH)),
            _const_spec((1, CONV_CH)),
            _const_spec((1, CONV_CH)),
            _const_spec((1, CONV_CH)),
            _const_spec((Q_COLS, D_MODEL)),
            _const_spec((CONV_CH, D_MODEL)),
            _const_spec((D_MODEL, D_MODEL)),
            _const_spec((1, D_MODEL)),
            _const_spec((1, D_MODEL)),
        ],
        out_specs=pl.BlockSpec((None, ts, D_MODEL), tile),
        out_shape=jax.ShapeDtypeStruct((batch, seq, D_MODEL), jnp.float32),
        scratch_shapes=[
            pltpu.VMEM((3, N_HEADS, BLOCK, BAND), jnp.float32),
            pltpu.VMEM((CONV_WIDTH, SUBLANES, CONV_CH), jnp.float32),
            pltpu.VMEM((CONV_SLABS, ts + 2 * HALO, LANES), jnp.float32),
            pltpu.VMEM((ts, Q_COLS), jnp.bfloat16),
            pltpu.VMEM((ts, CONV_CH), jnp.float32),
        ],
        compiler_params=pltpu.CompilerParams(
            dimension_semantics=(_ARB, _ARB), vmem_limit_bytes=VMEM_LIMIT_BYTES),
        name="token_mix",
    )(rel_bias, sink, buckets, qkv, qkv, cv, cv, cv, gt, xn, b_gate, conv_w, conv_b, cln_g, cln_b,
      w_ao, w_co, w_out, ln1_g, ln1_b)


def _ffn_kernel(h_ref, wg_ref, wu_ref, wd_ref, g_ref, b_ref, o_ref):
    dot = functools.partial(jnp.dot, preferred_element_type=jnp.float32)
    h = h_ref[...]
    hb = h.astype(jnp.bfloat16)
    gate = dot(hb, wg_ref[...])
    up = dot(hb, wu_ref[...])
    hid = (gate * jax.nn.sigmoid(gate) * up).astype(jnp.bfloat16)
    o_ref[...] = _layer_norm(ALPHA * h + dot(hid, wd_ref[...]), g_ref[...], b_ref[...])


def _ffn(h2d, w_gate, w_up, w_down, ln_g, ln_b):
    n_tok = h2d.shape[0]
    assert n_tok % TM_FFN == 0
    row = lambda i: (i, 0)
    return pl.pallas_call(
        _ffn_kernel,
        grid=(n_tok // TM_FFN,),
        in_specs=[
            pl.BlockSpec((TM_FFN, D_MODEL), row),
            _const_spec(w_gate.shape),
            _const_spec(w_up.shape),
            _const_spec(w_down.shape),
            _const_spec((1, D_MODEL)),
            _const_spec((1, D_MODEL)),
        ],
        out_specs=pl.BlockSpec((TM_FFN, D_MODEL), row),
        out_shape=jax.ShapeDtypeStruct((n_tok, D_MODEL), jnp.float32),
        compiler_params=pltpu.CompilerParams(
            dimension_semantics=(_ARB,), vmem_limit_bytes=VMEM_LIMIT_BYTES),
        name="swiglu_ffn",
    )(h2d, w_gate, w_up, w_down, ln_g, ln_b)


def kernel(x, ln_in_g, ln_in_b, rel_bias, w_in, b_gate, conv_w, conv_b, conv_ln_g, conv_ln_b,
           w_conv_out, w_attn_out, sink, w_out, ln1_g, ln1_b, w_gate, w_up, w_down, ln2_g, ln2_b):
    batch, seq, d_model = x.shape
    assert d_model == D_MODEL and w_in.shape[0] == DEPTH
    bf16 = jnp.bfloat16
    row = lambda v: v.reshape(1, -1)
    n_tok = batch * seq

    xn, qkv, cv, gt = _inproj(x.reshape(n_tok, D_MODEL), row(ln_in_g), row(ln_in_b),
                              w_in[0].astype(bf16))
    h = _mix(batch, seq, rel_bias, sink[0],
             qkv.reshape(batch, seq, QKV_COLS), cv.reshape(batch, seq, CONV_COLS),
             gt.reshape(batch, seq, GATE_COLS), xn.reshape(batch, seq, D_MODEL),
             row(b_gate[0]), conv_w[0], row(conv_b[0]), row(conv_ln_g[0]), row(conv_ln_b[0]),
             w_attn_out[0].astype(bf16), w_conv_out[0].astype(bf16), w_out[0].astype(bf16),
             row(ln1_g[0]), row(ln1_b[0]))
    out = _ffn(h.reshape(n_tok, D_MODEL), w_gate[0].astype(bf16), w_up[0].astype(bf16),
               w_down[0].astype(bf16), row(ln2_g[0]), row(ln2_b[0]))
    return out.reshape(batch, seq, D_MODEL)
```

```python
import functools

import numpy as np
import jax
import jax.numpy as jnp
from jax import lax
from jax.experimental import pallas as pl
from jax.experimental.pallas import tpu as pltpu

D_MODEL = 1024
HEAD_DIM = 64
N_HEADS = 8
N_KV_HEADS = 2
WINDOW = 128
BLOCK = 128
NUM_BUCKETS = 32
MAX_DISTANCE = 128
CONV_CH = 512
CONV_WIDTH = 31
CONV_HALF = (CONV_WIDTH - 1) // 2
Q_COLS = N_HEADS * HEAD_DIM
KV_COLS = N_KV_HEADS * HEAD_DIM
CONV_COLS = 2 * CONV_CH
GATE_COLS = 2 * D_MODEL
QKV_COLS = Q_COLS + 2 * KV_COLS
DEPTH = 1
ALPHA = (2.0 * DEPTH) ** 0.25
LN_EPS = 1e-5
NEG_INF = -1e30
MASKED_BUCKET = NUM_BUCKETS

BAND = 3 * BLOCK
HALO = 16
CONV_ROWS = 32
SUBLANES = 8
LANES = 128
CONV_SLABS = CONV_CH // LANES

VMEM_LIMIT_BYTES = 56 * 1024 * 1024

TS_IN = 512
TS_MIX = 512
TM_FFN = 512

_ARB = "arbitrary"


def _layer_norm(x, g, b):
    mu = jnp.mean(x, axis=-1, keepdims=True)
    xc = x - mu
    var = jnp.mean(xc * xc, axis=-1, keepdims=True)
    return xc * lax.rsqrt(var + LN_EPS) * g + b


def _const_spec(shape):
    return pl.BlockSpec(shape, lambda *_: (0,) * len(shape), pipeline_mode=pl.Buffered(1))


def _smem_spec():
    return pl.BlockSpec(memory_space=pltpu.SMEM)


def _halo_specs(ts, seq, cols):
    per_tile = ts // HALO
    last = seq // HALO - 1
    prev = pl.BlockSpec((None, HALO, cols), lambda b, i: (b, jnp.maximum(i * per_tile - 1, 0), 0))
    nxt = pl.BlockSpec((None, HALO, cols), lambda b, i: (b, jnp.minimum((i + 1) * per_tile, last), 0))
    return prev, nxt


def _inproj_kernel(x_ref, xp_ref, xnx_ref, g_ref, b_ref, w_ref, bgate_ref,
                   convw_ref, convb_ref, clng_ref, clnb_ref,
                   xn_ref, qkv_ref, gt_ref, yc_ref,
                   xb_s, wb_s, glu_s, dw_s):
    b_id = pl.program_id(0)
    i_id = pl.program_id(1)
    n_tiles = pl.num_programs(1)
    ts = x_ref.shape[0]
    dot = functools.partial(jnp.dot, preferred_element_type=jnp.float32)
    o1 = QKV_COLS
    o2 = o1 + CONV_COLS
    o3 = o2 + D_MODEL

    @pl.when((b_id == 0) & (i_id == 0))
    def _init():
        for k in range(CONV_WIDTH):
            wb_s[k] = jnp.broadcast_to(convw_ref[k:k + 1, :], (SUBLANES, CONV_CH))

    g = g_ref[...]
    b = b_ref[...]
    xn = _layer_norm(x_ref[...], g, b)
    xn_ref[...] = xn
    xb_s[0:HALO, :] = _layer_norm(xp_ref[...], g, b).astype(xb_s.dtype)
    xb_s[HALO:HALO + ts, :] = xn.astype(xb_s.dtype)
    xb_s[HALO + ts:2 * HALO + ts, :] = _layer_norm(xnx_ref[...], g, b).astype(xb_s.dtype)

    u = dot(xb_s[...], w_ref[:, o1:o2])
    glu = u[:, :CONV_CH] * jax.nn.sigmoid(u[:, CONV_CH:])
    row = lax.broadcasted_iota(jnp.int32, (ts + 2 * HALO, 1), 0)
    pad = ((row < HALO) & (i_id == 0)) | ((row >= HALO + ts) & (i_id == n_tiles - 1))
    glu = jnp.where(pad, 0.0, glu)
    for c in range(CONV_SLABS):
        glu_s[c] = glu[:, c * LANES:(c + 1) * LANES]

    xb = xb_s[HALO:HALO + ts, :]
    qkv_ref[...] = dot(xb, w_ref[:, :o1]).astype(qkv_ref.dtype)
    bgate = bgate_ref[...]
    gt_ref[:, :D_MODEL] = jax.nn.sigmoid(
        dot(xb, w_ref[:, o2:o3]) + bgate[:, :D_MODEL]).astype(gt_ref.dtype)
    gt_ref[:, D_MODEL:] = jax.nn.sigmoid(
        dot(xb, w_ref[:, o3:]) + bgate[:, D_MODEL:]).astype(gt_ref.dtype)

    groups = CONV_ROWS // SUBLANES
    first_tap = HALO - CONV_HALF

    def conv_chunk(chunk, carry):
        r0 = pl.multiple_of(chunk * CONV_ROWS, CONV_ROWS)
        accs = [[jnp.zeros((SUBLANES, LANES), jnp.float32) for _ in range(CONV_SLABS)]
                for _ in range(groups)]
        for k in range(CONV_WIDTH):
            for c in range(CONV_SLABS):
                w = wb_s[k, :, c * LANES:(c + 1) * LANES]
                for gi in range(groups):
                    tap = glu_s[c, pl.ds(r0 + gi * SUBLANES + first_tap + k, SUBLANES), :]
                    accs[gi][c] = accs[gi][c] + tap * w
        dw_s[pl.ds(r0, CONV_ROWS), :] = jnp.concatenate(
            [jnp.concatenate(a, axis=1) for a in accs], axis=0)
        return carry

    lax.fori_loop(0, ts // CONV_ROWS, conv_chunk, 0)
    y = _layer_norm(dw_s[...] + convb_ref[...], clng_ref[...], clnb_ref[...])
    yc_ref[...] = (y * jax.nn.sigmoid(y)).astype(yc_ref.dtype)


def _inproj(x, ln_g, ln_b, w_in_bf16, b_gate, conv_w, conv_b, cln_g, cln_b):
    batch, seq, _ = x.shape
    ts = TS_IN
    assert seq % ts == 0 and ts % CONV_ROWS == 0 and ts % HALO == 0 and HALO >= CONV_HALF
    tile = lambda b, i: (b, i, 0)
    prev_spec, next_spec = _halo_specs(ts, seq, D_MODEL)
    bf16 = jnp.bfloat16
    return pl.pallas_call(
        _inproj_kernel,
        grid=(batch, seq // ts),
        in_specs=[
            pl.BlockSpec((None, ts, D_MODEL), tile),
            prev_spec,
            next_spec,
            _const_spec((1, D_MODEL)),
            _const_spec((1, D_MODEL)),
            _const_spec(w_in_bf16.shape),
            _const_spec((1, GATE_COLS)),
            _const_spec((CONV_WIDTH, CONV_CH)),
            _const_spec((1, CONV_CH)),
            _const_spec((1, CONV_CH)),
            _const_spec((1, CONV_CH)),
        ],
        out_specs=[
            pl.BlockSpec((None, ts, D_MODEL), tile),
            pl.BlockSpec((None, ts, QKV_COLS), tile),
            pl.BlockSpec((None, ts, GATE_COLS), tile),
            pl.BlockSpec((None, ts, CONV_CH), tile),
        ],
        out_shape=[
            jax.ShapeDtypeStruct((batch, seq, D_MODEL), jnp.float32),
            jax.ShapeDtypeStruct((batch, seq, QKV_COLS), bf16),
            jax.ShapeDtypeStruct((batch, seq, GATE_COLS), bf16),
            jax.ShapeDtypeStruct((batch, seq, CONV_CH), bf16),
        ],
        scratch_shapes=[
            pltpu.VMEM((ts + 2 * HALO, D_MODEL), bf16),
            pltpu.VMEM((CONV_WIDTH, SUBLANES, CONV_CH), jnp.float32),
            pltpu.VMEM((CONV_SLABS, ts + 2 * HALO, LANES), jnp.float32),
            pltpu.VMEM((ts, CONV_CH), jnp.float32),
        ],
        compiler_params=pltpu.CompilerParams(
            dimension_semantics=(_ARB, _ARB), vmem_limit_bytes=VMEM_LIMIT_BYTES),
        name="ln_inproj_conv",
    )(x, x, x, ln_g, ln_b, w_in_bf16, b_gate, conv_w, conv_b, cln_g, cln_b)


def _band_buckets():
    nb = NUM_BUCKETS // 2
    qi = np.arange(BLOCK)[:, None]
    kj = np.arange(BAND)[None, :]
    rel = kj - BLOCK - qi
    ret = (rel > 0).astype(np.int32) * nb
    n = np.abs(rel)
    max_exact = nb // 2
    large = max_exact + (np.log(np.maximum(n, 1) / max_exact)
                         / np.log(MAX_DISTANCE / max_exact) * (nb - max_exact)).astype(np.int32)
    large = np.minimum(large, nb - 1)
    buckets = (ret + np.where(n < max_exact, n, large)).astype(np.int32)
    return np.where(n <= WINDOW, buckets, MASKED_BUCKET).astype(np.int32)


def _mix_kernel(n_seq_blocks,
                relb_ref, sink_ref, bk_ref,
                q_ref, kv_ref, yc_ref, gt_ref, xn_ref,
                wao_ref, wco_ref, wout_ref, ln1g_ref, ln1b_ref,
                h_ref,
                bias_s, o_s):
    b_id = pl.program_id(0)
    i_id = pl.program_id(1)
    ts = q_ref.shape[0]
    blocks_per_tile = ts // BLOCK
    dot = functools.partial(jnp.dot, preferred_element_type=jnp.float32)

    @pl.when((b_id == 0) & (i_id == 0))
    def _init():
        bk = bk_ref[...]
        col = lax.broadcasted_iota(jnp.int32, (BLOCK, BAND), 1)
        for h in range(N_HEADS):
            def body(b, acc):
                return jnp.where(bk == b, relb_ref[b, h], acc)
            mid = lax.fori_loop(0, NUM_BUCKETS, body,
                                jnp.full((BLOCK, BAND), NEG_INF, jnp.float32))
            bias_s[0, h] = jnp.where(col < BLOCK, NEG_INF, mid)
            bias_s[1, h] = mid
            bias_s[2, h] = jnp.where(col >= 2 * BLOCK, NEG_INF, mid)

    lane = lax.broadcasted_iota(jnp.int32, (BLOCK, LANES), 1)
    low_half = lane < HEAD_DIM
    for j in range(blocks_per_tile):
        n = i_id * blocks_per_tile + j
        variant = jnp.where(n == 0, 0, jnp.where(n == n_seq_blocks - 1, 2, 1))
        prev_r = pl.multiple_of(jnp.maximum(n - 1, 0) * BLOCK, BLOCK)
        own_r = pl.multiple_of(n * BLOCK, BLOCK)
        next_r = pl.multiple_of(jnp.minimum(n + 1, n_seq_blocks - 1) * BLOCK, BLOCK)
        kvb = jnp.concatenate([kv_ref[pl.ds(prev_r, BLOCK), :],
                               kv_ref[pl.ds(own_r, BLOCK), :],
                               kv_ref[pl.ds(next_r, BLOCK), :]], axis=0)
        kk = kvb[:, :LANES]
        vv = kvb[:, LANES:]
        kk_sw = pltpu.roll(kk, HEAD_DIM, axis=1)
        vv_sw = pltpu.roll(vv, HEAD_DIM, axis=1)

        q = q_ref[j * BLOCK:(j + 1) * BLOCK, :]
        q = q * jnp.asarray(HEAD_DIM ** -0.5, q.dtype)
        zero = jnp.zeros((BLOCK, LANES), q.dtype)
        q_lo, q_hi = [], []
        for p in range(N_HEADS // 2):
            q2 = q[:, p * LANES:(p + 1) * LANES]
            q_lo.append(jnp.where(low_half, q2, zero))
            q_hi.append(jnp.where(low_half, zero, q2))
        heads_a = (0, 2, 5, 7)
        heads_b = (1, 3, 4, 6)
        lhs_a = jnp.concatenate([q_lo[0], q_lo[1], q_hi[2], q_hi[3]], axis=0)
        lhs_b = jnp.concatenate([q_hi[0], q_hi[1], q_lo[2], q_lo[3]], axis=0)
        nt = (((1,), (1,)), ((), ()))
        s_a = lax.dot_general(lhs_a, kk, nt, preferred_element_type=jnp.float32)
        s_b = lax.dot_general(lhs_b, kk_sw, nt, preferred_element_type=jnp.float32)

        def softmax_rows(s_all, heads):
            ps, inv_ls = [], []
            for r, h in enumerate(heads):
                s = s_all[r * BLOCK:(r + 1) * BLOCK, :] + bias_s[variant, h]
                sink = sink_ref[h]
                m = jnp.maximum(jnp.max(s, axis=-1, keepdims=True), sink)
                p = jnp.exp(s - m)
                l = jnp.sum(p, axis=-1, keepdims=True) + jnp.exp(sink - m)
                ps.append(p.astype(jnp.bfloat16))
                inv_ls.append(1.0 / l)
            return jnp.concatenate(ps, axis=0), inv_ls

        p_a, inv_a = softmax_rows(s_a, heads_a)
        p_b, inv_b = softmax_rows(s_b, heads_b)
        r_a = dot(p_a, vv)
        r_b = dot(p_b, vv_sw)
        o_head = {}
        for r, h in enumerate(heads_a):
            o_head[h] = r_a[r * BLOCK:(r + 1) * BLOCK, :] * inv_a[r]
        for r, h in enumerate(heads_b):
            o_head[h] = r_b[r * BLOCK:(r + 1) * BLOCK, :] * inv_b[r]
        o_pairs = [jnp.where(low_half, o_head[2 * p], o_head[2 * p + 1])
                   for p in range(N_HEADS // 2)]
        o_s[j * BLOCK:(j + 1) * BLOCK, :] = jnp.concatenate(o_pairs, axis=1).astype(o_s.dtype)

    y_a = dot(o_s[...], wao_ref[...])
    y_c = dot(yc_ref[...], wco_ref[...])
    merged = (gt_ref[:, :D_MODEL].astype(jnp.float32) * y_a
              + gt_ref[:, D_MODEL:].astype(jnp.float32) * y_c)
    mix = dot(merged.astype(jnp.bfloat16), wout_ref[...])
    h_ref[...] = _layer_norm(ALPHA * xn_ref[...] + mix, ln1g_ref[...], ln1b_ref[...])


def _mix(rel_bias, sink, qkv, yc, gt, xn, w_ao, w_co, w_out, ln1_g, ln1_b):
    batch, seq, _ = xn.shape
    ts = TS_MIX
    assert seq % ts == 0 and ts % BLOCK == 0 and seq // BLOCK >= 2
    n_seq_blocks = seq // BLOCK
    buckets = jnp.asarray(_band_buckets())
    tile = lambda b, i: (b, i, 0)
    return pl.pallas_call(
        functools.partial(_mix_kernel, n_seq_blocks),
        grid=(batch, seq // ts),
        in_specs=[
            _smem_spec(),
            _smem_spec(),
            _const_spec((BLOCK, BAND)),
            pl.BlockSpec((None, ts, Q_COLS), tile),
            pl.BlockSpec((None, seq, 2 * KV_COLS), lambda b, i: (b, 0, Q_COLS // (2 * KV_COLS))),
            pl.BlockSpec((None, ts, CONV_CH), tile),
            pl.BlockSpec((None, ts, GATE_COLS), tile),
            pl.BlockSpec((None, ts, D_MODEL), tile),
            _const_spec((Q_COLS, D_MODEL)),
            _const_spec((CONV_CH, D_MODEL)),
            _const_spec((D_MODEL, D_MODEL)),
            _const_spec((1, D_MODEL)),
            _const_spec((1, D_MODEL)),
        ],
        out_specs=pl.BlockSpec((None, ts, D_MODEL), tile),
        out_shape=jax.ShapeDtypeStruct((batch, seq, D_MODEL), jnp.float32),
        scratch_shapes=[
            pltpu.VMEM((3, N_HEADS, BLOCK, BAND), jnp.float32),
            pltpu.VMEM((ts, Q_COLS), jnp.bfloat16),
        ],
        compiler_params=pltpu.CompilerParams(
            dimension_semantics=(_ARB, _ARB), vmem_limit_bytes=VMEM_LIMIT_BYTES),
        name="token_mix",
    )(rel_bias, sink, buckets, qkv, qkv, yc, gt, xn, w_ao, w_co, w_out, ln1_g, ln1_b)


def _ffn_kernel(h_ref, wg_ref, wu_ref, wd_ref, g_ref, b_ref, o_ref):
    dot = functools.partial(jnp.dot, preferred_element_type=jnp.float32)
    h = h_ref[...]
    hb = h.astype(jnp.bfloat16)
    gate = dot(hb, wg_ref[...])
    up = dot(hb, wu_ref[...])
    hid = (gate * jax.nn.sigmoid(gate) * up).astype(jnp.bfloat16)
    o_ref[...] = _layer_norm(ALPHA * h + dot(hid, wd_ref[...]), g_ref[...], b_ref[...])


def _ffn(h2d, w_gate, w_up, w_down, ln_g, ln_b):
    n_tok = h2d.shape[0]
    assert n_tok % TM_FFN == 0
    row = lambda i: (i, 0)
    return pl.pallas_call(
        _ffn_kernel,
        grid=(n_tok // TM_FFN,),
        in_specs=[
            pl.BlockSpec((TM_FFN, D_MODEL), row),
            _const_spec(w_gate.shape),
            _const_spec(w_up.shape),
            _const_spec(w_down.shape),
            _const_spec((1, D_MODEL)),
            _const_spec((1, D_MODEL)),
        ],
        out_specs=pl.BlockSpec((TM_FFN, D_MODEL), row),
        out_shape=jax.ShapeDtypeStruct((n_tok, D_MODEL), jnp.float32),
        compiler_params=pltpu.CompilerParams(
            dimension_semantics=(_ARB,), vmem_limit_bytes=VMEM_LIMIT_BYTES),
        name="swiglu_ffn",
    )(h2d, w_gate, w_up, w_down, ln_g, ln_b)


def kernel(x, ln_in_g, ln_in_b, rel_bias, w_in, b_gate, conv_w, conv_b, conv_ln_g, conv_ln_b,
           w_conv_out, w_attn_out, sink, w_out, ln1_g, ln1_b, w_gate, w_up, w_down, ln2_g, ln2_b):
    batch, seq, d_model = x.shape
    assert d_model == D_MODEL and w_in.shape[0] == DEPTH
    bf16 = jnp.bfloat16
    row = lambda v: v.reshape(1, -1)
    n_tok = batch * seq

    xn, qkv, gt, yc = _inproj(x, row(ln_in_g), row(ln_in_b), w_in[0].astype(bf16), row(b_gate[0]),
                              conv_w[0], row(conv_b[0]), row(conv_ln_g[0]), row(conv_ln_b[0]))
    h = _mix(rel_bias, sink[0], qkv, yc, gt, xn,
             w_attn_out[0].astype(bf16), w_conv_out[0].astype(bf16), w_out[0].astype(bf16),
             row(ln1_g[0]), row(ln1_b[0]))
    out = _ffn(h.reshape(n_tok, D_MODEL), w_gate[0].astype(bf16), w_up[0].astype(bf16),
               w_down[0].astype(bf16), row(ln2_g[0]), row(ln2_b[0]))
    return out.reshape(batch, seq, D_MODEL)
```

```python
import functools

import numpy as np
import jax
import jax.numpy as jnp
from jax import lax
from jax.experimental import pallas as pl
from jax.experimental.pallas import tpu as pltpu

D_MODEL = 1024
HEAD_DIM = 64
N_HEADS = 8
N_KV_HEADS = 2
WINDOW = 128
BLOCK = 128
NUM_BUCKETS = 32
MAX_DISTANCE = 128
CONV_CH = 512
CONV_WIDTH = 31
CONV_HALF = (CONV_WIDTH - 1) // 2
Q_COLS = N_HEADS * HEAD_DIM
KV_COLS = N_KV_HEADS * HEAD_DIM
CONV_COLS = 2 * CONV_CH
GATE_COLS = 2 * D_MODEL
QKV_COLS = Q_COLS + 2 * KV_COLS
DEPTH = 1
ALPHA = (2.0 * DEPTH) ** 0.25
LN_EPS = 1e-5
NEG_INF = -1e30
MASKED_BUCKET = NUM_BUCKETS

BAND = 3 * BLOCK
HALO = 16
CONV_ROWS = 32
SUBLANES = 8
LANES = 128
CONV_SLABS = CONV_CH // LANES

VMEM_LIMIT_BYTES = 56 * 1024 * 1024

TS_IN = 512
TS_MIX = 512
TM_FFN = 512

_ARB = "arbitrary"


def _layer_norm(x, g, b):
    mu = jnp.mean(x, axis=-1, keepdims=True)
    xc = x - mu
    var = jnp.mean(xc * xc, axis=-1, keepdims=True)
    return xc * lax.rsqrt(var + LN_EPS) * g + b


def _const_spec(shape):
    return pl.BlockSpec(shape, lambda *_: (0,) * len(shape), pipeline_mode=pl.Buffered(1))


def _smem_spec():
    return pl.BlockSpec(memory_space=pltpu.SMEM)


def _halo_specs(ts, seq, cols):
    per_tile = ts // HALO
    last = seq // HALO - 1
    prev = pl.BlockSpec((None, HALO, cols), lambda b, i: (b, jnp.maximum(i * per_tile - 1, 0), 0))
    nxt = pl.BlockSpec((None, HALO, cols), lambda b, i: (b, jnp.minimum((i + 1) * per_tile, last), 0))
    return prev, nxt


def _inproj_kernel(x_ref, xp_ref, xnx_ref, g_ref, b_ref, w_ref, bgate_ref,
                   convw_ref, convb_ref, clng_ref, clnb_ref,
                   xn_ref, qkv_ref, gt_ref, yc_ref,
                   xb_s, wb_s, glu_s, dw_s):
    b_id = pl.program_id(0)
    i_id = pl.program_id(1)
    n_tiles = pl.num_programs(1)
    ts = x_ref.shape[0]
    dot = functools.partial(jnp.dot, preferred_element_type=jnp.float32)
    o1 = QKV_COLS
    o2 = o1 + CONV_COLS
    o3 = o2 + D_MODEL

    @pl.when((b_id == 0) & (i_id == 0))
    def _init():
        for k in range(CONV_WIDTH):
            wb_s[k] = jnp.broadcast_to(convw_ref[k:k + 1, :], (SUBLANES, CONV_CH))

    g = g_ref[...]
    b = b_ref[...]
    xn = _layer_norm(x_ref[...], g, b)
    xn_ref[...] = xn
    xb_s[0:HALO, :] = _layer_norm(xp_ref[...], g, b).astype(xb_s.dtype)
    xb_s[HALO:HALO + ts, :] = xn.astype(xb_s.dtype)
    xb_s[HALO + ts:2 * HALO + ts, :] = _layer_norm(xnx_ref[...], g, b).astype(xb_s.dtype)

    u = dot(xb_s[...], w_ref[:, o1:o2])
    glu = u[:, :CONV_CH] * jax.nn.sigmoid(u[:, CONV_CH:])
    row = lax.broadcasted_iota(jnp.int32, (ts + 2 * HALO, 1), 0)
    pad = ((row < HALO) & (i_id == 0)) | ((row >= HALO + ts) & (i_id == n_tiles - 1))
    glu = jnp.where(pad, 0.0, glu)
    for c in range(CONV_SLABS):
        glu_s[c] = glu[:, c * LANES:(c + 1) * LANES]

    xb = xb_s[HALO:HALO + ts, :]
    r_qkv = dot(xb, w_ref[:, :o1])
    r_g1 = dot(xb, w_ref[:, o2:o3])
    r_g2 = dot(xb, w_ref[:, o3:])
    qkv_ref[...] = r_qkv.astype(qkv_ref.dtype)
    bgate = bgate_ref[...]
    gt_ref[:, :D_MODEL] = jax.nn.sigmoid(r_g1 + bgate[:, :D_MODEL]).astype(gt_ref.dtype)
    gt_ref[:, D_MODEL:] = jax.nn.sigmoid(r_g2 + bgate[:, D_MODEL:]).astype(gt_ref.dtype)

    def zero_after(v):
        z = jnp.right_shift(jnp.right_shift(pltpu.bitcast(v, jnp.uint32), 16), 16)
        return pltpu.bitcast(z, jnp.float32)

    anchors = []
    for res in (r_qkv, r_g1, r_g2):
        for c0 in range(0, res.shape[1], 256):
            anchors.append(zero_after(res[ts - SUBLANES:ts, c0:c0 + LANES]))

    groups = CONV_ROWS // SUBLANES
    first_tap = HALO - CONV_HALF
    n_chunks = ts // CONV_ROWS

    accs = None
    for chunk in range(n_chunks):
        r0 = chunk * CONV_ROWS
        zero = anchors[min(chunk * len(anchors) // n_chunks, len(anchors) - 1)]
        if accs is not None:
            zero = zero + zero_after(accs[-1][-1])
        accs = [[zero for _ in range(CONV_SLABS)] for _ in range(groups)]
        for k in range(CONV_WIDTH):
            for c in range(CONV_SLABS):
                w = wb_s[k, :, c * LANES:(c + 1) * LANES] + zero
                for gi in range(groups):
                    r = r0 + gi * SUBLANES + first_tap + k
                    accs[gi][c] = accs[gi][c] + glu_s[c, r:r + SUBLANES, :] * w
        dw_s[r0:r0 + CONV_ROWS, :] = jnp.concatenate(
            [jnp.concatenate(a, axis=1) for a in accs], axis=0)
    y = _layer_norm(dw_s[...] + convb_ref[...], clng_ref[...], clnb_ref[...])
    yc_ref[...] = (y * jax.nn.sigmoid(y)).astype(yc_ref.dtype)


def _inproj(x, ln_g, ln_b, w_in_bf16, b_gate, conv_w, conv_b, cln_g, cln_b):
    batch, seq, _ = x.shape
    ts = TS_IN
    assert seq % ts == 0 and ts % CONV_ROWS == 0 and ts % HALO == 0 and HALO >= CONV_HALF
    tile = lambda b, i: (b, i, 0)
    prev_spec, next_spec = _halo_specs(ts, seq, D_MODEL)
    bf16 = jnp.bfloat16
    return pl.pallas_call(
        _inproj_kernel,
        grid=(batch, seq // ts),
        in_specs=[
            pl.BlockSpec((None, ts, D_MODEL), tile),
            prev_spec,
            next_spec,
            _const_spec((1, D_MODEL)),
            _const_spec((1, D_MODEL)),
            _const_spec(w_in_bf16.shape),
            _const_spec((1, GATE_COLS)),
            _const_spec((CONV_WIDTH, CONV_CH)),
            _const_spec((1, CONV_CH)),
            _const_spec((1, CONV_CH)),
            _const_spec((1, CONV_CH)),
        ],
        out_specs=[
            pl.BlockSpec((None, ts, D_MODEL), tile),
            pl.BlockSpec((None, ts, QKV_COLS), tile),
            pl.BlockSpec((None, ts, GATE_COLS), tile),
            pl.BlockSpec((None, ts, CONV_CH), tile),
        ],
        out_shape=[
            jax.ShapeDtypeStruct((batch, seq, D_MODEL), jnp.float32),
            jax.ShapeDtypeStruct((batch, seq, QKV_COLS), bf16),
            jax.ShapeDtypeStruct((batch, seq, GATE_COLS), bf16),
            jax.ShapeDtypeStruct((batch, seq, CONV_CH), bf16),
        ],
        scratch_shapes=[
            pltpu.VMEM((ts + 2 * HALO, D_MODEL), bf16),
            pltpu.VMEM((CONV_WIDTH, SUBLANES, CONV_CH), jnp.float32),
            pltpu.VMEM((CONV_SLABS, ts + 2 * HALO, LANES), jnp.float32),
            pltpu.VMEM((ts, CONV_CH), jnp.float32),
        ],
        compiler_params=pltpu.CompilerParams(
            dimension_semantics=(_ARB, _ARB), vmem_limit_bytes=VMEM_LIMIT_BYTES),
        name="ln_inproj_conv",
    )(x, x, x, ln_g, ln_b, w_in_bf16, b_gate, conv_w, conv_b, cln_g, cln_b)


def _band_buckets():
    nb = NUM_BUCKETS // 2
    qi = np.arange(BLOCK)[:, None]
    kj = np.arange(BAND)[None, :]
    rel = kj - BLOCK - qi
    ret = (rel > 0).astype(np.int32) * nb
    n = np.abs(rel)
    max_exact = nb // 2
    large = max_exact + (np.log(np.maximum(n, 1) / max_exact)
                         / np.log(MAX_DISTANCE / max_exact) * (nb - max_exact)).astype(np.int32)
    large = np.minimum(large, nb - 1)
    buckets = (ret + np.where(n < max_exact, n, large)).astype(np.int32)
    return np.where(n <= WINDOW, buckets, MASKED_BUCKET).astype(np.int32)


def _mix_kernel(n_seq_blocks,
                relb_ref, sink_ref, bk_ref,
                q_ref, kv_ref, yc_ref, gt_ref, xn_ref,
                wao_ref, wco_ref, wout_ref, ln1g_ref, ln1b_ref,
                h_ref,
                bias_s, o_s):
    b_id = pl.program_id(0)
    i_id = pl.program_id(1)
    ts = q_ref.shape[0]
    blocks_per_tile = ts // BLOCK
    dot = functools.partial(jnp.dot, preferred_element_type=jnp.float32)

    @pl.when((b_id == 0) & (i_id == 0))
    def _init():
        bk = bk_ref[...]
        col = lax.broadcasted_iota(jnp.int32, (BLOCK, BAND), 1)
        for h in range(N_HEADS):
            def body(b, acc):
                return jnp.where(bk == b, relb_ref[b, h], acc)
            mid = lax.fori_loop(0, NUM_BUCKETS, body,
                                jnp.full((BLOCK, BAND), NEG_INF, jnp.float32))
            bias_s[0, h] = jnp.where(col < BLOCK, NEG_INF, mid)
            bias_s[1, h] = mid
            bias_s[2, h] = jnp.where(col >= 2 * BLOCK, NEG_INF, mid)

    lane = lax.broadcasted_iota(jnp.int32, (BLOCK, LANES), 1)
    low_half = lane < HEAD_DIM
    for j in range(blocks_per_tile):
        n = i_id * blocks_per_tile + j
        variant = jnp.where(n == 0, 0, jnp.where(n == n_seq_blocks - 1, 2, 1))
        prev_r = pl.multiple_of(jnp.maximum(n - 1, 0) * BLOCK, BLOCK)
        own_r = pl.multiple_of(n * BLOCK, BLOCK)
        next_r = pl.multiple_of(jnp.minimum(n + 1, n_seq_blocks - 1) * BLOCK, BLOCK)
        kvb = jnp.concatenate([kv_ref[pl.ds(prev_r, BLOCK), :],
                               kv_ref[pl.ds(own_r, BLOCK), :],
                               kv_ref[pl.ds(next_r, BLOCK), :]], axis=0)
        kk = kvb[:, :LANES]
        vv = kvb[:, LANES:]
        kk_sw = pltpu.roll(kk, HEAD_DIM, axis=1)
        vv_sw = pltpu.roll(vv, HEAD_DIM, axis=1)

        q = q_ref[j * BLOCK:(j + 1) * BLOCK, :]
        q = q * jnp.asarray(HEAD_DIM ** -0.5, q.dtype)
        zero = jnp.zeros((BLOCK, LANES), q.dtype)
        q_lo, q_hi = [], []
        for p in range(N_HEADS // 2):
            q2 = q[:, p * LANES:(p + 1) * LANES]
            q_lo.append(jnp.where(low_half, q2, zero))
            q_hi.append(jnp.where(low_half, zero, q2))
        heads_a = (0, 2, 5, 7)
        heads_b = (1, 3, 4, 6)
        lhs_a = jnp.concatenate([q_lo[0], q_lo[1], q_hi[2], q_hi[3]], axis=0)
        lhs_b = jnp.concatenate([q_hi[0], q_hi[1], q_lo[2], q_lo[3]], axis=0)
        nt = (((1,), (1,)), ((), ()))
        s_a = lax.dot_general(lhs_a, kk, nt, preferred_element_type=jnp.float32)
        s_b = lax.dot_general(lhs_b, kk_sw, nt, preferred_element_type=jnp.float32)

        def softmax_rows(s_all, heads):
            ps, inv_ls = [], []
            for r, h in enumerate(heads):
                s = s_all[r * BLOCK:(r + 1) * BLOCK, :] + bias_s[variant, h]
                sink = sink_ref[h]
                m = jnp.maximum(jnp.max(s, axis=-1, keepdims=True), sink)
                p = jnp.exp(s - m)
                l = jnp.sum(p, axis=-1, keepdims=True) + jnp.exp(sink - m)
                ps.append(p.astype(jnp.bfloat16))
                inv_ls.append(1.0 / l)
            return jnp.concatenate(ps, axis=0), inv_ls

        p_a, inv_a = softmax_rows(s_a, heads_a)
        p_b, inv_b = softmax_rows(s_b, heads_b)
        r_a = dot(p_a, vv)
        r_b = dot(p_b, vv_sw)
        o_head = {}
        for r, h in enumerate(heads_a):
            o_head[h] = r_a[r * BLOCK:(r + 1) * BLOCK, :] * inv_a[r]
        for r, h in enumerate(heads_b):
            o_head[h] = r_b[r * BLOCK:(r + 1) * BLOCK, :] * inv_b[r]
        o_pairs = [jnp.where(low_half, o_head[2 * p], o_head[2 * p + 1])
                   for p in range(N_HEADS // 2)]
        o_s[j * BLOCK:(j + 1) * BLOCK, :] = jnp.concatenate(o_pairs, axis=1).astype(o_s.dtype)

    y_a = dot(o_s[...], wao_ref[...])
    y_c = dot(yc_ref[...], wco_ref[...])
    merged = (gt_ref[:, :D_MODEL].astype(jnp.float32) * y_a
              + gt_ref[:, D_MODEL:].astype(jnp.float32) * y_c)
    mix = dot(merged.astype(jnp.bfloat16), wout_ref[...])
    h_ref[...] = _layer_norm(ALPHA * xn_ref[...] + mix, ln1g_ref[...], ln1b_ref[...])


def _mix(rel_bias, sink, qkv, yc, gt, xn, w_ao, w_co, w_out, ln1_g, ln1_b):
    batch, seq, _ = xn.shape
    ts = TS_MIX
    assert seq % ts == 0 and ts % BLOCK == 0 and seq // BLOCK >= 2
    n_seq_blocks = seq // BLOCK
    buckets = jnp.asarray(_band_buckets())
    tile = lambda b, i: (b, i, 0)
    return pl.pallas_call(
        functools.partial(_mix_kernel, n_seq_blocks),
        grid=(batch, seq // ts),
        in_specs=[
            _smem_spec(),
            _smem_spec(),
            _const_spec((BLOCK, BAND)),
            pl.BlockSpec((None, ts, Q_COLS), tile),
            pl.BlockSpec((None, seq, 2 * KV_COLS), lambda b, i: (b, 0, Q_COLS // (2 * KV_COLS))),
            pl.BlockSpec((None, ts, CONV_CH), tile),
            pl.BlockSpec((None, ts, GATE_COLS), tile),
            pl.BlockSpec((None, ts, D_MODEL), tile),
            _const_spec((Q_COLS, D_MODEL)),
            _const_spec((CONV_CH, D_MODEL)),
            _const_spec((D_MODEL, D_MODEL)),
            _const_spec((1, D_MODEL)),
            _const_spec((1, D_MODEL)),
        ],
        out_specs=pl.BlockSpec((None, ts, D_MODEL), tile),
        out_shape=jax.ShapeDtypeStruct((batch, seq, D_MODEL), jnp.float32),
        scratch_shapes=[
            pltpu.VMEM((3, N_HEADS, BLOCK, BAND), jnp.float32),
            pltpu.VMEM((ts, Q_COLS), jnp.bfloat16),
        ],
        compiler_params=pltpu.CompilerParams(
            dimension_semantics=(_ARB, _ARB), vmem_limit_bytes=VMEM_LIMIT_BYTES),
        name="token_mix",
    )(rel_bias, sink, buckets, qkv, qkv, yc, gt, xn, w_ao, w_co, w_out, ln1_g, ln1_b)


def _ffn_kernel(h_ref, wg_ref, wu_ref, wd_ref, g_ref, b_ref, o_ref):
    dot = functools.partial(jnp.dot, preferred_element_type=jnp.float32)
    h = h_ref[...]
    hb = h.astype(jnp.bfloat16)
    gate = dot(hb, wg_ref[...])
    up = dot(hb, wu_ref[...])
    hid = (gate * jax.nn.sigmoid(gate) * up).astype(jnp.bfloat16)
    o_ref[...] = _layer_norm(ALPHA * h + dot(hid, wd_ref[...]), g_ref[...], b_ref[...])


def _ffn(h2d, w_gate, w_up, w_down, ln_g, ln_b):
    n_tok = h2d.shape[0]
    assert n_tok % TM_FFN == 0
    row = lambda i: (i, 0)
    return pl.pallas_call(
        _ffn_kernel,
        grid=(n_tok // TM_FFN,),
        in_specs=[
            pl.BlockSpec((TM_FFN, D_MODEL), row),
            _const_spec(w_gate.shape),
            _const_spec(w_up.shape),
            _const_spec(w_down.shape),
            _const_spec((1, D_MODEL)),
            _const_spec((1, D_MODEL)),
        ],
        out_specs=pl.BlockSpec((TM_FFN, D_MODEL), row),
        out_shape=jax.ShapeDtypeStruct((n_tok, D_MODEL), jnp.float32),
        compiler_params=pltpu.CompilerParams(
            dimension_semantics=(_ARB,), vmem_limit_bytes=VMEM_LIMIT_BYTES),
        name="swiglu_ffn",
    )(h2d, w_gate, w_up, w_down, ln_g, ln_b)


def kernel(x, ln_in_g, ln_in_b, rel_bias, w_in, b_gate, conv_w, conv_b, conv_ln_g, conv_ln_b,
           w_conv_out, w_attn_out, sink, w_out, ln1_g, ln1_b, w_gate, w_up, w_down, ln2_g, ln2_b):
    batch, seq, d_model = x.shape
    assert d_model == D_MODEL and w_in.shape[0] == DEPTH
    bf16 = jnp.bfloat16
    row = lambda v: v.reshape(1, -1)
    n_tok = batch * seq

    xn, qkv, gt, yc = _inproj(x, row(ln_in_g), row(ln_in_b), w_in[0].astype(bf16), row(b_gate[0]),
                              conv_w[0], row(conv_b[0]), row(conv_ln_g[0]), row(conv_ln_b[0]))
    h = _mix(rel_bias, sink[0], qkv, yc, gt, xn,
             w_attn_out[0].astype(bf16), w_conv_out[0].astype(bf16), w_out[0].astype(bf16),
             row(ln1_g[0]), row(ln1_b[0]))
    out = _ffn(h.reshape(n_tok, D_MODEL), w_gate[0].astype(bf16), w_up[0].astype(bf16),
               w_down[0].astype(bf16), row(ln2_g[0]), row(ln2_b[0]))
    return out.reshape(batch, seq, D_MODEL)
```

```python
import functools

import numpy as np
import jax
import jax.numpy as jnp
from jax import lax
from jax.experimental import pallas as pl
from jax.experimental.pallas import tpu as pltpu

D_MODEL = 1024
HEAD_DIM = 64
N_HEADS = 8
N_KV_HEADS = 2
WINDOW = 128
BLOCK = 128
NUM_BUCKETS = 32
MAX_DISTANCE = 128
CONV_CH = 512
CONV_WIDTH = 31
CONV_HALF = (CONV_WIDTH - 1) // 2
Q_COLS = N_HEADS * HEAD_DIM
KV_COLS = N_KV_HEADS * HEAD_DIM
CONV_COLS = 2 * CONV_CH
GATE_COLS = 2 * D_MODEL
QKV_COLS = Q_COLS + 2 * KV_COLS
DEPTH = 1
ALPHA = (2.0 * DEPTH) ** 0.25
LN_EPS = 1e-5
LOG2E = 1.4426950408889634
Q_SCALE = HEAD_DIM ** -0.5 * LOG2E
NEG_INF = -1e30
MASKED_BUCKET = NUM_BUCKETS

BAND = 3 * BLOCK
HALO = 16
CONV_ROWS = 32
SUBLANES = 8
BF16_SUBLANES = 16
LANES = 128
CONV_SLABS = CONV_CH // LANES

VMEM_LIMIT_BYTES = 56 * 1024 * 1024

TS_IN = 512
TS_MIX = 512
TM_FFN = 512

_ARB = "arbitrary"


def _layer_norm(x, g, b):
    mu = jnp.mean(x, axis=-1, keepdims=True)
    xc = x - mu
    var = jnp.mean(xc * xc, axis=-1, keepdims=True)
    return xc * lax.rsqrt(var + LN_EPS) * g + b


def _const_spec(shape):
    return pl.BlockSpec(shape, lambda *_: (0,) * len(shape), pipeline_mode=pl.Buffered(1))


def _smem_spec():
    return pl.BlockSpec(memory_space=pltpu.SMEM)


def _halo_specs(ts, seq, cols):
    per_tile = ts // HALO
    last = seq // HALO - 1
    prev = pl.BlockSpec((None, HALO, cols), lambda b, i: (b, jnp.maximum(i * per_tile - 1, 0), 0))
    nxt = pl.BlockSpec((None, HALO, cols), lambda b, i: (b, jnp.minimum((i + 1) * per_tile, last), 0))
    return prev, nxt


def _inproj_kernel(n_cast,
                   x_ref, xp_ref, xnx_ref, g_ref, b_ref, w_ref, bgate_ref,
                   convw_ref, convb_ref, clng_ref, clnb_ref, *rest):
    cast_in = rest[:n_cast]
    xn_ref, qkv_ref, gt_ref, yc_ref = rest[n_cast:n_cast + 4]
    cast_out = rest[n_cast + 4:2 * n_cast + 4]
    xb_s, wb_s, glu_s, dw_s = rest[2 * n_cast + 4:]
    b_id = pl.program_id(0)
    i_id = pl.program_id(1)
    n_tiles = pl.num_programs(1)
    ts = x_ref.shape[0]
    dot = functools.partial(jnp.dot, preferred_element_type=jnp.float32)
    o1 = QKV_COLS
    o2 = o1 + CONV_COLS
    o3 = o2 + D_MODEL

    @pl.when((b_id == 0) & (i_id == 0))
    def _init():
        for k in range(CONV_WIDTH):
            wb_s[k] = jnp.broadcast_to(convw_ref[k:k + 1, :], (SUBLANES, CONV_CH))

    @pl.when(i_id == 0)
    def _cast_weights():
        for src, dst in zip(cast_in, cast_out):
            dst[...] = src[...].astype(dst.dtype)

    g = g_ref[...]
    b = b_ref[...]
    xn = _layer_norm(x_ref[...], g, b)
    xn_ref[...] = xn
    xb_s[0:HALO, :] = _layer_norm(xp_ref[...], g, b).astype(xb_s.dtype)
    xb_s[HALO:HALO + ts, :] = xn.astype(xb_s.dtype)
    xb_s[HALO + ts:2 * HALO + ts, :] = _layer_norm(xnx_ref[...], g, b).astype(xb_s.dtype)

    u = dot(xb_s[...], w_ref[:, o1:o2])
    glu = u[:, :CONV_CH] * jax.nn.sigmoid(u[:, CONV_CH:])
    row = lax.broadcasted_iota(jnp.int32, (ts + 2 * HALO, 1), 0)
    pad = ((row < HALO) & (i_id == 0)) | ((row >= HALO + ts) & (i_id == n_tiles - 1))
    glu = jnp.where(pad, 0.0, glu)
    for c in range(CONV_SLABS):
        glu_s[c] = glu[:, c * LANES:(c + 1) * LANES]

    xb = xb_s[HALO:HALO + ts, :]
    qkv_ref[:, :Q_COLS] = (dot(xb, w_ref[:, :Q_COLS]) * Q_SCALE).astype(qkv_ref.dtype)
    qkv_ref[:, Q_COLS:] = dot(xb, w_ref[:, Q_COLS:o1]).astype(qkv_ref.dtype)
    bgate = bgate_ref[...]
    gt_ref[:, :D_MODEL] = jax.nn.sigmoid(
        dot(xb, w_ref[:, o2:o3]) + bgate[:, :D_MODEL]).astype(gt_ref.dtype)
    gt_ref[:, D_MODEL:] = jax.nn.sigmoid(
        dot(xb, w_ref[:, o3:]) + bgate[:, D_MODEL:]).astype(gt_ref.dtype)

    groups = CONV_ROWS // SUBLANES
    first_tap = HALO - CONV_HALF

    def conv_chunk(chunk, carry):
        r0 = pl.multiple_of(chunk * CONV_ROWS, CONV_ROWS)
        accs = [[jnp.zeros((SUBLANES, LANES), jnp.float32) for _ in range(CONV_SLABS)]
                for _ in range(groups)]
        for k in range(CONV_WIDTH):
            for c in range(CONV_SLABS):
                w = wb_s[k, :, c * LANES:(c + 1) * LANES]
                for gi in range(groups):
                    tap = glu_s[c, pl.ds(r0 + gi * SUBLANES + first_tap + k, SUBLANES), :]
                    accs[gi][c] = accs[gi][c] + tap * w
        dw_s[pl.ds(r0, CONV_ROWS), :] = jnp.concatenate(
            [jnp.concatenate(a, axis=1) for a in accs], axis=0)
        return carry

    lax.fori_loop(0, ts // CONV_ROWS, conv_chunk, 0)
    y = _layer_norm(dw_s[...] + convb_ref[...], clng_ref[...], clnb_ref[...])
    yc_ref[...] = (y * jax.nn.sigmoid(y)).astype(yc_ref.dtype)


def _inproj(x, ln_g, ln_b, w_in_bf16, b_gate, conv_w, conv_b, cln_g, cln_b, later_weights):
    batch, seq, _ = x.shape
    ts = TS_IN
    assert seq % ts == 0 and ts % CONV_ROWS == 0 and ts % HALO == 0 and HALO >= CONV_HALF
    tile = lambda b, i: (b, i, 0)
    prev_spec, next_spec = _halo_specs(ts, seq, D_MODEL)
    bf16 = jnp.bfloat16
    cast_specs = []
    for w in later_weights:
        rows, cols = w.shape
        assert rows % (batch * BF16_SUBLANES) == 0
        cast_specs.append(pl.BlockSpec((rows // batch, cols), lambda b, i: (b, 0)))
    outs = pl.pallas_call(
        functools.partial(_inproj_kernel, len(later_weights)),
        grid=(batch, seq // ts),
        in_specs=[
            pl.BlockSpec((None, ts, D_MODEL), tile),
            prev_spec,
            next_spec,
            _const_spec((1, D_MODEL)),
            _const_spec((1, D_MODEL)),
            _const_spec(w_in_bf16.shape),
            _const_spec((1, GATE_COLS)),
            _const_spec((CONV_WIDTH, CONV_CH)),
            _const_spec((1, CONV_CH)),
            _const_spec((1, CONV_CH)),
            _const_spec((1, CONV_CH)),
        ] + cast_specs,
        out_specs=[
            pl.BlockSpec((None, ts, D_MODEL), tile),
            pl.BlockSpec((None, ts, QKV_COLS), tile),
            pl.BlockSpec((None, ts, GATE_COLS), tile),
            pl.BlockSpec((None, ts, CONV_CH), tile),
        ] + cast_specs,
        out_shape=[
            jax.ShapeDtypeStruct((batch, seq, D_MODEL), jnp.float32),
            jax.ShapeDtypeStruct((batch, seq, QKV_COLS), bf16),
            jax.ShapeDtypeStruct((batch, seq, GATE_COLS), bf16),
            jax.ShapeDtypeStruct((batch, seq, CONV_CH), bf16),
        ] + [jax.ShapeDtypeStruct(w.shape, bf16) for w in later_weights],
        scratch_shapes=[
            pltpu.VMEM((ts + 2 * HALO, D_MODEL), bf16),
            pltpu.VMEM((CONV_WIDTH, SUBLANES, CONV_CH), jnp.float32),
            pltpu.VMEM((CONV_SLABS, ts + 2 * HALO, LANES), jnp.float32),
            pltpu.VMEM((ts, CONV_CH), jnp.float32),
        ],
        compiler_params=pltpu.CompilerParams(
            dimension_semantics=(_ARB, _ARB), vmem_limit_bytes=VMEM_LIMIT_BYTES),
        name="ln_inproj_conv",
    )(x, x, x, ln_g, ln_b, w_in_bf16, b_gate, conv_w, conv_b, cln_g, cln_b, *later_weights)
    return outs[:4], outs[4:]


def _band_buckets():
    nb = NUM_BUCKETS // 2
    qi = np.arange(BLOCK)[:, None]
    kj = np.arange(BAND)[None, :]
    rel = kj - BLOCK - qi
    ret = (rel > 0).astype(np.int32) * nb
    n = np.abs(rel)
    max_exact = nb // 2
    large = max_exact + (np.log(np.maximum(n, 1) / max_exact)
                         / np.log(MAX_DISTANCE / max_exact) * (nb - max_exact)).astype(np.int32)
    large = np.minimum(large, nb - 1)
    buckets = (ret + np.where(n < max_exact, n, large)).astype(np.int32)
    return np.where(n <= WINDOW, buckets, MASKED_BUCKET).astype(np.int32)


def _mix_kernel(n_seq_blocks,
                relb_ref, sink_ref, bk_ref,
                q_ref, kv_ref, yc_ref, gt_ref, xn_ref,
                wao_ref, wco_ref, wout_ref, ln1g_ref, ln1b_ref,
                h_ref,
                bias_s, o_s):
    b_id = pl.program_id(0)
    i_id = pl.program_id(1)
    ts = q_ref.shape[0]
    blocks_per_tile = ts // BLOCK
    dot = functools.partial(jnp.dot, preferred_element_type=jnp.float32)

    @pl.when((b_id == 0) & (i_id == 0))
    def _init():
        bk = bk_ref[...]
        col = lax.broadcasted_iota(jnp.int32, (BLOCK, BAND), 1)
        for h in range(N_HEADS):
            def body(b, acc):
                return jnp.where(bk == b, relb_ref[b, h], acc)
            mid = LOG2E * lax.fori_loop(0, NUM_BUCKETS, body,
                                        jnp.full((BLOCK, BAND), NEG_INF, jnp.float32))
            bias_s[0, h] = jnp.where(col < BLOCK, NEG_INF, mid)
            bias_s[1, h] = mid
            bias_s[2, h] = jnp.where(col >= 2 * BLOCK, NEG_INF, mid)

    lane = lax.broadcasted_iota(jnp.int32, (BLOCK, LANES), 1)
    low_half = lane < HEAD_DIM
    for j in range(blocks_per_tile):
        n = i_id * blocks_per_tile + j
        variant = jnp.where(n == 0, 0, jnp.where(n == n_seq_blocks - 1, 2, 1))
        prev_r = pl.multiple_of(jnp.maximum(n - 1, 0) * BLOCK, BLOCK)
        own_r = pl.multiple_of(n * BLOCK, BLOCK)
        next_r = pl.multiple_of(jnp.minimum(n + 1, n_seq_blocks - 1) * BLOCK, BLOCK)
        kvb = jnp.concatenate([kv_ref[pl.ds(prev_r, BLOCK), :],
                               kv_ref[pl.ds(own_r, BLOCK), :],
                               kv_ref[pl.ds(next_r, BLOCK), :]], axis=0)
        kk = kvb[:, :LANES]
        vv = kvb[:, LANES:]
        kk_sw = pltpu.roll(kk, HEAD_DIM, axis=1)
        vv_sw = pltpu.roll(vv, HEAD_DIM, axis=1)

        q = q_ref[j * BLOCK:(j + 1) * BLOCK, :]
        zero = jnp.zeros((BLOCK, LANES), q.dtype)
        q_lo, q_hi = [], []
        for p in range(N_HEADS // 2):
            q2 = q[:, p * LANES:(p + 1) * LANES]
            q_lo.append(jnp.where(low_half, q2, zero))
            q_hi.append(jnp.where(low_half, zero, q2))
        heads_a = (0, 2, 5, 7)
        heads_b = (1, 3, 4, 6)
        lhs_a = jnp.concatenate([q_lo[0], q_lo[1], q_hi[2], q_hi[3]], axis=0)
        lhs_b = jnp.concatenate([q_hi[0], q_hi[1], q_lo[2], q_lo[3]], axis=0)
        nt = (((1,), (1,)), ((), ()))
        s_a = lax.dot_general(lhs_a, kk, nt, preferred_element_type=jnp.float32)
        s_b = lax.dot_general(lhs_b, kk_sw, nt, preferred_element_type=jnp.float32)

        def softmax_rows(s_all, heads):
            ps, inv_ls = [], []
            for r, h in enumerate(heads):
                s = s_all[r * BLOCK:(r + 1) * BLOCK, :] + bias_s[variant, h]
                sink = jnp.full((BLOCK, 1), sink_ref[h], jnp.float32) * LOG2E
                m = jnp.maximum(jnp.max(s, axis=-1, keepdims=True), sink)
                p = jnp.exp2(s - m)
                l = jnp.sum(p, axis=-1, keepdims=True) + jnp.exp2(sink - m)
                ps.append(p.astype(jnp.bfloat16))
                inv_ls.append(1.0 / l)
            return jnp.concatenate(ps, axis=0), inv_ls

        p_a, inv_a = softmax_rows(s_a, heads_a)
        p_b, inv_b = softmax_rows(s_b, heads_b)
        r_a = dot(p_a, vv)
        r_b = dot(p_b, vv_sw)
        o_head = {}
        for r, h in enumerate(heads_a):
            o_head[h] = r_a[r * BLOCK:(r + 1) * BLOCK, :] * inv_a[r]
        for r, h in enumerate(heads_b):
            o_head[h] = r_b[r * BLOCK:(r + 1) * BLOCK, :] * inv_b[r]
        o_pairs = [jnp.where(low_half, o_head[2 * p], o_head[2 * p + 1])
                   for p in range(N_HEADS // 2)]
        o_s[j * BLOCK:(j + 1) * BLOCK, :] = jnp.concatenate(o_pairs, axis=1).astype(o_s.dtype)

    y_a = dot(o_s[...], wao_ref[...])
    y_c = dot(yc_ref[...], wco_ref[...])
    merged = (gt_ref[:, :D_MODEL].astype(jnp.float32) * y_a
              + gt_ref[:, D_MODEL:].astype(jnp.float32) * y_c)
    mix = dot(merged.astype(jnp.bfloat16), wout_ref[...])
    h_ref[...] = _layer_norm(ALPHA * xn_ref[...] + mix, ln1g_ref[...], ln1b_ref[...])


def _mix(rel_bias, sink, qkv, yc, gt, xn, w_ao, w_co, w_out, ln1_g, ln1_b):
    batch, seq, _ = xn.shape
    ts = TS_MIX
    assert seq % ts == 0 and ts % BLOCK == 0 and seq // BLOCK >= 2
    n_seq_blocks = seq // BLOCK
    buckets = jnp.asarray(_band_buckets())
    tile = lambda b, i: (b, i, 0)
    return pl.pallas_call(
        functools.partial(_mix_kernel, n_seq_blocks),
        grid=(batch, seq // ts),
        in_specs=[
            _smem_spec(),
            _smem_spec(),
            _const_spec((BLOCK, BAND)),
            pl.BlockSpec((None, ts, Q_COLS), tile),
            pl.BlockSpec((None, seq, 2 * KV_COLS), lambda b, i: (b, 0, Q_COLS // (2 * KV_COLS))),
            pl.BlockSpec((None, ts, CONV_CH), tile),
            pl.BlockSpec((None, ts, GATE_COLS), tile),
            pl.BlockSpec((None, ts, D_MODEL), tile),
            _const_spec((Q_COLS, D_MODEL)),
            _const_spec((CONV_CH, D_MODEL)),
            _const_spec((D_MODEL, D_MODEL)),
            _const_spec((1, D_MODEL)),
            _const_spec((1, D_MODEL)),
        ],
        out_specs=pl.BlockSpec((None, ts, D_MODEL), tile),
        out_shape=jax.ShapeDtypeStruct((batch, seq, D_MODEL), jnp.float32),
        scratch_shapes=[
            pltpu.VMEM((3, N_HEADS, BLOCK, BAND), jnp.float32),
            pltpu.VMEM((ts, Q_COLS), jnp.bfloat16),
        ],
        compiler_params=pltpu.CompilerParams(
            dimension_semantics=(_ARB, _ARB), vmem_limit_bytes=VMEM_LIMIT_BYTES),
        name="token_mix",
    )(rel_bias, sink, buckets, qkv, qkv, yc, gt, xn, w_ao, w_co, w_out, ln1_g, ln1_b)


def _ffn_kernel(h_ref, wg_ref, wu_ref, wd_ref, g_ref, b_ref, o_ref):
    dot = functools.partial(jnp.dot, preferred_element_type=jnp.float32)
    h = h_ref[...]
    hb = h.astype(jnp.bfloat16)
    gate = dot(hb, wg_ref[...])
    up = dot(hb, wu_ref[...])
    hid = (gate * jax.nn.sigmoid(gate) * up).astype(jnp.bfloat16)
    o_ref[...] = _layer_norm(ALPHA * h + dot(hid, wd_ref[...]), g_ref[...], b_ref[...])


def _ffn(h2d, w_gate, w_up, w_down, ln_g, ln_b):
    n_tok = h2d.shape[0]
    assert n_tok % TM_FFN == 0
    row = lambda i: (i, 0)
    return pl.pallas_call(
        _ffn_kernel,
        grid=(n_tok // TM_FFN,),
        in_specs=[
            pl.BlockSpec((TM_FFN, D_MODEL), row),
            _const_spec(w_gate.shape),
            _const_spec(w_up.shape),
            _const_spec(w_down.shape),
            _const_spec((1, D_MODEL)),
            _const_spec((1, D_MODEL)),
        ],
        out_specs=pl.BlockSpec((TM_FFN, D_MODEL), row),
        out_shape=jax.ShapeDtypeStruct((n_tok, D_MODEL), jnp.float32),
        compiler_params=pltpu.CompilerParams(
            dimension_semantics=(_ARB,), vmem_limit_bytes=VMEM_LIMIT_BYTES),
        name="swiglu_ffn",
    )(h2d, w_gate, w_up, w_down, ln_g, ln_b)


def kernel(x, ln_in_g, ln_in_b, rel_bias, w_in, b_gate, conv_w, conv_b, conv_ln_g, conv_ln_b,
           w_conv_out, w_attn_out, sink, w_out, ln1_g, ln1_b, w_gate, w_up, w_down, ln2_g, ln2_b):
    batch, seq, d_model = x.shape
    assert d_model == D_MODEL and w_in.shape[0] == DEPTH
    bf16 = jnp.bfloat16
    row = lambda v: v.reshape(1, -1)
    n_tok = batch * seq

    (xn, qkv, gt, yc), (w_ao, w_co, w_o, w_g, w_u, w_d) = _inproj(
        x, row(ln_in_g), row(ln_in_b), w_in[0].astype(bf16), row(b_gate[0]),
        conv_w[0], row(conv_b[0]), row(conv_ln_g[0]), row(conv_ln_b[0]),
        (w_attn_out[0], w_conv_out[0], w_out[0], w_gate[0], w_up[0], w_down[0]))
    h = _mix(rel_bias, sink[0], qkv, yc, gt, xn, w_ao, w_co, w_o, row(ln1_g[0]), row(ln1_b[0]))
    out = _ffn(h.reshape(n_tok, D_MODEL), w_g, w_u, w_d, row(ln2_g[0]), row(ln2_b[0]))
    return out.reshape(batch, seq, D_MODEL)
```

```python
import functools

import numpy as np
import jax
import jax.numpy as jnp
from jax import lax
from jax.experimental import pallas as pl
from jax.experimental.pallas import tpu as pltpu

D_MODEL = 1024
HEAD_DIM = 64
N_HEADS = 8
N_KV_HEADS = 2
WINDOW = 128
BLOCK = 128
NUM_BUCKETS = 32
MAX_DISTANCE = 128
CONV_CH = 512
CONV_WIDTH = 31
CONV_HALF = (CONV_WIDTH - 1) // 2
Q_COLS = N_HEADS * HEAD_DIM
KV_COLS = N_KV_HEADS * HEAD_DIM
CONV_COLS = 2 * CONV_CH
GATE_COLS = 2 * D_MODEL
QKV_COLS = Q_COLS + 2 * KV_COLS
DEPTH = 1
ALPHA = (2.0 * DEPTH) ** 0.25
LN_EPS = 1e-5
LOG2E = 1.4426950408889634
Q_SCALE = HEAD_DIM ** -0.5 * LOG2E
NEG_INF = -1e30
MASKED_BUCKET = NUM_BUCKETS

BAND = 3 * BLOCK
HALO = 16
CONV_ROWS = 32
SUBLANES = 8
BF16_SUBLANES = 16
LANES = 128
MXU_COLS = 256
CONV_SLABS = CONV_CH // LANES

VMEM_LIMIT_BYTES = 56 * 1024 * 1024

TS_IN = 512
TS_MIX = 512
TM_FFN = 512

_ARB = "arbitrary"


def _layer_norm(x, g, b):
    mu = jnp.mean(x, axis=-1, keepdims=True)
    xc = x - mu
    var = jnp.mean(xc * xc, axis=-1, keepdims=True)
    return xc * lax.rsqrt(var + LN_EPS) * g + b


def _const_spec(shape):
    return pl.BlockSpec(shape, lambda *_: (0,) * len(shape), pipeline_mode=pl.Buffered(1))


def _smem_spec():
    return pl.BlockSpec(memory_space=pltpu.SMEM)


def _halo_specs(ts, seq, cols):
    per_tile = ts // HALO
    last = seq // HALO - 1
    prev = pl.BlockSpec((None, HALO, cols), lambda b, i: (b, jnp.maximum(i * per_tile - 1, 0), 0))
    nxt = pl.BlockSpec((None, HALO, cols), lambda b, i: (b, jnp.minimum((i + 1) * per_tile, last), 0))
    return prev, nxt


def _inproj_kernel(x_ref, xp_ref, xnx_ref, g_ref, b_ref, w_ref, bgate_ref,
                   convw_ref, convb_ref, clng_ref, clnb_ref,
                   wao_ref, wco_ref, wout_ref, wgate_ref, wup_ref, wdown_ref,
                   xn_ref, qkv_ref, gt_ref, yc_ref,
                   wao_o, wco_o, wout_o, wgu_o, wdown_o,
                   xb_s, wb_s, glu_s, dw_s):
    b_id = pl.program_id(0)
    i_id = pl.program_id(1)
    n_tiles = pl.num_programs(1)
    ts = x_ref.shape[0]
    dot = functools.partial(jnp.dot, preferred_element_type=jnp.float32)
    o1 = QKV_COLS
    o2 = o1 + CONV_COLS
    o3 = o2 + D_MODEL

    @pl.when((b_id == 0) & (i_id == 0))
    def _init():
        for k in range(CONV_WIDTH):
            wb_s[k] = jnp.broadcast_to(convw_ref[k:k + 1, :], (SUBLANES, CONV_CH))

    @pl.when(i_id == 0)
    def _cast_weights():
        for src, dst in ((wao_ref, wao_o), (wco_ref, wco_o), (wout_ref, wout_o), (wdown_ref, wdown_o)):
            dst[...] = src[...].astype(dst.dtype)
        for j in range(wgate_ref.shape[1] // MXU_COLS):
            src_cols = slice(j * MXU_COLS, (j + 1) * MXU_COLS)
            wgu_o[:, 2 * j * MXU_COLS:(2 * j + 1) * MXU_COLS] = wgate_ref[:, src_cols].astype(wgu_o.dtype)
            wgu_o[:, (2 * j + 1) * MXU_COLS:(2 * j + 2) * MXU_COLS] = wup_ref[:, src_cols].astype(wgu_o.dtype)

    g = g_ref[...]
    b = b_ref[...]
    xn = _layer_norm(x_ref[...], g, b)
    xn_ref[...] = xn
    xb_s[0:HALO, :] = _layer_norm(xp_ref[...], g, b).astype(xb_s.dtype)
    xb_s[HALO:HALO + ts, :] = xn.astype(xb_s.dtype)
    xb_s[HALO + ts:2 * HALO + ts, :] = _layer_norm(xnx_ref[...], g, b).astype(xb_s.dtype)

    u = dot(xb_s[...], w_ref[:, o1:o2])
    glu = u[:, :CONV_CH] * jax.nn.sigmoid(u[:, CONV_CH:])
    row = lax.broadcasted_iota(jnp.int32, (ts + 2 * HALO, 1), 0)
    pad = ((row < HALO) & (i_id == 0)) | ((row >= HALO + ts) & (i_id == n_tiles - 1))
    glu = jnp.where(pad, 0.0, glu)
    for c in range(CONV_SLABS):
        glu_s[c] = glu[:, c * LANES:(c + 1) * LANES]

    xb = xb_s[HALO:HALO + ts, :]
    qkv_ref[:, :Q_COLS] = (dot(xb, w_ref[:, :Q_COLS]) * Q_SCALE).astype(qkv_ref.dtype)
    qkv_ref[:, Q_COLS:] = dot(xb, w_ref[:, Q_COLS:o1]).astype(qkv_ref.dtype)
    bgate = bgate_ref[...]
    gt_ref[:, :D_MODEL] = jax.nn.sigmoid(
        dot(xb, w_ref[:, o2:o3]) + bgate[:, :D_MODEL]).astype(gt_ref.dtype)
    gt_ref[:, D_MODEL:] = jax.nn.sigmoid(
        dot(xb, w_ref[:, o3:]) + bgate[:, D_MODEL:]).astype(gt_ref.dtype)

    groups = CONV_ROWS // SUBLANES
    first_tap = HALO - CONV_HALF

    def conv_chunk(chunk, carry):
        r0 = pl.multiple_of(chunk * CONV_ROWS, CONV_ROWS)
        accs = [[jnp.zeros((SUBLANES, LANES), jnp.float32) for _ in range(CONV_SLABS)]
                for _ in range(groups)]
        for k in range(CONV_WIDTH):
            for c in range(CONV_SLABS):
                w = wb_s[k, :, c * LANES:(c + 1) * LANES]
                for gi in range(groups):
                    tap = glu_s[c, pl.ds(r0 + gi * SUBLANES + first_tap + k, SUBLANES), :]
                    accs[gi][c] = accs[gi][c] + tap * w
        dw_s[pl.ds(r0, CONV_ROWS), :] = jnp.concatenate(
            [jnp.concatenate(a, axis=1) for a in accs], axis=0)
        return carry

    lax.fori_loop(0, ts // CONV_ROWS, conv_chunk, 0)
    y = _layer_norm(dw_s[...] + convb_ref[...], clng_ref[...], clnb_ref[...])
    yc_ref[...] = (y * jax.nn.sigmoid(y)).astype(yc_ref.dtype)


def _inproj(x, ln_g, ln_b, w_in_bf16, b_gate, conv_w, conv_b, cln_g, cln_b, later_weights):
    batch, seq, _ = x.shape
    ts = TS_IN
    assert seq % ts == 0 and ts % CONV_ROWS == 0 and ts % HALO == 0 and HALO >= CONV_HALF
    tile = lambda b, i: (b, i, 0)
    prev_spec, next_spec = _halo_specs(ts, seq, D_MODEL)
    bf16 = jnp.bfloat16
    w_ao, w_co, w_out, w_gate, w_up, w_down = later_weights
    assert w_gate.shape == w_up.shape and w_gate.shape[1] % MXU_COLS == 0
    cast_out_shapes = [w_ao.shape, w_co.shape, w_out.shape,
                       (w_gate.shape[0], 2 * w_gate.shape[1]), w_down.shape]

    def slice_spec(shape):
        rows, cols = shape
        assert rows % (batch * BF16_SUBLANES) == 0
        return pl.BlockSpec((rows // batch, cols), lambda b, i: (b, 0))

    cast_in_specs = [slice_spec(w.shape) for w in later_weights]
    cast_out_specs = [slice_spec(s) for s in cast_out_shapes]
    outs = pl.pallas_call(
        _inproj_kernel,
        grid=(batch, seq // ts),
        in_specs=[
            pl.BlockSpec((None, ts, D_MODEL), tile),
            prev_spec,
            next_spec,
            _const_spec((1, D_MODEL)),
            _const_spec((1, D_MODEL)),
            _const_spec(w_in_bf16.shape),
            _const_spec((1, GATE_COLS)),
            _const_spec((CONV_WIDTH, CONV_CH)),
            _const_spec((1, CONV_CH)),
            _const_spec((1, CONV_CH)),
            _const_spec((1, CONV_CH)),
        ] + cast_in_specs,
        out_specs=[
            pl.BlockSpec((None, ts, D_MODEL), tile),
            pl.BlockSpec((None, ts, QKV_COLS), tile),
            pl.BlockSpec((None, ts, GATE_COLS), tile),
            pl.BlockSpec((None, ts, CONV_CH), tile),
        ] + cast_out_specs,
        out_shape=[
            jax.ShapeDtypeStruct((batch, seq, D_MODEL), jnp.float32),
            jax.ShapeDtypeStruct((batch, seq, QKV_COLS), bf16),
            jax.ShapeDtypeStruct((batch, seq, GATE_COLS), bf16),
            jax.ShapeDtypeStruct((batch, seq, CONV_CH), bf16),
        ] + [jax.ShapeDtypeStruct(s, bf16) for s in cast_out_shapes],
        scratch_shapes=[
            pltpu.VMEM((ts + 2 * HALO, D_MODEL), bf16),
            pltpu.VMEM((CONV_WIDTH, SUBLANES, CONV_CH), jnp.float32),
            pltpu.VMEM((CONV_SLABS, ts + 2 * HALO, LANES), jnp.float32),
            pltpu.VMEM((ts, CONV_CH), jnp.float32),
        ],
        compiler_params=pltpu.CompilerParams(
            dimension_semantics=(_ARB, _ARB), vmem_limit_bytes=VMEM_LIMIT_BYTES),
        name="ln_inproj_conv",
    )(x, x, x, ln_g, ln_b, w_in_bf16, b_gate, conv_w, conv_b, cln_g, cln_b, *later_weights)
    return outs[:4], outs[4:]


def _band_buckets():
    nb = NUM_BUCKETS // 2
    qi = np.arange(BLOCK)[:, None]
    kj = np.arange(BAND)[None, :]
    rel = kj - BLOCK - qi
    ret = (rel > 0).astype(np.int32) * nb
    n = np.abs(rel)
    max_exact = nb // 2
    large = max_exact + (np.log(np.maximum(n, 1) / max_exact)
                         / np.log(MAX_DISTANCE / max_exact) * (nb - max_exact)).astype(np.int32)
    large = np.minimum(large, nb - 1)
    buckets = (ret + np.where(n < max_exact, n, large)).astype(np.int32)
    return np.where(n <= WINDOW, buckets, MASKED_BUCKET).astype(np.int32)


def _mix_kernel(n_seq_blocks,
                relb_ref, sink_ref, bk_ref,
                q_ref, kv_ref, yc_ref, gt_ref, xn_ref,
                wao_ref, wco_ref, wout_ref, ln1g_ref, ln1b_ref,
                h_ref,
                bias_s, o_s):
    b_id = pl.program_id(0)
    i_id = pl.program_id(1)
    ts = q_ref.shape[0]
    blocks_per_tile = ts // BLOCK
    dot = functools.partial(jnp.dot, preferred_element_type=jnp.float32)

    @pl.when((b_id == 0) & (i_id == 0))
    def _init():
        bk = bk_ref[...]
        col = lax.broadcasted_iota(jnp.int32, (BLOCK, BAND), 1)
        for h in range(N_HEADS):
            def body(b, acc):
                return jnp.where(bk == b, relb_ref[b, h], acc)
            mid = LOG2E * lax.fori_loop(0, NUM_BUCKETS, body,
                                        jnp.full((BLOCK, BAND), NEG_INF, jnp.float32))
            bias_s[0, h] = jnp.where(col < BLOCK, NEG_INF, mid)
            bias_s[1, h] = mid
            bias_s[2, h] = jnp.where(col >= 2 * BLOCK, NEG_INF, mid)

    lane = lax.broadcasted_iota(jnp.int32, (BLOCK, LANES), 1)
    low_half = lane < HEAD_DIM
    for j in range(blocks_per_tile):
        n = i_id * blocks_per_tile + j
        variant = jnp.where(n == 0, 0, jnp.where(n == n_seq_blocks - 1, 2, 1))
        prev_r = pl.multiple_of(jnp.maximum(n - 1, 0) * BLOCK, BLOCK)
        own_r = pl.multiple_of(n * BLOCK, BLOCK)
        next_r = pl.multiple_of(jnp.minimum(n + 1, n_seq_blocks - 1) * BLOCK, BLOCK)
        kvb = jnp.concatenate([kv_ref[pl.ds(prev_r, BLOCK), :],
                               kv_ref[pl.ds(own_r, BLOCK), :],
                               kv_ref[pl.ds(next_r, BLOCK), :]], axis=0)
        kk = kvb[:, :LANES]
        vv = kvb[:, LANES:]
        kk_sw = pltpu.roll(kk, HEAD_DIM, axis=1)
        vv_sw = pltpu.roll(vv, HEAD_DIM, axis=1)

        q = q_ref[j * BLOCK:(j + 1) * BLOCK, :]
        zero = jnp.zeros((BLOCK, LANES), q.dtype)
        q_lo, q_hi = [], []
        for p in range(N_HEADS // 2):
            q2 = q[:, p * LANES:(p + 1) * LANES]
            q_lo.append(jnp.where(low_half, q2, zero))
            q_hi.append(jnp.where(low_half, zero, q2))
        heads_a = (0, 2, 5, 7)
        heads_b = (1, 3, 4, 6)
        lhs_a = jnp.concatenate([q_lo[0], q_lo[1], q_hi[2], q_hi[3]], axis=0)
        lhs_b = jnp.concatenate([q_hi[0], q_hi[1], q_lo[2], q_lo[3]], axis=0)
        nt = (((1,), (1,)), ((), ()))
        s_a = lax.dot_general(lhs_a, kk, nt, preferred_element_type=jnp.float32)
        s_b = lax.dot_general(lhs_b, kk_sw, nt, preferred_element_type=jnp.float32)

        def softmax_rows(s_all, heads):
            ps, inv_ls = [], []
            for r, h in enumerate(heads):
                s = s_all[r * BLOCK:(r + 1) * BLOCK, :] + bias_s[variant, h]
                sink = jnp.full((BLOCK, 1), sink_ref[h], jnp.float32) * LOG2E
                m = jnp.maximum(jnp.max(s, axis=-1, keepdims=True), sink)
                p = jnp.exp2(s - m)
                l = jnp.sum(p, axis=-1, keepdims=True) + jnp.exp2(sink - m)
                ps.append(p.astype(jnp.bfloat16))
                inv_ls.append(1.0 / l)
            return jnp.concatenate(ps, axis=0), inv_ls

        p_a, inv_a = softmax_rows(s_a, heads_a)
        p_b, inv_b = softmax_rows(s_b, heads_b)
        r_a = dot(p_a, vv)
        r_b = dot(p_b, vv_sw)
        o_head = {}
        for r, h in enumerate(heads_a):
            o_head[h] = r_a[r * BLOCK:(r + 1) * BLOCK, :] * inv_a[r]
        for r, h in enumerate(heads_b):
            o_head[h] = r_b[r * BLOCK:(r + 1) * BLOCK, :] * inv_b[r]
        o_pairs = [jnp.where(low_half, o_head[2 * p], o_head[2 * p + 1])
                   for p in range(N_HEADS // 2)]
        o_s[j * BLOCK:(j + 1) * BLOCK, :] = jnp.concatenate(o_pairs, axis=1).astype(o_s.dtype)

    y_a = dot(o_s[...], wao_ref[...])
    y_c = dot(yc_ref[...], wco_ref[...])
    merged = (gt_ref[:, :D_MODEL].astype(jnp.float32) * y_a
              + gt_ref[:, D_MODEL:].astype(jnp.float32) * y_c)
    mix = dot(merged.astype(jnp.bfloat16), wout_ref[...])
    h_ref[...] = _layer_norm(ALPHA * xn_ref[...] + mix, ln1g_ref[...], ln1b_ref[...])


def _mix(rel_bias, sink, qkv, yc, gt, xn, w_ao, w_co, w_out, ln1_g, ln1_b):
    batch, seq, _ = xn.shape
    ts = TS_MIX
    assert seq % ts == 0 and ts % BLOCK == 0 and seq // BLOCK >= 2
    n_seq_blocks = seq // BLOCK
    buckets = jnp.asarray(_band_buckets())
    tile = lambda b, i: (b, i, 0)
    return pl.pallas_call(
        functools.partial(_mix_kernel, n_seq_blocks),
        grid=(batch, seq // ts),
        in_specs=[
            _smem_spec(),
            _smem_spec(),
            _const_spec((BLOCK, BAND)),
            pl.BlockSpec((None, ts, Q_COLS), tile),
            pl.BlockSpec((None, seq, 2 * KV_COLS), lambda b, i: (b, 0, Q_COLS // (2 * KV_COLS))),
            pl.BlockSpec((None, ts, CONV_CH), tile),
            pl.BlockSpec((None, ts, GATE_COLS), tile),
            pl.BlockSpec((None, ts, D_MODEL), tile),
            _const_spec((Q_COLS, D_MODEL)),
            _const_spec((CONV_CH, D_MODEL)),
            _const_spec((D_MODEL, D_MODEL)),
            _const_spec((1, D_MODEL)),
            _const_spec((1, D_MODEL)),
        ],
        out_specs=pl.BlockSpec((None, ts, D_MODEL), tile),
        out_shape=jax.ShapeDtypeStruct((batch, seq, D_MODEL), jnp.float32),
        scratch_shapes=[
            pltpu.VMEM((3, N_HEADS, BLOCK, BAND), jnp.float32),
            pltpu.VMEM((ts, Q_COLS), jnp.bfloat16),
        ],
        compiler_params=pltpu.CompilerParams(
            dimension_semantics=(_ARB, _ARB), vmem_limit_bytes=VMEM_LIMIT_BYTES),
        name="token_mix",
    )(rel_bias, sink, buckets, qkv, qkv, yc, gt, xn, w_ao, w_co, w_out, ln1_g, ln1_b)


def _ffn_kernel(h_ref, wgu_ref, wd_ref, g_ref, b_ref, o_ref):
    dot = functools.partial(jnp.dot, preferred_element_type=jnp.float32)
    h = h_ref[...]
    hb = h.astype(jnp.bfloat16)
    gu = dot(hb, wgu_ref[...])
    hid = []
    for c0 in range(0, gu.shape[1], 2 * MXU_COLS):
        gate = gu[:, c0:c0 + MXU_COLS]
        up = gu[:, c0 + MXU_COLS:c0 + 2 * MXU_COLS]
        hid.append((gate * jax.nn.sigmoid(gate) * up).astype(jnp.bfloat16))
    hid = jnp.concatenate(hid, axis=1)
    o_ref[...] = _layer_norm(ALPHA * h + dot(hid, wd_ref[...]), g_ref[...], b_ref[...])


def _ffn(h2d, w_gu, w_down, ln_g, ln_b):
    n_tok = h2d.shape[0]
    assert n_tok % TM_FFN == 0
    row = lambda i: (i, 0)
    return pl.pallas_call(
        _ffn_kernel,
        grid=(n_tok // TM_FFN,),
        in_specs=[
            pl.BlockSpec((TM_FFN, D_MODEL), row),
            _const_spec(w_gu.shape),
            _const_spec(w_down.shape),
            _const_spec((1, D_MODEL)),
            _const_spec((1, D_MODEL)),
        ],
        out_specs=pl.BlockSpec((TM_FFN, D_MODEL), row),
        out_shape=jax.ShapeDtypeStruct((n_tok, D_MODEL), jnp.float32),
        compiler_params=pltpu.CompilerParams(
            dimension_semantics=(_ARB,), vmem_limit_bytes=VMEM_LIMIT_BYTES),
        name="swiglu_ffn",
    )(h2d, w_gu, w_down, ln_g, ln_b)


def kernel(x, ln_in_g, ln_in_b, rel_bias, w_in, b_gate, conv_w, conv_b, conv_ln_g, conv_ln_b,
           w_conv_out, w_attn_out, sink, w_out, ln1_g, ln1_b, w_gate, w_up, w_down, ln2_g, ln2_b):
    batch, seq, d_model = x.shape
    assert d_model == D_MODEL and w_in.shape[0] == DEPTH
    bf16 = jnp.bfloat16
    row = lambda v: v.reshape(1, -1)
    n_tok = batch * seq

    (xn, qkv, gt, yc), (w_ao, w_co, w_o, w_gu, w_d) = _inproj(
        x, row(ln_in_g), row(ln_in_b), w_in[0].astype(bf16), row(b_gate[0]),
        conv_w[0], row(conv_b[0]), row(conv_ln_g[0]), row(conv_ln_b[0]),
        (w_attn_out[0], w_conv_out[0], w_out[0], w_gate[0], w_up[0], w_down[0]))
    h = _mix(rel_bias, sink[0], qkv, yc, gt, xn, w_ao, w_co, w_o, row(ln1_g[0]), row(ln1_b[0]))
    out = _ffn(h.reshape(n_tok, D_MODEL), w_gu, w_d, row(ln2_g[0]), row(ln2_b[0]))
    return out.reshape(batch, seq, D_MODEL)
```

```python
import functools

import numpy as np
import jax
import jax.numpy as jnp
from jax import lax
from jax.experimental import pallas as pl
from jax.experimental.pallas import tpu as pltpu

D_MODEL = 1024
HEAD_DIM = 64
N_HEADS = 8
N_KV_HEADS = 2
WINDOW = 128
BLOCK = 128
NUM_BUCKETS = 32
MAX_DISTANCE = 128
CONV_CH = 512
CONV_WIDTH = 31
CONV_HALF = (CONV_WIDTH - 1) // 2
Q_COLS = N_HEADS * HEAD_DIM
KV_COLS = N_KV_HEADS * HEAD_DIM
CONV_COLS = 2 * CONV_CH
GATE_COLS = 2 * D_MODEL
QKV_COLS = Q_COLS + 2 * KV_COLS
DEPTH = 1
ALPHA = (2.0 * DEPTH) ** 0.25
LN_EPS = 1e-5
LOG2E = 1.4426950408889634
Q_SCALE = HEAD_DIM ** -0.5 * LOG2E
NEG_INF = -1e30
MASKED_BUCKET = NUM_BUCKETS

HEAD_GROUPS = (((0, 2, 5, 7), False), ((1, 3, 4, 6), True))
BAND = 3 * BLOCK
HALO = 16
CONV_ROWS = 32
CONV_GROUP = 4
SUBLANES = 8
BF16_SUBLANES = 16
LANES = 128
MXU_COLS = 256
CONV_SLABS = CONV_CH // LANES

VMEM_LIMIT_BYTES = 56 * 1024 * 1024

TS_IN = 512
TS_MIX = 512
TM_FFN = 1024
FFN_SUBTILE = 512

_ARB = "arbitrary"


def _layer_norm(x, g, b):
    mu = jnp.mean(x, axis=-1, keepdims=True)
    xc = x - mu
    var = jnp.mean(xc * xc, axis=-1, keepdims=True)
    return xc * lax.rsqrt(var + LN_EPS) * g + b


def _const_spec(shape):
    return pl.BlockSpec(shape, lambda *_: (0,) * len(shape), pipeline_mode=pl.Buffered(1))


def _smem_spec():
    return pl.BlockSpec(memory_space=pltpu.SMEM)


def _halo_specs(ts, seq, cols):
    per_tile = ts // HALO
    last = seq // HALO - 1
    prev = pl.BlockSpec((None, HALO, cols), lambda b, i: (b, jnp.maximum(i * per_tile - 1, 0), 0))
    nxt = pl.BlockSpec((None, HALO, cols), lambda b, i: (b, jnp.minimum((i + 1) * per_tile, last), 0))
    return prev, nxt


def _inproj_kernel(x_ref, xp_ref, xnx_ref, g_ref, b_ref, w_ref, bgate_ref,
                   convw_ref, convb_ref, clng_ref, clnb_ref,
                   wao_ref, wco_ref, wout_ref, wgate_ref, wup_ref, wdown_ref,
                   xn_ref, qkv_ref, gt_ref, yc_ref,
                   wao_o, wco_o, wout_o, wgu_o, wdown_o,
                   xb_s, wb_s, glu_s, ge_s, go_s, dw_s):
    b_id = pl.program_id(0)
    i_id = pl.program_id(1)
    n_tiles = pl.num_programs(1)
    ts = x_ref.shape[0]
    dot = functools.partial(jnp.dot, preferred_element_type=jnp.float32)
    o1 = QKV_COLS
    o2 = o1 + CONV_COLS
    o3 = o2 + D_MODEL

    @pl.when((b_id == 0) & (i_id == 0))
    def _init():
        for k in range(CONV_WIDTH):
            wb_s[k] = jnp.broadcast_to(convw_ref[k:k + 1, :].astype(wb_s.dtype), (BF16_SUBLANES, CONV_CH))
        glu_s[:, ts + 2 * HALO:, :] = jnp.zeros((CONV_SLABS, SUBLANES, LANES), jnp.float32)

    @pl.when(i_id == 0)
    def _cast_weights():
        for src, dst in ((wao_ref, wao_o), (wco_ref, wco_o), (wout_ref, wout_o), (wdown_ref, wdown_o)):
            dst[...] = src[...].astype(dst.dtype)
        for j in range(wgate_ref.shape[1] // MXU_COLS):
            src_cols = slice(j * MXU_COLS, (j + 1) * MXU_COLS)
            wgu_o[:, 2 * j * MXU_COLS:(2 * j + 1) * MXU_COLS] = wgate_ref[:, src_cols].astype(wgu_o.dtype)
            wgu_o[:, (2 * j + 1) * MXU_COLS:(2 * j + 2) * MXU_COLS] = wup_ref[:, src_cols].astype(wgu_o.dtype)

    g = g_ref[...]
    b = b_ref[...]
    xn = _layer_norm(x_ref[...], g, b)
    xn_ref[...] = xn
    xb_s[0:HALO, :] = _layer_norm(xp_ref[...], g, b).astype(xb_s.dtype)
    xb_s[HALO:HALO + ts, :] = xn.astype(xb_s.dtype)
    xb_s[HALO + ts:2 * HALO + ts, :] = _layer_norm(xnx_ref[...], g, b).astype(xb_s.dtype)

    u = dot(xb_s[...], w_ref[:, o1:o2])
    glu = u[:, :CONV_CH] * jax.nn.sigmoid(u[:, CONV_CH:])
    row = lax.broadcasted_iota(jnp.int32, (ts + 2 * HALO, 1), 0)
    pad = ((row < HALO) & (i_id == 0)) | ((row >= HALO + ts) & (i_id == n_tiles - 1))
    glu = jnp.where(pad, 0.0, glu)
    rows_ext = ts + 2 * HALO
    for c in range(CONV_SLABS):
        glu_s[c, 0:rows_ext, :] = glu[:, c * LANES:(c + 1) * LANES]
    for c in range(CONV_SLABS):
        for parity, dst in enumerate((ge_s, go_s)):
            win = glu_s[c, parity:parity + rows_ext, :].astype(jnp.bfloat16)
            dst[c] = pltpu.bitcast(win, jnp.uint32)

    xb = xb_s[HALO:HALO + ts, :]
    qkv_ref[:, :Q_COLS] = (dot(xb, w_ref[:, :Q_COLS]) * Q_SCALE).astype(qkv_ref.dtype)
    qkv_ref[:, Q_COLS:] = dot(xb, w_ref[:, Q_COLS:o1]).astype(qkv_ref.dtype)
    bgate = bgate_ref[...]
    gt_ref[:, :D_MODEL] = jax.nn.sigmoid(
        dot(xb, w_ref[:, o2:o3]) + bgate[:, :D_MODEL]).astype(gt_ref.dtype)
    gt_ref[:, D_MODEL:] = jax.nn.sigmoid(
        dot(xb, w_ref[:, o3:]) + bgate[:, D_MODEL:]).astype(gt_ref.dtype)

    first_tap = HALO - CONV_HALF

    def conv_chunk(chunk, carry):
        r0 = pl.multiple_of(chunk * CONV_ROWS, CONV_ROWS)
        w0 = pl.multiple_of(chunk * (CONV_ROWS // 2), CONV_ROWS // 2)
        n_groups = CONV_ROWS // BF16_SUBLANES
        for c in range(CONV_SLABS):
            lanes = slice(c * LANES, (c + 1) * LANES)
            accs = [jnp.zeros((BF16_SUBLANES, LANES), jnp.float32) for _ in range(n_groups)]
            for k0 in range(0, CONV_WIDTH, CONV_GROUP):
                parts = [None] * n_groups
                for k in range(k0, min(k0 + CONV_GROUP, CONV_WIDTH)):
                    w = wb_s[k, :, lanes]
                    for g in range(n_groups):
                        off = g * BF16_SUBLANES + first_tap + k
                        src = go_s if off % 2 else ge_s
                        words = src[c, pl.ds(w0 + off // 2, SUBLANES), :]
                        prod = pltpu.bitcast(words, jnp.bfloat16) * w
                        parts[g] = prod if parts[g] is None else parts[g] + prod
                for g in range(n_groups):
                    accs[g] = accs[g] + parts[g].astype(jnp.float32)
            for g in range(n_groups):
                dw_s[pl.ds(r0 + g * BF16_SUBLANES, BF16_SUBLANES), lanes] = accs[g]
        return carry

    lax.fori_loop(0, ts // CONV_ROWS, conv_chunk, 0)
    y = _layer_norm(dw_s[...] + convb_ref[...], clng_ref[...], clnb_ref[...])
    yc_ref[...] = (y * jax.nn.sigmoid(y)).astype(yc_ref.dtype)


def _inproj(x, ln_g, ln_b, w_in_bf16, b_gate, conv_w, conv_b, cln_g, cln_b, later_weights):
    batch, seq, _ = x.shape
    ts = TS_IN
    assert seq % ts == 0 and ts % CONV_ROWS == 0 and ts % HALO == 0 and HALO >= CONV_HALF
    tile = lambda b, i: (b, i, 0)
    prev_spec, next_spec = _halo_specs(ts, seq, D_MODEL)
    bf16 = jnp.bfloat16
    w_ao, w_co, w_out, w_gate, w_up, w_down = later_weights
    assert w_gate.shape == w_up.shape and w_gate.shape[1] % MXU_COLS == 0
    cast_out_shapes = [w_ao.shape, w_co.shape, w_out.shape,
                       (w_gate.shape[0], 2 * w_gate.shape[1]), w_down.shape]

    def slice_spec(shape):
        rows, cols = shape
        assert rows % (batch * BF16_SUBLANES) == 0
        return pl.BlockSpec((rows // batch, cols), lambda b, i: (b, 0))

    cast_in_specs = [slice_spec(w.shape) for w in later_weights]
    cast_out_specs = [slice_spec(s) for s in cast_out_shapes]
    outs = pl.pallas_call(
        _inproj_kernel,
        grid=(batch, seq // ts),
        in_specs=[
            pl.BlockSpec((None, ts, D_MODEL), tile),
            prev_spec,
            next_spec,
            _const_spec((1, D_MODEL)),
            _const_spec((1, D_MODEL)),
            _const_spec(w_in_bf16.shape),
            _const_spec((1, GATE_COLS)),
            _const_spec((CONV_WIDTH, CONV_CH)),
            _const_spec((1, CONV_CH)),
            _const_spec((1, CONV_CH)),
            _const_spec((1, CONV_CH)),
        ] + cast_in_specs,
        out_specs=[
            pl.BlockSpec((None, ts, D_MODEL), tile),
            pl.BlockSpec((None, ts, QKV_COLS), tile),
            pl.BlockSpec((None, ts, GATE_COLS), tile),
            pl.BlockSpec((None, ts, CONV_CH), tile),
        ] + cast_out_specs,
        out_shape=[
            jax.ShapeDtypeStruct((batch, seq, D_MODEL), jnp.float32),
            jax.ShapeDtypeStruct((batch, seq, QKV_COLS), bf16),
            jax.ShapeDtypeStruct((batch, seq, GATE_COLS), bf16),
            jax.ShapeDtypeStruct((batch, seq, CONV_CH), bf16),
        ] + [jax.ShapeDtypeStruct(s, bf16) for s in cast_out_shapes],
        scratch_shapes=[
            pltpu.VMEM((ts + 2 * HALO, D_MODEL), bf16),
            pltpu.VMEM((CONV_WIDTH, BF16_SUBLANES, CONV_CH), bf16),
            pltpu.VMEM((CONV_SLABS, ts + 2 * HALO + SUBLANES, LANES), jnp.float32),
            pltpu.VMEM((CONV_SLABS, ts // 2 + HALO, LANES), jnp.uint32),
            pltpu.VMEM((CONV_SLABS, ts // 2 + HALO, LANES), jnp.uint32),
            pltpu.VMEM((ts, CONV_CH), jnp.float32),
        ],
        compiler_params=pltpu.CompilerParams(
            dimension_semantics=(_ARB, _ARB), vmem_limit_bytes=VMEM_LIMIT_BYTES),
        name="ln_inproj_conv",
    )(x, x, x, ln_g, ln_b, w_in_bf16, b_gate, conv_w, conv_b, cln_g, cln_b, *later_weights)
    return outs[:4], outs[4:]


def _band_buckets():
    nb = NUM_BUCKETS // 2
    qi = np.arange(BLOCK)[:, None]
    kj = np.arange(BAND)[None, :]
    rel = kj - BLOCK - qi
    ret = (rel > 0).astype(np.int32) * nb
    n = np.abs(rel)
    max_exact = nb // 2
    large = max_exact + (np.log(np.maximum(n, 1) / max_exact)
                         / np.log(MAX_DISTANCE / max_exact) * (nb - max_exact)).astype(np.int32)
    large = np.minimum(large, nb - 1)
    buckets = (ret + np.where(n < max_exact, n, large)).astype(np.int32)
    return np.where(n <= WINDOW, buckets, MASKED_BUCKET).astype(np.int32)


def _mix_kernel(n_seq_blocks,
                relb_ref, sink_ref, bk_ref,
                q_ref, kv_ref, yc_ref, gt_ref, xn_ref,
                wao_ref, wco_ref, wout_ref, ln1g_ref, ln1b_ref,
                h_ref,
                bias_s, o_s):
    b_id = pl.program_id(0)
    i_id = pl.program_id(1)
    ts = q_ref.shape[0]
    blocks_per_tile = ts // BLOCK
    dot = functools.partial(jnp.dot, preferred_element_type=jnp.float32)

    @pl.when((b_id == 0) & (i_id == 0))
    def _init():
        bk = bk_ref[...]
        col = lax.broadcasted_iota(jnp.int32, (BLOCK, BAND), 1)
        for h in range(N_HEADS):
            def body(b, acc):
                return jnp.where(bk == b, relb_ref[b, h], acc)
            mid = LOG2E * lax.fori_loop(0, NUM_BUCKETS, body,
                                        jnp.full((BLOCK, BAND), NEG_INF, jnp.float32))
            bias_s[0, h] = jnp.where(col < BLOCK, NEG_INF, mid)
            bias_s[1, h] = mid
            bias_s[2, h] = jnp.where(col >= 2 * BLOCK, NEG_INF, mid)

    lane = lax.broadcasted_iota(jnp.int32, (BLOCK, LANES), 1)
    low_half = lane < HEAD_DIM
    for j in range(blocks_per_tile):
        n = i_id * blocks_per_tile + j
        variant = jnp.where(n == 0, 0, jnp.where(n == n_seq_blocks - 1, 2, 1))
        prev_r = pl.multiple_of(jnp.maximum(n - 1, 0) * BLOCK, BLOCK)
        own_r = pl.multiple_of(n * BLOCK, BLOCK)
        next_r = pl.multiple_of(jnp.minimum(n + 1, n_seq_blocks - 1) * BLOCK, BLOCK)
        kvb = jnp.concatenate([kv_ref[pl.ds(prev_r, BLOCK), :],
                               kv_ref[pl.ds(own_r, BLOCK), :],
                               kv_ref[pl.ds(next_r, BLOCK), :]], axis=0)
        kk = kvb[:, :LANES]
        vv = kvb[:, LANES:]
        kk_sw = pltpu.roll(kk, HEAD_DIM, axis=1)
        vv_sw = pltpu.roll(vv, HEAD_DIM, axis=1)

        q = q_ref[j * BLOCK:(j + 1) * BLOCK, :]
        zero = jnp.zeros((BLOCK, LANES), q.dtype)
        q_lo, q_hi = [], []
        for p in range(N_HEADS // 2):
            q2 = q[:, p * LANES:(p + 1) * LANES]
            q_lo.append(jnp.where(low_half, q2, zero))
            q_hi.append(jnp.where(low_half, zero, q2))
        q_head = {2 * p: q_lo[p] for p in range(N_HEADS // 2)}
        q_head.update({2 * p + 1: q_hi[p] for p in range(N_HEADS // 2)})
        nt = (((1,), (1,)), ((), ()))
        o_head = {}
        for heads, swapped in HEAD_GROUPS:
            k_op = kk_sw if swapped else kk
            v_op = vv_sw if swapped else vv
            lhs = jnp.concatenate([q_head[h] for h in heads], axis=0)
            s_all = lax.dot_general(lhs, k_op, nt, preferred_element_type=jnp.float32)
            ps, inv_ls = [], []
            for r, h in enumerate(heads):
                s = s_all[r * BLOCK:(r + 1) * BLOCK, :] + bias_s[variant, h]
                sink = jnp.full((BLOCK, 1), sink_ref[h], jnp.float32) * LOG2E
                m = jnp.maximum(jnp.max(s, axis=-1, keepdims=True), sink)
                p = jnp.exp2(s - m)
                l = jnp.sum(p, axis=-1, keepdims=True) + jnp.exp2(sink - m)
                ps.append(p.astype(jnp.bfloat16))
                inv_ls.append(1.0 / l)
            r_all = dot(jnp.concatenate(ps, axis=0), v_op)
            for r, h in enumerate(heads):
                o_head[h] = r_all[r * BLOCK:(r + 1) * BLOCK, :] * inv_ls[r]
        o_pairs = [jnp.where(low_half, o_head[2 * p], o_head[2 * p + 1])
                   for p in range(N_HEADS // 2)]
        o_s[j * BLOCK:(j + 1) * BLOCK, :] = jnp.concatenate(o_pairs, axis=1).astype(o_s.dtype)

    y_a = dot(o_s[...], wao_ref[...])
    y_c = dot(yc_ref[...], wco_ref[...])
    merged = (gt_ref[:, :D_MODEL].astype(jnp.float32) * y_a
              + gt_ref[:, D_MODEL:].astype(jnp.float32) * y_c)
    mix = dot(merged.astype(jnp.bfloat16), wout_ref[...])
    h_ref[...] = _layer_norm(ALPHA * xn_ref[...] + mix, ln1g_ref[...], ln1b_ref[...])


def _mix(rel_bias, sink, qkv, yc, gt, xn, w_ao, w_co, w_out, ln1_g, ln1_b):
    batch, seq, _ = xn.shape
    ts = TS_MIX
    assert seq % ts == 0 and ts % BLOCK == 0 and seq // BLOCK >= 2
    n_seq_blocks = seq // BLOCK
    buckets = jnp.asarray(_band_buckets())
    tile = lambda b, i: (b, i, 0)
    return pl.pallas_call(
        functools.partial(_mix_kernel, n_seq_blocks),
        grid=(batch, seq // ts),
        in_specs=[
            _smem_spec(),
            _smem_spec(),
            _const_spec((BLOCK, BAND)),
            pl.BlockSpec((None, ts, Q_COLS), tile),
            pl.BlockSpec((None, seq, 2 * KV_COLS), lambda b, i: (b, 0, Q_COLS // (2 * KV_COLS))),
            pl.BlockSpec((None, ts, CONV_CH), tile),
            pl.BlockSpec((None, ts, GATE_COLS), tile),
            pl.BlockSpec((None, ts, D_MODEL), tile),
            _const_spec((Q_COLS, D_MODEL)),
            _const_spec((CONV_CH, D_MODEL)),
            _const_spec((D_MODEL, D_MODEL)),
            _const_spec((1, D_MODEL)),
            _const_spec((1, D_MODEL)),
        ],
        out_specs=pl.BlockSpec((None, ts, D_MODEL), tile),
        out_shape=jax.ShapeDtypeStruct((batch, seq, D_MODEL), jnp.float32),
        scratch_shapes=[
            pltpu.VMEM((3, N_HEADS, BLOCK, BAND), jnp.float32),
            pltpu.VMEM((ts, Q_COLS), jnp.bfloat16),
        ],
        compiler_params=pltpu.CompilerParams(
            dimension_semantics=(_ARB, _ARB), vmem_limit_bytes=VMEM_LIMIT_BYTES),
        name="token_mix",
    )(rel_bias, sink, buckets, qkv, qkv, yc, gt, xn, w_ao, w_co, w_out, ln1_g, ln1_b)


def _ffn_kernel(h_ref, wgu_ref, wd_ref, g_ref, b_ref, o_ref):
    dot = functools.partial(jnp.dot, preferred_element_type=jnp.float32)
    for r0 in range(0, h_ref.shape[0], FFN_SUBTILE):
        rows = slice(r0, r0 + FFN_SUBTILE)
        h = h_ref[rows, :]
        hb = h.astype(jnp.bfloat16)
        gu = dot(hb, wgu_ref[...])
        hid = []
        for c0 in range(0, gu.shape[1], 2 * MXU_COLS):
            gate = gu[:, c0:c0 + MXU_COLS]
            up = gu[:, c0 + MXU_COLS:c0 + 2 * MXU_COLS]
            hid.append((gate * jax.nn.sigmoid(gate) * up).astype(jnp.bfloat16))
        hid = jnp.concatenate(hid, axis=1)
        o_ref[rows, :] = _layer_norm(ALPHA * h + dot(hid, wd_ref[...]), g_ref[...], b_ref[...])


def _ffn(h2d, w_gu, w_down, ln_g, ln_b):
    n_tok = h2d.shape[0]
    assert n_tok % TM_FFN == 0
    row = lambda i: (i, 0)
    return pl.pallas_call(
        _ffn_kernel,
        grid=(n_tok // TM_FFN,),
        in_specs=[
            pl.BlockSpec((TM_FFN, D_MODEL), row),
            _const_spec(w_gu.shape),
            _const_spec(w_down.shape),
            _const_spec((1, D_MODEL)),
            _const_spec((1, D_MODEL)),
        ],
        out_specs=pl.BlockSpec((TM_FFN, D_MODEL), row),
        out_shape=jax.ShapeDtypeStruct((n_tok, D_MODEL), jnp.float32),
        compiler_params=pltpu.CompilerParams(
            dimension_semantics=(_ARB,), vmem_limit_bytes=VMEM_LIMIT_BYTES),
        name="swiglu_ffn",
    )(h2d, w_gu, w_down, ln_g, ln_b)


def kernel(x, ln_in_g, ln_in_b, rel_bias, w_in, b_gate, conv_w, conv_b, conv_ln_g, conv_ln_b,
           w_conv_out, w_attn_out, sink, w_out, ln1_g, ln1_b, w_gate, w_up, w_down, ln2_g, ln2_b):
    batch, seq, d_model = x.shape
    assert d_model == D_MODEL and w_in.shape[0] == DEPTH
    bf16 = jnp.bfloat16
    row = lambda v: v.reshape(1, -1)
    n_tok = batch * seq

    (xn, qkv, gt, yc), (w_ao, w_co, w_o, w_gu, w_d) = _inproj(
        x, row(ln_in_g), row(ln_in_b), w_in[0].astype(bf16), row(b_gate[0]),
        conv_w[0], row(conv_b[0]), row(conv_ln_g[0]), row(conv_ln_b[0]),
        (w_attn_out[0], w_conv_out[0], w_out[0], w_gate[0], w_up[0], w_down[0]))
    h = _mix(rel_bias, sink[0], qkv, yc, gt, xn, w_ao, w_co, w_o, row(ln1_g[0]), row(ln1_b[0]))
    out = _ffn(h.reshape(n_tok, D_MODEL), w_gu, w_d, row(ln2_g[0]), row(ln2_b[0]))
    return out.reshape(batch, seq, D_MODEL)
```

```python
import functools

import numpy as np
import jax
import jax.numpy as jnp
from jax import lax
from jax.experimental import pallas as pl
from jax.experimental.pallas import tpu as pltpu

D_MODEL = 1024
HEAD_DIM = 64
N_HEADS = 8
N_KV_HEADS = 2
WINDOW = 128
BLOCK = 128
NUM_BUCKETS = 32
MAX_DISTANCE = 128
CONV_CH = 512
CONV_WIDTH = 31
CONV_HALF = (CONV_WIDTH - 1) // 2
Q_COLS = N_HEADS * HEAD_DIM
KV_COLS = N_KV_HEADS * HEAD_DIM
CONV_COLS = 2 * CONV_CH
GATE_COLS = 2 * D_MODEL
QKV_COLS = Q_COLS + 2 * KV_COLS
DEPTH = 1
ALPHA = (2.0 * DEPTH) ** 0.25
LN_EPS = 1e-5
LOG2E = 1.4426950408889634
Q_SCALE = HEAD_DIM ** -0.5 * LOG2E
NEG_INF = -1e30
MASKED_BUCKET = NUM_BUCKETS

HEAD_GROUPS = (((0, 2, 5, 7), False), ((1, 3, 4, 6), True))
BAND = 3 * BLOCK
HALO = 16
CONV_ROWS = 32
CONV_GROUP = 4
SUBLANES = 8
BF16_SUBLANES = 16
LANES = 128
MXU_COLS = 256
CONV_SLABS = CONV_CH // LANES

VMEM_LIMIT_BYTES = 56 * 1024 * 1024

TS_IN = 512
TS_MIX = 512
TM_FFN = 1024
FFN_SUBTILE = 512

_ARB = "arbitrary"


def _layer_norm(x, g, b):
    mu = jnp.mean(x, axis=-1, keepdims=True)
    xc = x - mu
    var = jnp.mean(xc * xc, axis=-1, keepdims=True)
    return xc * lax.rsqrt(var + LN_EPS) * g + b


def _const_spec(shape):
    return pl.BlockSpec(shape, lambda *_: (0,) * len(shape), pipeline_mode=pl.Buffered(1))


def _smem_spec():
    return pl.BlockSpec(memory_space=pltpu.SMEM)


def _halo_specs(ts, seq, cols):
    per_tile = ts // HALO
    last = seq // HALO - 1
    prev = pl.BlockSpec((None, HALO, cols), lambda b, i: (b, jnp.maximum(i * per_tile - 1, 0), 0))
    nxt = pl.BlockSpec((None, HALO, cols), lambda b, i: (b, jnp.minimum((i + 1) * per_tile, last), 0))
    return prev, nxt


def _inproj_kernel(x_ref, xp_ref, xnx_ref, g_ref, b_ref, w_ref, bgate_ref,
                   convw_ref, convb_ref, clng_ref, clnb_ref,
                   wao_ref, wco_ref, wout_ref, wgate_ref, wup_ref, wdown_ref,
                   xn_ref, qkv_ref, gt_ref, yc_ref,
                   wao_o, wco_o, wout_o, wgu_o, wdown_o,
                   xb_s, wb_s, glu_s, ge_s, go_s, dw_s):
    b_id = pl.program_id(0)
    i_id = pl.program_id(1)
    n_tiles = pl.num_programs(1)
    ts = x_ref.shape[0]
    dot = functools.partial(jnp.dot, preferred_element_type=jnp.float32)
    o1 = QKV_COLS
    o2 = o1 + CONV_COLS
    o3 = o2 + D_MODEL

    @pl.when((b_id == 0) & (i_id == 0))
    def _init():
        for k in range(CONV_WIDTH):
            wb_s[k] = jnp.broadcast_to(convw_ref[k:k + 1, :].astype(wb_s.dtype), (BF16_SUBLANES, CONV_CH))
        glu_s[:, ts + 2 * HALO:, :] = jnp.zeros((CONV_SLABS, SUBLANES, LANES), jnp.float32)

    @pl.when(i_id == 0)
    def _cast_weights():
        for src, dst in ((wao_ref, wao_o), (wco_ref, wco_o), (wout_ref, wout_o), (wdown_ref, wdown_o)):
            dst[...] = src[...].astype(dst.dtype)
        for j in range(wgate_ref.shape[1] // MXU_COLS):
            src_cols = slice(j * MXU_COLS, (j + 1) * MXU_COLS)
            wgu_o[:, 2 * j * MXU_COLS:(2 * j + 1) * MXU_COLS] = wgate_ref[:, src_cols].astype(wgu_o.dtype)
            wgu_o[:, (2 * j + 1) * MXU_COLS:(2 * j + 2) * MXU_COLS] = wup_ref[:, src_cols].astype(wgu_o.dtype)

    g = g_ref[...]
    b = b_ref[...]
    xn = _layer_norm(x_ref[...], g, b)
    xn_ref[...] = xn
    xb_s[0:HALO, :] = _layer_norm(xp_ref[...], g, b).astype(xb_s.dtype)
    xb_s[HALO:HALO + ts, :] = xn.astype(xb_s.dtype)
    xb_s[HALO + ts:2 * HALO + ts, :] = _layer_norm(xnx_ref[...], g, b).astype(xb_s.dtype)

    u = dot(xb_s[...], w_ref[:, o1:o2])
    glu = u[:, :CONV_CH] * jax.nn.sigmoid(u[:, CONV_CH:])
    row = lax.broadcasted_iota(jnp.int32, (ts + 2 * HALO, 1), 0)
    pad = ((row < HALO) & (i_id == 0)) | ((row >= HALO + ts) & (i_id == n_tiles - 1))
    glu = jnp.where(pad, 0.0, glu)
    rows_ext = ts + 2 * HALO
    for c in range(CONV_SLABS):
        glu_s[c, 0:rows_ext, :] = glu[:, c * LANES:(c + 1) * LANES]
    for c in range(CONV_SLABS):
        for parity, dst in enumerate((ge_s, go_s)):
            win = glu_s[c, parity:parity + rows_ext, :].astype(jnp.bfloat16)
            dst[c] = pltpu.bitcast(win, jnp.uint32)

    xb = xb_s[HALO:HALO + ts, :]
    qkv_ref[:, :Q_COLS] = (dot(xb, w_ref[:, :Q_COLS]) * Q_SCALE).astype(qkv_ref.dtype)
    qkv_ref[:, Q_COLS:] = dot(xb, w_ref[:, Q_COLS:o1]).astype(qkv_ref.dtype)
    bgate = bgate_ref[...]
    gt_ref[:, :D_MODEL] = jax.nn.sigmoid(
        dot(xb, w_ref[:, o2:o3]) + bgate[:, :D_MODEL]).astype(gt_ref.dtype)
    gt_ref[:, D_MODEL:] = jax.nn.sigmoid(
        dot(xb, w_ref[:, o3:]) + bgate[:, D_MODEL:]).astype(gt_ref.dtype)

    first_tap = HALO - CONV_HALF

    def conv_chunk(chunk, carry):
        r0 = pl.multiple_of(chunk * CONV_ROWS, CONV_ROWS)
        w0 = pl.multiple_of(chunk * (CONV_ROWS // 2), CONV_ROWS // 2)
        n_groups = CONV_ROWS // BF16_SUBLANES
        for c in range(CONV_SLABS):
            lanes = slice(c * LANES, (c + 1) * LANES)
            accs = [jnp.zeros((BF16_SUBLANES, LANES), jnp.float32) for _ in range(n_groups)]
            for k0 in range(0, CONV_WIDTH, CONV_GROUP):
                parts = [None] * n_groups
                for k in range(k0, min(k0 + CONV_GROUP, CONV_WIDTH)):
                    w = wb_s[k, :, lanes]
                    for g in range(n_groups):
                        off = g * BF16_SUBLANES + first_tap + k
                        src = go_s if off % 2 else ge_s
                        words = src[c, pl.ds(w0 + off // 2, SUBLANES), :]
                        prod = pltpu.bitcast(words, jnp.bfloat16) * w
                        parts[g] = prod if parts[g] is None else parts[g] + prod
                for g in range(n_groups):
                    accs[g] = accs[g] + parts[g].astype(jnp.float32)
            for g in range(n_groups):
                dw_s[pl.ds(r0 + g * BF16_SUBLANES, BF16_SUBLANES), lanes] = accs[g]
        return carry

    lax.fori_loop(0, ts // CONV_ROWS, conv_chunk, 0)
    y = _layer_norm(dw_s[...] + convb_ref[...], clng_ref[...], clnb_ref[...])
    yc_ref[...] = (y * jax.nn.sigmoid(y)).astype(yc_ref.dtype)


def _inproj(x, ln_g, ln_b, w_in_bf16, b_gate, conv_w, conv_b, cln_g, cln_b, later_weights):
    batch, seq, _ = x.shape
    ts = TS_IN
    assert seq % ts == 0 and ts % CONV_ROWS == 0 and ts % HALO == 0 and HALO >= CONV_HALF
    tile = lambda b, i: (b, i, 0)
    prev_spec, next_spec = _halo_specs(ts, seq, D_MODEL)
    bf16 = jnp.bfloat16
    w_ao, w_co, w_out, w_gate, w_up, w_down = later_weights
    assert w_gate.shape == w_up.shape and w_gate.shape[1] % MXU_COLS == 0
    cast_out_shapes = [w_ao.shape, w_co.shape, w_out.shape,
                       (w_gate.shape[0], 2 * w_gate.shape[1]), w_down.shape]

    def slice_spec(shape):
        rows, cols = shape
        assert rows % (batch * BF16_SUBLANES) == 0
        return pl.BlockSpec((rows // batch, cols), lambda b, i: (b, 0))

    cast_in_specs = [slice_spec(w.shape) for w in later_weights]
    cast_out_specs = [slice_spec(s) for s in cast_out_shapes]
    outs = pl.pallas_call(
        _inproj_kernel,
        grid=(batch, seq // ts),
        in_specs=[
            pl.BlockSpec((None, ts, D_MODEL), tile),
            prev_spec,
            next_spec,
            _const_spec((1, D_MODEL)),
            _const_spec((1, D_MODEL)),
            _const_spec(w_in_bf16.shape),
            _const_spec((1, GATE_COLS)),
            _const_spec((CONV_WIDTH, CONV_CH)),
            _const_spec((1, CONV_CH)),
            _const_spec((1, CONV_CH)),
            _const_spec((1, CONV_CH)),
        ] + cast_in_specs,
        out_specs=[
            pl.BlockSpec((None, ts, D_MODEL), tile),
            pl.BlockSpec((None, ts, QKV_COLS), tile),
            pl.BlockSpec((None, ts, GATE_COLS), tile),
            pl.BlockSpec((None, ts, CONV_CH), tile),
        ] + cast_out_specs,
        out_shape=[
            jax.ShapeDtypeStruct((batch, seq, D_MODEL), jnp.float32),
            jax.ShapeDtypeStruct((batch, seq, QKV_COLS), bf16),
            jax.ShapeDtypeStruct((batch, seq, GATE_COLS), bf16),
            jax.ShapeDtypeStruct((batch, seq, CONV_CH), bf16),
        ] + [jax.ShapeDtypeStruct(s, bf16) for s in cast_out_shapes],
        scratch_shapes=[
            pltpu.VMEM((ts + 2 * HALO, D_MODEL), bf16),
            pltpu.VMEM((CONV_WIDTH, BF16_SUBLANES, CONV_CH), bf16),
            pltpu.VMEM((CONV_SLABS, ts + 2 * HALO + SUBLANES, LANES), jnp.float32),
            pltpu.VMEM((CONV_SLABS, ts // 2 + HALO, LANES), jnp.uint32),
            pltpu.VMEM((CONV_SLABS, ts // 2 + HALO, LANES), jnp.uint32),
            pltpu.VMEM((ts, CONV_CH), jnp.float32),
        ],
        compiler_params=pltpu.CompilerParams(
            dimension_semantics=(_ARB, _ARB), vmem_limit_bytes=VMEM_LIMIT_BYTES),
        name="ln_inproj_conv",
    )(x, x, x, ln_g, ln_b, w_in_bf16, b_gate, conv_w, conv_b, cln_g, cln_b, *later_weights)
    return outs[:4], outs[4:]


def _band_buckets():
    nb = NUM_BUCKETS // 2
    qi = np.arange(BLOCK)[:, None]
    kj = np.arange(BAND)[None, :]
    rel = kj - BLOCK - qi
    ret = (rel > 0).astype(np.int32) * nb
    n = np.abs(rel)
    max_exact = nb // 2
    large = max_exact + (np.log(np.maximum(n, 1) / max_exact)
                         / np.log(MAX_DISTANCE / max_exact) * (nb - max_exact)).astype(np.int32)
    large = np.minimum(large, nb - 1)
    buckets = (ret + np.where(n < max_exact, n, large)).astype(np.int32)
    return np.where(n <= WINDOW, buckets, MASKED_BUCKET).astype(np.int32)


def _mix_kernel(n_seq_blocks,
                relb_ref, sink_ref, bk_ref,
                q_ref, kv_ref, yc_ref, gt_ref, xn_ref,
                wao_ref, wco_ref, wout_ref, ln1g_ref, ln1b_ref,
                h_ref,
                bias_s, o_s):
    b_id = pl.program_id(0)
    i_id = pl.program_id(1)
    ts = q_ref.shape[0]
    blocks_per_tile = ts // BLOCK
    dot = functools.partial(jnp.dot, preferred_element_type=jnp.float32)

    @pl.when((b_id == 0) & (i_id == 0))
    def _init():
        bk = bk_ref[...]
        col = lax.broadcasted_iota(jnp.int32, (BLOCK, BAND), 1)
        for h in range(N_HEADS):
            def body(b, acc):
                return jnp.where(bk == b, relb_ref[b, h], acc)
            mid = LOG2E * lax.fori_loop(0, NUM_BUCKETS, body,
                                        jnp.full((BLOCK, BAND), NEG_INF, jnp.float32))
            bias_s[0, h] = jnp.where(col < BLOCK, NEG_INF, mid)
            bias_s[1, h] = mid
            bias_s[2, h] = jnp.where(col >= 2 * BLOCK, NEG_INF, mid)

    lane = lax.broadcasted_iota(jnp.int32, (BLOCK, LANES), 1)
    low_half = lane < HEAD_DIM
    nt = (((1,), (1,)), ((), ()))

    def block_operands(j):
        n = i_id * blocks_per_tile + j
        variant = jnp.where(n == 0, 0, jnp.where(n == n_seq_blocks - 1, 2, 1))
        prev_r = pl.multiple_of(jnp.maximum(n - 1, 0) * BLOCK, BLOCK)
        own_r = pl.multiple_of(n * BLOCK, BLOCK)
        next_r = pl.multiple_of(jnp.minimum(n + 1, n_seq_blocks - 1) * BLOCK, BLOCK)
        kvb = jnp.concatenate([kv_ref[pl.ds(prev_r, BLOCK), :],
                               kv_ref[pl.ds(own_r, BLOCK), :],
                               kv_ref[pl.ds(next_r, BLOCK), :]], axis=0)
        kk = kvb[:, :LANES]
        vv = kvb[:, LANES:]
        q = q_ref[j * BLOCK:(j + 1) * BLOCK, :]
        zero = jnp.zeros((BLOCK, LANES), q.dtype)
        q_head = {}
        for p in range(N_HEADS // 2):
            q2 = q[:, p * LANES:(p + 1) * LANES]
            q_head[2 * p] = jnp.where(low_half, q2, zero)
            q_head[2 * p + 1] = jnp.where(low_half, zero, q2)
        return dict(variant=variant, q_head=q_head,
                    k=(kk, pltpu.roll(kk, HEAD_DIM, axis=1)),
                    v=(vv, pltpu.roll(vv, HEAD_DIM, axis=1)))

    def scores(blk, heads, swapped):
        lhs = jnp.concatenate([blk["q_head"][h] for h in heads], axis=0)
        return lax.dot_general(lhs, blk["k"][swapped], nt, preferred_element_type=jnp.float32)

    def softmax_values(blk, heads, swapped, s_all, o_head):
        ps, inv_ls = [], []
        for r, h in enumerate(heads):
            s = s_all[r * BLOCK:(r + 1) * BLOCK, :] + bias_s[blk["variant"], h]
            sink = jnp.full((BLOCK, 1), sink_ref[h], jnp.float32) * LOG2E
            m = jnp.maximum(jnp.max(s, axis=-1, keepdims=True), sink)
            p = jnp.exp2(s - m)
            l = jnp.sum(p, axis=-1, keepdims=True) + jnp.exp2(sink - m)
            ps.append(p.astype(jnp.bfloat16))
            inv_ls.append(1.0 / l)
        r_all = dot(jnp.concatenate(ps, axis=0), blk["v"][swapped])
        for r, h in enumerate(heads):
            o_head[h] = r_all[r * BLOCK:(r + 1) * BLOCK, :] * inv_ls[r]

    stages = [(j, heads, int(swapped)) for j in range(blocks_per_tile) for heads, swapped in HEAD_GROUPS]
    yc_chunk = D_MODEL * 2 // len(stages)
    assert yc_chunk % MXU_COLS == 0 and len(stages) % 2 == 0
    yc_parts = []
    blocks = {}
    o_heads = {}
    pending = None
    for c in range(len(stages) + 1):
        nxt = None
        if c < len(stages):
            j, heads, swapped = stages[c]
            if j not in blocks:
                blocks[j] = block_operands(j)
                o_heads[j] = {}
            nxt = (j, heads, swapped, scores(blocks[j], heads, swapped))
        if c % 2 == 1 and len(yc_parts) * yc_chunk < D_MODEL:
            c0 = len(yc_parts) * yc_chunk
            yc_parts.append(dot(yc_ref[...], wco_ref[:, c0:c0 + yc_chunk]))
        if pending is not None:
            j, heads, swapped, s_all = pending
            softmax_values(blocks[j], heads, swapped, s_all, o_heads[j])
            if len(o_heads[j]) == N_HEADS:
                o_pairs = [jnp.where(low_half, o_heads[j][2 * p], o_heads[j][2 * p + 1])
                           for p in range(N_HEADS // 2)]
                o_s[j * BLOCK:(j + 1) * BLOCK, :] = jnp.concatenate(o_pairs, axis=1).astype(o_s.dtype)
        pending = nxt

    y_a = dot(o_s[...], wao_ref[...])
    y_c = jnp.concatenate(yc_parts, axis=1)
    merged = (gt_ref[:, :D_MODEL].astype(jnp.float32) * y_a
              + gt_ref[:, D_MODEL:].astype(jnp.float32) * y_c)
    mix = dot(merged.astype(jnp.bfloat16), wout_ref[...])
    h_ref[...] = _layer_norm(ALPHA * xn_ref[...] + mix, ln1g_ref[...], ln1b_ref[...])


def _mix(rel_bias, sink, qkv, yc, gt, xn, w_ao, w_co, w_out, ln1_g, ln1_b):
    batch, seq, _ = xn.shape
    ts = TS_MIX
    assert seq % ts == 0 and ts % BLOCK == 0 and seq // BLOCK >= 2
    n_seq_blocks = seq // BLOCK
    buckets = jnp.asarray(_band_buckets())
    tile = lambda b, i: (b, i, 0)
    return pl.pallas_call(
        functools.partial(_mix_kernel, n_seq_blocks),
        grid=(batch, seq // ts),
        in_specs=[
            _smem_spec(),
            _smem_spec(),
            _const_spec((BLOCK, BAND)),
            pl.BlockSpec((None, ts, Q_COLS), tile),
            pl.BlockSpec((None, seq, 2 * KV_COLS), lambda b, i: (b, 0, Q_COLS // (2 * KV_COLS))),
            pl.BlockSpec((None, ts, CONV_CH), tile),
            pl.BlockSpec((None, ts, GATE_COLS), tile),
            pl.BlockSpec((None, ts, D_MODEL), tile),
            _const_spec((Q_COLS, D_MODEL)),
            _const_spec((CONV_CH, D_MODEL)),
            _const_spec((D_MODEL, D_MODEL)),
            _const_spec((1, D_MODEL)),
            _const_spec((1, D_MODEL)),
        ],
        out_specs=pl.BlockSpec((None, ts, D_MODEL), tile),
        out_shape=jax.ShapeDtypeStruct((batch, seq, D_MODEL), jnp.float32),
        scratch_shapes=[
            pltpu.VMEM((3, N_HEADS, BLOCK, BAND), jnp.float32),
            pltpu.VMEM((ts, Q_COLS), jnp.bfloat16),
        ],
        compiler_params=pltpu.CompilerParams(
            dimension_semantics=(_ARB, _ARB), vmem_limit_bytes=VMEM_LIMIT_BYTES),
        name="token_mix",
    )(rel_bias, sink, buckets, qkv, qkv, yc, gt, xn, w_ao, w_co, w_out, ln1_g, ln1_b)


def _ffn_kernel(h_ref, wgu_ref, wd_ref, g_ref, b_ref, o_ref):
    dot = functools.partial(jnp.dot, preferred_element_type=jnp.float32)
    for r0 in range(0, h_ref.shape[0], FFN_SUBTILE):
        rows = slice(r0, r0 + FFN_SUBTILE)
        h = h_ref[rows, :]
        hb = h.astype(jnp.bfloat16)
        gu = dot(hb, wgu_ref[...])
        hid = []
        for c0 in range(0, gu.shape[1], 2 * MXU_COLS):
            gate = gu[:, c0:c0 + MXU_COLS]
            up = gu[:, c0 + MXU_COLS:c0 + 2 * MXU_COLS]
            hid.append((gate * jax.nn.sigmoid(gate) * up).astype(jnp.bfloat16))
        hid = jnp.concatenate(hid, axis=1)
        o_ref[rows, :] = _layer_norm(ALPHA * h + dot(hid, wd_ref[...]), g_ref[...], b_ref[...])


def _ffn(h2d, w_gu, w_down, ln_g, ln_b):
    n_tok = h2d.shape[0]
    assert n_tok % TM_FFN == 0
    row = lambda i: (i, 0)
    return pl.pallas_call(
        _ffn_kernel,
        grid=(n_tok // TM_FFN,),
        in_specs=[
            pl.BlockSpec((TM_FFN, D_MODEL), row),
            _const_spec(w_gu.shape),
            _const_spec(w_down.shape),
            _const_spec((1, D_MODEL)),
            _const_spec((1, D_MODEL)),
        ],
        out_specs=pl.BlockSpec((TM_FFN, D_MODEL), row),
        out_shape=jax.ShapeDtypeStruct((n_tok, D_MODEL), jnp.float32),
        compiler_params=pltpu.CompilerParams(
            dimension_semantics=(_ARB,), vmem_limit_bytes=VMEM_LIMIT_BYTES),
        name="swiglu_ffn",
    )(h2d, w_gu, w_down, ln_g, ln_b)


def kernel(x, ln_in_g, ln_in_b, rel_bias, w_in, b_gate, conv_w, conv_b, conv_ln_g, conv_ln_b,
           w_conv_out, w_attn_out, sink, w_out, ln1_g, ln1_b, w_gate, w_up, w_down, ln2_g, ln2_b):
    batch, seq, d_model = x.shape
    assert d_model == D_MODEL and w_in.shape[0] == DEPTH
    bf16 = jnp.bfloat16
    row = lambda v: v.reshape(1, -1)
    n_tok = batch * seq

    (xn, qkv, gt, yc), (w_ao, w_co, w_o, w_gu, w_d) = _inproj(
        x, row(ln_in_g), row(ln_in_b), w_in[0].astype(bf16), row(b_gate[0]),
        conv_w[0], row(conv_b[0]), row(conv_ln_g[0]), row(conv_ln_b[0]),
        (w_attn_out[0], w_conv_out[0], w_out[0], w_gate[0], w_up[0], w_down[0]))
    h = _mix(rel_bias, sink[0], qkv, yc, gt, xn, w_ao, w_co, w_o, row(ln1_g[0]), row(ln1_b[0]))
    out = _ffn(h.reshape(n_tok, D_MODEL), w_gu, w_d, row(ln2_g[0]), row(ln2_b[0]))
    return out.reshape(batch, seq, D_MODEL)
```

```python
import functools

import numpy as np
import jax
import jax.numpy as jnp
from jax import lax
from jax.experimental import pallas as pl
from jax.experimental.pallas import tpu as pltpu

D_MODEL = 1024
HEAD_DIM = 64
N_HEADS = 8
N_KV_HEADS = 2
WINDOW = 128
BLOCK = 128
NUM_BUCKETS = 32
MAX_DISTANCE = 128
CONV_CH = 512
CONV_WIDTH = 31
CONV_HALF = (CONV_WIDTH - 1) // 2
Q_COLS = N_HEADS * HEAD_DIM
KV_COLS = N_KV_HEADS * HEAD_DIM
CONV_COLS = 2 * CONV_CH
GATE_COLS = 2 * D_MODEL
QKV_COLS = Q_COLS + 2 * KV_COLS
DEPTH = 1
ALPHA = (2.0 * DEPTH) ** 0.25
LN_EPS = 1e-5
LOG2E = 1.4426950408889634
Q_SCALE = HEAD_DIM ** -0.5 * LOG2E
NEG_INF = -1e30
MASKED_BUCKET = NUM_BUCKETS

HEAD_GROUPS = (((0, 2, 5, 7), False), ((1, 3, 4, 6), True))
BAND = 3 * BLOCK
HALO = 16
CONV_ROWS = 32
CONV_GROUP = 4
SUBLANES = 8
BF16_SUBLANES = 16
LANES = 128
MXU_COLS = 256
CONV_SLABS = CONV_CH // LANES

VMEM_LIMIT_BYTES = 56 * 1024 * 1024

TS_IN = 512
TS_MIX = 512
TM_FFN = 1024
FFN_SUBTILE = 512

_ARB = "arbitrary"


def _layer_norm(x, g, b):
    mu = jnp.mean(x, axis=-1, keepdims=True)
    xc = x - mu
    var = jnp.mean(xc * xc, axis=-1, keepdims=True)
    return xc * lax.rsqrt(var + LN_EPS) * g + b


def _const_spec(shape):
    return pl.BlockSpec(shape, lambda *_: (0,) * len(shape), pipeline_mode=pl.Buffered(1))


def _smem_spec():
    return pl.BlockSpec(memory_space=pltpu.SMEM)


def _halo_specs(ts, seq, cols):
    per_tile = ts // HALO
    last = seq // HALO - 1
    prev = pl.BlockSpec((None, HALO, cols), lambda b, i: (b, jnp.maximum(i * per_tile - 1, 0), 0))
    nxt = pl.BlockSpec((None, HALO, cols), lambda b, i: (b, jnp.minimum((i + 1) * per_tile, last), 0))
    return prev, nxt


def _inproj_kernel(x_ref, xp_ref, xnx_ref, g_ref, b_ref, w_ref, bgate_ref,
                   convw_ref,
                   wao_ref, wco_ref, wout_ref, wgate_ref, wup_ref, wdown_ref,
                   xn_ref, qkv_ref, gt_ref, dw_ref,
                   wao_o, wco_o, wout_o, wgu_o, wdown_o,
                   xb_s, wb_s, glu_s, ge_s, go_s):
    b_id = pl.program_id(0)
    i_id = pl.program_id(1)
    n_tiles = pl.num_programs(1)
    ts = x_ref.shape[0]
    dot = functools.partial(jnp.dot, preferred_element_type=jnp.float32)
    o1 = QKV_COLS
    o2 = o1 + CONV_COLS
    o3 = o2 + D_MODEL

    @pl.when((b_id == 0) & (i_id == 0))
    def _init():
        for k in range(CONV_WIDTH):
            wb_s[k] = jnp.broadcast_to(convw_ref[k:k + 1, :].astype(wb_s.dtype), (BF16_SUBLANES, CONV_CH))
        glu_s[:, ts + 2 * HALO:, :] = jnp.zeros((CONV_SLABS, SUBLANES, LANES), jnp.float32)

    @pl.when(i_id == 0)
    def _cast_weights():
        for src, dst in ((wao_ref, wao_o), (wco_ref, wco_o), (wout_ref, wout_o), (wdown_ref, wdown_o)):
            dst[...] = src[...].astype(dst.dtype)
        for j in range(wgate_ref.shape[1] // MXU_COLS):
            src_cols = slice(j * MXU_COLS, (j + 1) * MXU_COLS)
            wgu_o[:, 2 * j * MXU_COLS:(2 * j + 1) * MXU_COLS] = wgate_ref[:, src_cols].astype(wgu_o.dtype)
            wgu_o[:, (2 * j + 1) * MXU_COLS:(2 * j + 2) * MXU_COLS] = wup_ref[:, src_cols].astype(wgu_o.dtype)

    g = g_ref[...]
    b = b_ref[...]
    xn = _layer_norm(x_ref[...], g, b)
    xn_ref[...] = xn
    xb_s[0:HALO, :] = _layer_norm(xp_ref[...], g, b).astype(xb_s.dtype)
    xb_s[HALO:HALO + ts, :] = xn.astype(xb_s.dtype)
    xb_s[HALO + ts:2 * HALO + ts, :] = _layer_norm(xnx_ref[...], g, b).astype(xb_s.dtype)

    u = dot(xb_s[...], w_ref[:, o1:o2])
    glu = u[:, :CONV_CH] * jax.nn.sigmoid(u[:, CONV_CH:])
    row = lax.broadcasted_iota(jnp.int32, (ts + 2 * HALO, 1), 0)
    pad = ((row < HALO) & (i_id == 0)) | ((row >= HALO + ts) & (i_id == n_tiles - 1))
    glu = jnp.where(pad, 0.0, glu)
    rows_ext = ts + 2 * HALO
    for c in range(CONV_SLABS):
        glu_s[c, 0:rows_ext, :] = glu[:, c * LANES:(c + 1) * LANES]
    for c in range(CONV_SLABS):
        for parity, dst in enumerate((ge_s, go_s)):
            win = glu_s[c, parity:parity + rows_ext, :].astype(jnp.bfloat16)
            dst[c] = pltpu.bitcast(win, jnp.uint32)

    xb = xb_s[HALO:HALO + ts, :]
    qkv_ref[:, :Q_COLS] = (dot(xb, w_ref[:, :Q_COLS]) * Q_SCALE).astype(qkv_ref.dtype)
    qkv_ref[:, Q_COLS:] = dot(xb, w_ref[:, Q_COLS:o1]).astype(qkv_ref.dtype)
    bgate = bgate_ref[...]
    gt_ref[:, :D_MODEL] = jax.nn.sigmoid(
        dot(xb, w_ref[:, o2:o3]) + bgate[:, :D_MODEL]).astype(gt_ref.dtype)
    gt_ref[:, D_MODEL:] = jax.nn.sigmoid(
        dot(xb, w_ref[:, o3:]) + bgate[:, D_MODEL:]).astype(gt_ref.dtype)

    first_tap = HALO - CONV_HALF

    def conv_chunk(chunk, carry):
        r0 = pl.multiple_of(chunk * CONV_ROWS, CONV_ROWS)
        w0 = pl.multiple_of(chunk * (CONV_ROWS // 2), CONV_ROWS // 2)
        n_groups = CONV_ROWS // BF16_SUBLANES
        for c in range(CONV_SLABS):
            lanes = slice(c * LANES, (c + 1) * LANES)
            accs = [jnp.zeros((BF16_SUBLANES, LANES), jnp.float32) for _ in range(n_groups)]
            for k0 in range(0, CONV_WIDTH, CONV_GROUP):
                parts = [None] * n_groups
                for k in range(k0, min(k0 + CONV_GROUP, CONV_WIDTH)):
                    w = wb_s[k, :, lanes]
                    for g in range(n_groups):
                        off = g * BF16_SUBLANES + first_tap + k
                        src = go_s if off % 2 else ge_s
                        words = src[c, pl.ds(w0 + off // 2, SUBLANES), :]
                        prod = pltpu.bitcast(words, jnp.bfloat16) * w
                        parts[g] = prod if parts[g] is None else parts[g] + prod
                for g in range(n_groups):
                    accs[g] = accs[g] + parts[g].astype(jnp.float32)
            for g in range(n_groups):
                dw_ref[pl.ds(r0 + g * BF16_SUBLANES, BF16_SUBLANES), lanes] = accs[g]
        return carry

    lax.fori_loop(0, ts // CONV_ROWS, conv_chunk, 0)


def _inproj(x, ln_g, ln_b, w_in_bf16, b_gate, conv_w, later_weights):
    batch, seq, _ = x.shape
    ts = TS_IN
    assert seq % ts == 0 and ts % CONV_ROWS == 0 and ts % HALO == 0 and HALO >= CONV_HALF
    tile = lambda b, i: (b, i, 0)
    prev_spec, next_spec = _halo_specs(ts, seq, D_MODEL)
    bf16 = jnp.bfloat16
    w_ao, w_co, w_out, w_gate, w_up, w_down = later_weights
    assert w_gate.shape == w_up.shape and w_gate.shape[1] % MXU_COLS == 0
    cast_out_shapes = [w_ao.shape, w_co.shape, w_out.shape,
                       (w_gate.shape[0], 2 * w_gate.shape[1]), w_down.shape]

    def slice_spec(shape):
        rows, cols = shape
        assert rows % (batch * BF16_SUBLANES) == 0
        return pl.BlockSpec((rows // batch, cols), lambda b, i: (b, 0))

    cast_in_specs = [slice_spec(w.shape) for w in later_weights]
    cast_out_specs = [slice_spec(s) for s in cast_out_shapes]
    outs = pl.pallas_call(
        _inproj_kernel,
        grid=(batch, seq // ts),
        in_specs=[
            pl.BlockSpec((None, ts, D_MODEL), tile),
            prev_spec,
            next_spec,
            _const_spec((1, D_MODEL)),
            _const_spec((1, D_MODEL)),
            _const_spec(w_in_bf16.shape),
            _const_spec((1, GATE_COLS)),
            _const_spec((CONV_WIDTH, CONV_CH)),
        ] + cast_in_specs,
        out_specs=[
            pl.BlockSpec((None, ts, D_MODEL), tile),
            pl.BlockSpec((None, ts, QKV_COLS), tile),
            pl.BlockSpec((None, ts, GATE_COLS), tile),
            pl.BlockSpec((None, ts, CONV_CH), tile),
        ] + cast_out_specs,
        out_shape=[
            jax.ShapeDtypeStruct((batch, seq, D_MODEL), jnp.float32),
            jax.ShapeDtypeStruct((batch, seq, QKV_COLS), bf16),
            jax.ShapeDtypeStruct((batch, seq, GATE_COLS), bf16),
            jax.ShapeDtypeStruct((batch, seq, CONV_CH), jnp.float32),
        ] + [jax.ShapeDtypeStruct(s, bf16) for s in cast_out_shapes],
        scratch_shapes=[
            pltpu.VMEM((ts + 2 * HALO, D_MODEL), bf16),
            pltpu.VMEM((CONV_WIDTH, BF16_SUBLANES, CONV_CH), bf16),
            pltpu.VMEM((CONV_SLABS, ts + 2 * HALO + SUBLANES, LANES), jnp.float32),
            pltpu.VMEM((CONV_SLABS, ts // 2 + HALO, LANES), jnp.uint32),
            pltpu.VMEM((CONV_SLABS, ts // 2 + HALO, LANES), jnp.uint32),
        ],
        compiler_params=pltpu.CompilerParams(
            dimension_semantics=(_ARB, _ARB), vmem_limit_bytes=VMEM_LIMIT_BYTES),
        name="ln_inproj_conv",
    )(x, x, x, ln_g, ln_b, w_in_bf16, b_gate, conv_w, *later_weights)
    return outs[:4], outs[4:]


def _band_buckets():
    nb = NUM_BUCKETS // 2
    qi = np.arange(BLOCK)[:, None]
    kj = np.arange(BAND)[None, :]
    rel = kj - BLOCK - qi
    ret = (rel > 0).astype(np.int32) * nb
    n = np.abs(rel)
    max_exact = nb // 2
    large = max_exact + (np.log(np.maximum(n, 1) / max_exact)
                         / np.log(MAX_DISTANCE / max_exact) * (nb - max_exact)).astype(np.int32)
    large = np.minimum(large, nb - 1)
    buckets = (ret + np.where(n < max_exact, n, large)).astype(np.int32)
    return np.where(n <= WINDOW, buckets, MASKED_BUCKET).astype(np.int32)


def _mix_kernel(n_seq_blocks,
                relb_ref, sink_ref, bk_ref,
                q_ref, kv_ref, dw_ref, gt_ref, xn_ref,
                convb_ref, clng_ref, clnb_ref,
                wao_ref, wco_ref, wout_ref, ln1g_ref, ln1b_ref,
                h_ref,
                bias_s, o_s):
    b_id = pl.program_id(0)
    i_id = pl.program_id(1)
    ts = q_ref.shape[0]
    blocks_per_tile = ts // BLOCK
    dot = functools.partial(jnp.dot, preferred_element_type=jnp.float32)

    @pl.when((b_id == 0) & (i_id == 0))
    def _init():
        bk = bk_ref[...]
        col = lax.broadcasted_iota(jnp.int32, (BLOCK, BAND), 1)
        for h in range(N_HEADS):
            def body(b, acc):
                return jnp.where(bk == b, relb_ref[b, h], acc)
            mid = LOG2E * lax.fori_loop(0, NUM_BUCKETS, body,
                                        jnp.full((BLOCK, BAND), NEG_INF, jnp.float32))
            bias_s[0, h] = jnp.where(col < BLOCK, NEG_INF, mid)
            bias_s[1, h] = mid
            bias_s[2, h] = jnp.where(col >= 2 * BLOCK, NEG_INF, mid)

    lane = lax.broadcasted_iota(jnp.int32, (BLOCK, LANES), 1)
    low_half = lane < HEAD_DIM
    nt = (((1,), (1,)), ((), ()))

    def block_operands(j):
        n = i_id * blocks_per_tile + j
        variant = jnp.where(n == 0, 0, jnp.where(n == n_seq_blocks - 1, 2, 1))
        prev_r = pl.multiple_of(jnp.maximum(n - 1, 0) * BLOCK, BLOCK)
        own_r = pl.multiple_of(n * BLOCK, BLOCK)
        next_r = pl.multiple_of(jnp.minimum(n + 1, n_seq_blocks - 1) * BLOCK, BLOCK)
        kvb = jnp.concatenate([kv_ref[pl.ds(prev_r, BLOCK), :],
                               kv_ref[pl.ds(own_r, BLOCK), :],
                               kv_ref[pl.ds(next_r, BLOCK), :]], axis=0)
        kk = kvb[:, :LANES]
        vv = kvb[:, LANES:]
        q = q_ref[j * BLOCK:(j + 1) * BLOCK, :]
        zero = jnp.zeros((BLOCK, LANES), q.dtype)
        q_head = {}
        for p in range(N_HEADS // 2):
            q2 = q[:, p * LANES:(p + 1) * LANES]
            q_head[2 * p] = jnp.where(low_half, q2, zero)
            q_head[2 * p + 1] = jnp.where(low_half, zero, q2)
        return dict(variant=variant, q_head=q_head,
                    k=(kk, pltpu.roll(kk, HEAD_DIM, axis=1)),
                    v=(vv, pltpu.roll(vv, HEAD_DIM, axis=1)))

    def scores(blk, heads, swapped):
        lhs = jnp.concatenate([blk["q_head"][h] for h in heads], axis=0)
        return lax.dot_general(lhs, blk["k"][swapped], nt, preferred_element_type=jnp.float32)

    def softmax_values(blk, heads, swapped, s_all, o_head):
        ps, inv_ls = [], []
        for r, h in enumerate(heads):
            s = s_all[r * BLOCK:(r + 1) * BLOCK, :] + bias_s[blk["variant"], h]
            sink = jnp.full((BLOCK, 1), sink_ref[h], jnp.float32) * LOG2E
            m = jnp.maximum(jnp.max(s, axis=-1, keepdims=True), sink)
            p = jnp.exp2(s - m)
            l = jnp.sum(p, axis=-1, keepdims=True) + jnp.exp2(sink - m)
            ps.append(p.astype(jnp.bfloat16))
            inv_ls.append(1.0 / l)
        r_all = dot(jnp.concatenate(ps, axis=0), blk["v"][swapped])
        for r, h in enumerate(heads):
            o_head[h] = r_all[r * BLOCK:(r + 1) * BLOCK, :] * inv_ls[r]

    stages = [(j, heads, int(swapped)) for j in range(blocks_per_tile) for heads, swapped in HEAD_GROUPS]
    yc_chunk = D_MODEL * 2 // len(stages)
    assert yc_chunk % MXU_COLS == 0 and len(stages) % 2 == 0
    yc_parts = []
    y = _layer_norm(dw_ref[...] + convb_ref[...], clng_ref[...], clnb_ref[...])
    yc = (y * jax.nn.sigmoid(y)).astype(jnp.bfloat16)
    blocks = {}
    o_heads = {}
    pending = None
    for c in range(len(stages) + 1):
        nxt = None
        if c < len(stages):
            j, heads, swapped = stages[c]
            if j not in blocks:
                blocks[j] = block_operands(j)
                o_heads[j] = {}
            nxt = (j, heads, swapped, scores(blocks[j], heads, swapped))
        if c % 2 == 1 and len(yc_parts) * yc_chunk < D_MODEL:
            c0 = len(yc_parts) * yc_chunk
            yc_parts.append(dot(yc, wco_ref[:, c0:c0 + yc_chunk]))
        if pending is not None:
            j, heads, swapped, s_all = pending
            softmax_values(blocks[j], heads, swapped, s_all, o_heads[j])
            if len(o_heads[j]) == N_HEADS:
                o_pairs = [jnp.where(low_half, o_heads[j][2 * p], o_heads[j][2 * p + 1])
                           for p in range(N_HEADS // 2)]
                o_s[j * BLOCK:(j + 1) * BLOCK, :] = jnp.concatenate(o_pairs, axis=1).astype(o_s.dtype)
        pending = nxt

    y_a = dot(o_s[...], wao_ref[...])
    y_c = jnp.concatenate(yc_parts, axis=1)
    merged = (gt_ref[:, :D_MODEL].astype(jnp.float32) * y_a
              + gt_ref[:, D_MODEL:].astype(jnp.float32) * y_c)
    mix = dot(merged.astype(jnp.bfloat16), wout_ref[...])
    h_ref[...] = _layer_norm(ALPHA * xn_ref[...] + mix, ln1g_ref[...], ln1b_ref[...])


def _mix(rel_bias, sink, qkv, dw, gt, xn, conv_b, cln_g, cln_b, w_ao, w_co, w_out, ln1_g, ln1_b):
    batch, seq, _ = xn.shape
    ts = TS_MIX
    assert seq % ts == 0 and ts % BLOCK == 0 and seq // BLOCK >= 2
    n_seq_blocks = seq // BLOCK
    buckets = jnp.asarray(_band_buckets())
    tile = lambda b, i: (b, i, 0)
    return pl.pallas_call(
        functools.partial(_mix_kernel, n_seq_blocks),
        grid=(batch, seq // ts),
        in_specs=[
            _smem_spec(),
            _smem_spec(),
            _const_spec((BLOCK, BAND)),
            pl.BlockSpec((None, ts, Q_COLS), tile),
            pl.BlockSpec((None, seq, 2 * KV_COLS), lambda b, i: (b, 0, Q_COLS // (2 * KV_COLS))),
            pl.BlockSpec((None, ts, CONV_CH), tile),
            pl.BlockSpec((None, ts, GATE_COLS), tile),
            pl.BlockSpec((None, ts, D_MODEL), tile),
            _const_spec((1, CONV_CH)),
            _const_spec((1, CONV_CH)),
            _const_spec((1, CONV_CH)),
            _const_spec((Q_COLS, D_MODEL)),
            _const_spec((CONV_CH, D_MODEL)),
            _const_spec((D_MODEL, D_MODEL)),
            _const_spec((1, D_MODEL)),
            _const_spec((1, D_MODEL)),
        ],
        out_specs=pl.BlockSpec((None, ts, D_MODEL), tile),
        out_shape=jax.ShapeDtypeStruct((batch, seq, D_MODEL), jnp.float32),
        scratch_shapes=[
            pltpu.VMEM((3, N_HEADS, BLOCK, BAND), jnp.float32),
            pltpu.VMEM((ts, Q_COLS), jnp.bfloat16),
        ],
        compiler_params=pltpu.CompilerParams(
            dimension_semantics=(_ARB, _ARB), vmem_limit_bytes=VMEM_LIMIT_BYTES),
        name="token_mix",
    )(rel_bias, sink, buckets, qkv, qkv, dw, gt, xn, conv_b, cln_g, cln_b, w_ao, w_co, w_out, ln1_g, ln1_b)


def _ffn_kernel(h_ref, wgu_ref, wd_ref, g_ref, b_ref, o_ref):
    dot = functools.partial(jnp.dot, preferred_element_type=jnp.float32)
    for r0 in range(0, h_ref.shape[0], FFN_SUBTILE):
        rows = slice(r0, r0 + FFN_SUBTILE)
        h = h_ref[rows, :]
        hb = h.astype(jnp.bfloat16)
        gu = dot(hb, wgu_ref[...])
        hid = []
        for c0 in range(0, gu.shape[1], 2 * MXU_COLS):
            gate = gu[:, c0:c0 + MXU_COLS]
            up = gu[:, c0 + MXU_COLS:c0 + 2 * MXU_COLS]
            hid.append((gate * jax.nn.sigmoid(gate) * up).astype(jnp.bfloat16))
        hid = jnp.concatenate(hid, axis=1)
        o_ref[rows, :] = _layer_norm(ALPHA * h + dot(hid, wd_ref[...]), g_ref[...], b_ref[...])


def _ffn(h2d, w_gu, w_down, ln_g, ln_b):
    n_tok = h2d.shape[0]
    assert n_tok % TM_FFN == 0
    row = lambda i: (i, 0)
    return pl.pallas_call(
        _ffn_kernel,
        grid=(n_tok // TM_FFN,),
        in_specs=[
            pl.BlockSpec((TM_FFN, D_MODEL), row),
            _const_spec(w_gu.shape),
            _const_spec(w_down.shape),
            _const_spec((1, D_MODEL)),
            _const_spec((1, D_MODEL)),
        ],
        out_specs=pl.BlockSpec((TM_FFN, D_MODEL), row),
        out_shape=jax.ShapeDtypeStruct((n_tok, D_MODEL), jnp.float32),
        compiler_params=pltpu.CompilerParams(
            dimension_semantics=(_ARB,), vmem_limit_bytes=VMEM_LIMIT_BYTES),
        name="swiglu_ffn",
    )(h2d, w_gu, w_down, ln_g, ln_b)


def kernel(x, ln_in_g, ln_in_b, rel_bias, w_in, b_gate, conv_w, conv_b, conv_ln_g, conv_ln_b,
           w_conv_out, w_attn_out, sink, w_out, ln1_g, ln1_b, w_gate, w_up, w_down, ln2_g, ln2_b):
    batch, seq, d_model = x.shape
    assert d_model == D_MODEL and w_in.shape[0] == DEPTH
    bf16 = jnp.bfloat16
    row = lambda v: v.reshape(1, -1)
    n_tok = batch * seq

    (xn, qkv, gt, dw), (w_ao, w_co, w_o, w_gu, w_d) = _inproj(
        x, row(ln_in_g), row(ln_in_b), w_in[0].astype(bf16), row(b_gate[0]), conv_w[0],
        (w_attn_out[0], w_conv_out[0], w_out[0], w_gate[0], w_up[0], w_down[0]))
    h = _mix(rel_bias, sink[0], qkv, dw, gt, xn,
             row(conv_b[0]), row(conv_ln_g[0]), row(conv_ln_b[0]),
             w_ao, w_co, w_o, row(ln1_g[0]), row(ln1_b[0]))
    out = _ffn(h.reshape(n_tok, D_MODEL), w_gu, w_d, row(ln2_g[0]), row(ln2_b[0]))
    return out.reshape(batch, seq, D_MODEL)
```

```python
import functools

import numpy as np
import jax
import jax.numpy as jnp
from jax import lax
from jax.experimental import pallas as pl
from jax.experimental.pallas import tpu as pltpu

D_MODEL = 1024
HEAD_DIM = 64
N_HEADS = 8
N_KV_HEADS = 2
WINDOW = 128
BLOCK = 128
NUM_BUCKETS = 32
MAX_DISTANCE = 128
CONV_CH = 512
CONV_WIDTH = 31
CONV_HALF = (CONV_WIDTH - 1) // 2
Q_COLS = N_HEADS * HEAD_DIM
KV_COLS = N_KV_HEADS * HEAD_DIM
CONV_COLS = 2 * CONV_CH
GATE_COLS = 2 * D_MODEL
QKV_COLS = Q_COLS + 2 * KV_COLS
DEPTH = 1
ALPHA = (2.0 * DEPTH) ** 0.25
LN_EPS = 1e-5
LOG2E = 1.4426950408889634
Q_SCALE = HEAD_DIM ** -0.5 * LOG2E
NEG_INF = -1e30
MASKED_BUCKET = NUM_BUCKETS

HEAD_GROUPS = (((0, 2, 5, 7), False), ((1, 3, 4, 6), True))
BAND = 3 * BLOCK
HALO = 16
CONV_ROWS = 32
CONV_GROUP = 4
SUBLANES = 8
BF16_SUBLANES = 16
LANES = 128
MXU_COLS = 256
CONV_SLABS = CONV_CH // LANES

VMEM_LIMIT_BYTES = 56 * 1024 * 1024

TS_IN = 512
TS_MIX = 512
TM_FFN = 1024
FFN_SUBTILE = 512

_ARB = "arbitrary"


def _layer_norm(x, g, b):
    mu = jnp.mean(x, axis=-1, keepdims=True)
    xc = x - mu
    var = jnp.mean(xc * xc, axis=-1, keepdims=True)
    return xc * lax.rsqrt(var + LN_EPS) * g + b


def _const_spec(shape):
    return pl.BlockSpec(shape, lambda *_: (0,) * len(shape), pipeline_mode=pl.Buffered(1))


def _smem_spec():
    return pl.BlockSpec(memory_space=pltpu.SMEM)


def _halo_specs(ts, seq, cols):
    per_tile = ts // HALO
    last = seq // HALO - 1
    prev = pl.BlockSpec((None, HALO, cols), lambda b, i: (b, jnp.maximum(i * per_tile - 1, 0), 0))
    nxt = pl.BlockSpec((None, HALO, cols), lambda b, i: (b, jnp.minimum((i + 1) * per_tile, last), 0))
    return prev, nxt


def _inproj_kernel(x_ref, xp_ref, xnx_ref, g_ref, b_ref, w_ref, bgate_ref,
                   convw_ref,
                   wao_ref, wco_ref, wout_ref, wgate_ref, wup_ref, wdown_ref,
                   xn_ref, qkv_ref, gt_ref, dw_ref,
                   wao_o, wco_o, wout_o, wgu_o, wdown_o,
                   xb_s, wb_s, glu_s, ge_s, go_s):
    b_id = pl.program_id(0)
    i_id = pl.program_id(1)
    n_tiles = pl.num_programs(1)
    ts = x_ref.shape[0]
    dot = functools.partial(jnp.dot, preferred_element_type=jnp.float32)
    o1 = QKV_COLS
    o2 = o1 + CONV_COLS
    o3 = o2 + D_MODEL

    @pl.when((b_id == 0) & (i_id == 0))
    def _init():
        for k in range(CONV_WIDTH):
            wb_s[k] = jnp.broadcast_to(convw_ref[k:k + 1, :].astype(wb_s.dtype), (BF16_SUBLANES, CONV_CH))
        glu_s[:, ts + 2 * HALO:, :] = jnp.zeros((CONV_SLABS, SUBLANES, LANES), jnp.float32)

    @pl.when(i_id == 0)
    def _cast_weights():
        for src, dst in ((wao_ref, wao_o), (wco_ref, wco_o), (wout_ref, wout_o), (wdown_ref, wdown_o)):
            dst[...] = src[...].astype(dst.dtype)
        for j in range(wgate_ref.shape[1] // MXU_COLS):
            src_cols = slice(j * MXU_COLS, (j + 1) * MXU_COLS)
            wgu_o[:, 2 * j * MXU_COLS:(2 * j + 1) * MXU_COLS] = wgate_ref[:, src_cols].astype(wgu_o.dtype)
            wgu_o[:, (2 * j + 1) * MXU_COLS:(2 * j + 2) * MXU_COLS] = wup_ref[:, src_cols].astype(wgu_o.dtype)

    g = g_ref[...]
    b = b_ref[...]
    xn = _layer_norm(x_ref[...], g, b)
    xn_ref[...] = xn
    xb_s[0:HALO, :] = _layer_norm(xp_ref[...], g, b).astype(xb_s.dtype)
    xb_s[HALO:HALO + ts, :] = xn.astype(xb_s.dtype)
    xb_s[HALO + ts:2 * HALO + ts, :] = _layer_norm(xnx_ref[...], g, b).astype(xb_s.dtype)

    u = dot(xb_s[...], w_ref[:, o1:o2])
    glu = u[:, :CONV_CH] * jax.nn.sigmoid(u[:, CONV_CH:])
    row = lax.broadcasted_iota(jnp.int32, (ts + 2 * HALO, 1), 0)
    pad = ((row < HALO) & (i_id == 0)) | ((row >= HALO + ts) & (i_id == n_tiles - 1))
    glu = jnp.where(pad, 0.0, glu)
    rows_ext = ts + 2 * HALO
    for c in range(CONV_SLABS):
        glu_s[c, 0:rows_ext, :] = glu[:, c * LANES:(c + 1) * LANES]
    for c in range(CONV_SLABS):
        for parity, dst in enumerate((ge_s, go_s)):
            win = glu_s[c, parity:parity + rows_ext, :].astype(jnp.bfloat16)
            dst[c] = pltpu.bitcast(win, jnp.uint32)

    xb = xb_s[HALO:HALO + ts, :]
    qkv_ref[:, :Q_COLS] = (dot(xb, w_ref[:, :Q_COLS]) * Q_SCALE).astype(qkv_ref.dtype)
    qkv_ref[:, Q_COLS:] = dot(xb, w_ref[:, Q_COLS:o1]).astype(qkv_ref.dtype)
    bgate = bgate_ref[...]
    gt_ref[:, :D_MODEL] = jax.nn.sigmoid(
        dot(xb, w_ref[:, o2:o3]) + bgate[:, :D_MODEL]).astype(gt_ref.dtype)
    gt_ref[:, D_MODEL:] = jax.nn.sigmoid(
        dot(xb, w_ref[:, o3:]) + bgate[:, D_MODEL:]).astype(gt_ref.dtype)

    first_tap = HALO - CONV_HALF

    def conv_chunk(chunk, carry):
        r0 = pl.multiple_of(chunk * CONV_ROWS, CONV_ROWS)
        w0 = pl.multiple_of(chunk * (CONV_ROWS // 2), CONV_ROWS // 2)
        n_groups = CONV_ROWS // BF16_SUBLANES
        for c in range(CONV_SLABS):
            lanes = slice(c * LANES, (c + 1) * LANES)
            accs = [jnp.zeros((BF16_SUBLANES, LANES), jnp.float32) for _ in range(n_groups)]
            for k0 in range(0, CONV_WIDTH, CONV_GROUP):
                parts = [None] * n_groups
                for k in range(k0, min(k0 + CONV_GROUP, CONV_WIDTH)):
                    w = wb_s[k, :, lanes]
                    for g in range(n_groups):
                        off = g * BF16_SUBLANES + first_tap + k
                        src = go_s if off % 2 else ge_s
                        words = src[c, pl.ds(w0 + off // 2, SUBLANES), :]
                        prod = pltpu.bitcast(words, jnp.bfloat16) * w
                        parts[g] = prod if parts[g] is None else parts[g] + prod
                for g in range(n_groups):
                    accs[g] = accs[g] + parts[g].astype(jnp.float32)
            for g in range(n_groups):
                dw_ref[pl.ds(r0 + g * BF16_SUBLANES, BF16_SUBLANES), lanes] = accs[g]
        return carry

    lax.fori_loop(0, ts // CONV_ROWS, conv_chunk, 0)


def _inproj(x, ln_g, ln_b, w_in_bf16, b_gate, conv_w, later_weights):
    batch, seq, _ = x.shape
    ts = TS_IN
    assert seq % ts == 0 and ts % CONV_ROWS == 0 and ts % HALO == 0 and HALO >= CONV_HALF
    tile = lambda b, i: (b, i, 0)
    prev_spec, next_spec = _halo_specs(ts, seq, D_MODEL)
    bf16 = jnp.bfloat16
    w_ao, w_co, w_out, w_gate, w_up, w_down = later_weights
    assert w_gate.shape == w_up.shape and w_gate.shape[1] % MXU_COLS == 0
    cast_out_shapes = [w_ao.shape, w_co.shape, w_out.shape,
                       (w_gate.shape[0], 2 * w_gate.shape[1]), w_down.shape]

    def slice_spec(shape):
        rows, cols = shape
        assert rows % (batch * BF16_SUBLANES) == 0
        return pl.BlockSpec((rows // batch, cols), lambda b, i: (b, 0))

    cast_in_specs = [slice_spec(w.shape) for w in later_weights]
    cast_out_specs = [slice_spec(s) for s in cast_out_shapes]
    outs = pl.pallas_call(
        _inproj_kernel,
        grid=(batch, seq // ts),
        in_specs=[
            pl.BlockSpec((None, ts, D_MODEL), tile),
            prev_spec,
            next_spec,
            _const_spec((1, D_MODEL)),
            _const_spec((1, D_MODEL)),
            _const_spec(w_in_bf16.shape),
            _const_spec((1, GATE_COLS)),
            _const_spec((CONV_WIDTH, CONV_CH)),
        ] + cast_in_specs,
        out_specs=[
            pl.BlockSpec((None, ts, D_MODEL), tile),
            pl.BlockSpec((None, ts, QKV_COLS), tile),
            pl.BlockSpec((None, ts, GATE_COLS), tile),
            pl.BlockSpec((None, ts, CONV_CH), tile),
        ] + cast_out_specs,
        out_shape=[
            jax.ShapeDtypeStruct((batch, seq, D_MODEL), jnp.float32),
            jax.ShapeDtypeStruct((batch, seq, QKV_COLS), bf16),
            jax.ShapeDtypeStruct((batch, seq, GATE_COLS), bf16),
            jax.ShapeDtypeStruct((batch, seq, CONV_CH), jnp.float32),
        ] + [jax.ShapeDtypeStruct(s, bf16) for s in cast_out_shapes],
        scratch_shapes=[
            pltpu.VMEM((ts + 2 * HALO, D_MODEL), bf16),
            pltpu.VMEM((CONV_WIDTH, BF16_SUBLANES, CONV_CH), bf16),
            pltpu.VMEM((CONV_SLABS, ts + 2 * HALO + SUBLANES, LANES), jnp.float32),
            pltpu.VMEM((CONV_SLABS, ts // 2 + HALO, LANES), jnp.uint32),
            pltpu.VMEM((CONV_SLABS, ts // 2 + HALO, LANES), jnp.uint32),
        ],
        compiler_params=pltpu.CompilerParams(
            dimension_semantics=(_ARB, _ARB), vmem_limit_bytes=VMEM_LIMIT_BYTES),
        name="ln_inproj_conv",
    )(x, x, x, ln_g, ln_b, w_in_bf16, b_gate, conv_w, *later_weights)
    return outs[:4], outs[4:]


def _band_buckets():
    nb = NUM_BUCKETS // 2
    qi = np.arange(BLOCK)[:, None]
    kj = np.arange(BAND)[None, :]
    rel = kj - BLOCK - qi
    ret = (rel > 0).astype(np.int32) * nb
    n = np.abs(rel)
    max_exact = nb // 2
    large = max_exact + (np.log(np.maximum(n, 1) / max_exact)
                         / np.log(MAX_DISTANCE / max_exact) * (nb - max_exact)).astype(np.int32)
    large = np.minimum(large, nb - 1)
    buckets = (ret + np.where(n < max_exact, n, large)).astype(np.int32)
    return np.where(n <= WINDOW, buckets, MASKED_BUCKET).astype(np.int32)


def _mix_kernel(n_seq_blocks,
                relb_ref, sink_ref, bk_ref,
                q_ref, kv_ref, dw_ref, gt_ref, xn_ref,
                convb_ref, clng_ref, clnb_ref,
                wao_ref, wco_ref, wout_ref,
                r_ref,
                bias_s, o_s):
    b_id = pl.program_id(0)
    i_id = pl.program_id(1)
    ts = q_ref.shape[0]
    blocks_per_tile = ts // BLOCK
    dot = functools.partial(jnp.dot, preferred_element_type=jnp.float32)

    @pl.when((b_id == 0) & (i_id == 0))
    def _init():
        bk = bk_ref[...]
        col = lax.broadcasted_iota(jnp.int32, (BLOCK, BAND), 1)
        for h in range(N_HEADS):
            def body(b, acc):
                return jnp.where(bk == b, relb_ref[b, h], acc)
            mid = LOG2E * lax.fori_loop(0, NUM_BUCKETS, body,
                                        jnp.full((BLOCK, BAND), NEG_INF, jnp.float32))
            bias_s[0, h] = jnp.where(col < BLOCK, NEG_INF, mid)
            bias_s[1, h] = mid
            bias_s[2, h] = jnp.where(col >= 2 * BLOCK, NEG_INF, mid)

    lane = lax.broadcasted_iota(jnp.int32, (BLOCK, LANES), 1)
    low_half = lane < HEAD_DIM
    nt = (((1,), (1,)), ((), ()))

    def block_operands(j):
        n = i_id * blocks_per_tile + j
        variant = jnp.where(n == 0, 0, jnp.where(n == n_seq_blocks - 1, 2, 1))
        prev_r = pl.multiple_of(jnp.maximum(n - 1, 0) * BLOCK, BLOCK)
        own_r = pl.multiple_of(n * BLOCK, BLOCK)
        next_r = pl.multiple_of(jnp.minimum(n + 1, n_seq_blocks - 1) * BLOCK, BLOCK)
        kvb = jnp.concatenate([kv_ref[pl.ds(prev_r, BLOCK), :],
                               kv_ref[pl.ds(own_r, BLOCK), :],
                               kv_ref[pl.ds(next_r, BLOCK), :]], axis=0)
        kk = kvb[:, :LANES]
        vv = kvb[:, LANES:]
        q = q_ref[j * BLOCK:(j + 1) * BLOCK, :]
        zero = jnp.zeros((BLOCK, LANES), q.dtype)
        q_head = {}
        for p in range(N_HEADS // 2):
            q2 = q[:, p * LANES:(p + 1) * LANES]
            q_head[2 * p] = jnp.where(low_half, q2, zero)
            q_head[2 * p + 1] = jnp.where(low_half, zero, q2)
        return dict(variant=variant, q_head=q_head,
                    k=(kk, pltpu.roll(kk, HEAD_DIM, axis=1)),
                    v=(vv, pltpu.roll(vv, HEAD_DIM, axis=1)))

    def scores(blk, heads, swapped):
        lhs = jnp.concatenate([blk["q_head"][h] for h in heads], axis=0)
        return lax.dot_general(lhs, blk["k"][swapped], nt, preferred_element_type=jnp.float32)

    def softmax_values(blk, heads, swapped, s_all, o_head):
        ps, inv_ls = [], []
        for r, h in enumerate(heads):
            s = s_all[r * BLOCK:(r + 1) * BLOCK, :] + bias_s[blk["variant"], h]
            sink = jnp.full((BLOCK, 1), sink_ref[h], jnp.float32) * LOG2E
            m = jnp.maximum(jnp.max(s, axis=-1, keepdims=True), sink)
            p = jnp.exp2(s - m)
            l = jnp.sum(p, axis=-1, keepdims=True) + jnp.exp2(sink - m)
            ps.append(p.astype(jnp.bfloat16))
            inv_ls.append(1.0 / l)
        r_all = dot(jnp.concatenate(ps, axis=0), blk["v"][swapped])
        for r, h in enumerate(heads):
            o_head[h] = r_all[r * BLOCK:(r + 1) * BLOCK, :] * inv_ls[r]

    stages = [(j, heads, int(swapped)) for j in range(blocks_per_tile) for heads, swapped in HEAD_GROUPS]
    yc_chunk = D_MODEL * 2 // len(stages)
    assert yc_chunk % MXU_COLS == 0 and len(stages) % 2 == 0
    yc_parts = []
    y = _layer_norm(dw_ref[...] + convb_ref[...], clng_ref[...], clnb_ref[...])
    yc = (y * jax.nn.sigmoid(y)).astype(jnp.bfloat16)
    blocks = {}
    o_heads = {}
    pending = None
    for c in range(len(stages) + 1):
        nxt = None
        if c < len(stages):
            j, heads, swapped = stages[c]
            if j not in blocks:
                blocks[j] = block_operands(j)
                o_heads[j] = {}
            nxt = (j, heads, swapped, scores(blocks[j], heads, swapped))
        if c % 2 == 1 and len(yc_parts) * yc_chunk < D_MODEL:
            c0 = len(yc_parts) * yc_chunk
            yc_parts.append(dot(yc, wco_ref[:, c0:c0 + yc_chunk]))
        if pending is not None:
            j, heads, swapped, s_all = pending
            softmax_values(blocks[j], heads, swapped, s_all, o_heads[j])
            if len(o_heads[j]) == N_HEADS:
                o_pairs = [jnp.where(low_half, o_heads[j][2 * p], o_heads[j][2 * p + 1])
                           for p in range(N_HEADS // 2)]
                o_s[j * BLOCK:(j + 1) * BLOCK, :] = jnp.concatenate(o_pairs, axis=1).astype(o_s.dtype)
        pending = nxt

    y_a = dot(o_s[...], wao_ref[...])
    y_c = jnp.concatenate(yc_parts, axis=1)
    merged = (gt_ref[:, :D_MODEL].astype(jnp.float32) * y_a
              + gt_ref[:, D_MODEL:].astype(jnp.float32) * y_c)
    mix = dot(merged.astype(jnp.bfloat16), wout_ref[...])
    r_ref[...] = ALPHA * xn_ref[...] + mix


def _mix(rel_bias, sink, qkv, dw, gt, xn, conv_b, cln_g, cln_b, w_ao, w_co, w_out):
    batch, seq, _ = xn.shape
    ts = TS_MIX
    assert seq % ts == 0 and ts % BLOCK == 0 and seq // BLOCK >= 2
    n_seq_blocks = seq // BLOCK
    buckets = jnp.asarray(_band_buckets())
    tile = lambda b, i: (b, i, 0)
    return pl.pallas_call(
        functools.partial(_mix_kernel, n_seq_blocks),
        grid=(batch, seq // ts),
        in_specs=[
            _smem_spec(),
            _smem_spec(),
            _const_spec((BLOCK, BAND)),
            pl.BlockSpec((None, ts, Q_COLS), tile),
            pl.BlockSpec((None, seq, 2 * KV_COLS), lambda b, i: (b, 0, Q_COLS // (2 * KV_COLS))),
            pl.BlockSpec((None, ts, CONV_CH), tile),
            pl.BlockSpec((None, ts, GATE_COLS), tile),
            pl.BlockSpec((None, ts, D_MODEL), tile),
            _const_spec((1, CONV_CH)),
            _const_spec((1, CONV_CH)),
            _const_spec((1, CONV_CH)),
            _const_spec((Q_COLS, D_MODEL)),
            _const_spec((CONV_CH, D_MODEL)),
            _const_spec((D_MODEL, D_MODEL)),
        ],
        out_specs=pl.BlockSpec((None, ts, D_MODEL), tile),
        out_shape=jax.ShapeDtypeStruct((batch, seq, D_MODEL), jnp.float32),
        scratch_shapes=[
            pltpu.VMEM((3, N_HEADS, BLOCK, BAND), jnp.float32),
            pltpu.VMEM((ts, Q_COLS), jnp.bfloat16),
        ],
        compiler_params=pltpu.CompilerParams(
            dimension_semantics=(_ARB, _ARB), vmem_limit_bytes=VMEM_LIMIT_BYTES),
        name="token_mix",
    )(rel_bias, sink, buckets, qkv, qkv, dw, gt, xn, conv_b, cln_g, cln_b, w_ao, w_co, w_out)


def _ffn_kernel(r_ref, g1_ref, b1_ref, wgu_ref, wd_ref, g_ref, b_ref, o_ref):
    dot = functools.partial(jnp.dot, preferred_element_type=jnp.float32)
    for r0 in range(0, r_ref.shape[0], FFN_SUBTILE):
        rows = slice(r0, r0 + FFN_SUBTILE)
        h = _layer_norm(r_ref[rows, :], g1_ref[...], b1_ref[...])
        hb = h.astype(jnp.bfloat16)
        gu = dot(hb, wgu_ref[...])
        hid = []
        for c0 in range(0, gu.shape[1], 2 * MXU_COLS):
            gate = gu[:, c0:c0 + MXU_COLS]
            up = gu[:, c0 + MXU_COLS:c0 + 2 * MXU_COLS]
            hid.append((gate * jax.nn.sigmoid(gate) * up).astype(jnp.bfloat16))
        hid = jnp.concatenate(hid, axis=1)
        o_ref[rows, :] = _layer_norm(ALPHA * h + dot(hid, wd_ref[...]), g_ref[...], b_ref[...])


def _ffn(r2d, ln1_g, ln1_b, w_gu, w_down, ln_g, ln_b):
    n_tok = r2d.shape[0]
    assert n_tok % TM_FFN == 0
    row = lambda i: (i, 0)
    return pl.pallas_call(
        _ffn_kernel,
        grid=(n_tok // TM_FFN,),
        in_specs=[
            pl.BlockSpec((TM_FFN, D_MODEL), row),
            _const_spec((1, D_MODEL)),
            _const_spec((1, D_MODEL)),
            _const_spec(w_gu.shape),
            _const_spec(w_down.shape),
            _const_spec((1, D_MODEL)),
            _const_spec((1, D_MODEL)),
        ],
        out_specs=pl.BlockSpec((TM_FFN, D_MODEL), row),
        out_shape=jax.ShapeDtypeStruct((n_tok, D_MODEL), jnp.float32),
        compiler_params=pltpu.CompilerParams(
            dimension_semantics=(_ARB,), vmem_limit_bytes=VMEM_LIMIT_BYTES),
        name="swiglu_ffn",
    )(r2d, ln1_g, ln1_b, w_gu, w_down, ln_g, ln_b)


def kernel(x, ln_in_g, ln_in_b, rel_bias, w_in, b_gate, conv_w, conv_b, conv_ln_g, conv_ln_b,
           w_conv_out, w_attn_out, sink, w_out, ln1_g, ln1_b, w_gate, w_up, w_down, ln2_g, ln2_b):
    batch, seq, d_model = x.shape
    assert d_model == D_MODEL and w_in.shape[0] == DEPTH
    bf16 = jnp.bfloat16
    row = lambda v: v.reshape(1, -1)
    n_tok = batch * seq

    (xn, qkv, gt, dw), (w_ao, w_co, w_o, w_gu, w_d) = _inproj(
        x, row(ln_in_g), row(ln_in_b), w_in[0].astype(bf16), row(b_gate[0]), conv_w[0],
        (w_attn_out[0], w_conv_out[0], w_out[0], w_gate[0], w_up[0], w_down[0]))
    r = _mix(rel_bias, sink[0], qkv, dw, gt, xn,
             row(conv_b[0]), row(conv_ln_g[0]), row(conv_ln_b[0]), w_ao, w_co, w_o)
    out = _ffn(r.reshape(n_tok, D_MODEL), row(ln1_g[0]), row(ln1_b[0]), w_gu, w_d,
               row(ln2_g[0]), row(ln2_b[0]))
    return out.reshape(batch, seq, D_MODEL)
```

```python
import functools

import numpy as np
import jax
import jax.numpy as jnp
from jax import lax
from jax.experimental import pallas as pl
from jax.experimental.pallas import tpu as pltpu

D_MODEL = 1024
HEAD_DIM = 64
N_HEADS = 8
N_KV_HEADS = 2
WINDOW = 128
BLOCK = 128
NUM_BUCKETS = 32
MAX_DISTANCE = 128
CONV_CH = 512
CONV_WIDTH = 31
CONV_HALF = (CONV_WIDTH - 1) // 2
Q_COLS = N_HEADS * HEAD_DIM
KV_COLS = N_KV_HEADS * HEAD_DIM
CONV_COLS = 2 * CONV_CH
GATE_COLS = 2 * D_MODEL
QKV_COLS = Q_COLS + 2 * KV_COLS
DEPTH = 1
ALPHA = (2.0 * DEPTH) ** 0.25
LN_EPS = 1e-5
LOG2E = 1.4426950408889634
Q_SCALE = HEAD_DIM ** -0.5 * LOG2E
NEG_INF = -1e30
MASKED_BUCKET = NUM_BUCKETS

HEAD_GROUPS = (((0, 2, 5, 7), False), ((1, 3, 4, 6), True))
BAND = 3 * BLOCK
HALO = 16
CONV_ROWS = 64
CONV_GROUP = 4
SUBLANES = 8
BF16_SUBLANES = 16
LANES = 128
MXU_COLS = 256
CONV_SLABS = CONV_CH // LANES

VMEM_LIMIT_BYTES = 56 * 1024 * 1024

TS_IN = 512
TS_MIX = 512
TM_FFN = 1024
FFN_SUBTILE = 256

_ARB = "arbitrary"


def _layer_norm(x, g, b):
    mu = jnp.mean(x, axis=-1, keepdims=True)
    xc = x - mu
    var = jnp.mean(xc * xc, axis=-1, keepdims=True)
    return xc * lax.rsqrt(var + LN_EPS) * g + b


def _const_spec(shape):
    return pl.BlockSpec(shape, lambda *_: (0,) * len(shape), pipeline_mode=pl.Buffered(1))


def _smem_spec():
    return pl.BlockSpec(memory_space=pltpu.SMEM)


def _halo_specs(ts, seq, cols):
    per_tile = ts // HALO
    last = seq // HALO - 1
    prev = pl.BlockSpec((None, HALO, cols), lambda b, i: (b, jnp.maximum(i * per_tile - 1, 0), 0))
    nxt = pl.BlockSpec((None, HALO, cols), lambda b, i: (b, jnp.minimum((i + 1) * per_tile, last), 0))
    return prev, nxt


def _inproj_kernel(x_ref, xp_ref, xnx_ref, g_ref, b_ref, w_ref, bgate_ref,
                   convw_ref,
                   wao_ref, wco_ref, wout_ref, wgate_ref, wup_ref, wdown_ref,
                   xn_ref, qkv_ref, gt_ref, dw_ref,
                   wao_o, wco_o, wout_o, wgu_o, wdown_o,
                   xb_s, wb_s, glu_s, ge_s, go_s):
    b_id = pl.program_id(0)
    i_id = pl.program_id(1)
    n_tiles = pl.num_programs(1)
    ts = x_ref.shape[0]
    dot = functools.partial(jnp.dot, preferred_element_type=jnp.float32)
    o1 = QKV_COLS
    o2 = o1 + CONV_COLS
    o3 = o2 + D_MODEL

    @pl.when((b_id == 0) & (i_id == 0))
    def _init():
        for k in range(CONV_WIDTH):
            wb_s[k] = jnp.broadcast_to(convw_ref[k:k + 1, :].astype(wb_s.dtype), (BF16_SUBLANES, CONV_CH))
        glu_s[:, ts + 2 * HALO:, :] = jnp.zeros((CONV_SLABS, SUBLANES, LANES), jnp.float32)

    @pl.when(i_id == 0)
    def _cast_weights():
        for src, dst in ((wao_ref, wao_o), (wco_ref, wco_o), (wout_ref, wout_o), (wdown_ref, wdown_o)):
            dst[...] = src[...].astype(dst.dtype)
        for j in range(wgate_ref.shape[1] // MXU_COLS):
            src_cols = slice(j * MXU_COLS, (j + 1) * MXU_COLS)
            wgu_o[:, 2 * j * MXU_COLS:(2 * j + 1) * MXU_COLS] = wgate_ref[:, src_cols].astype(wgu_o.dtype)
            wgu_o[:, (2 * j + 1) * MXU_COLS:(2 * j + 2) * MXU_COLS] = wup_ref[:, src_cols].astype(wgu_o.dtype)

    g = g_ref[...]
    b = b_ref[...]
    xn = _layer_norm(x_ref[...], g, b)
    xn_ref[...] = xn
    xb_s[0:HALO, :] = _layer_norm(xp_ref[...], g, b).astype(xb_s.dtype)
    xb_s[HALO:HALO + ts, :] = xn.astype(xb_s.dtype)
    xb_s[HALO + ts:2 * HALO + ts, :] = _layer_norm(xnx_ref[...], g, b).astype(xb_s.dtype)

    u = dot(xb_s[...], w_ref[:, o1:o2])
    glu = u[:, :CONV_CH] * jax.nn.sigmoid(u[:, CONV_CH:])
    row = lax.broadcasted_iota(jnp.int32, (ts + 2 * HALO, 1), 0)
    pad = ((row < HALO) & (i_id == 0)) | ((row >= HALO + ts) & (i_id == n_tiles - 1))
    glu = jnp.where(pad, 0.0, glu)
    rows_ext = ts + 2 * HALO
    for c in range(CONV_SLABS):
        glu_s[c, 0:rows_ext, :] = glu[:, c * LANES:(c + 1) * LANES]
    for c in range(CONV_SLABS):
        for parity, dst in enumerate((ge_s, go_s)):
            win = glu_s[c, parity:parity + rows_ext, :].astype(jnp.bfloat16)
            dst[c] = pltpu.bitcast(win, jnp.uint32)

    xb = xb_s[HALO:HALO + ts, :]
    qkv_ref[:, :Q_COLS] = (dot(xb, w_ref[:, :Q_COLS]) * Q_SCALE).astype(qkv_ref.dtype)
    qkv_ref[:, Q_COLS:] = dot(xb, w_ref[:, Q_COLS:o1]).astype(qkv_ref.dtype)
    bgate = bgate_ref[...]
    gt_ref[:, :D_MODEL] = jax.nn.sigmoid(
        dot(xb, w_ref[:, o2:o3]) + bgate[:, :D_MODEL]).astype(gt_ref.dtype)
    gt_ref[:, D_MODEL:] = jax.nn.sigmoid(
        dot(xb, w_ref[:, o3:]) + bgate[:, D_MODEL:]).astype(gt_ref.dtype)

    first_tap = HALO - CONV_HALF

    def conv_chunk(chunk, carry):
        r0 = pl.multiple_of(chunk * CONV_ROWS, CONV_ROWS)
        w0 = pl.multiple_of(chunk * (CONV_ROWS // 2), CONV_ROWS // 2)
        n_groups = CONV_ROWS // BF16_SUBLANES
        for c in range(CONV_SLABS):
            lanes = slice(c * LANES, (c + 1) * LANES)
            accs = [jnp.zeros((BF16_SUBLANES, LANES), jnp.float32) for _ in range(n_groups)]
            for k0 in range(0, CONV_WIDTH, CONV_GROUP):
                parts = [None] * n_groups
                for k in range(k0, min(k0 + CONV_GROUP, CONV_WIDTH)):
                    w = wb_s[k, :, lanes]
                    for g in range(n_groups):
                        off = g * BF16_SUBLANES + first_tap + k
                        src = go_s if off % 2 else ge_s
                        words = src[c, pl.ds(w0 + off // 2, SUBLANES), :]
                        prod = pltpu.bitcast(words, jnp.bfloat16) * w
                        parts[g] = prod if parts[g] is None else parts[g] + prod
                for g in range(n_groups):
                    accs[g] = accs[g] + parts[g].astype(jnp.float32)
            for g in range(n_groups):
                dw_ref[pl.ds(r0 + g * BF16_SUBLANES, BF16_SUBLANES), lanes] = accs[g]
        return carry

    lax.fori_loop(0, ts // CONV_ROWS, conv_chunk, 0)


def _inproj(x, ln_g, ln_b, w_in_bf16, b_gate, conv_w, later_weights):
    batch, seq, _ = x.shape
    ts = TS_IN
    assert seq % ts == 0 and ts % CONV_ROWS == 0 and ts % HALO == 0 and HALO >= CONV_HALF
    tile = lambda b, i: (b, i, 0)
    prev_spec, next_spec = _halo_specs(ts, seq, D_MODEL)
    bf16 = jnp.bfloat16
    w_ao, w_co, w_out, w_gate, w_up, w_down = later_weights
    assert w_gate.shape == w_up.shape and w_gate.shape[1] % MXU_COLS == 0
    cast_out_shapes = [w_ao.shape, w_co.shape, w_out.shape,
                       (w_gate.shape[0], 2 * w_gate.shape[1]), w_down.shape]

    def slice_spec(shape):
        rows, cols = shape
        assert rows % (batch * BF16_SUBLANES) == 0
        return pl.BlockSpec((rows // batch, cols), lambda b, i: (b, 0))

    cast_in_specs = [slice_spec(w.shape) for w in later_weights]
    cast_out_specs = [slice_spec(s) for s in cast_out_shapes]
    outs = pl.pallas_call(
        _inproj_kernel,
        grid=(batch, seq // ts),
        in_specs=[
            pl.BlockSpec((None, ts, D_MODEL), tile),
            prev_spec,
            next_spec,
            _const_spec((1, D_MODEL)),
            _const_spec((1, D_MODEL)),
            _const_spec(w_in_bf16.shape),
            _const_spec((1, GATE_COLS)),
            _const_spec((CONV_WIDTH, CONV_CH)),
        ] + cast_in_specs,
        out_specs=[
            pl.BlockSpec((None, ts, D_MODEL), tile),
            pl.BlockSpec((None, ts, QKV_COLS), tile),
            pl.BlockSpec((None, ts, GATE_COLS), tile),
            pl.BlockSpec((None, ts, CONV_CH), tile),
        ] + cast_out_specs,
        out_shape=[
            jax.ShapeDtypeStruct((batch, seq, D_MODEL), jnp.float32),
            jax.ShapeDtypeStruct((batch, seq, QKV_COLS), bf16),
            jax.ShapeDtypeStruct((batch, seq, GATE_COLS), bf16),
            jax.ShapeDtypeStruct((batch, seq, CONV_CH), jnp.float32),
        ] + [jax.ShapeDtypeStruct(s, bf16) for s in cast_out_shapes],
        scratch_shapes=[
            pltpu.VMEM((ts + 2 * HALO, D_MODEL), bf16),
            pltpu.VMEM((CONV_WIDTH, BF16_SUBLANES, CONV_CH), bf16),
            pltpu.VMEM((CONV_SLABS, ts + 2 * HALO + SUBLANES, LANES), jnp.float32),
            pltpu.VMEM((CONV_SLABS, ts // 2 + HALO, LANES), jnp.uint32),
            pltpu.VMEM((CONV_SLABS, ts // 2 + HALO, LANES), jnp.uint32),
        ],
        compiler_params=pltpu.CompilerParams(
            dimension_semantics=(_ARB, _ARB), vmem_limit_bytes=VMEM_LIMIT_BYTES),
        name="ln_inproj_conv",
    )(x, x, x, ln_g, ln_b, w_in_bf16, b_gate, conv_w, *later_weights)
    return outs[:4], outs[4:]


def _band_buckets():
    nb = NUM_BUCKETS // 2
    qi = np.arange(BLOCK)[:, None]
    kj = np.arange(BAND)[None, :]
    rel = kj - BLOCK - qi
    ret = (rel > 0).astype(np.int32) * nb
    n = np.abs(rel)
    max_exact = nb // 2
    large = max_exact + (np.log(np.maximum(n, 1) / max_exact)
                         / np.log(MAX_DISTANCE / max_exact) * (nb - max_exact)).astype(np.int32)
    large = np.minimum(large, nb - 1)
    buckets = (ret + np.where(n < max_exact, n, large)).astype(np.int32)
    return np.where(n <= WINDOW, buckets, MASKED_BUCKET).astype(np.int32)


def _mix_kernel(n_seq_blocks,
                relb_ref, sink_ref, bk_ref,
                q_ref, kv_ref, dw_ref, gt_ref, xn_ref,
                convb_ref, clng_ref, clnb_ref,
                wao_ref, wco_ref, wout_ref,
                r_ref,
                bias_s, o_s):
    b_id = pl.program_id(0)
    i_id = pl.program_id(1)
    ts = q_ref.shape[0]
    blocks_per_tile = ts // BLOCK
    dot = functools.partial(jnp.dot, preferred_element_type=jnp.float32)

    @pl.when((b_id == 0) & (i_id == 0))
    def _init():
        bk = bk_ref[...]
        col = lax.broadcasted_iota(jnp.int32, (BLOCK, BAND), 1)
        for h in range(N_HEADS):
            def body(b, acc):
                return jnp.where(bk == b, relb_ref[b, h], acc)
            mid = LOG2E * lax.fori_loop(0, NUM_BUCKETS, body,
                                        jnp.full((BLOCK, BAND), NEG_INF, jnp.float32))
            bias_s[0, h] = jnp.where(col < BLOCK, NEG_INF, mid)
            bias_s[1, h] = mid
            bias_s[2, h] = jnp.where(col >= 2 * BLOCK, NEG_INF, mid)

    lane = lax.broadcasted_iota(jnp.int32, (BLOCK, LANES), 1)
    low_half = lane < HEAD_DIM
    nt = (((1,), (1,)), ((), ()))

    def block_operands(j):
        n = i_id * blocks_per_tile + j
        variant = jnp.where(n == 0, 0, jnp.where(n == n_seq_blocks - 1, 2, 1))
        prev_r = pl.multiple_of(jnp.maximum(n - 1, 0) * BLOCK, BLOCK)
        own_r = pl.multiple_of(n * BLOCK, BLOCK)
        next_r = pl.multiple_of(jnp.minimum(n + 1, n_seq_blocks - 1) * BLOCK, BLOCK)
        kvb = jnp.concatenate([kv_ref[pl.ds(prev_r, BLOCK), :],
                               kv_ref[pl.ds(own_r, BLOCK), :],
                               kv_ref[pl.ds(next_r, BLOCK), :]], axis=0)
        kk = kvb[:, :LANES]
        vv = kvb[:, LANES:]
        q = q_ref[j * BLOCK:(j + 1) * BLOCK, :]
        zero = jnp.zeros((BLOCK, LANES), q.dtype)
        q_head = {}
        for p in range(N_HEADS // 2):
            q2 = q[:, p * LANES:(p + 1) * LANES]
            q_head[2 * p] = jnp.where(low_half, q2, zero)
            q_head[2 * p + 1] = jnp.where(low_half, zero, q2)
        return dict(variant=variant, q_head=q_head,
                    k=(kk, pltpu.roll(kk, HEAD_DIM, axis=1)),
                    v=(vv, pltpu.roll(vv, HEAD_DIM, axis=1)))

    def scores(blk, heads, swapped):
        lhs = jnp.concatenate([blk["q_head"][h] for h in heads], axis=0)
        return lax.dot_general(lhs, blk["k"][swapped], nt, preferred_element_type=jnp.float32)

    def softmax_values(blk, heads, swapped, s_all, o_head):
        ps, inv_ls = [], []
        for r, h in enumerate(heads):
            s = s_all[r * BLOCK:(r + 1) * BLOCK, :] + bias_s[blk["variant"], h]
            sink = jnp.full((BLOCK, 1), sink_ref[h], jnp.float32) * LOG2E
            m = jnp.maximum(jnp.max(s, axis=-1, keepdims=True), sink)
            p = jnp.exp2(s - m)
            l = jnp.sum(p, axis=-1, keepdims=True) + jnp.exp2(sink - m)
            ps.append(p.astype(jnp.bfloat16))
            inv_ls.append(1.0 / l)
        r_all = dot(jnp.concatenate(ps, axis=0), blk["v"][swapped])
        for r, h in enumerate(heads):
            o_head[h] = r_all[r * BLOCK:(r + 1) * BLOCK, :] * inv_ls[r]

    stages = [(j, heads, int(swapped)) for j in range(blocks_per_tile) for heads, swapped in HEAD_GROUPS]
    yc_chunk = D_MODEL * 2 // len(stages)
    assert yc_chunk % MXU_COLS == 0 and len(stages) % 2 == 0
    yc_parts = []
    y = _layer_norm(dw_ref[...] + convb_ref[...], clng_ref[...], clnb_ref[...])
    yc = (y * jax.nn.sigmoid(y)).astype(jnp.bfloat16)
    blocks = {}
    o_heads = {}
    pending = None
    for c in range(len(stages) + 1):
        nxt = None
        if c < len(stages):
            j, heads, swapped = stages[c]
            if j not in blocks:
                blocks[j] = block_operands(j)
                o_heads[j] = {}
            nxt = (j, heads, swapped, scores(blocks[j], heads, swapped))
        if c % 2 == 1 and len(yc_parts) * yc_chunk < D_MODEL:
            c0 = len(yc_parts) * yc_chunk
            yc_parts.append(dot(yc, wco_ref[:, c0:c0 + yc_chunk]))
        if pending is not None:
            j, heads, swapped, s_all = pending
            softmax_values(blocks[j], heads, swapped, s_all, o_heads[j])
            if len(o_heads[j]) == N_HEADS:
                o_pairs = [jnp.where(low_half, o_heads[j][2 * p], o_heads[j][2 * p + 1])
                           for p in range(N_HEADS // 2)]
                o_s[j * BLOCK:(j + 1) * BLOCK, :] = jnp.concatenate(o_pairs, axis=1).astype(o_s.dtype)
        pending = nxt

    y_a = dot(o_s[...], wao_ref[...])
    y_c = jnp.concatenate(yc_parts, axis=1)
    merged = (gt_ref[:, :D_MODEL].astype(jnp.float32) * y_a
              + gt_ref[:, D_MODEL:].astype(jnp.float32) * y_c)
    mix = dot(merged.astype(jnp.bfloat16), wout_ref[...])
    r_ref[...] = ALPHA * xn_ref[...] + mix


def _mix(rel_bias, sink, qkv, dw, gt, xn, conv_b, cln_g, cln_b, w_ao, w_co, w_out):
    batch, seq, _ = xn.shape
    ts = TS_MIX
    assert seq % ts == 0 and ts % BLOCK == 0 and seq // BLOCK >= 2
    n_seq_blocks = seq // BLOCK
    buckets = jnp.asarray(_band_buckets())
    tile = lambda b, i: (b, i, 0)
    return pl.pallas_call(
        functools.partial(_mix_kernel, n_seq_blocks),
        grid=(batch, seq // ts),
        in_specs=[
            _smem_spec(),
            _smem_spec(),
            _const_spec((BLOCK, BAND)),
            pl.BlockSpec((None, ts, Q_COLS), tile),
            pl.BlockSpec((None, seq, 2 * KV_COLS), lambda b, i: (b, 0, Q_COLS // (2 * KV_COLS))),
            pl.BlockSpec((None, ts, CONV_CH), tile),
            pl.BlockSpec((None, ts, GATE_COLS), tile),
            pl.BlockSpec((None, ts, D_MODEL), tile),
            _const_spec((1, CONV_CH)),
            _const_spec((1, CONV_CH)),
            _const_spec((1, CONV_CH)),
            _const_spec((Q_COLS, D_MODEL)),
            _const_spec((CONV_CH, D_MODEL)),
            _const_spec((D_MODEL, D_MODEL)),
        ],
        out_specs=pl.BlockSpec((None, ts, D_MODEL), tile),
        out_shape=jax.ShapeDtypeStruct((batch, seq, D_MODEL), jnp.float32),
        scratch_shapes=[
            pltpu.VMEM((3, N_HEADS, BLOCK, BAND), jnp.float32),
            pltpu.VMEM((ts, Q_COLS), jnp.bfloat16),
        ],
        compiler_params=pltpu.CompilerParams(
            dimension_semantics=(_ARB, _ARB), vmem_limit_bytes=VMEM_LIMIT_BYTES),
        name="token_mix",
    )(rel_bias, sink, buckets, qkv, qkv, dw, gt, xn, conv_b, cln_g, cln_b, w_ao, w_co, w_out)


def _ffn_kernel(r_ref, g1_ref, b1_ref, wgu_ref, wd_ref, g_ref, b_ref, o_ref):
    dot = functools.partial(jnp.dot, preferred_element_type=jnp.float32)
    for r0 in range(0, r_ref.shape[0], FFN_SUBTILE):
        rows = slice(r0, r0 + FFN_SUBTILE)
        h = _layer_norm(r_ref[rows, :], g1_ref[...], b1_ref[...])
        hb = h.astype(jnp.bfloat16)
        gu = dot(hb, wgu_ref[...])
        hid = []
        for c0 in range(0, gu.shape[1], 2 * MXU_COLS):
            gate = gu[:, c0:c0 + MXU_COLS]
            up = gu[:, c0 + MXU_COLS:c0 + 2 * MXU_COLS]
            hid.append((gate * jax.nn.sigmoid(gate) * up).astype(jnp.bfloat16))
        hid = jnp.concatenate(hid, axis=1)
        o_ref[rows, :] = _layer_norm(ALPHA * h + dot(hid, wd_ref[...]), g_ref[...], b_ref[...])


def _ffn(r2d, ln1_g, ln1_b, w_gu, w_down, ln_g, ln_b):
    n_tok = r2d.shape[0]
    assert n_tok % TM_FFN == 0
    row = lambda i: (i, 0)
    return pl.pallas_call(
        _ffn_kernel,
        grid=(n_tok // TM_FFN,),
        in_specs=[
            pl.BlockSpec((TM_FFN, D_MODEL), row),
            _const_spec((1, D_MODEL)),
            _const_spec((1, D_MODEL)),
            _const_spec(w_gu.shape),
            _const_spec(w_down.shape),
            _const_spec((1, D_MODEL)),
            _const_spec((1, D_MODEL)),
        ],
        out_specs=pl.BlockSpec((TM_FFN, D_MODEL), row),
        out_shape=jax.ShapeDtypeStruct((n_tok, D_MODEL), jnp.float32),
        compiler_params=pltpu.CompilerParams(
            dimension_semantics=(_ARB,), vmem_limit_bytes=VMEM_LIMIT_BYTES),
        name="swiglu_ffn",
    )(r2d, ln1_g, ln1_b, w_gu, w_down, ln_g, ln_b)


def kernel(x, ln_in_g, ln_in_b, rel_bias, w_in, b_gate, conv_w, conv_b, conv_ln_g, conv_ln_b,
           w_conv_out, w_attn_out, sink, w_out, ln1_g, ln1_b, w_gate, w_up, w_down, ln2_g, ln2_b):
    batch, seq, d_model = x.shape
    assert d_model == D_MODEL and w_in.shape[0] == DEPTH
    bf16 = jnp.bfloat16
    row = lambda v: v.reshape(1, -1)
    n_tok = batch * seq

    (xn, qkv, gt, dw), (w_ao, w_co, w_o, w_gu, w_d) = _inproj(
        x, row(ln_in_g), row(ln_in_b), w_in[0].astype(bf16), row(b_gate[0]), conv_w[0],
        (w_attn_out[0], w_conv_out[0], w_out[0], w_gate[0], w_up[0], w_down[0]))
    r = _mix(rel_bias, sink[0], qkv, dw, gt, xn,
             row(conv_b[0]), row(conv_ln_g[0]), row(conv_ln_b[0]), w_ao, w_co, w_o)
    out = _ffn(r.reshape(n_tok, D_MODEL), row(ln1_g[0]), row(ln1_b[0]), w_gu, w_d,
               row(ln2_g[0]), row(ln2_b[0]))
    return out.reshape(batch, seq, D_MODEL)
```

```python
import functools

import numpy as np
import jax
import jax.numpy as jnp
from jax import lax
from jax.experimental import pallas as pl
from jax.experimental.pallas import tpu as pltpu

D_MODEL = 1024
HEAD_DIM = 64
N_HEADS = 8
N_KV_HEADS = 2
WINDOW = 128
BLOCK = 128
NUM_BUCKETS = 32
MAX_DISTANCE = 128
CONV_CH = 512
CONV_WIDTH = 31
CONV_HALF = (CONV_WIDTH - 1) // 2
Q_COLS = N_HEADS * HEAD_DIM
KV_COLS = N_KV_HEADS * HEAD_DIM
CONV_COLS = 2 * CONV_CH
GATE_COLS = 2 * D_MODEL
QKV_COLS = Q_COLS + 2 * KV_COLS
DEPTH = 1
ALPHA = (2.0 * DEPTH) ** 0.25
LN_EPS = 1e-5
LOG2E = 1.4426950408889634
Q_SCALE = HEAD_DIM ** -0.5 * LOG2E
NEG_INF = -1e30
MASKED_BUCKET = NUM_BUCKETS

HEAD_GROUPS = (((0, 2, 5, 7), False), ((1, 3, 4, 6), True))
BAND = 3 * BLOCK
HALO = 16
CONV_ROWS = 64
CONV_GROUP = 4
SUBLANES = 8
BF16_SUBLANES = 16
LANES = 128
MXU_COLS = 256
CONV_SLABS = CONV_CH // LANES

VMEM_LIMIT_BYTES = 56 * 1024 * 1024

TS_IN = 512
TS_MIX = 512
TM_FFN = 1024
FFN_SUBTILE = 256

_ARB = "arbitrary"


def _layer_norm(x, g, b):
    mu = jnp.mean(x, axis=-1, keepdims=True)
    xc = x - mu
    var = jnp.mean(xc * xc, axis=-1, keepdims=True)
    return xc * lax.rsqrt(var + LN_EPS) * g + b


def _const_spec(shape):
    return pl.BlockSpec(shape, lambda *_: (0,) * len(shape), pipeline_mode=pl.Buffered(1))


def _smem_spec():
    return pl.BlockSpec(memory_space=pltpu.SMEM)


def _halo_specs(ts, seq, cols):
    per_tile = ts // HALO
    last = seq // HALO - 1
    prev = pl.BlockSpec((None, HALO, cols), lambda b, i: (b, jnp.maximum(i * per_tile - 1, 0), 0))
    nxt = pl.BlockSpec((None, HALO, cols), lambda b, i: (b, jnp.minimum((i + 1) * per_tile, last), 0))
    return prev, nxt


def _inproj_kernel(x_ref, xp_ref, xnx_ref, g_ref, b_ref, w_ref, bgate_ref,
                   convw_ref,
                   wao_ref, wco_ref, wout_ref, wgate_ref, wup_ref, wdown_ref,
                   xn_ref, qkv_ref, gt_ref, dw_ref,
                   wao_o, wco_o, wout_o, wgu_o, wdown_o,
                   xb_s, wb_s, glu_s, ge_s, go_s):
    b_id = pl.program_id(0)
    i_id = pl.program_id(1)
    n_tiles = pl.num_programs(1)
    ts = x_ref.shape[0]
    dot = functools.partial(jnp.dot, preferred_element_type=jnp.float32)
    o1 = QKV_COLS
    o2 = o1 + CONV_COLS
    o3 = o2 + D_MODEL

    @pl.when((b_id == 0) & (i_id == 0))
    def _init():
        for k in range(CONV_WIDTH):
            wb_s[k] = jnp.broadcast_to(convw_ref[k:k + 1, :].astype(wb_s.dtype), (BF16_SUBLANES, CONV_CH))
        glu_s[:, ts + 2 * HALO:, :] = jnp.zeros((CONV_SLABS, SUBLANES, LANES), jnp.float32)

    @pl.when(i_id == 0)
    def _cast_weights():
        for src, dst in ((wao_ref, wao_o), (wco_ref, wco_o), (wout_ref, wout_o), (wdown_ref, wdown_o)):
            dst[...] = src[...].astype(dst.dtype)
        for j in range(wgate_ref.shape[1] // MXU_COLS):
            src_cols = slice(j * MXU_COLS, (j + 1) * MXU_COLS)
            wgu_o[:, 2 * j * MXU_COLS:(2 * j + 1) * MXU_COLS] = wgate_ref[:, src_cols].astype(wgu_o.dtype)
            wgu_o[:, (2 * j + 1) * MXU_COLS:(2 * j + 2) * MXU_COLS] = wup_ref[:, src_cols].astype(wgu_o.dtype)

    g = g_ref[...]
    b = b_ref[...]
    xn = _layer_norm(x_ref[...], g, b)
    xn_ref[...] = xn
    xb_s[0:HALO, :] = _layer_norm(xp_ref[...], g, b).astype(xb_s.dtype)
    xb_s[HALO:HALO + ts, :] = xn.astype(xb_s.dtype)
    xb_s[HALO + ts:2 * HALO + ts, :] = _layer_norm(xnx_ref[...], g, b).astype(xb_s.dtype)

    u = dot(xb_s[...], w_ref[:, o1:o2])
    glu = u[:, :CONV_CH] * jax.nn.sigmoid(u[:, CONV_CH:])
    row = lax.broadcasted_iota(jnp.int32, (ts + 2 * HALO, 1), 0)
    pad = ((row < HALO) & (i_id == 0)) | ((row >= HALO + ts) & (i_id == n_tiles - 1))
    glu = jnp.where(pad, 0.0, glu)
    rows_ext = ts + 2 * HALO
    for c in range(CONV_SLABS):
        glu_s[c, 0:rows_ext, :] = glu[:, c * LANES:(c + 1) * LANES]
    for c in range(CONV_SLABS):
        for parity, dst in enumerate((ge_s, go_s)):
            win = glu_s[c, parity:parity + rows_ext, :].astype(jnp.bfloat16)
            dst[c] = pltpu.bitcast(win, jnp.uint32)

    xb = xb_s[HALO:HALO + ts, :]
    qkv_ref[:, :Q_COLS] = (dot(xb, w_ref[:, :Q_COLS]) * Q_SCALE).astype(qkv_ref.dtype)
    qkv_ref[:, Q_COLS:] = dot(xb, w_ref[:, Q_COLS:o1]).astype(qkv_ref.dtype)
    bgate = bgate_ref[...]
    gt_ref[:, :D_MODEL] = jax.nn.sigmoid(
        dot(xb, w_ref[:, o2:o3]) + bgate[:, :D_MODEL]).astype(gt_ref.dtype)
    gt_ref[:, D_MODEL:] = jax.nn.sigmoid(
        dot(xb, w_ref[:, o3:]) + bgate[:, D_MODEL:]).astype(gt_ref.dtype)

    first_tap = HALO - CONV_HALF

    def conv_chunk(chunk, carry):
        r0 = pl.multiple_of(chunk * CONV_ROWS, CONV_ROWS)
        w0 = pl.multiple_of(chunk * (CONV_ROWS // 2), CONV_ROWS // 2)
        n_groups = CONV_ROWS // BF16_SUBLANES
        for c in range(CONV_SLABS):
            lanes = slice(c * LANES, (c + 1) * LANES)
            accs = [jnp.zeros((BF16_SUBLANES, LANES), jnp.float32) for _ in range(n_groups)]
            for k0 in range(0, CONV_WIDTH, CONV_GROUP):
                parts = [None] * n_groups
                for k in range(k0, min(k0 + CONV_GROUP, CONV_WIDTH)):
                    w = wb_s[k, :, lanes]
                    for g in range(n_groups):
                        off = g * BF16_SUBLANES + first_tap + k
                        src = go_s if off % 2 else ge_s
                        words = src[c, pl.ds(w0 + off // 2, SUBLANES), :]
                        prod = pltpu.bitcast(words, jnp.bfloat16) * w
                        parts[g] = prod if parts[g] is None else parts[g] + prod
                for g in range(n_groups):
                    accs[g] = accs[g] + parts[g].astype(jnp.float32)
            for g in range(n_groups):
                dw_ref[pl.ds(r0 + g * BF16_SUBLANES, BF16_SUBLANES), lanes] = accs[g]
        return carry

    lax.fori_loop(0, ts // CONV_ROWS, conv_chunk, 0)


def _inproj(x, ln_g, ln_b, w_in_bf16, b_gate, conv_w, later_weights):
    batch, seq, _ = x.shape
    ts = TS_IN
    assert seq % ts == 0 and ts % CONV_ROWS == 0 and ts % HALO == 0 and HALO >= CONV_HALF
    tile = lambda b, i: (b, i, 0)
    prev_spec, next_spec = _halo_specs(ts, seq, D_MODEL)
    bf16 = jnp.bfloat16
    w_ao, w_co, w_out, w_gate, w_up, w_down = later_weights
    assert w_gate.shape == w_up.shape and w_gate.shape[1] % MXU_COLS == 0
    cast_out_shapes = [w_ao.shape, w_co.shape, w_out.shape,
                       (w_gate.shape[0], 2 * w_gate.shape[1]), w_down.shape]

    def slice_spec(shape):
        rows, cols = shape
        assert rows % (batch * BF16_SUBLANES) == 0
        return pl.BlockSpec((rows // batch, cols), lambda b, i: (b, 0))

    cast_in_specs = [slice_spec(w.shape) for w in later_weights]
    cast_out_specs = [slice_spec(s) for s in cast_out_shapes]
    outs = pl.pallas_call(
        _inproj_kernel,
        grid=(batch, seq // ts),
        in_specs=[
            pl.BlockSpec((None, ts, D_MODEL), tile),
            prev_spec,
            next_spec,
            _const_spec((1, D_MODEL)),
            _const_spec((1, D_MODEL)),
            _const_spec(w_in_bf16.shape),
            _const_spec((1, GATE_COLS)),
            _const_spec((CONV_WIDTH, CONV_CH)),
        ] + cast_in_specs,
        out_specs=[
            pl.BlockSpec((None, ts, D_MODEL), tile),
            pl.BlockSpec((None, ts, QKV_COLS), tile),
            pl.BlockSpec((None, ts, GATE_COLS), tile),
            pl.BlockSpec((None, ts, CONV_CH), tile),
        ] + cast_out_specs,
        out_shape=[
            jax.ShapeDtypeStruct((batch, seq, D_MODEL), jnp.float32),
            jax.ShapeDtypeStruct((batch, seq, QKV_COLS), bf16),
            jax.ShapeDtypeStruct((batch, seq, GATE_COLS), bf16),
            jax.ShapeDtypeStruct((batch, seq, CONV_CH), jnp.float32),
        ] + [jax.ShapeDtypeStruct(s, bf16) for s in cast_out_shapes],
        scratch_shapes=[
            pltpu.VMEM((ts + 2 * HALO, D_MODEL), bf16),
            pltpu.VMEM((CONV_WIDTH, BF16_SUBLANES, CONV_CH), bf16),
            pltpu.VMEM((CONV_SLABS, ts + 2 * HALO + SUBLANES, LANES), jnp.float32),
            pltpu.VMEM((CONV_SLABS, ts // 2 + HALO, LANES), jnp.uint32),
            pltpu.VMEM((CONV_SLABS, ts // 2 + HALO, LANES), jnp.uint32),
        ],
        compiler_params=pltpu.CompilerParams(
            dimension_semantics=(_ARB, _ARB), vmem_limit_bytes=VMEM_LIMIT_BYTES),
        name="ln_inproj_conv",
    )(x, x, x, ln_g, ln_b, w_in_bf16, b_gate, conv_w, *later_weights)
    return outs[:4], outs[4:]


def _band_buckets():
    nb = NUM_BUCKETS // 2
    qi = np.arange(BLOCK)[:, None]
    kj = np.arange(BAND)[None, :]
    rel = kj - BLOCK - qi
    ret = (rel > 0).astype(np.int32) * nb
    n = np.abs(rel)
    max_exact = nb // 2
    large = max_exact + (np.log(np.maximum(n, 1) / max_exact)
                         / np.log(MAX_DISTANCE / max_exact) * (nb - max_exact)).astype(np.int32)
    large = np.minimum(large, nb - 1)
    buckets = (ret + np.where(n < max_exact, n, large)).astype(np.int32)
    return np.where(n <= WINDOW, buckets, MASKED_BUCKET).astype(np.int32)


def _mix_kernel(n_seq_blocks,
                relb_ref, sink_ref, bk_ref,
                q_ref, kv_ref, dw_ref, gt_ref, xn_ref,
                convb_ref, clng_ref, clnb_ref,
                wao_ref, wco_ref, wout_ref,
                r_ref,
                bias_s, o_s):
    b_id = pl.program_id(0)
    i_id = pl.program_id(1)
    ts = q_ref.shape[0]
    blocks_per_tile = ts // BLOCK
    dot = functools.partial(jnp.dot, preferred_element_type=jnp.float32)

    @pl.when((b_id == 0) & (i_id == 0))
    def _init():
        bk = bk_ref[...]
        col = lax.broadcasted_iota(jnp.int32, (BLOCK, BAND), 1)
        for h in range(N_HEADS):
            def body(b, acc):
                return jnp.where(bk == b, relb_ref[b, h], acc)
            mid = LOG2E * lax.fori_loop(0, NUM_BUCKETS, body,
                                        jnp.full((BLOCK, BAND), NEG_INF, jnp.float32))
            bias_s[0, h] = jnp.where(col < BLOCK, NEG_INF, mid)
            bias_s[1, h] = mid
            bias_s[2, h] = jnp.where(col >= 2 * BLOCK, NEG_INF, mid)

    lane = lax.broadcasted_iota(jnp.int32, (BLOCK, LANES), 1)
    low_half = lane < HEAD_DIM
    nt = (((1,), (1,)), ((), ()))

    def block_operands(j):
        n = i_id * blocks_per_tile + j
        variant = jnp.where(n == 0, 0, jnp.where(n == n_seq_blocks - 1, 2, 1))
        prev_r = pl.multiple_of(jnp.maximum(n - 1, 0) * BLOCK, BLOCK)
        own_r = pl.multiple_of(n * BLOCK, BLOCK)
        next_r = pl.multiple_of(jnp.minimum(n + 1, n_seq_blocks - 1) * BLOCK, BLOCK)
        kvb = jnp.concatenate([kv_ref[pl.ds(prev_r, BLOCK), :],
                               kv_ref[pl.ds(own_r, BLOCK), :],
                               kv_ref[pl.ds(next_r, BLOCK), :]], axis=0)
        kk = kvb[:, :LANES]
        vv = kvb[:, LANES:]
        q = q_ref[j * BLOCK:(j + 1) * BLOCK, :]
        zero = jnp.zeros((BLOCK, LANES), q.dtype)
        q_head = {}
        for p in range(N_HEADS // 2):
            q2 = q[:, p * LANES:(p + 1) * LANES]
            q_head[2 * p] = jnp.where(low_half, q2, zero)
            q_head[2 * p + 1] = jnp.where(low_half, zero, q2)
        return dict(variant=variant, q_head=q_head,
                    k=(kk, pltpu.roll(kk, HEAD_DIM, axis=1)),
                    v=(vv, pltpu.roll(vv, HEAD_DIM, axis=1)))

    def scores(blk, heads, swapped):
        lhs = jnp.concatenate([blk["q_head"][h] for h in heads], axis=0)
        return lax.dot_general(lhs, blk["k"][swapped], nt, preferred_element_type=jnp.float32)

    def softmax_values(blk, heads, swapped, s_all, o_head):
        ps, inv_ls = [], []
        for r, h in enumerate(heads):
            s = s_all[r * BLOCK:(r + 1) * BLOCK, :] + bias_s[blk["variant"], h]
            sink = jnp.full((BLOCK, 1), sink_ref[h], jnp.float32) * LOG2E
            m = jnp.maximum(jnp.max(s, axis=-1, keepdims=True), sink)
            p = jnp.exp2(s - m)
            l = jnp.sum(p, axis=-1, keepdims=True) + jnp.exp2(sink - m)
            ps.append(p.astype(jnp.bfloat16))
            inv_ls.append(1.0 / l)
        r_all = dot(jnp.concatenate(ps, axis=0), blk["v"][swapped])
        for r, h in enumerate(heads):
            o_head[h] = r_all[r * BLOCK:(r + 1) * BLOCK, :] * inv_ls[r]

    stages = [(j, heads, int(swapped)) for j in range(blocks_per_tile) for heads, swapped in HEAD_GROUPS]
    yc_chunk = D_MODEL * 2 // len(stages)
    assert yc_chunk % MXU_COLS == 0 and len(stages) % 2 == 0
    yc_parts = []
    y = _layer_norm(dw_ref[...] + convb_ref[...], clng_ref[...], clnb_ref[...])
    yc = (y * jax.nn.sigmoid(y)).astype(jnp.bfloat16)
    blocks = {}
    o_heads = {}
    pending = None
    for c in range(len(stages) + 1):
        nxt = None
        if c < len(stages):
            j, heads, swapped = stages[c]
            if j not in blocks:
                blocks[j] = block_operands(j)
                o_heads[j] = {}
            nxt = (j, heads, swapped, scores(blocks[j], heads, swapped))
        if c % 2 == 1 and len(yc_parts) * yc_chunk < D_MODEL:
            c0 = len(yc_parts) * yc_chunk
            yc_parts.append(dot(yc, wco_ref[:, c0:c0 + yc_chunk]))
        if pending is not None:
            j, heads, swapped, s_all = pending
            softmax_values(blocks[j], heads, swapped, s_all, o_heads[j])
            if len(o_heads[j]) == N_HEADS:
                o_pairs = [jnp.where(low_half, o_heads[j][2 * p], o_heads[j][2 * p + 1])
                           for p in range(N_HEADS // 2)]
                o_s[j * BLOCK:(j + 1) * BLOCK, :] = jnp.concatenate(o_pairs, axis=1).astype(o_s.dtype)
        pending = nxt

    y_a = dot(o_s[...], wao_ref[...])
    y_c = jnp.concatenate(yc_parts, axis=1)
    merged = (gt_ref[:, :D_MODEL].astype(jnp.float32) * y_a
              + gt_ref[:, D_MODEL:].astype(jnp.float32) * y_c)
    mix = dot(merged.astype(jnp.bfloat16), wout_ref[...])
    r_ref[...] = ALPHA * xn_ref[...] + mix


def _mix(rel_bias, sink, qkv, dw, gt, xn, conv_b, cln_g, cln_b, w_ao, w_co, w_out):
    batch, seq, _ = xn.shape
    ts = TS_MIX
    assert seq % ts == 0 and ts % BLOCK == 0 and seq // BLOCK >= 2
    n_seq_blocks = seq // BLOCK
    buckets = jnp.asarray(_band_buckets())
    tile = lambda b, i: (b, i, 0)
    return pl.pallas_call(
        functools.partial(_mix_kernel, n_seq_blocks),
        grid=(batch, seq // ts),
        in_specs=[
            _smem_spec(),
            _smem_spec(),
            _const_spec((BLOCK, BAND)),
            pl.BlockSpec((None, ts, Q_COLS), tile),
            pl.BlockSpec((None, seq, 2 * KV_COLS), lambda b, i: (b, 0, Q_COLS // (2 * KV_COLS))),
            pl.BlockSpec((None, ts, CONV_CH), tile),
            pl.BlockSpec((None, ts, GATE_COLS), tile),
            pl.BlockSpec((None, ts, D_MODEL), tile),
            _const_spec((1, CONV_CH)),
            _const_spec((1, CONV_CH)),
            _const_spec((1, CONV_CH)),
            _const_spec((Q_COLS, D_MODEL)),
            _const_spec((CONV_CH, D_MODEL)),
            _const_spec((D_MODEL, D_MODEL)),
        ],
        out_specs=pl.BlockSpec((None, ts, D_MODEL), tile),
        out_shape=jax.ShapeDtypeStruct((batch, seq, D_MODEL), jnp.float32),
        scratch_shapes=[
            pltpu.VMEM((3, N_HEADS, BLOCK, BAND), jnp.float32),
            pltpu.VMEM((ts, Q_COLS), jnp.bfloat16),
        ],
        compiler_params=pltpu.CompilerParams(
            dimension_semantics=(_ARB, _ARB), vmem_limit_bytes=VMEM_LIMIT_BYTES),
        name="token_mix",
    )(rel_bias, sink, buckets, qkv, qkv, dw, gt, xn, conv_b, cln_g, cln_b, w_ao, w_co, w_out)


def _ffn_kernel(r_ref, g1_ref, b1_ref, wgu_ref, wd_ref, g_ref, b_ref, o_ref):
    dot = functools.partial(jnp.dot, preferred_element_type=jnp.float32)
    n_sub = r_ref.shape[0] // FFN_SUBTILE
    state = {}

    def up_stage(s):
        rows = slice(s * FFN_SUBTILE, (s + 1) * FFN_SUBTILE)
        h = _layer_norm(r_ref[rows, :], g1_ref[...], b1_ref[...])
        gu = dot(h.astype(jnp.bfloat16), wgu_ref[...])
        hid = []
        for c0 in range(0, gu.shape[1], 2 * MXU_COLS):
            gate = gu[:, c0:c0 + MXU_COLS]
            up = gu[:, c0 + MXU_COLS:c0 + 2 * MXU_COLS]
            hid.append((gate * jax.nn.sigmoid(gate) * up).astype(jnp.bfloat16))
        state[s] = (h, jnp.concatenate(hid, axis=1))

    def down_stage(s):
        rows = slice(s * FFN_SUBTILE, (s + 1) * FFN_SUBTILE)
        h, hid = state.pop(s)
        o_ref[rows, :] = _layer_norm(ALPHA * h + dot(hid, wd_ref[...]), g_ref[...], b_ref[...])

    up_stage(0)
    for s in range(n_sub):
        if s + 1 < n_sub:
            up_stage(s + 1)
        down_stage(s)


def _ffn(r2d, ln1_g, ln1_b, w_gu, w_down, ln_g, ln_b):
    n_tok = r2d.shape[0]
    assert n_tok % TM_FFN == 0
    row = lambda i: (i, 0)
    return pl.pallas_call(
        _ffn_kernel,
        grid=(n_tok // TM_FFN,),
        in_specs=[
            pl.BlockSpec((TM_FFN, D_MODEL), row),
            _const_spec((1, D_MODEL)),
            _const_spec((1, D_MODEL)),
            _const_spec(w_gu.shape),
            _const_spec(w_down.shape),
            _const_spec((1, D_MODEL)),
            _const_spec((1, D_MODEL)),
        ],
        out_specs=pl.BlockSpec((TM_FFN, D_MODEL), row),
        out_shape=jax.ShapeDtypeStruct((n_tok, D_MODEL), jnp.float32),
        compiler_params=pltpu.CompilerParams(
            dimension_semantics=(_ARB,), vmem_limit_bytes=VMEM_LIMIT_BYTES),
        name="swiglu_ffn",
    )(r2d, ln1_g, ln1_b, w_gu, w_down, ln_g, ln_b)


def kernel(x, ln_in_g, ln_in_b, rel_bias, w_in, b_gate, conv_w, conv_b, conv_ln_g, conv_ln_b,
           w_conv_out, w_attn_out, sink, w_out, ln1_g, ln1_b, w_gate, w_up, w_down, ln2_g, ln2_b):
    batch, seq, d_model = x.shape
    assert d_model == D_MODEL and w_in.shape[0] == DEPTH
    bf16 = jnp.bfloat16
    row = lambda v: v.reshape(1, -1)
    n_tok = batch * seq

    (xn, qkv, gt, dw), (w_ao, w_co, w_o, w_gu, w_d) = _inproj(
        x, row(ln_in_g), row(ln_in_b), w_in[0].astype(bf16), row(b_gate[0]), conv_w[0],
        (w_attn_out[0], w_conv_out[0], w_out[0], w_gate[0], w_up[0], w_down[0]))
    r = _mix(rel_bias, sink[0], qkv, dw, gt, xn,
             row(conv_b[0]), row(conv_ln_g[0]), row(conv_ln_b[0]), w_ao, w_co, w_o)
    out = _ffn(r.reshape(n_tok, D_MODEL), row(ln1_g[0]), row(ln1_b[0]), w_gu, w_d,
               row(ln2_g[0]), row(ln2_b[0]))
    return out.reshape(batch, seq, D_MODEL)
```

```python
import functools

import numpy as np
import jax
import jax.numpy as jnp
from jax import lax
from jax.experimental import pallas as pl
from jax.experimental.pallas import tpu as pltpu

D_MODEL = 1024
HEAD_DIM = 64
N_HEADS = 8
N_KV_HEADS = 2
WINDOW = 128
BLOCK = 128
NUM_BUCKETS = 32
MAX_DISTANCE = 128
CONV_CH = 512
CONV_WIDTH = 31
CONV_HALF = (CONV_WIDTH - 1) // 2
Q_COLS = N_HEADS * HEAD_DIM
KV_COLS = N_KV_HEADS * HEAD_DIM
CONV_COLS = 2 * CONV_CH
GATE_COLS = 2 * D_MODEL
QKV_COLS = Q_COLS + 2 * KV_COLS
DEPTH = 1
ALPHA = (2.0 * DEPTH) ** 0.25
LN_EPS = 1e-5
LOG2E = 1.4426950408889634
Q_SCALE = HEAD_DIM ** -0.5 * LOG2E
NEG_INF = -1e30
MASKED_BUCKET = NUM_BUCKETS

HEAD_GROUPS = (((0, 2, 5, 7), False), ((1, 3, 4, 6), True))
BAND = 3 * BLOCK
HALO = 16
CONV_ROWS = 64
CONV_GROUP = 4
SUBLANES = 8
BF16_SUBLANES = 16
LANES = 128
MXU_COLS = 256
CONV_SLABS = CONV_CH // LANES

VMEM_LIMIT_BYTES = 56 * 1024 * 1024

TS_IN = 512
TS_MIX = 512
TM_FFN = 1024
FFN_SUBTILE = 256

_ARB = "arbitrary"


def _layer_norm(x, g, b):
    mu = jnp.mean(x, axis=-1, keepdims=True)
    xc = x - mu
    var = jnp.mean(xc * xc, axis=-1, keepdims=True)
    return xc * lax.rsqrt(var + LN_EPS) * g + b


def _const_spec(shape):
    return pl.BlockSpec(shape, lambda *_: (0,) * len(shape), pipeline_mode=pl.Buffered(1))


def _smem_spec():
    return pl.BlockSpec(memory_space=pltpu.SMEM)


def _halo_specs(ts, seq, cols):
    per_tile = ts // HALO
    last = seq // HALO - 1
    prev = pl.BlockSpec((None, HALO, cols), lambda b, i: (b, jnp.maximum(i * per_tile - 1, 0), 0))
    nxt = pl.BlockSpec((None, HALO, cols), lambda b, i: (b, jnp.minimum((i + 1) * per_tile, last), 0))
    return prev, nxt


def _inproj_kernel(x_ref, xp_ref, xnx_ref, g_ref, b_ref, w_ref, bgate_ref,
                   convw_ref,
                   wao_ref, wco_ref, wout_ref, wgate_ref, wup_ref, wdown_ref,
                   xn_ref, qkv_ref, gt_ref, dw_ref,
                   wao_o, wco_o, wout_o, wgu_o, wdown_o,
                   xb_s, wb_s, ge_s, go_s):
    b_id = pl.program_id(0)
    i_id = pl.program_id(1)
    n_tiles = pl.num_programs(1)
    ts = x_ref.shape[0]
    dot = functools.partial(jnp.dot, preferred_element_type=jnp.float32)
    o1 = QKV_COLS
    o2 = o1 + CONV_COLS
    o3 = o2 + D_MODEL

    @pl.when((b_id == 0) & (i_id == 0))
    def _init():
        for k in range(CONV_WIDTH):
            wb_s[k] = jnp.broadcast_to(convw_ref[k:k + 1, :].astype(wb_s.dtype), (BF16_SUBLANES, CONV_CH))

    @pl.when(i_id == 0)
    def _cast_weights():
        for src, dst in ((wao_ref, wao_o), (wco_ref, wco_o), (wout_ref, wout_o), (wdown_ref, wdown_o)):
            dst[...] = src[...].astype(dst.dtype)
        for j in range(wgate_ref.shape[1] // MXU_COLS):
            src_cols = slice(j * MXU_COLS, (j + 1) * MXU_COLS)
            wgu_o[:, 2 * j * MXU_COLS:(2 * j + 1) * MXU_COLS] = wgate_ref[:, src_cols].astype(wgu_o.dtype)
            wgu_o[:, (2 * j + 1) * MXU_COLS:(2 * j + 2) * MXU_COLS] = wup_ref[:, src_cols].astype(wgu_o.dtype)

    g = g_ref[...]
    b = b_ref[...]
    xn = _layer_norm(x_ref[...], g, b)
    xn_ref[...] = xn
    xb_s[0:HALO, :] = _layer_norm(xp_ref[...], g, b).astype(xb_s.dtype)
    xb_s[HALO:HALO + ts, :] = xn.astype(xb_s.dtype)
    xb_s[HALO + ts:2 * HALO + ts, :] = _layer_norm(xnx_ref[...], g, b).astype(xb_s.dtype)

    u = dot(xb_s[...], w_ref[:, o1:o2])
    glu = u[:, :CONV_CH] * jax.nn.sigmoid(u[:, CONV_CH:])
    row = lax.broadcasted_iota(jnp.int32, (ts + 2 * HALO, 1), 0)
    pad = ((row < HALO) & (i_id == 0)) | ((row >= HALO + ts) & (i_id == n_tiles - 1))
    glu = jnp.where(pad, 0.0, glu)
    rows_ext = ts + 2 * HALO
    glu_up = pltpu.roll(glu, rows_ext - 1, axis=0)
    for c in range(CONV_SLABS):
        lanes = slice(c * LANES, (c + 1) * LANES)
        ge_s[c] = pltpu.bitcast(glu[:, lanes].astype(jnp.bfloat16), jnp.uint32)
        go_s[c] = pltpu.bitcast(glu_up[:, lanes].astype(jnp.bfloat16), jnp.uint32)

    xb = xb_s[HALO:HALO + ts, :]
    bgate = bgate_ref[...]
    gt_ref[:, :D_MODEL] = jax.nn.sigmoid(
        dot(xb, w_ref[:, o2:o3]) + bgate[:, :D_MODEL]).astype(gt_ref.dtype)
    gt_ref[:, D_MODEL:] = jax.nn.sigmoid(
        dot(xb, w_ref[:, o3:]) + bgate[:, D_MODEL:]).astype(gt_ref.dtype)
    qkv_ref[:, :Q_COLS] = (dot(xb, w_ref[:, :Q_COLS]) * Q_SCALE).astype(qkv_ref.dtype)
    qkv_ref[:, Q_COLS:] = dot(xb, w_ref[:, Q_COLS:o1]).astype(qkv_ref.dtype)

    first_tap = HALO - CONV_HALF

    def conv_chunk(chunk, carry):
        r0 = pl.multiple_of(chunk * CONV_ROWS, CONV_ROWS)
        w0 = pl.multiple_of(chunk * (CONV_ROWS // 2), CONV_ROWS // 2)
        n_groups = CONV_ROWS // BF16_SUBLANES
        for c in range(CONV_SLABS):
            lanes = slice(c * LANES, (c + 1) * LANES)
            accs = [jnp.zeros((BF16_SUBLANES, LANES), jnp.float32) for _ in range(n_groups)]
            for k0 in range(0, CONV_WIDTH, CONV_GROUP):
                parts = [None] * n_groups
                for k in range(k0, min(k0 + CONV_GROUP, CONV_WIDTH)):
                    w = wb_s[k, :, lanes]
                    for g in range(n_groups):
                        off = g * BF16_SUBLANES + first_tap + k
                        src = go_s if off % 2 else ge_s
                        words = src[c, pl.ds(w0 + off // 2, SUBLANES), :]
                        prod = pltpu.bitcast(words, jnp.bfloat16) * w
                        parts[g] = prod if parts[g] is None else parts[g] + prod
                for g in range(n_groups):
                    accs[g] = accs[g] + parts[g].astype(jnp.float32)
            for g in range(n_groups):
                dw_ref[pl.ds(r0 + g * BF16_SUBLANES, BF16_SUBLANES), lanes] = accs[g]
        return carry

    lax.fori_loop(0, ts // CONV_ROWS, conv_chunk, 0)


def _inproj(x, ln_g, ln_b, w_in_bf16, b_gate, conv_w, later_weights):
    batch, seq, _ = x.shape
    ts = TS_IN
    assert seq % ts == 0 and ts % CONV_ROWS == 0 and ts % HALO == 0 and HALO >= CONV_HALF
    tile = lambda b, i: (b, i, 0)
    prev_spec, next_spec = _halo_specs(ts, seq, D_MODEL)
    bf16 = jnp.bfloat16
    w_ao, w_co, w_out, w_gate, w_up, w_down = later_weights
    assert w_gate.shape == w_up.shape and w_gate.shape[1] % MXU_COLS == 0
    cast_out_shapes = [w_ao.shape, w_co.shape, w_out.shape,
                       (w_gate.shape[0], 2 * w_gate.shape[1]), w_down.shape]

    def slice_spec(shape):
        rows, cols = shape
        assert rows % (batch * BF16_SUBLANES) == 0
        return pl.BlockSpec((rows // batch, cols), lambda b, i: (b, 0))

    cast_in_specs = [slice_spec(w.shape) for w in later_weights]
    cast_out_specs = [slice_spec(s) for s in cast_out_shapes]
    outs = pl.pallas_call(
        _inproj_kernel,
        grid=(batch, seq // ts),
        in_specs=[
            pl.BlockSpec((None, ts, D_MODEL), tile),
            prev_spec,
            next_spec,
            _const_spec((1, D_MODEL)),
            _const_spec((1, D_MODEL)),
            _const_spec(w_in_bf16.shape),
            _const_spec((1, GATE_COLS)),
            _const_spec((CONV_WIDTH, CONV_CH)),
        ] + cast_in_specs,
        out_specs=[
            pl.BlockSpec((None, ts, D_MODEL), tile),
            pl.BlockSpec((None, ts, QKV_COLS), tile),
            pl.BlockSpec((None, ts, GATE_COLS), tile),
            pl.BlockSpec((None, ts, CONV_CH), tile),
        ] + cast_out_specs,
        out_shape=[
            jax.ShapeDtypeStruct((batch, seq, D_MODEL), jnp.float32),
            jax.ShapeDtypeStruct((batch, seq, QKV_COLS), bf16),
            jax.ShapeDtypeStruct((batch, seq, GATE_COLS), bf16),
            jax.ShapeDtypeStruct((batch, seq, CONV_CH), jnp.float32),
        ] + [jax.ShapeDtypeStruct(s, bf16) for s in cast_out_shapes],
        scratch_shapes=[
            pltpu.VMEM((ts + 2 * HALO, D_MODEL), bf16),
            pltpu.VMEM((CONV_WIDTH, BF16_SUBLANES, CONV_CH), bf16),
            pltpu.VMEM((CONV_SLABS, ts // 2 + HALO, LANES), jnp.uint32),
            pltpu.VMEM((CONV_SLABS, ts // 2 + HALO, LANES), jnp.uint32),
        ],
        compiler_params=pltpu.CompilerParams(
            dimension_semantics=(_ARB, _ARB), vmem_limit_bytes=VMEM_LIMIT_BYTES),
        name="ln_inproj_conv",
    )(x, x, x, ln_g, ln_b, w_in_bf16, b_gate, conv_w, *later_weights)
    return outs[:4], outs[4:]


def _band_buckets():
    nb = NUM_BUCKETS // 2
    qi = np.arange(BLOCK)[:, None]
    kj = np.arange(BAND)[None, :]
    rel = kj - BLOCK - qi
    ret = (rel > 0).astype(np.int32) * nb
    n = np.abs(rel)
    max_exact = nb // 2
    large = max_exact + (np.log(np.maximum(n, 1) / max_exact)
                         / np.log(MAX_DISTANCE / max_exact) * (nb - max_exact)).astype(np.int32)
    large = np.minimum(large, nb - 1)
    buckets = (ret + np.where(n < max_exact, n, large)).astype(np.int32)
    return np.where(n <= WINDOW, buckets, MASKED_BUCKET).astype(np.int32)


def _mix_kernel(n_seq_blocks,
                relb_ref, sink_ref, bk_ref,
                q_ref, kv_ref, dw_ref, gt_ref, xn_ref,
                convb_ref, clng_ref, clnb_ref,
                wao_ref, wco_ref, wout_ref,
                r_ref,
                bias_s, o_s):
    b_id = pl.program_id(0)
    i_id = pl.program_id(1)
    ts = q_ref.shape[0]
    blocks_per_tile = ts // BLOCK
    dot = functools.partial(jnp.dot, preferred_element_type=jnp.float32)

    @pl.when((b_id == 0) & (i_id == 0))
    def _init():
        bk = bk_ref[...]
        col = lax.broadcasted_iota(jnp.int32, (BLOCK, BAND), 1)
        for h in range(N_HEADS):
            def body(b, acc):
                return jnp.where(bk == b, relb_ref[b, h], acc)
            mid = LOG2E * lax.fori_loop(0, NUM_BUCKETS, body,
                                        jnp.full((BLOCK, BAND), NEG_INF, jnp.float32))
            bias_s[0, h] = jnp.where(col < BLOCK, NEG_INF, mid)
            bias_s[1, h] = mid
            bias_s[2, h] = jnp.where(col >= 2 * BLOCK, NEG_INF, mid)

    lane = lax.broadcasted_iota(jnp.int32, (BLOCK, LANES), 1)
    low_half = lane < HEAD_DIM
    nt = (((1,), (1,)), ((), ()))

    def block_operands(j):
        n = i_id * blocks_per_tile + j
        variant = jnp.where(n == 0, 0, jnp.where(n == n_seq_blocks - 1, 2, 1))
        prev_r = pl.multiple_of(jnp.maximum(n - 1, 0) * BLOCK, BLOCK)
        own_r = pl.multiple_of(n * BLOCK, BLOCK)
        next_r = pl.multiple_of(jnp.minimum(n + 1, n_seq_blocks - 1) * BLOCK, BLOCK)
        kvb = jnp.concatenate([kv_ref[pl.ds(prev_r, BLOCK), :],
                               kv_ref[pl.ds(own_r, BLOCK), :],
                               kv_ref[pl.ds(next_r, BLOCK), :]], axis=0)
        kk = kvb[:, :LANES]
        vv = kvb[:, LANES:]
        q = q_ref[j * BLOCK:(j + 1) * BLOCK, :]
        zero = jnp.zeros((BLOCK, LANES), q.dtype)
        q_head = {}
        for p in range(N_HEADS // 2):
            q2 = q[:, p * LANES:(p + 1) * LANES]
            q_head[2 * p] = jnp.where(low_half, q2, zero)
            q_head[2 * p + 1] = jnp.where(low_half, zero, q2)
        return dict(variant=variant, q_head=q_head,
                    k=(kk, pltpu.roll(kk, HEAD_DIM, axis=1)),
                    v=(vv, pltpu.roll(vv, HEAD_DIM, axis=1)))

    def scores(blk, heads, swapped):
        lhs = jnp.concatenate([blk["q_head"][h] for h in heads], axis=0)
        return lax.dot_general(lhs, blk["k"][swapped], nt, preferred_element_type=jnp.float32)

    def softmax_values(blk, heads, swapped, s_all, o_head):
        ps, inv_ls = [], []
        for r, h in enumerate(heads):
            s = s_all[r * BLOCK:(r + 1) * BLOCK, :] + bias_s[blk["variant"], h]
            sink = jnp.full((BLOCK, 1), sink_ref[h], jnp.float32) * LOG2E
            m = jnp.maximum(jnp.max(s, axis=-1, keepdims=True), sink)
            p = jnp.exp2(s - m)
            l = jnp.sum(p, axis=-1, keepdims=True) + jnp.exp2(sink - m)
            ps.append(p.astype(jnp.bfloat16))
            inv_ls.append(1.0 / l)
        r_all = dot(jnp.concatenate(ps, axis=0), blk["v"][swapped])
        for r, h in enumerate(heads):
            o_head[h] = r_all[r * BLOCK:(r + 1) * BLOCK, :] * inv_ls[r]

    stages = [(j, heads, int(swapped)) for j in range(blocks_per_tile) for heads, swapped in HEAD_GROUPS]
    yc_chunk = D_MODEL * 2 // len(stages)
    assert yc_chunk % MXU_COLS == 0 and len(stages) % 2 == 0
    yc_parts = []
    y = _layer_norm(dw_ref[...] + convb_ref[...], clng_ref[...], clnb_ref[...])
    yc = (y * jax.nn.sigmoid(y)).astype(jnp.bfloat16)
    blocks = {}
    o_heads = {}
    pending = None
    for c in range(len(stages) + 1):
        nxt = None
        if c < len(stages):
            j, heads, swapped = stages[c]
            if j not in blocks:
                blocks[j] = block_operands(j)
                o_heads[j] = {}
            nxt = (j, heads, swapped, scores(blocks[j], heads, swapped))
        if c % 2 == 1 and len(yc_parts) * yc_chunk < D_MODEL:
            c0 = len(yc_parts) * yc_chunk
            yc_parts.append(dot(yc, wco_ref[:, c0:c0 + yc_chunk]))
        if pending is not None:
            j, heads, swapped, s_all = pending
            softmax_values(blocks[j], heads, swapped, s_all, o_heads[j])
            if len(o_heads[j]) == N_HEADS:
                o_pairs = [jnp.where(low_half, o_heads[j][2 * p], o_heads[j][2 * p + 1])
                           for p in range(N_HEADS // 2)]
                o_s[j * BLOCK:(j + 1) * BLOCK, :] = jnp.concatenate(o_pairs, axis=1).astype(o_s.dtype)
        pending = nxt

    y_a = dot(o_s[...], wao_ref[...])
    y_c = jnp.concatenate(yc_parts, axis=1)
    merged = (gt_ref[:, :D_MODEL].astype(jnp.float32) * y_a
              + gt_ref[:, D_MODEL:].astype(jnp.float32) * y_c)
    mix = dot(merged.astype(jnp.bfloat16), wout_ref[...])
    r_ref[...] = ALPHA * xn_ref[...] + mix


def _mix(rel_bias, sink, qkv, dw, gt, xn, conv_b, cln_g, cln_b, w_ao, w_co, w_out):
    batch, seq, _ = xn.shape
    ts = TS_MIX
    assert seq % ts == 0 and ts % BLOCK == 0 and seq // BLOCK >= 2
    n_seq_blocks = seq // BLOCK
    buckets = jnp.asarray(_band_buckets())
    tile = lambda b, i: (b, i, 0)
    return pl.pallas_call(
        functools.partial(_mix_kernel, n_seq_blocks),
        grid=(batch, seq // ts),
        in_specs=[
            _smem_spec(),
            _smem_spec(),
            _const_spec((BLOCK, BAND)),
            pl.BlockSpec((None, ts, Q_COLS), tile),
            pl.BlockSpec((None, seq, 2 * KV_COLS), lambda b, i: (b, 0, Q_COLS // (2 * KV_COLS))),
            pl.BlockSpec((None, ts, CONV_CH), tile),
            pl.BlockSpec((None, ts, GATE_COLS), tile),
            pl.BlockSpec((None, ts, D_MODEL), tile),
            _const_spec((1, CONV_CH)),
            _const_spec((1, CONV_CH)),
            _const_spec((1, CONV_CH)),
            _const_spec((Q_COLS, D_MODEL)),
            _const_spec((CONV_CH, D_MODEL)),
            _const_spec((D_MODEL, D_MODEL)),
        ],
        out_specs=pl.BlockSpec((None, ts, D_MODEL), tile),
        out_shape=jax.ShapeDtypeStruct((batch, seq, D_MODEL), jnp.float32),
        scratch_shapes=[
            pltpu.VMEM((3, N_HEADS, BLOCK, BAND), jnp.float32),
            pltpu.VMEM((ts, Q_COLS), jnp.bfloat16),
        ],
        compiler_params=pltpu.CompilerParams(
            dimension_semantics=(_ARB, _ARB), vmem_limit_bytes=VMEM_LIMIT_BYTES),
        name="token_mix",
    )(rel_bias, sink, buckets, qkv, qkv, dw, gt, xn, conv_b, cln_g, cln_b, w_ao, w_co, w_out)


def _ffn_kernel(r_ref, g1_ref, b1_ref, wgu_ref, wd_ref, g_ref, b_ref, o_ref):
    dot = functools.partial(jnp.dot, preferred_element_type=jnp.float32)
    n_sub = r_ref.shape[0] // FFN_SUBTILE
    state = {}

    def up_stage(s):
        rows = slice(s * FFN_SUBTILE, (s + 1) * FFN_SUBTILE)
        h = _layer_norm(r_ref[rows, :], g1_ref[...], b1_ref[...])
        gu = dot(h.astype(jnp.bfloat16), wgu_ref[...])
        hid = []
        for c0 in range(0, gu.shape[1], 2 * MXU_COLS):
            gate = gu[:, c0:c0 + MXU_COLS]
            up = gu[:, c0 + MXU_COLS:c0 + 2 * MXU_COLS]
            hid.append((gate * jax.nn.sigmoid(gate) * up).astype(jnp.bfloat16))
        state[s] = (h, jnp.concatenate(hid, axis=1))

    def down_stage(s):
        rows = slice(s * FFN_SUBTILE, (s + 1) * FFN_SUBTILE)
        h, hid = state.pop(s)
        o_ref[rows, :] = _layer_norm(ALPHA * h + dot(hid, wd_ref[...]), g_ref[...], b_ref[...])

    up_stage(0)
    for s in range(n_sub):
        if s + 1 < n_sub:
            up_stage(s + 1)
        down_stage(s)


def _ffn(r2d, ln1_g, ln1_b, w_gu, w_down, ln_g, ln_b):
    n_tok = r2d.shape[0]
    assert n_tok % TM_FFN == 0
    row = lambda i: (i, 0)
    return pl.pallas_call(
        _ffn_kernel,
        grid=(n_tok // TM_FFN,),
        in_specs=[
            pl.BlockSpec((TM_FFN, D_MODEL), row),
            _const_spec((1, D_MODEL)),
            _const_spec((1, D_MODEL)),
            _const_spec(w_gu.shape),
            _const_spec(w_down.shape),
            _const_spec((1, D_MODEL)),
            _const_spec((1, D_MODEL)),
        ],
        out_specs=pl.BlockSpec((TM_FFN, D_MODEL), row),
        out_shape=jax.ShapeDtypeStruct((n_tok, D_MODEL), jnp.float32),
        compiler_params=pltpu.CompilerParams(
            dimension_semantics=(_ARB,), vmem_limit_bytes=VMEM_LIMIT_BYTES),
        name="swiglu_ffn",
    )(r2d, ln1_g, ln1_b, w_gu, w_down, ln_g, ln_b)


def kernel(x, ln_in_g, ln_in_b, rel_bias, w_in, b_gate, conv_w, conv_b, conv_ln_g, conv_ln_b,
           w_conv_out, w_attn_out, sink, w_out, ln1_g, ln1_b, w_gate, w_up, w_down, ln2_g, ln2_b):
    batch, seq, d_model = x.shape
    assert d_model == D_MODEL and w_in.shape[0] == DEPTH
    bf16 = jnp.bfloat16
    row = lambda v: v.reshape(1, -1)
    n_tok = batch * seq

    (xn, qkv, gt, dw), (w_ao, w_co, w_o, w_gu, w_d) = _inproj(
        x, row(ln_in_g), row(ln_in_b), w_in[0].astype(bf16), row(b_gate[0]), conv_w[0],
        (w_attn_out[0], w_conv_out[0], w_out[0], w_gate[0], w_up[0], w_down[0]))
    r = _mix(rel_bias, sink[0], qkv, dw, gt, xn,
             row(conv_b[0]), row(conv_ln_g[0]), row(conv_ln_b[0]), w_ao, w_co, w_o)
    out = _ffn(r.reshape(n_tok, D_MODEL), row(ln1_g[0]), row(ln1_b[0]), w_gu, w_d,
               row(ln2_g[0]), row(ln2_b[0]))
    return out.reshape(batch, seq, D_MODEL)
```

```python
import functools

import numpy as np
import jax
import jax.numpy as jnp
from jax import lax
from jax.experimental import pallas as pl
from jax.experimental.pallas import tpu as pltpu

D_MODEL = 1024
HEAD_DIM = 64
N_HEADS = 8
N_KV_HEADS = 2
WINDOW = 128
BLOCK = 128
NUM_BUCKETS = 32
MAX_DISTANCE = 128
CONV_CH = 512
CONV_WIDTH = 31
CONV_HALF = (CONV_WIDTH - 1) // 2
Q_COLS = N_HEADS * HEAD_DIM
KV_COLS = N_KV_HEADS * HEAD_DIM
CONV_COLS = 2 * CONV_CH
GATE_COLS = 2 * D_MODEL
QKV_COLS = Q_COLS + 2 * KV_COLS
DEPTH = 1
ALPHA = (2.0 * DEPTH) ** 0.25
LN_EPS = 1e-5
LOG2E = 1.4426950408889634
Q_SCALE = HEAD_DIM ** -0.5 * LOG2E
NEG_INF = -1e30
MASKED_BUCKET = NUM_BUCKETS

HEAD_GROUPS = (((0, 2, 5, 7), False), ((1, 3, 4, 6), True))
BAND = 3 * BLOCK
HALO = 16
CONV_ROWS = 64
CONV_GROUP = 4
SUBLANES = 8
BF16_SUBLANES = 16
LANES = 128
MXU_COLS = 256
CONV_SLABS = CONV_CH // LANES

VMEM_LIMIT_BYTES = 56 * 1024 * 1024

TS_IN = 512
TS_MIX = 512
TM_FFN = 1024
FFN_SUBTILE = 256

_ARB = "arbitrary"


def _layer_norm(x, g, b):
    mu = jnp.mean(x, axis=-1, keepdims=True)
    xc = x - mu
    var = jnp.mean(xc * xc, axis=-1, keepdims=True)
    return xc * lax.rsqrt(var + LN_EPS) * g + b


def _const_spec(shape):
    return pl.BlockSpec(shape, lambda *_: (0,) * len(shape), pipeline_mode=pl.Buffered(1))


def _smem_spec():
    return pl.BlockSpec(memory_space=pltpu.SMEM)


def _halo_specs(ts, seq, cols):
    per_tile = ts // HALO
    last = seq // HALO - 1
    prev = pl.BlockSpec((None, HALO, cols), lambda b, i: (b, jnp.maximum(i * per_tile - 1, 0), 0))
    nxt = pl.BlockSpec((None, HALO, cols), lambda b, i: (b, jnp.minimum((i + 1) * per_tile, last), 0))
    return prev, nxt


def _inproj_kernel(x_ref, xp_ref, xnx_ref, g_ref, b_ref, w_ref, bgate_ref,
                   convw_ref,
                   wao_ref, wco_ref, wout_ref, wgate_ref, wup_ref, wdown_ref,
                   xn_ref, qkv_ref, gt_ref, dw_ref,
                   wao_o, wco_o, wout_o, wgu_o, wdown_o,
                   xb_s, wb_s, ge_s, go_s):
    b_id = pl.program_id(0)
    i_id = pl.program_id(1)
    n_tiles = pl.num_programs(1)
    ts = x_ref.shape[0]
    dot = functools.partial(jnp.dot, preferred_element_type=jnp.float32)
    o1 = QKV_COLS
    o2 = o1 + CONV_COLS
    o3 = o2 + D_MODEL

    @pl.when((b_id == 0) & (i_id == 0))
    def _init():
        for k in range(CONV_WIDTH):
            wb_s[k] = jnp.broadcast_to(convw_ref[k:k + 1, :].astype(wb_s.dtype), (BF16_SUBLANES, CONV_CH))

    @pl.when(i_id == 0)
    def _cast_weights():
        for src, dst in ((wao_ref, wao_o), (wco_ref, wco_o), (wout_ref, wout_o), (wdown_ref, wdown_o)):
            dst[...] = src[...].astype(dst.dtype)
        for j in range(wgate_ref.shape[1] // MXU_COLS):
            src_cols = slice(j * MXU_COLS, (j + 1) * MXU_COLS)
            wgu_o[:, 2 * j * MXU_COLS:(2 * j + 1) * MXU_COLS] = wgate_ref[:, src_cols].astype(wgu_o.dtype)
            wgu_o[:, (2 * j + 1) * MXU_COLS:(2 * j + 2) * MXU_COLS] = wup_ref[:, src_cols].astype(wgu_o.dtype)

    g = g_ref[...]
    b = b_ref[...]
    xn = _layer_norm(x_ref[...], g, b)
    xn_ref[...] = xn
    xb_s[0:HALO, :] = _layer_norm(xp_ref[...], g, b).astype(xb_s.dtype)
    xb_s[HALO:HALO + ts, :] = xn.astype(xb_s.dtype)
    xb_s[HALO + ts:2 * HALO + ts, :] = _layer_norm(xnx_ref[...], g, b).astype(xb_s.dtype)

    u = dot(xb_s[...], w_ref[:, o1:o2])
    glu = u[:, :CONV_CH] * jax.nn.sigmoid(u[:, CONV_CH:])
    row = lax.broadcasted_iota(jnp.int32, (ts + 2 * HALO, 1), 0)
    pad = ((row < HALO) & (i_id == 0)) | ((row >= HALO + ts) & (i_id == n_tiles - 1))
    glu = jnp.where(pad, 0.0, glu)
    rows_ext = ts + 2 * HALO
    glu_up = pltpu.roll(glu, rows_ext - 1, axis=0)
    for c in range(CONV_SLABS):
        lanes = slice(c * LANES, (c + 1) * LANES)
        ge_s[c] = pltpu.bitcast(glu[:, lanes].astype(jnp.bfloat16), jnp.uint32)
        go_s[c] = pltpu.bitcast(glu_up[:, lanes].astype(jnp.bfloat16), jnp.uint32)

    xb = xb_s[HALO:HALO + ts, :]
    bgate = bgate_ref[...]
    gt_ref[:, :D_MODEL] = jax.nn.sigmoid(
        dot(xb, w_ref[:, o2:o3]) + bgate[:, :D_MODEL]).astype(gt_ref.dtype)
    gt_ref[:, D_MODEL:] = jax.nn.sigmoid(
        dot(xb, w_ref[:, o3:]) + bgate[:, D_MODEL:]).astype(gt_ref.dtype)
    qkv_ref[:, :Q_COLS] = (dot(xb, w_ref[:, :Q_COLS]) * Q_SCALE).astype(qkv_ref.dtype)
    qkv_ref[:, Q_COLS:] = dot(xb, w_ref[:, Q_COLS:o1]).astype(qkv_ref.dtype)

    first_tap = HALO - CONV_HALF

    def conv_chunk(chunk, carry):
        r0 = pl.multiple_of(chunk * CONV_ROWS, CONV_ROWS)
        w0 = pl.multiple_of(chunk * (CONV_ROWS // 2), CONV_ROWS // 2)
        n_groups = CONV_ROWS // BF16_SUBLANES
        for c in range(CONV_SLABS):
            lanes = slice(c * LANES, (c + 1) * LANES)
            accs = [jnp.zeros((BF16_SUBLANES, LANES), jnp.float32) for _ in range(n_groups)]
            for k0 in range(0, CONV_WIDTH, CONV_GROUP):
                parts = [None] * n_groups
                for k in range(k0, min(k0 + CONV_GROUP, CONV_WIDTH)):
                    w = wb_s[k, :, lanes]
                    for g in range(n_groups):
                        off = g * BF16_SUBLANES + first_tap + k
                        src = go_s if off % 2 else ge_s
                        words = src[c, pl.ds(w0 + off // 2, SUBLANES), :]
                        prod = pltpu.bitcast(words, jnp.bfloat16) * w
                        parts[g] = prod if parts[g] is None else parts[g] + prod
                for g in range(n_groups):
                    accs[g] = accs[g] + parts[g].astype(jnp.float32)
            for g in range(n_groups):
                dw_ref[pl.ds(r0 + g * BF16_SUBLANES, BF16_SUBLANES), lanes] = accs[g]
        return carry

    lax.fori_loop(0, ts // CONV_ROWS, conv_chunk, 0)


def _inproj(x, ln_g, ln_b, w_in_bf16, b_gate, conv_w, later_weights):
    batch, seq, _ = x.shape
    ts = TS_IN
    assert seq % ts == 0 and ts % CONV_ROWS == 0 and ts % HALO == 0 and HALO >= CONV_HALF
    tile = lambda b, i: (b, i, 0)
    prev_spec, next_spec = _halo_specs(ts, seq, D_MODEL)
    bf16 = jnp.bfloat16
    w_ao, w_co, w_out, w_gate, w_up, w_down = later_weights
    assert w_gate.shape == w_up.shape and w_gate.shape[1] % MXU_COLS == 0
    cast_out_shapes = [w_ao.shape, w_co.shape, w_out.shape,
                       (w_gate.shape[0], 2 * w_gate.shape[1]), w_down.shape]

    def slice_spec(shape):
        rows, cols = shape
        assert rows % (batch * BF16_SUBLANES) == 0
        return pl.BlockSpec((rows // batch, cols), lambda b, i: (b, 0))

    cast_in_specs = [slice_spec(w.shape) for w in later_weights]
    cast_out_specs = [slice_spec(s) for s in cast_out_shapes]
    outs = pl.pallas_call(
        _inproj_kernel,
        grid=(batch, seq // ts),
        in_specs=[
            pl.BlockSpec((None, ts, D_MODEL), tile),
            prev_spec,
            next_spec,
            _const_spec((1, D_MODEL)),
            _const_spec((1, D_MODEL)),
            _const_spec(w_in_bf16.shape),
            _const_spec((1, GATE_COLS)),
            _const_spec((CONV_WIDTH, CONV_CH)),
        ] + cast_in_specs,
        out_specs=[
            pl.BlockSpec((None, ts, D_MODEL), tile),
            pl.BlockSpec((None, ts, QKV_COLS), tile),
            pl.BlockSpec((None, ts, GATE_COLS), tile),
            pl.BlockSpec((None, ts, CONV_CH), tile),
        ] + cast_out_specs,
        out_shape=[
            jax.ShapeDtypeStruct((batch, seq, D_MODEL), jnp.float32),
            jax.ShapeDtypeStruct((batch, seq, QKV_COLS), bf16),
            jax.ShapeDtypeStruct((batch, seq, GATE_COLS), bf16),
            jax.ShapeDtypeStruct((batch, seq, CONV_CH), jnp.float32),
        ] + [jax.ShapeDtypeStruct(s, bf16) for s in cast_out_shapes],
        scratch_shapes=[
            pltpu.VMEM((ts + 2 * HALO, D_MODEL), bf16),
            pltpu.VMEM((CONV_WIDTH, BF16_SUBLANES, CONV_CH), bf16),
            pltpu.VMEM((CONV_SLABS, ts // 2 + HALO, LANES), jnp.uint32),
            pltpu.VMEM((CONV_SLABS, ts // 2 + HALO, LANES), jnp.uint32),
        ],
        compiler_params=pltpu.CompilerParams(
            dimension_semantics=(_ARB, _ARB), vmem_limit_bytes=VMEM_LIMIT_BYTES),
        name="ln_inproj_conv",
    )(x, x, x, ln_g, ln_b, w_in_bf16, b_gate, conv_w, *later_weights)
    return outs[:4], outs[4:]


def _band_buckets():
    nb = NUM_BUCKETS // 2
    qi = np.arange(BLOCK)[:, None]
    kj = np.arange(BAND)[None, :]
    rel = kj - BLOCK - qi
    ret = (rel > 0).astype(np.int32) * nb
    n = np.abs(rel)
    max_exact = nb // 2
    large = max_exact + (np.log(np.maximum(n, 1) / max_exact)
                         / np.log(MAX_DISTANCE / max_exact) * (nb - max_exact)).astype(np.int32)
    large = np.minimum(large, nb - 1)
    buckets = (ret + np.where(n < max_exact, n, large)).astype(np.int32)
    return np.where(n <= WINDOW, buckets, MASKED_BUCKET).astype(np.int32).T.copy()


def _mix_kernel(n_seq_blocks,
                relb_ref, sink_ref, bk_ref,
                q_ref, kv_ref, dw_ref, gt_ref, xn_ref,
                convb_ref, clng_ref, clnb_ref,
                wao_ref, wco_ref, wout_ref,
                r_ref,
                bias_s, o_s):
    b_id = pl.program_id(0)
    i_id = pl.program_id(1)
    ts = q_ref.shape[0]
    blocks_per_tile = ts // BLOCK
    dot = functools.partial(jnp.dot, preferred_element_type=jnp.float32)

    @pl.when((b_id == 0) & (i_id == 0))
    def _init():
        bk = bk_ref[...]
        key = lax.broadcasted_iota(jnp.int32, (BAND, BLOCK), 0)
        for h in range(N_HEADS):
            def body(b, acc):
                return jnp.where(bk == b, relb_ref[b, h], acc)
            mid = LOG2E * lax.fori_loop(0, NUM_BUCKETS, body,
                                        jnp.full((BAND, BLOCK), NEG_INF, jnp.float32))
            bias_s[0, h] = jnp.where(key < BLOCK, NEG_INF, mid)
            bias_s[1, h] = mid
            bias_s[2, h] = jnp.where(key >= 2 * BLOCK, NEG_INF, mid)

    lane = lax.broadcasted_iota(jnp.int32, (BLOCK, LANES), 1)
    low_half = lane < HEAD_DIM
    nt = (((1,), (1,)), ((), ()))

    def block_operands(j):
        n = i_id * blocks_per_tile + j
        variant = jnp.where(n == 0, 0, jnp.where(n == n_seq_blocks - 1, 2, 1))
        prev_r = pl.multiple_of(jnp.maximum(n - 1, 0) * BLOCK, BLOCK)
        own_r = pl.multiple_of(n * BLOCK, BLOCK)
        next_r = pl.multiple_of(jnp.minimum(n + 1, n_seq_blocks - 1) * BLOCK, BLOCK)
        kvb = jnp.concatenate([kv_ref[pl.ds(prev_r, BLOCK), :],
                               kv_ref[pl.ds(own_r, BLOCK), :],
                               kv_ref[pl.ds(next_r, BLOCK), :]], axis=0)
        kk = kvb[:, :LANES]
        vv = kvb[:, LANES:]
        q = q_ref[j * BLOCK:(j + 1) * BLOCK, :]
        zero = jnp.zeros((BLOCK, LANES), q.dtype)
        q_head = {}
        for p in range(N_HEADS // 2):
            q2 = q[:, p * LANES:(p + 1) * LANES]
            q_head[2 * p] = jnp.where(low_half, q2, zero)
            q_head[2 * p + 1] = jnp.where(low_half, zero, q2)
        return dict(variant=variant, q_head=q_head, k=(kk, pltpu.roll(kk, HEAD_DIM, axis=1)), v=vv)

    def scores(blk, heads, swapped):
        q_all = jnp.concatenate([blk["q_head"][h] for h in heads], axis=0)
        return lax.dot_general(blk["k"][swapped], q_all, nt,
                               preferred_element_type=jnp.float32)

    def softmax_values(blk, heads, swapped, s_all, o_head):
        ps, inv_ls = [], []
        for r, h in enumerate(heads):
            s = s_all[:, r * BLOCK:(r + 1) * BLOCK] + bias_s[blk["variant"], h]
            sink = jnp.full((1, BLOCK), sink_ref[h], jnp.float32) * LOG2E
            m = jnp.maximum(jnp.max(s, axis=0, keepdims=True), sink)
            p = jnp.exp2(s - m)
            l = jnp.sum(p, axis=0, keepdims=True) + jnp.exp2(sink - m)
            ps.append(p.astype(jnp.bfloat16))
            inv_ls.append(1.0 / l)
        r_all = lax.dot_general(blk["v"], jnp.concatenate(ps, axis=1), (((0,), (0,)), ((), ())),
                                preferred_element_type=jnp.float32)
        for r, h in enumerate(heads):
            g0 = (h // (N_HEADS // N_KV_HEADS)) * HEAD_DIM
            o_head[h] = r_all[g0:g0 + HEAD_DIM, r * BLOCK:(r + 1) * BLOCK] * inv_ls[r]

    stages = [(j, heads, int(swapped)) for j in range(blocks_per_tile) for heads, swapped in HEAD_GROUPS]
    yc_chunk = D_MODEL * 2 // len(stages)
    assert yc_chunk % MXU_COLS == 0 and len(stages) % 2 == 0
    yc_parts = []
    y = _layer_norm(dw_ref[...] + convb_ref[...], clng_ref[...], clnb_ref[...])
    yc = (y * jax.nn.sigmoid(y)).astype(jnp.bfloat16)
    blocks = {}
    o_heads = {}
    pending = None
    for c in range(len(stages) + 1):
        nxt = None
        if c < len(stages):
            j, heads, swapped = stages[c]
            if j not in blocks:
                blocks[j] = block_operands(j)
                o_heads[j] = {}
            nxt = (j, heads, swapped, scores(blocks[j], heads, swapped))
        if c % 2 == 1 and len(yc_parts) * yc_chunk < D_MODEL:
            c0 = len(yc_parts) * yc_chunk
            yc_parts.append(dot(yc, wco_ref[:, c0:c0 + yc_chunk]))
        if pending is not None:
            j, heads, swapped, s_all = pending
            softmax_values(blocks[j], heads, swapped, s_all, o_heads[j])
            if len(o_heads[j]) == N_HEADS:
                o_pairs = [jnp.concatenate([o_heads[j][2 * p], o_heads[j][2 * p + 1]], axis=0).T
                           for p in range(N_HEADS // 2)]
                o_s[j * BLOCK:(j + 1) * BLOCK, :] = jnp.concatenate(o_pairs, axis=1).astype(o_s.dtype)
        pending = nxt

    y_a = dot(o_s[...], wao_ref[...])
    y_c = jnp.concatenate(yc_parts, axis=1)
    merged = (gt_ref[:, :D_MODEL].astype(jnp.float32) * y_a
              + gt_ref[:, D_MODEL:].astype(jnp.float32) * y_c)
    mix = dot(merged.astype(jnp.bfloat16), wout_ref[...])
    r_ref[...] = ALPHA * xn_ref[...] + mix


def _mix(rel_bias, sink, qkv, dw, gt, xn, conv_b, cln_g, cln_b, w_ao, w_co, w_out):
    batch, seq, _ = xn.shape
    ts = TS_MIX
    assert seq % ts == 0 and ts % BLOCK == 0 and seq // BLOCK >= 2
    n_seq_blocks = seq // BLOCK
    buckets = jnp.asarray(_band_buckets())
    tile = lambda b, i: (b, i, 0)
    return pl.pallas_call(
        functools.partial(_mix_kernel, n_seq_blocks),
        grid=(batch, seq // ts),
        in_specs=[
            _smem_spec(),
            _smem_spec(),
            _const_spec((BAND, BLOCK)),
            pl.BlockSpec((None, ts, Q_COLS), tile),
            pl.BlockSpec((None, seq, 2 * KV_COLS), lambda b, i: (b, 0, Q_COLS // (2 * KV_COLS))),
            pl.BlockSpec((None, ts, CONV_CH), tile),
            pl.BlockSpec((None, ts, GATE_COLS), tile),
            pl.BlockSpec((None, ts, D_MODEL), tile),
            _const_spec((1, CONV_CH)),
            _const_spec((1, CONV_CH)),
            _const_spec((1, CONV_CH)),
            _const_spec((Q_COLS, D_MODEL)),
            _const_spec((CONV_CH, D_MODEL)),
            _const_spec((D_MODEL, D_MODEL)),
        ],
        out_specs=pl.BlockSpec((None, ts, D_MODEL), tile),
        out_shape=jax.ShapeDtypeStruct((batch, seq, D_MODEL), jnp.float32),
        scratch_shapes=[
            pltpu.VMEM((3, N_HEADS, BAND, BLOCK), jnp.float32),
            pltpu.VMEM((ts, Q_COLS), jnp.bfloat16),
        ],
        compiler_params=pltpu.CompilerParams(
            dimension_semantics=(_ARB, _ARB), vmem_limit_bytes=VMEM_LIMIT_BYTES),
        name="token_mix",
    )(rel_bias, sink, buckets, qkv, qkv, dw, gt, xn, conv_b, cln_g, cln_b, w_ao, w_co, w_out)


def _ffn_kernel(r_ref, g1_ref, b1_ref, wgu_ref, wd_ref, g_ref, b_ref, o_ref):
    dot = functools.partial(jnp.dot, preferred_element_type=jnp.float32)
    n_sub = r_ref.shape[0] // FFN_SUBTILE
    state = {}

    def up_stage(s):
        rows = slice(s * FFN_SUBTILE, (s + 1) * FFN_SUBTILE)
        h = _layer_norm(r_ref[rows, :], g1_ref[...], b1_ref[...])
        gu = dot(h.astype(jnp.bfloat16), wgu_ref[...])
        hid = []
        for c0 in range(0, gu.shape[1], 2 * MXU_COLS):
            gate = gu[:, c0:c0 + MXU_COLS]
            up = gu[:, c0 + MXU_COLS:c0 + 2 * MXU_COLS]
            hid.append((gate * jax.nn.sigmoid(gate) * up).astype(jnp.bfloat16))
        state[s] = (h, jnp.concatenate(hid, axis=1))

    def down_stage(s):
        rows = slice(s * FFN_SUBTILE, (s + 1) * FFN_SUBTILE)
        h, hid = state.pop(s)
        o_ref[rows, :] = _layer_norm(ALPHA * h + dot(hid, wd_ref[...]), g_ref[...], b_ref[...])

    up_stage(0)
    for s in range(n_sub):
        if s + 1 < n_sub:
            up_stage(s + 1)
        down_stage(s)


def _ffn(r2d, ln1_g, ln1_b, w_gu, w_down, ln_g, ln_b):
    n_tok = r2d.shape[0]
    assert n_tok % TM_FFN == 0
    row = lambda i: (i, 0)
    return pl.pallas_call(
        _ffn_kernel,
        grid=(n_tok // TM_FFN,),
        in_specs=[
            pl.BlockSpec((TM_FFN, D_MODEL), row),
            _const_spec((1, D_MODEL)),
            _const_spec((1, D_MODEL)),
            _const_spec(w_gu.shape),
            _const_spec(w_down.shape),
            _const_spec((1, D_MODEL)),
            _const_spec((1, D_MODEL)),
        ],
        out_specs=pl.BlockSpec((TM_FFN, D_MODEL), row),
        out_shape=jax.ShapeDtypeStruct((n_tok, D_MODEL), jnp.float32),
        compiler_params=pltpu.CompilerParams(
            dimension_semantics=(_ARB,), vmem_limit_bytes=VMEM_LIMIT_BYTES),
        name="swiglu_ffn",
    )(r2d, ln1_g, ln1_b, w_gu, w_down, ln_g, ln_b)


def kernel(x, ln_in_g, ln_in_b, rel_bias, w_in, b_gate, conv_w, conv_b, conv_ln_g, conv_ln_b,
           w_conv_out, w_attn_out, sink, w_out, ln1_g, ln1_b, w_gate, w_up, w_down, ln2_g, ln2_b):
    batch, seq, d_model = x.shape
    assert d_model == D_MODEL and w_in.shape[0] == DEPTH
    bf16 = jnp.bfloat16
    row = lambda v: v.reshape(1, -1)
    n_tok = batch * seq

    (xn, qkv, gt, dw), (w_ao, w_co, w_o, w_gu, w_d) = _inproj(
        x, row(ln_in_g), row(ln_in_b), w_in[0].astype(bf16), row(b_gate[0]), conv_w[0],
        (w_attn_out[0], w_conv_out[0], w_out[0], w_gate[0], w_up[0], w_down[0]))
    r = _mix(rel_bias, sink[0], qkv, dw, gt, xn,
             row(conv_b[0]), row(conv_ln_g[0]), row(conv_ln_b[0]), w_ao, w_co, w_o)
    out = _ffn(r.reshape(n_tok, D_MODEL), row(ln1_g[0]), row(ln1_b[0]), w_gu, w_d,
               row(ln2_g[0]), row(ln2_b[0]))
    return out.reshape(batch, seq, D_MODEL)
```

```python
import functools

import numpy as np
import jax
import jax.numpy as jnp
from jax import lax
from jax.experimental import pallas as pl
from jax.experimental.pallas import tpu as pltpu

D_MODEL = 1024
HEAD_DIM = 64
N_HEADS = 8
N_KV_HEADS = 2
WINDOW = 128
BLOCK = 128
NUM_BUCKETS = 32
MAX_DISTANCE = 128
CONV_CH = 512
CONV_WIDTH = 31
CONV_HALF = (CONV_WIDTH - 1) // 2
Q_COLS = N_HEADS * HEAD_DIM
KV_COLS = N_KV_HEADS * HEAD_DIM
CONV_COLS = 2 * CONV_CH
GATE_COLS = 2 * D_MODEL
QKV_COLS = Q_COLS + 2 * KV_COLS
DEPTH = 1
ALPHA = (2.0 * DEPTH) ** 0.25
LN_EPS = 1e-5
LOG2E = 1.4426950408889634
Q_SCALE = HEAD_DIM ** -0.5 * LOG2E
NEG_INF = -1e30
MASKED_BUCKET = NUM_BUCKETS

HEAD_GROUPS = (((0, 2, 5, 7), False), ((1, 3, 4, 6), True))
BAND = 3 * BLOCK
HALO = 16
CONV_ROWS = 64
CONV_GROUP = 4
SUBLANES = 8
BF16_SUBLANES = 16
LANES = 128
MXU_COLS = 256
CONV_SLABS = CONV_CH // LANES

VMEM_LIMIT_BYTES = 56 * 1024 * 1024
INPROJ_VMEM_LIMIT_BYTES = 60 * 1024 * 1024
WCAST_ROWS = 128

TS_IN = 512
TS_MIX = 512
TM_FFN = 1024
FFN_SUBTILE = 256

_ARB = "arbitrary"


def _layer_norm(x, g, b):
    mu = jnp.mean(x, axis=-1, keepdims=True)
    xc = x - mu
    var = jnp.mean(xc * xc, axis=-1, keepdims=True)
    return xc * lax.rsqrt(var + LN_EPS) * g + b


def _const_spec(shape):
    return pl.BlockSpec(shape, lambda *_: (0,) * len(shape), pipeline_mode=pl.Buffered(1))


def _smem_spec():
    return pl.BlockSpec(memory_space=pltpu.SMEM)


def _halo_specs(ts, seq, cols):
    per_tile = ts // HALO
    last = seq // HALO - 1
    prev = pl.BlockSpec((None, HALO, cols), lambda b, i: (b, jnp.maximum(i * per_tile - 1, 0), 0))
    nxt = pl.BlockSpec((None, HALO, cols), lambda b, i: (b, jnp.minimum((i + 1) * per_tile, last), 0))
    return prev, nxt


def _inproj_kernel(x_ref, xp_ref, xnx_ref, g_ref, b_ref, w32_ref, bgate_ref,
                   convw_ref,
                   wao_ref, wco_ref, wout_ref, wgate_ref, wup_ref, wdown_ref,
                   xn_ref, qkv_ref, gt_ref, dw_ref,
                   wao_o, wco_o, wout_o, wgu_o, wdown_o,
                   w_ref, xb_s, wb_s, ge_s, go_s):
    b_id = pl.program_id(0)
    i_id = pl.program_id(1)
    n_tiles = pl.num_programs(1)
    ts = x_ref.shape[0]
    dot = functools.partial(jnp.dot, preferred_element_type=jnp.float32)
    o1 = QKV_COLS
    o2 = o1 + CONV_COLS
    o3 = o2 + D_MODEL

    @pl.when((b_id == 0) & (i_id == 0))
    def _init():
        for k in range(CONV_WIDTH):
            wb_s[k] = jnp.broadcast_to(convw_ref[k:k + 1, :].astype(wb_s.dtype), (BF16_SUBLANES, CONV_CH))
        for r0 in range(0, w32_ref.shape[0], WCAST_ROWS):
            w_ref[r0:r0 + WCAST_ROWS, :] = w32_ref[r0:r0 + WCAST_ROWS, :].astype(w_ref.dtype)

    @pl.when(i_id == 0)
    def _cast_weights():
        for src, dst in ((wao_ref, wao_o), (wco_ref, wco_o), (wout_ref, wout_o), (wdown_ref, wdown_o)):
            dst[...] = src[...].astype(dst.dtype)
        for j in range(wgate_ref.shape[1] // MXU_COLS):
            src_cols = slice(j * MXU_COLS, (j + 1) * MXU_COLS)
            wgu_o[:, 2 * j * MXU_COLS:(2 * j + 1) * MXU_COLS] = wgate_ref[:, src_cols].astype(wgu_o.dtype)
            wgu_o[:, (2 * j + 1) * MXU_COLS:(2 * j + 2) * MXU_COLS] = wup_ref[:, src_cols].astype(wgu_o.dtype)

    g = g_ref[...]
    b = b_ref[...]
    xn = _layer_norm(x_ref[...], g, b)
    xn_ref[...] = xn
    xb_s[0:HALO, :] = _layer_norm(xp_ref[...], g, b).astype(xb_s.dtype)
    xb_s[HALO:HALO + ts, :] = xn.astype(xb_s.dtype)
    xb_s[HALO + ts:2 * HALO + ts, :] = _layer_norm(xnx_ref[...], g, b).astype(xb_s.dtype)

    u = dot(xb_s[...], w_ref[:, o1:o2])
    glu = u[:, :CONV_CH] * jax.nn.sigmoid(u[:, CONV_CH:])
    row = lax.broadcasted_iota(jnp.int32, (ts + 2 * HALO, 1), 0)
    pad = ((row < HALO) & (i_id == 0)) | ((row >= HALO + ts) & (i_id == n_tiles - 1))
    glu = jnp.where(pad, 0.0, glu)
    rows_ext = ts + 2 * HALO
    glu_up = pltpu.roll(glu, rows_ext - 1, axis=0)
    for c in range(CONV_SLABS):
        lanes = slice(c * LANES, (c + 1) * LANES)
        ge_s[c] = pltpu.bitcast(glu[:, lanes].astype(jnp.bfloat16), jnp.uint32)
        go_s[c] = pltpu.bitcast(glu_up[:, lanes].astype(jnp.bfloat16), jnp.uint32)

    xb = xb_s[HALO:HALO + ts, :]
    bgate = bgate_ref[...]
    gt_ref[:, :D_MODEL] = jax.nn.sigmoid(
        dot(xb, w_ref[:, o2:o3]) + bgate[:, :D_MODEL]).astype(gt_ref.dtype)
    gt_ref[:, D_MODEL:] = jax.nn.sigmoid(
        dot(xb, w_ref[:, o3:]) + bgate[:, D_MODEL:]).astype(gt_ref.dtype)
    qkv_ref[:, :Q_COLS] = (dot(xb, w_ref[:, :Q_COLS]) * Q_SCALE).astype(qkv_ref.dtype)
    qkv_ref[:, Q_COLS:] = dot(xb, w_ref[:, Q_COLS:o1]).astype(qkv_ref.dtype)

    first_tap = HALO - CONV_HALF

    def conv_chunk(chunk, carry):
        r0 = pl.multiple_of(chunk * CONV_ROWS, CONV_ROWS)
        w0 = pl.multiple_of(chunk * (CONV_ROWS // 2), CONV_ROWS // 2)
        n_groups = CONV_ROWS // BF16_SUBLANES
        for c in range(CONV_SLABS):
            lanes = slice(c * LANES, (c + 1) * LANES)
            accs = [jnp.zeros((BF16_SUBLANES, LANES), jnp.float32) for _ in range(n_groups)]
            for k0 in range(0, CONV_WIDTH, CONV_GROUP):
                parts = [None] * n_groups
                for k in range(k0, min(k0 + CONV_GROUP, CONV_WIDTH)):
                    w = wb_s[k, :, lanes]
                    for g in range(n_groups):
                        off = g * BF16_SUBLANES + first_tap + k
                        src = go_s if off % 2 else ge_s
                        words = src[c, pl.ds(w0 + off // 2, SUBLANES), :]
                        prod = pltpu.bitcast(words, jnp.bfloat16) * w
                        parts[g] = prod if parts[g] is None else parts[g] + prod
                for g in range(n_groups):
                    accs[g] = accs[g] + parts[g].astype(jnp.float32)
            for g in range(n_groups):
                dw_ref[pl.ds(r0 + g * BF16_SUBLANES, BF16_SUBLANES), lanes] = accs[g]
        return carry

    lax.fori_loop(0, ts // CONV_ROWS, conv_chunk, 0)


def _inproj(x, ln_g, ln_b, w_in, b_gate, conv_w, later_weights):
    batch, seq, _ = x.shape
    ts = TS_IN
    assert seq % ts == 0 and ts % CONV_ROWS == 0 and ts % HALO == 0 and HALO >= CONV_HALF
    tile = lambda b, i: (b, i, 0)
    prev_spec, next_spec = _halo_specs(ts, seq, D_MODEL)
    bf16 = jnp.bfloat16
    w_ao, w_co, w_out, w_gate, w_up, w_down = later_weights
    assert w_gate.shape == w_up.shape and w_gate.shape[1] % MXU_COLS == 0
    cast_out_shapes = [w_ao.shape, w_co.shape, w_out.shape,
                       (w_gate.shape[0], 2 * w_gate.shape[1]), w_down.shape]

    def slice_spec(shape):
        rows, cols = shape
        assert rows % (batch * BF16_SUBLANES) == 0
        return pl.BlockSpec((rows // batch, cols), lambda b, i: (b, 0))

    cast_in_specs = [slice_spec(w.shape) for w in later_weights]
    cast_out_specs = [slice_spec(s) for s in cast_out_shapes]
    outs = pl.pallas_call(
        _inproj_kernel,
        grid=(batch, seq // ts),
        in_specs=[
            pl.BlockSpec((None, ts, D_MODEL), tile),
            prev_spec,
            next_spec,
            _const_spec((1, D_MODEL)),
            _const_spec((1, D_MODEL)),
            _const_spec(w_in.shape),
            _const_spec((1, GATE_COLS)),
            _const_spec((CONV_WIDTH, CONV_CH)),
        ] + cast_in_specs,
        out_specs=[
            pl.BlockSpec((None, ts, D_MODEL), tile),
            pl.BlockSpec((None, ts, QKV_COLS), tile),
            pl.BlockSpec((None, ts, GATE_COLS), tile),
            pl.BlockSpec((None, ts, CONV_CH), tile),
        ] + cast_out_specs,
        out_shape=[
            jax.ShapeDtypeStruct((batch, seq, D_MODEL), jnp.float32),
            jax.ShapeDtypeStruct((batch, seq, QKV_COLS), bf16),
            jax.ShapeDtypeStruct((batch, seq, GATE_COLS), bf16),
            jax.ShapeDtypeStruct((batch, seq, CONV_CH), jnp.float32),
        ] + [jax.ShapeDtypeStruct(s, bf16) for s in cast_out_shapes],
        scratch_shapes=[
            pltpu.VMEM(w_in.shape, bf16),
            pltpu.VMEM((ts + 2 * HALO, D_MODEL), bf16),
            pltpu.VMEM((CONV_WIDTH, BF16_SUBLANES, CONV_CH), bf16),
            pltpu.VMEM((CONV_SLABS, ts // 2 + HALO, LANES), jnp.uint32),
            pltpu.VMEM((CONV_SLABS, ts // 2 + HALO, LANES), jnp.uint32),
        ],
        compiler_params=pltpu.CompilerParams(
            dimension_semantics=(_ARB, _ARB), vmem_limit_bytes=INPROJ_VMEM_LIMIT_BYTES),
        name="ln_inproj_conv",
    )(x, x, x, ln_g, ln_b, w_in, b_gate, conv_w, *later_weights)
    return outs[:4], outs[4:]


def _band_buckets():
    nb = NUM_BUCKETS // 2
    qi = np.arange(BLOCK)[:, None]
    kj = np.arange(BAND)[None, :]
    rel = kj - BLOCK - qi
    ret = (rel > 0).astype(np.int32) * nb
    n = np.abs(rel)
    max_exact = nb // 2
    large = max_exact + (np.log(np.maximum(n, 1) / max_exact)
                         / np.log(MAX_DISTANCE / max_exact) * (nb - max_exact)).astype(np.int32)
    large = np.minimum(large, nb - 1)
    buckets = (ret + np.where(n < max_exact, n, large)).astype(np.int32)
    return np.where(n <= WINDOW, buckets, MASKED_BUCKET).astype(np.int32)


def _mix_kernel(n_seq_blocks,
                relb_ref, sink_ref, bk_ref,
                q_ref, kv_ref, dw_ref, gt_ref, xn_ref,
                convb_ref, clng_ref, clnb_ref,
                wao_ref, wco_ref, wout_ref,
                r_ref,
                bias_s, o_s):
    b_id = pl.program_id(0)
    i_id = pl.program_id(1)
    ts = q_ref.shape[0]
    blocks_per_tile = ts // BLOCK
    dot = functools.partial(jnp.dot, preferred_element_type=jnp.float32)

    @pl.when((b_id == 0) & (i_id == 0))
    def _init():
        bk = bk_ref[...]
        col = lax.broadcasted_iota(jnp.int32, (BLOCK, BAND), 1)
        for h in range(N_HEADS):
            def body(b, acc):
                return jnp.where(bk == b, relb_ref[b, h], acc)
            mid = LOG2E * lax.fori_loop(0, NUM_BUCKETS, body,
                                        jnp.full((BLOCK, BAND), NEG_INF, jnp.float32))
            bias_s[0, h] = jnp.where(col < BLOCK, NEG_INF, mid)
            bias_s[1, h] = mid
            bias_s[2, h] = jnp.where(col >= 2 * BLOCK, NEG_INF, mid)

    lane = lax.broadcasted_iota(jnp.int32, (BLOCK, LANES), 1)
    low_half = lane < HEAD_DIM
    nt = (((1,), (1,)), ((), ()))

    def block_operands(j):
        n = i_id * blocks_per_tile + j
        variant = jnp.where(n == 0, 0, jnp.where(n == n_seq_blocks - 1, 2, 1))
        prev_r = pl.multiple_of(jnp.maximum(n - 1, 0) * BLOCK, BLOCK)
        own_r = pl.multiple_of(n * BLOCK, BLOCK)
        next_r = pl.multiple_of(jnp.minimum(n + 1, n_seq_blocks - 1) * BLOCK, BLOCK)
        kvb = jnp.concatenate([kv_ref[pl.ds(prev_r, BLOCK), :],
                               kv_ref[pl.ds(own_r, BLOCK), :],
                               kv_ref[pl.ds(next_r, BLOCK), :]], axis=0)
        kk = kvb[:, :LANES]
        vv = kvb[:, LANES:]
        q = q_ref[j * BLOCK:(j + 1) * BLOCK, :]
        zero = jnp.zeros((BLOCK, LANES), q.dtype)
        q_head = {}
        for p in range(N_HEADS // 2):
            q2 = q[:, p * LANES:(p + 1) * LANES]
            q_head[2 * p] = jnp.where(low_half, q2, zero)
            q_head[2 * p + 1] = jnp.where(low_half, zero, q2)
        return dict(variant=variant, q_head=q_head,
                    k=(kk, pltpu.roll(kk, HEAD_DIM, axis=1)),
                    v=(vv, pltpu.roll(vv, HEAD_DIM, axis=1)))

    def scores(blk, heads, swapped):
        lhs = jnp.concatenate([blk["q_head"][h] for h in heads], axis=0)
        return lax.dot_general(lhs, blk["k"][swapped], nt, preferred_element_type=jnp.float32)

    def softmax_values(blk, heads, swapped, s_all, o_head):
        ps, inv_ls = [], []
        for r, h in enumerate(heads):
            s = s_all[r * BLOCK:(r + 1) * BLOCK, :] + bias_s[blk["variant"], h]
            sink = jnp.full((BLOCK, 1), sink_ref[h], jnp.float32) * LOG2E
            m = jnp.maximum(jnp.max(s, axis=-1, keepdims=True), sink)
            p = jnp.exp2(s - m)
            l = jnp.sum(p, axis=-1, keepdims=True) + jnp.exp2(sink - m)
            ps.append(p.astype(jnp.bfloat16))
            inv_ls.append(1.0 / l)
        r_all = dot(jnp.concatenate(ps, axis=0), blk["v"][swapped])
        for r, h in enumerate(heads):
            o_head[h] = r_all[r * BLOCK:(r + 1) * BLOCK, :] * inv_ls[r]

    stages = [(j, heads, int(swapped)) for j in range(blocks_per_tile) for heads, swapped in HEAD_GROUPS]
    yc_chunk = D_MODEL * 2 // len(stages)
    assert yc_chunk % MXU_COLS == 0 and len(stages) % 2 == 0
    yc_parts = []
    y = _layer_norm(dw_ref[...] + convb_ref[...], clng_ref[...], clnb_ref[...])
    yc = (y * jax.nn.sigmoid(y)).astype(jnp.bfloat16)
    blocks = {}
    o_heads = {}
    pending = None
    for c in range(len(stages) + 1):
        nxt = None
        if c < len(stages):
            j, heads, swapped = stages[c]
            if j not in blocks:
                blocks[j] = block_operands(j)
                o_heads[j] = {}
            nxt = (j, heads, swapped, scores(blocks[j], heads, swapped))
        if c % 2 == 1 and len(yc_parts) * yc_chunk < D_MODEL:
            c0 = len(yc_parts) * yc_chunk
            yc_parts.append(dot(yc, wco_ref[:, c0:c0 + yc_chunk]))
        if pending is not None:
            j, heads, swapped, s_all = pending
            softmax_values(blocks[j], heads, swapped, s_all, o_heads[j])
            if len(o_heads[j]) == N_HEADS:
                o_pairs = [jnp.where(low_half, o_heads[j][2 * p], o_heads[j][2 * p + 1])
                           for p in range(N_HEADS // 2)]
                o_s[j * BLOCK:(j + 1) * BLOCK, :] = jnp.concatenate(o_pairs, axis=1).astype(o_s.dtype)
        pending = nxt

    y_a = dot(o_s[...], wao_ref[...])
    y_c = jnp.concatenate(yc_parts, axis=1)
    merged = (gt_ref[:, :D_MODEL].astype(jnp.float32) * y_a
              + gt_ref[:, D_MODEL:].astype(jnp.float32) * y_c)
    mix = dot(merged.astype(jnp.bfloat16), wout_ref[...])
    r_ref[...] = ALPHA * xn_ref[...] + mix


def _mix(rel_bias, sink, qkv, dw, gt, xn, conv_b, cln_g, cln_b, w_ao, w_co, w_out):
    batch, seq, _ = xn.shape
    ts = TS_MIX
    assert seq % ts == 0 and ts % BLOCK == 0 and seq // BLOCK >= 2
    n_seq_blocks = seq // BLOCK
    buckets = jnp.asarray(_band_buckets())
    tile = lambda b, i: (b, i, 0)
    return pl.pallas_call(
        functools.partial(_mix_kernel, n_seq_blocks),
        grid=(batch, seq // ts),
        in_specs=[
            _smem_spec(),
            _smem_spec(),
            _const_spec((BLOCK, BAND)),
            pl.BlockSpec((None, ts, Q_COLS), tile),
            pl.BlockSpec((None, seq, 2 * KV_COLS), lambda b, i: (b, 0, Q_COLS // (2 * KV_COLS))),
            pl.BlockSpec((None, ts, CONV_CH), tile),
            pl.BlockSpec((None, ts, GATE_COLS), tile),
            pl.BlockSpec((None, ts, D_MODEL), tile),
            _const_spec((1, CONV_CH)),
            _const_spec((1, CONV_CH)),
            _const_spec((1, CONV_CH)),
            _const_spec((Q_COLS, D_MODEL)),
            _const_spec((CONV_CH, D_MODEL)),
            _const_spec((D_MODEL, D_MODEL)),
        ],
        out_specs=pl.BlockSpec((None, ts, D_MODEL), tile),
        out_shape=jax.ShapeDtypeStruct((batch, seq, D_MODEL), jnp.float32),
        scratch_shapes=[
            pltpu.VMEM((3, N_HEADS, BLOCK, BAND), jnp.float32),
            pltpu.VMEM((ts, Q_COLS), jnp.bfloat16),
        ],
        compiler_params=pltpu.CompilerParams(
            dimension_semantics=(_ARB, _ARB), vmem_limit_bytes=VMEM_LIMIT_BYTES),
        name="token_mix",
    )(rel_bias, sink, buckets, qkv, qkv, dw, gt, xn, conv_b, cln_g, cln_b, w_ao, w_co, w_out)


def _ffn_kernel(r_ref, g1_ref, b1_ref, wgu_ref, wd_ref, g_ref, b_ref, o_ref):
    dot = functools.partial(jnp.dot, preferred_element_type=jnp.float32)
    n_sub = r_ref.shape[0] // FFN_SUBTILE
    state = {}

    def up_stage(s):
        rows = slice(s * FFN_SUBTILE, (s + 1) * FFN_SUBTILE)
        h = _layer_norm(r_ref[rows, :], g1_ref[...], b1_ref[...])
        gu = dot(h.astype(jnp.bfloat16), wgu_ref[...])
        hid = []
        for c0 in range(0, gu.shape[1], 2 * MXU_COLS):
            gate = gu[:, c0:c0 + MXU_COLS]
            up = gu[:, c0 + MXU_COLS:c0 + 2 * MXU_COLS]
            hid.append((gate * jax.nn.sigmoid(gate) * up).astype(jnp.bfloat16))
        state[s] = (h, jnp.concatenate(hid, axis=1))

    def down_stage(s):
        rows = slice(s * FFN_SUBTILE, (s + 1) * FFN_SUBTILE)
        h, hid = state.pop(s)
        o_ref[rows, :] = _layer_norm(ALPHA * h + dot(hid, wd_ref[...]), g_ref[...], b_ref[...])

    up_stage(0)
    for s in range(n_sub):
        if s + 1 < n_sub:
            up_stage(s + 1)
        down_stage(s)


def _ffn(r2d, ln1_g, ln1_b, w_gu, w_down, ln_g, ln_b):
    n_tok = r2d.shape[0]
    assert n_tok % TM_FFN == 0
    row = lambda i: (i, 0)
    return pl.pallas_call(
        _ffn_kernel,
        grid=(n_tok // TM_FFN,),
        in_specs=[
            pl.BlockSpec((TM_FFN, D_MODEL), row),
            _const_spec((1, D_MODEL)),
            _const_spec((1, D_MODEL)),
            _const_spec(w_gu.shape),
            _const_spec(w_down.shape),
            _const_spec((1, D_MODEL)),
            _const_spec((1, D_MODEL)),
        ],
        out_specs=pl.BlockSpec((TM_FFN, D_MODEL), row),
        out_shape=jax.ShapeDtypeStruct((n_tok, D_MODEL), jnp.float32),
        compiler_params=pltpu.CompilerParams(
            dimension_semantics=(_ARB,), vmem_limit_bytes=VMEM_LIMIT_BYTES),
        name="swiglu_ffn",
    )(r2d, ln1_g, ln1_b, w_gu, w_down, ln_g, ln_b)


def kernel(x, ln_in_g, ln_in_b, rel_bias, w_in, b_gate, conv_w, conv_b, conv_ln_g, conv_ln_b,
           w_conv_out, w_attn_out, sink, w_out, ln1_g, ln1_b, w_gate, w_up, w_down, ln2_g, ln2_b):
    batch, seq, d_model = x.shape
    assert d_model == D_MODEL and w_in.shape[0] == DEPTH
    bf16 = jnp.bfloat16
    row = lambda v: v.reshape(1, -1)
    n_tok = batch * seq

    (xn, qkv, gt, dw), (w_ao, w_co, w_o, w_gu, w_d) = _inproj(
        x, row(ln_in_g), row(ln_in_b), w_in[0], row(b_gate[0]), conv_w[0],
        (w_attn_out[0], w_conv_out[0], w_out[0], w_gate[0], w_up[0], w_down[0]))
    r = _mix(rel_bias, sink[0], qkv, dw, gt, xn,
             row(conv_b[0]), row(conv_ln_g[0]), row(conv_ln_b[0]), w_ao, w_co, w_o)
    out = _ffn(r.reshape(n_tok, D_MODEL), row(ln1_g[0]), row(ln1_b[0]), w_gu, w_d,
               row(ln2_g[0]), row(ln2_b[0]))
    return out.reshape(batch, seq, D_MODEL)
```

```python
import functools

import numpy as np
import jax
import jax.numpy as jnp
from jax import lax
from jax.experimental import pallas as pl
from jax.experimental.pallas import tpu as pltpu

D_MODEL = 1024
HEAD_DIM = 64
N_HEADS = 8
N_KV_HEADS = 2
WINDOW = 128
BLOCK = 128
NUM_BUCKETS = 32
MAX_DISTANCE = 128
CONV_CH = 512
CONV_WIDTH = 31
CONV_HALF = (CONV_WIDTH - 1) // 2
Q_COLS = N_HEADS * HEAD_DIM
KV_COLS = N_KV_HEADS * HEAD_DIM
CONV_COLS = 2 * CONV_CH
GATE_COLS = 2 * D_MODEL
QKV_COLS = Q_COLS + 2 * KV_COLS
DEPTH = 1
ALPHA = (2.0 * DEPTH) ** 0.25
LN_EPS = 1e-5
LOG2E = 1.4426950408889634
Q_SCALE = HEAD_DIM ** -0.5 * LOG2E
NEG_INF = -1e30
MASKED_BUCKET = NUM_BUCKETS

HEAD_GROUPS = (((0, 2, 5, 7), False), ((1, 3, 4, 6), True))
BAND = 3 * BLOCK
HALO = 16
CONV_ROWS = 64
CONV_GROUP = 4
SUBLANES = 8
BF16_SUBLANES = 16
LANES = 128
MXU_COLS = 256
CONV_SLABS = CONV_CH // LANES

VMEM_LIMIT_BYTES = 56 * 1024 * 1024
INPROJ_VMEM_LIMIT_BYTES = 60 * 1024 * 1024
WCAST_ROWS = 128

TS_IN = 512
TS_MIX = 512
TM_FFN = 1024
FFN_SUBTILE = 256

_ARB = "arbitrary"


def _layer_norm(x, g, b):
    mu = jnp.mean(x, axis=-1, keepdims=True)
    xc = x - mu
    var = jnp.mean(xc * xc, axis=-1, keepdims=True)
    return xc * lax.rsqrt(var + LN_EPS) * g + b


def _const_spec(shape):
    return pl.BlockSpec(shape, lambda *_: (0,) * len(shape), pipeline_mode=pl.Buffered(1))


def _smem_spec():
    return pl.BlockSpec(memory_space=pltpu.SMEM)


def _halo_specs(ts, seq, cols):
    per_tile = ts // HALO
    last = seq // HALO - 1
    prev = pl.BlockSpec((None, HALO, cols), lambda b, i: (b, jnp.maximum(i * per_tile - 1, 0), 0))
    nxt = pl.BlockSpec((None, HALO, cols), lambda b, i: (b, jnp.minimum((i + 1) * per_tile, last), 0))
    return prev, nxt


def _inproj_kernel(x_ref, xp_ref, xnx_ref, g_ref, b_ref, w32_ref, bgate_ref,
                   convw_ref,
                   wao_ref, wco_ref, wout_ref, wgate_ref, wup_ref, wdown_ref,
                   xn_ref, qkv_ref, gt_ref, dw_ref,
                   wao_o, wco_o, wout_o, wgu_o, wdown_o,
                   w_ref, xb_s, wb_s, ge_s, go_s):
    b_id = pl.program_id(0)
    i_id = pl.program_id(1)
    n_tiles = pl.num_programs(1)
    ts = x_ref.shape[0]
    dot = functools.partial(jnp.dot, preferred_element_type=jnp.float32)
    o1 = QKV_COLS
    o2 = o1 + CONV_COLS
    o3 = o2 + D_MODEL

    @pl.when((b_id == 0) & (i_id == 0))
    def _init():
        for k in range(CONV_WIDTH):
            wb_s[k] = jnp.broadcast_to(convw_ref[k:k + 1, :].astype(wb_s.dtype), (BF16_SUBLANES, CONV_CH))
        for r0 in range(0, w32_ref.shape[0], WCAST_ROWS):
            w_ref[r0:r0 + WCAST_ROWS, :] = w32_ref[r0:r0 + WCAST_ROWS, :].astype(w_ref.dtype)

    @pl.when(i_id == 0)
    def _cast_weights():
        for src, dst in ((wao_ref, wao_o), (wco_ref, wco_o), (wout_ref, wout_o), (wdown_ref, wdown_o)):
            dst[...] = src[...].astype(dst.dtype)
        for j in range(wgate_ref.shape[1] // MXU_COLS):
            src_cols = slice(j * MXU_COLS, (j + 1) * MXU_COLS)
            wgu_o[:, 2 * j * MXU_COLS:(2 * j + 1) * MXU_COLS] = wgate_ref[:, src_cols].astype(wgu_o.dtype)
            wgu_o[:, (2 * j + 1) * MXU_COLS:(2 * j + 2) * MXU_COLS] = wup_ref[:, src_cols].astype(wgu_o.dtype)

    g = g_ref[...]
    b = b_ref[...]
    xn = _layer_norm(x_ref[...], g, b)
    xn_ref[...] = xn
    xb_s[0:HALO, :] = _layer_norm(xp_ref[...], g, b).astype(xb_s.dtype)
    xb_s[HALO:HALO + ts, :] = xn.astype(xb_s.dtype)
    xb_s[HALO + ts:2 * HALO + ts, :] = _layer_norm(xnx_ref[...], g, b).astype(xb_s.dtype)

    u = dot(xb_s[...], w_ref[:, o1:o2])
    glu = u[:, :CONV_CH] * jax.nn.sigmoid(u[:, CONV_CH:])
    row = lax.broadcasted_iota(jnp.int32, (ts + 2 * HALO, 1), 0)
    pad = ((row < HALO) & (i_id == 0)) | ((row >= HALO + ts) & (i_id == n_tiles - 1))
    glu = jnp.where(pad, 0.0, glu)
    rows_ext = ts + 2 * HALO
    glu_up = pltpu.roll(glu, rows_ext - 1, axis=0)
    for c in range(CONV_SLABS):
        lanes = slice(c * LANES, (c + 1) * LANES)
        ge_s[c] = pltpu.bitcast(glu[:, lanes].astype(jnp.bfloat16), jnp.uint32)
        go_s[c] = pltpu.bitcast(glu_up[:, lanes].astype(jnp.bfloat16), jnp.uint32)

    xb = xb_s[HALO:HALO + ts, :]
    bgate = bgate_ref[...]
    gt_ref[:, :D_MODEL] = jax.nn.sigmoid(
        dot(xb, w_ref[:, o2:o3]) + bgate[:, :D_MODEL]).astype(gt_ref.dtype)
    gt_ref[:, D_MODEL:] = jax.nn.sigmoid(
        dot(xb, w_ref[:, o3:]) + bgate[:, D_MODEL:]).astype(gt_ref.dtype)
    qkv_ref[:, :Q_COLS] = (dot(xb, w_ref[:, :Q_COLS]) * Q_SCALE).astype(qkv_ref.dtype)
    qkv_ref[:, Q_COLS:] = dot(xb, w_ref[:, Q_COLS:o1]).astype(qkv_ref.dtype)

    first_tap = HALO - CONV_HALF

    def conv_chunk(chunk, carry):
        r0 = pl.multiple_of(chunk * CONV_ROWS, CONV_ROWS)
        w0 = pl.multiple_of(chunk * (CONV_ROWS // 2), CONV_ROWS // 2)
        n_groups = CONV_ROWS // BF16_SUBLANES
        for c in range(CONV_SLABS):
            lanes = slice(c * LANES, (c + 1) * LANES)
            accs = [jnp.zeros((BF16_SUBLANES, LANES), jnp.float32) for _ in range(n_groups)]
            for k0 in range(0, CONV_WIDTH, CONV_GROUP):
                parts = [None] * n_groups
                for k in range(k0, min(k0 + CONV_GROUP, CONV_WIDTH)):
                    w = wb_s[k, :, lanes]
                    for g in range(n_groups):
                        off = g * BF16_SUBLANES + first_tap + k
                        src = go_s if off % 2 else ge_s
                        words = src[c, pl.ds(w0 + off // 2, SUBLANES), :]
                        prod = pltpu.bitcast(words, jnp.bfloat16) * w
                        parts[g] = prod if parts[g] is None else parts[g] + prod
                for g in range(n_groups):
                    accs[g] = accs[g] + parts[g].astype(jnp.float32)
            for g in range(n_groups):
                dw_ref[pl.ds(r0 + g * BF16_SUBLANES, BF16_SUBLANES), lanes] = accs[g]
        return carry

    lax.fori_loop(0, ts // CONV_ROWS, conv_chunk, 0)


def _inproj(x, ln_g, ln_b, w_in, b_gate, conv_w, later_weights):
    batch, seq, _ = x.shape
    ts = TS_IN
    assert seq % ts == 0 and ts % CONV_ROWS == 0 and ts % HALO == 0 and HALO >= CONV_HALF
    tile = lambda b, i: (b, i, 0)
    prev_spec, next_spec = _halo_specs(ts, seq, D_MODEL)
    bf16 = jnp.bfloat16
    w_ao, w_co, w_out, w_gate, w_up, w_down = later_weights
    assert w_gate.shape == w_up.shape and w_gate.shape[1] % MXU_COLS == 0
    cast_out_shapes = [w_ao.shape, w_co.shape, w_out.shape,
                       (w_gate.shape[0], 2 * w_gate.shape[1]), w_down.shape]

    def slice_spec(shape):
        rows, cols = shape
        assert rows % (batch * BF16_SUBLANES) == 0
        return pl.BlockSpec((rows // batch, cols), lambda b, i: (b, 0))

    cast_in_specs = [slice_spec(w.shape) for w in later_weights]
    cast_out_specs = [slice_spec(s) for s in cast_out_shapes]
    outs = pl.pallas_call(
        _inproj_kernel,
        grid=(batch, seq // ts),
        in_specs=[
            pl.BlockSpec((None, ts, D_MODEL), tile),
            prev_spec,
            next_spec,
            _const_spec((1, D_MODEL)),
            _const_spec((1, D_MODEL)),
            _const_spec(w_in.shape),
            _const_spec((1, GATE_COLS)),
            _const_spec((CONV_WIDTH, CONV_CH)),
        ] + cast_in_specs,
        out_specs=[
            pl.BlockSpec((None, ts, D_MODEL), tile),
            pl.BlockSpec((None, ts, QKV_COLS), tile),
            pl.BlockSpec((None, ts, GATE_COLS), tile),
            pl.BlockSpec((None, ts, CONV_CH), tile),
        ] + cast_out_specs,
        out_shape=[
            jax.ShapeDtypeStruct((batch, seq, D_MODEL), jnp.float32),
            jax.ShapeDtypeStruct((batch, seq, QKV_COLS), bf16),
            jax.ShapeDtypeStruct((batch, seq, GATE_COLS), bf16),
            jax.ShapeDtypeStruct((batch, seq, CONV_CH), jnp.float32),
        ] + [jax.ShapeDtypeStruct(s, bf16) for s in cast_out_shapes],
        scratch_shapes=[
            pltpu.VMEM(w_in.shape, bf16),
            pltpu.VMEM((ts + 2 * HALO, D_MODEL), bf16),
            pltpu.VMEM((CONV_WIDTH, BF16_SUBLANES, CONV_CH), bf16),
            pltpu.VMEM((CONV_SLABS, ts // 2 + HALO, LANES), jnp.uint32),
            pltpu.VMEM((CONV_SLABS, ts // 2 + HALO, LANES), jnp.uint32),
        ],
        compiler_params=pltpu.CompilerParams(
            dimension_semantics=(_ARB, _ARB), vmem_limit_bytes=INPROJ_VMEM_LIMIT_BYTES),
        name="ln_inproj_conv",
    )(x, x, x, ln_g, ln_b, w_in, b_gate, conv_w, *later_weights)
    return outs[:4], outs[4:]


def _band_buckets():
    nb = NUM_BUCKETS // 2
    qi = np.arange(BLOCK)[:, None]
    kj = np.arange(BAND)[None, :]
    rel = kj - BLOCK - qi
    ret = (rel > 0).astype(np.int32) * nb
    n = np.abs(rel)
    max_exact = nb // 2
    large = max_exact + (np.log(np.maximum(n, 1) / max_exact)
                         / np.log(MAX_DISTANCE / max_exact) * (nb - max_exact)).astype(np.int32)
    large = np.minimum(large, nb - 1)
    buckets = (ret + np.where(n < max_exact, n, large)).astype(np.int32)
    return np.where(n <= WINDOW, buckets, MASKED_BUCKET).astype(np.int32)


def _mix_kernel(n_seq_blocks,
                relb_ref, sink_ref, bk_ref,
                q_ref, kv_ref, dw_ref, gt_ref, xn_ref,
                convb_ref, clng_ref, clnb_ref,
                wao_ref, wco_ref, wout_ref,
                r_ref,
                bias_s, o_s):
    b_id = pl.program_id(0)
    i_id = pl.program_id(1)
    ts = q_ref.shape[0]
    blocks_per_tile = ts // BLOCK
    dot = functools.partial(jnp.dot, preferred_element_type=jnp.float32)

    @pl.when((b_id == 0) & (i_id == 0))
    def _init():
        bk = bk_ref[...]
        col = lax.broadcasted_iota(jnp.int32, (BLOCK, BAND), 1)
        for h in range(N_HEADS):
            def body(b, acc):
                return jnp.where(bk == b, relb_ref[b, h], acc)
            mid = LOG2E * lax.fori_loop(0, NUM_BUCKETS, body,
                                        jnp.full((BLOCK, BAND), NEG_INF, jnp.float32))
            bias_s[0, h] = jnp.where(col < BLOCK, NEG_INF, mid)
            bias_s[1, h] = mid
            bias_s[2, h] = jnp.where(col >= 2 * BLOCK, NEG_INF, mid)

    lane = lax.broadcasted_iota(jnp.int32, (BLOCK, LANES), 1)
    low_half = lane < HEAD_DIM
    nt = (((1,), (1,)), ((), ()))

    def block_operands(j):
        n = i_id * blocks_per_tile + j
        variant = jnp.where(n == 0, 0, jnp.where(n == n_seq_blocks - 1, 2, 1))
        prev_r = pl.multiple_of(jnp.maximum(n - 1, 0) * BLOCK, BLOCK)
        own_r = pl.multiple_of(n * BLOCK, BLOCK)
        next_r = pl.multiple_of(jnp.minimum(n + 1, n_seq_blocks - 1) * BLOCK, BLOCK)
        kvb = jnp.concatenate([kv_ref[pl.ds(prev_r, BLOCK), :],
                               kv_ref[pl.ds(own_r, BLOCK), :],
                               kv_ref[pl.ds(next_r, BLOCK), :]], axis=0)
        kk = kvb[:, :LANES]
        vv = kvb[:, LANES:]
        q = q_ref[j * BLOCK:(j + 1) * BLOCK, :]
        zero = jnp.zeros((BLOCK, LANES), q.dtype)
        q_head = {}
        for p in range(N_HEADS // 2):
            q2 = q[:, p * LANES:(p + 1) * LANES]
            q_head[2 * p] = jnp.where(low_half, q2, zero)
            q_head[2 * p + 1] = jnp.where(low_half, zero, q2)
        return dict(variant=variant, q_head=q_head,
                    k=(kk, pltpu.roll(kk, HEAD_DIM, axis=1)),
                    v=(vv, pltpu.roll(vv, HEAD_DIM, axis=1)))

    def scores(blk, heads, swapped):
        lhs = jnp.concatenate([blk["q_head"][h] for h in heads], axis=0)
        return lax.dot_general(lhs, blk["k"][swapped], nt, preferred_element_type=jnp.float32)

    def softmax_values(blk, heads, swapped, s_all, o_head):
        ps, inv_ls = [], []
        for r, h in enumerate(heads):
            s = s_all[r * BLOCK:(r + 1) * BLOCK, :] + bias_s[blk["variant"], h]
            sink = jnp.full((BLOCK, 1), sink_ref[h], jnp.float32) * LOG2E
            m = jnp.maximum(jnp.max(s, axis=-1, keepdims=True), sink)
            p = jnp.exp2(s - m)
            l = jnp.sum(p, axis=-1, keepdims=True) + jnp.exp2(sink - m)
            ps.append(p.astype(jnp.bfloat16))
            inv_ls.append(1.0 / l)
        r_all = dot(jnp.concatenate(ps, axis=0), blk["v"][swapped])
        for r, h in enumerate(heads):
            o_head[h] = r_all[r * BLOCK:(r + 1) * BLOCK, :] * inv_ls[r]

    stages = [(j, heads, int(swapped)) for j in range(blocks_per_tile) for heads, swapped in HEAD_GROUPS]
    yc_chunk = D_MODEL * 2 // len(stages)
    assert yc_chunk % MXU_COLS == 0 and len(stages) % 2 == 0
    yc_parts = []
    y = _layer_norm(dw_ref[...] + convb_ref[...], clng_ref[...], clnb_ref[...])
    yc = (y * jax.nn.sigmoid(y)).astype(jnp.bfloat16)
    blocks = {}
    o_heads = {}
    pending = None
    for c in range(len(stages) + 1):
        nxt = None
        if c < len(stages):
            j, heads, swapped = stages[c]
            if j not in blocks:
                blocks[j] = block_operands(j)
                o_heads[j] = {}
            nxt = (j, heads, swapped, scores(blocks[j], heads, swapped))
        if c % 2 == 1 and len(yc_parts) * yc_chunk < D_MODEL:
            c0 = len(yc_parts) * yc_chunk
            yc_parts.append(dot(yc, wco_ref[:, c0:c0 + yc_chunk]))
        if pending is not None:
            j, heads, swapped, s_all = pending
            softmax_values(blocks[j], heads, swapped, s_all, o_heads[j])
            if len(o_heads[j]) == N_HEADS:
                o_pairs = [jnp.where(low_half, o_heads[j][2 * p], o_heads[j][2 * p + 1])
                           for p in range(N_HEADS // 2)]
                o_s[j * BLOCK:(j + 1) * BLOCK, :] = jnp.concatenate(o_pairs, axis=1).astype(o_s.dtype)
        pending = nxt

    y_a = dot(o_s[...], wao_ref[...])
    y_c = jnp.concatenate(yc_parts, axis=1)
    merged = (gt_ref[:, :D_MODEL].astype(jnp.float32) * y_a
              + gt_ref[:, D_MODEL:].astype(jnp.float32) * y_c)
    mix = dot(merged.astype(jnp.bfloat16), wout_ref[...])
    r_ref[...] = ALPHA * xn_ref[...] + mix


def _mix(rel_bias, sink, qkv, dw, gt, xn, conv_b, cln_g, cln_b, w_ao, w_co, w_out):
    batch, seq, _ = xn.shape
    ts = TS_MIX
    assert seq % ts == 0 and ts % BLOCK == 0 and seq // BLOCK >= 2
    n_seq_blocks = seq // BLOCK
    buckets = jnp.asarray(_band_buckets())
    tile = lambda b, i: (b, i, 0)
    return pl.pallas_call(
        functools.partial(_mix_kernel, n_seq_blocks),
        grid=(batch, seq // ts),
        in_specs=[
            _smem_spec(),
            _smem_spec(),
            _const_spec((BLOCK, BAND)),
            pl.BlockSpec((None, ts, Q_COLS), tile),
            pl.BlockSpec((None, seq, 2 * KV_COLS), lambda b, i: (b, 0, Q_COLS // (2 * KV_COLS))),
            pl.BlockSpec((None, ts, CONV_CH), tile),
            pl.BlockSpec((None, ts, GATE_COLS), tile),
            pl.BlockSpec((None, ts, D_MODEL), tile),
            _const_spec((1, CONV_CH)),
            _const_spec((1, CONV_CH)),
            _const_spec((1, CONV_CH)),
            _const_spec((Q_COLS, D_MODEL)),
            _const_spec((CONV_CH, D_MODEL)),
            _const_spec((D_MODEL, D_MODEL)),
        ],
        out_specs=pl.BlockSpec((None, ts, D_MODEL), tile),
        out_shape=jax.ShapeDtypeStruct((batch, seq, D_MODEL), jnp.float32),
        scratch_shapes=[
            pltpu.VMEM((3, N_HEADS, BLOCK, BAND), jnp.float32),
            pltpu.VMEM((ts, Q_COLS), jnp.bfloat16),
        ],
        compiler_params=pltpu.CompilerParams(
            dimension_semantics=(_ARB, _ARB), vmem_limit_bytes=VMEM_LIMIT_BYTES),
        name="token_mix",
    )(rel_bias, sink, buckets, qkv, qkv, dw, gt, xn, conv_b, cln_g, cln_b, w_ao, w_co, w_out)


def _ffn_kernel(r_ref, g1_ref, b1_ref, wgu_ref, wd_ref, g_ref, b_ref, o_ref):
    dot = functools.partial(jnp.dot, preferred_element_type=jnp.float32)
    n_sub = r_ref.shape[0] // FFN_SUBTILE
    state = {}

    def up_stage(s):
        rows = slice(s * FFN_SUBTILE, (s + 1) * FFN_SUBTILE)
        h = _layer_norm(r_ref[rows, :], g1_ref[...], b1_ref[...])
        gu = dot(h.astype(jnp.bfloat16), wgu_ref[...])
        hid = []
        for c0 in range(0, gu.shape[1], 2 * MXU_COLS):
            gate = gu[:, c0:c0 + MXU_COLS]
            up = gu[:, c0 + MXU_COLS:c0 + 2 * MXU_COLS]
            hid.append((gate * jax.nn.sigmoid(gate) * up).astype(jnp.bfloat16))
        state[s] = (h, jnp.concatenate(hid, axis=1))

    def down_stage(s):
        rows = slice(s * FFN_SUBTILE, (s + 1) * FFN_SUBTILE)
        h, hid = state.pop(s)
        o_ref[rows, :] = _layer_norm(ALPHA * h + dot(hid, wd_ref[...]), g_ref[...], b_ref[...])

    up_stage(0)
    for s in range(n_sub):
        if s + 1 < n_sub:
            up_stage(s + 1)
        down_stage(s)


def _ffn(r2d, ln1_g, ln1_b, w_gu, w_down, ln_g, ln_b):
    n_tok = r2d.shape[0]
    assert n_tok % TM_FFN == 0
    row = lambda i: (i, 0)
    return pl.pallas_call(
        _ffn_kernel,
        grid=(n_tok // TM_FFN,),
        in_specs=[
            pl.BlockSpec((TM_FFN, D_MODEL), row),
            _const_spec((1, D_MODEL)),
            _const_spec((1, D_MODEL)),
            _const_spec(w_gu.shape),
            _const_spec(w_down.shape),
            _const_spec((1, D_MODEL)),
            _const_spec((1, D_MODEL)),
        ],
        out_specs=pl.BlockSpec((TM_FFN, D_MODEL), row),
        out_shape=jax.ShapeDtypeStruct((n_tok, D_MODEL), jnp.float32),
        compiler_params=pltpu.CompilerParams(
            dimension_semantics=(_ARB,), vmem_limit_bytes=VMEM_LIMIT_BYTES),
        name="swiglu_ffn",
    )(r2d, ln1_g, ln1_b, w_gu, w_down, ln_g, ln_b)


def kernel(x, ln_in_g, ln_in_b, rel_bias, w_in, b_gate, conv_w, conv_b, conv_ln_g, conv_ln_b,
           w_conv_out, w_attn_out, sink, w_out, ln1_g, ln1_b, w_gate, w_up, w_down, ln2_g, ln2_b):
    batch, seq, d_model = x.shape
    assert d_model == D_MODEL and w_in.shape[0] == DEPTH
    row = lambda v: v.reshape(1, -1)
    n_tok = batch * seq

    (xn, qkv, gt, dw), (w_ao, w_co, w_o, w_gu, w_d) = _inproj(
        x, row(ln_in_g), row(ln_in_b), w_in[0], row(b_gate[0]), conv_w[0],
        (w_attn_out[0], w_conv_out[0], w_out[0], w_gate[0], w_up[0], w_down[0]))
    r = _mix(rel_bias, sink[0], qkv, dw, gt, xn,
             row(conv_b[0]), row(conv_ln_g[0]), row(conv_ln_b[0]), w_ao, w_co, w_o)
    out = _ffn(r.reshape(n_tok, D_MODEL), row(ln1_g[0]), row(ln1_b[0]), w_gu, w_d,
               row(ln2_g[0]), row(ln2_b[0]))
    return out.reshape(batch, seq, D_MODEL)
```

```python
import functools

import numpy as np
import jax
import jax.numpy as jnp
from jax import lax
from jax.experimental import pallas as pl
from jax.experimental.pallas import tpu as pltpu

D_MODEL = 1024
HEAD_DIM = 64
N_HEADS = 8
N_KV_HEADS = 2
WINDOW = 128
BLOCK = 128
NUM_BUCKETS = 32
MAX_DISTANCE = 128
CONV_CH = 512
CONV_WIDTH = 31
CONV_HALF = (CONV_WIDTH - 1) // 2
Q_COLS = N_HEADS * HEAD_DIM
KV_COLS = N_KV_HEADS * HEAD_DIM
CONV_COLS = 2 * CONV_CH
GATE_COLS = 2 * D_MODEL
QKV_COLS = Q_COLS + 2 * KV_COLS
DEPTH = 1
ALPHA = (2.0 * DEPTH) ** 0.25
LN_EPS = 1e-5
LOG2E = 1.4426950408889634
Q_SCALE = HEAD_DIM ** -0.5 * LOG2E
NEG_INF = -1e30
MASKED_BUCKET = NUM_BUCKETS

HEAD_GROUPS = (((0, 2, 5, 7), False), ((1, 3, 4, 6), True))
BAND = 3 * BLOCK
HALO = 16
CONV_ROWS = 64
CONV_GROUP = 4
SUBLANES = 8
BF16_SUBLANES = 16
LANES = 128
MXU_COLS = 256
CONV_SLABS = CONV_CH // LANES

VMEM_LIMIT_BYTES = 56 * 1024 * 1024
INPROJ_VMEM_LIMIT_BYTES = 60 * 1024 * 1024
WCAST_ROWS = 128

TS_IN = 512
TS_MIX = 512
TM_FFN = 1024
FFN_SUBTILE = 256

_ARB = "arbitrary"


def _layer_norm(x, g, b):
    mu = jnp.mean(x, axis=-1, keepdims=True)
    xc = x - mu
    var = jnp.mean(xc * xc, axis=-1, keepdims=True)
    return xc * lax.rsqrt(var + LN_EPS) * g + b


def _const_spec(shape):
    return pl.BlockSpec(shape, lambda *_: (0,) * len(shape), pipeline_mode=pl.Buffered(1))


def _smem_spec():
    return pl.BlockSpec(memory_space=pltpu.SMEM)


def _halo_specs(ts, seq, cols):
    per_tile = ts // HALO
    last = seq // HALO - 1
    prev = pl.BlockSpec((None, HALO, cols), lambda b, i: (b, jnp.maximum(i * per_tile - 1, 0), 0))
    nxt = pl.BlockSpec((None, HALO, cols), lambda b, i: (b, jnp.minimum((i + 1) * per_tile, last), 0))
    return prev, nxt


def _inproj_kernel(x_ref, xp_ref, xnx_ref, g_ref, b_ref, w32_ref, bgate_ref,
                   convw_ref,
                   wao_ref, wco_ref, wout_ref, wgate_ref, wup_ref, wdown_ref,
                   xn_ref, qkv_ref, gt_ref, dw_ref,
                   wao_o, wco_o, wout_o, wgu_o, wdown_o,
                   w_ref, xb_s, wb_s, ge_s, go_s):
    b_id = pl.program_id(0)
    i_id = pl.program_id(1)
    n_tiles = pl.num_programs(1)
    ts = x_ref.shape[0]
    dot = functools.partial(jnp.dot, preferred_element_type=jnp.float32)
    o1 = QKV_COLS
    o2 = o1 + CONV_COLS
    o3 = o2 + D_MODEL

    @pl.when((b_id == 0) & (i_id == 0))
    def _init():
        for k in range(CONV_WIDTH):
            wb_s[k] = jnp.broadcast_to(convw_ref[k:k + 1, :].astype(wb_s.dtype), (BF16_SUBLANES, CONV_CH))
        for r0 in range(0, w32_ref.shape[0], WCAST_ROWS):
            w_ref[r0:r0 + WCAST_ROWS, :] = w32_ref[r0:r0 + WCAST_ROWS, :].astype(w_ref.dtype)

    @pl.when(i_id == 0)
    def _cast_weights():
        for src, dst in ((wao_ref, wao_o), (wco_ref, wco_o), (wout_ref, wout_o), (wdown_ref, wdown_o)):
            dst[...] = src[...].astype(dst.dtype)
        for j in range(wgate_ref.shape[1] // MXU_COLS):
            src_cols = slice(j * MXU_COLS, (j + 1) * MXU_COLS)
            wgu_o[:, 2 * j * MXU_COLS:(2 * j + 1) * MXU_COLS] = wgate_ref[:, src_cols].astype(wgu_o.dtype)
            wgu_o[:, (2 * j + 1) * MXU_COLS:(2 * j + 2) * MXU_COLS] = wup_ref[:, src_cols].astype(wgu_o.dtype)

    g = g_ref[...]
    b = b_ref[...]
    xn = _layer_norm(x_ref[...], g, b)
    xn_ref[...] = xn
    xb_s[0:HALO, :] = _layer_norm(xp_ref[...], g, b).astype(xb_s.dtype)
    xb_s[HALO:HALO + ts, :] = xn.astype(xb_s.dtype)
    xb_s[HALO + ts:2 * HALO + ts, :] = _layer_norm(xnx_ref[...], g, b).astype(xb_s.dtype)

    u = dot(xb_s[...], w_ref[:, o1:o2])
    glu = u[:, :CONV_CH] * jax.nn.sigmoid(u[:, CONV_CH:])
    row = lax.broadcasted_iota(jnp.int32, (ts + 2 * HALO, 1), 0)
    pad = ((row < HALO) & (i_id == 0)) | ((row >= HALO + ts) & (i_id == n_tiles - 1))
    glu = jnp.where(pad, 0.0, glu)
    rows_ext = ts + 2 * HALO
    glu_up = pltpu.roll(glu, rows_ext - 1, axis=0)
    for c in range(CONV_SLABS):
        lanes = slice(c * LANES, (c + 1) * LANES)
        ge_s[c] = pltpu.bitcast(glu[:, lanes].astype(jnp.bfloat16), jnp.uint32)
        go_s[c] = pltpu.bitcast(glu_up[:, lanes].astype(jnp.bfloat16), jnp.uint32)

    xb = xb_s[HALO:HALO + ts, :]
    bgate = bgate_ref[...]
    gt_ref[:, :D_MODEL] = jax.nn.sigmoid(
        dot(xb, w_ref[:, o2:o3]) + bgate[:, :D_MODEL]).astype(gt_ref.dtype)
    gt_ref[:, D_MODEL:] = jax.nn.sigmoid(
        dot(xb, w_ref[:, o3:]) + bgate[:, D_MODEL:]).astype(gt_ref.dtype)
    qkv_ref[:, :Q_COLS] = (dot(xb, w_ref[:, :Q_COLS]) * Q_SCALE).astype(qkv_ref.dtype)
    qkv_ref[:, Q_COLS:] = dot(xb, w_ref[:, Q_COLS:o1]).astype(qkv_ref.dtype)

    first_tap = HALO - CONV_HALF

    def conv_chunk(chunk, carry):
        r0 = pl.multiple_of(chunk * CONV_ROWS, CONV_ROWS)
        w0 = pl.multiple_of(chunk * (CONV_ROWS // 2), CONV_ROWS // 2)
        n_groups = CONV_ROWS // BF16_SUBLANES
        for c in range(CONV_SLABS):
            lanes = slice(c * LANES, (c + 1) * LANES)
            accs = [jnp.zeros((BF16_SUBLANES, LANES), jnp.float32) for _ in range(n_groups)]
            for k0 in range(0, CONV_WIDTH, CONV_GROUP):
                parts = [None] * n_groups
                for k in range(k0, min(k0 + CONV_GROUP, CONV_WIDTH)):
                    w = wb_s[k, :, lanes]
                    for g in range(n_groups):
                        off = g * BF16_SUBLANES + first_tap + k
                        src = go_s if off % 2 else ge_s
                        words = src[c, pl.ds(w0 + off // 2, SUBLANES), :]
                        prod = pltpu.bitcast(words, jnp.bfloat16) * w
                        parts[g] = prod if parts[g] is None else parts[g] + prod
                for g in range(n_groups):
                    accs[g] = accs[g] + parts[g].astype(jnp.float32)
            for g in range(n_groups):
                dw_ref[pl.ds(r0 + g * BF16_SUBLANES, BF16_SUBLANES), lanes] = accs[g]
        return carry

    lax.fori_loop(0, ts // CONV_ROWS, conv_chunk, 0)


def _inproj(x, ln_g, ln_b, w_in, b_gate, conv_w, later_weights):
    batch, seq, _ = x.shape
    ts = TS_IN
    assert seq % ts == 0 and ts % CONV_ROWS == 0 and ts % HALO == 0 and HALO >= CONV_HALF
    tile = lambda b, i: (b, i, 0)
    prev_spec, next_spec = _halo_specs(ts, seq, D_MODEL)
    bf16 = jnp.bfloat16
    w_ao, w_co, w_out, w_gate, w_up, w_down = later_weights
    assert w_gate.shape == w_up.shape and w_gate.shape[1] % MXU_COLS == 0
    cast_out_shapes = [w_ao.shape, w_co.shape, w_out.shape,
                       (w_gate.shape[0], 2 * w_gate.shape[1]), w_down.shape]

    def slice_spec(shape):
        rows, cols = shape
        assert rows % (batch * BF16_SUBLANES) == 0
        return pl.BlockSpec((rows // batch, cols), lambda b, i: (b, 0))

    cast_in_specs = [slice_spec(w.shape) for w in later_weights]
    cast_out_specs = [slice_spec(s) for s in cast_out_shapes]
    outs = pl.pallas_call(
        _inproj_kernel,
        grid=(batch, seq // ts),
        in_specs=[
            pl.BlockSpec((None, ts, D_MODEL), tile),
            prev_spec,
            next_spec,
            _const_spec((1, D_MODEL)),
            _const_spec((1, D_MODEL)),
            _const_spec(w_in.shape),
            _const_spec((1, GATE_COLS)),
            _const_spec((CONV_WIDTH, CONV_CH)),
        ] + cast_in_specs,
        out_specs=[
            pl.BlockSpec((None, ts, D_MODEL), tile),
            pl.BlockSpec((None, ts, QKV_COLS), tile),
            pl.BlockSpec((None, ts, GATE_COLS), tile),
            pl.BlockSpec((None, ts, CONV_CH), tile),
        ] + cast_out_specs,
        out_shape=[
            jax.ShapeDtypeStruct((batch, seq, D_MODEL), jnp.float32),
            jax.ShapeDtypeStruct((batch, seq, QKV_COLS), bf16),
            jax.ShapeDtypeStruct((batch, seq, GATE_COLS), bf16),
            jax.ShapeDtypeStruct((batch, seq, CONV_CH), jnp.float32),
        ] + [jax.ShapeDtypeStruct(s, bf16) for s in cast_out_shapes],
        scratch_shapes=[
            pltpu.VMEM(w_in.shape, bf16),
            pltpu.VMEM((ts + 2 * HALO, D_MODEL), bf16),
            pltpu.VMEM((CONV_WIDTH, BF16_SUBLANES, CONV_CH), bf16),
            pltpu.VMEM((CONV_SLABS, ts // 2 + HALO, LANES), jnp.uint32),
            pltpu.VMEM((CONV_SLABS, ts // 2 + HALO, LANES), jnp.uint32),
        ],
        compiler_params=pltpu.CompilerParams(
            dimension_semantics=(_ARB, _ARB), vmem_limit_bytes=INPROJ_VMEM_LIMIT_BYTES),
        name="ln_inproj_conv",
    )(x, x, x, ln_g, ln_b, w_in, b_gate, conv_w, *later_weights)
    return outs[:4], outs[4:]


def _band_buckets():
    nb = NUM_BUCKETS // 2
    qi = np.arange(BLOCK)[:, None]
    kj = np.arange(BAND)[None, :]
    rel = kj - BLOCK - qi
    ret = (rel > 0).astype(np.int32) * nb
    n = np.abs(rel)
    max_exact = nb // 2
    large = max_exact + (np.log(np.maximum(n, 1) / max_exact)
                         / np.log(MAX_DISTANCE / max_exact) * (nb - max_exact)).astype(np.int32)
    large = np.minimum(large, nb - 1)
    buckets = (ret + np.where(n < max_exact, n, large)).astype(np.int32)
    return np.where(n <= WINDOW, buckets, MASKED_BUCKET).astype(np.int32)


def _mix_kernel(n_seq_blocks,
                relb_ref, sink_ref, bk_ref,
                q_ref, kv_ref, dw_ref, gt_ref,
                convb_ref, clng_ref, clnb_ref,
                wao_ref, wco_ref, wout_ref,
                mix_ref,
                bias_s, o_s):
    b_id = pl.program_id(0)
    i_id = pl.program_id(1)
    ts = q_ref.shape[0]
    blocks_per_tile = ts // BLOCK
    dot = functools.partial(jnp.dot, preferred_element_type=jnp.float32)

    @pl.when((b_id == 0) & (i_id == 0))
    def _init():
        bk = bk_ref[...]
        col = lax.broadcasted_iota(jnp.int32, (BLOCK, BAND), 1)
        for h in range(N_HEADS):
            def body(b, acc):
                return jnp.where(bk == b, relb_ref[b, h], acc)
            mid = LOG2E * lax.fori_loop(0, NUM_BUCKETS, body,
                                        jnp.full((BLOCK, BAND), NEG_INF, jnp.float32))
            bias_s[0, h] = jnp.where(col < BLOCK, NEG_INF, mid)
            bias_s[1, h] = mid
            bias_s[2, h] = jnp.where(col >= 2 * BLOCK, NEG_INF, mid)

    lane = lax.broadcasted_iota(jnp.int32, (BLOCK, LANES), 1)
    low_half = lane < HEAD_DIM
    nt = (((1,), (1,)), ((), ()))

    def block_operands(j):
        n = i_id * blocks_per_tile + j
        variant = jnp.where(n == 0, 0, jnp.where(n == n_seq_blocks - 1, 2, 1))
        prev_r = pl.multiple_of(jnp.maximum(n - 1, 0) * BLOCK, BLOCK)
        own_r = pl.multiple_of(n * BLOCK, BLOCK)
        next_r = pl.multiple_of(jnp.minimum(n + 1, n_seq_blocks - 1) * BLOCK, BLOCK)
        kvb = jnp.concatenate([kv_ref[pl.ds(prev_r, BLOCK), :],
                               kv_ref[pl.ds(own_r, BLOCK), :],
                               kv_ref[pl.ds(next_r, BLOCK), :]], axis=0)
        kk = kvb[:, :LANES]
        vv = kvb[:, LANES:]
        q = q_ref[j * BLOCK:(j + 1) * BLOCK, :]
        zero = jnp.zeros((BLOCK, LANES), q.dtype)
        q_head = {}
        for p in range(N_HEADS // 2):
            q2 = q[:, p * LANES:(p + 1) * LANES]
            q_head[2 * p] = jnp.where(low_half, q2, zero)
            q_head[2 * p + 1] = jnp.where(low_half, zero, q2)
        return dict(variant=variant, q_head=q_head,
                    k=(kk, pltpu.roll(kk, HEAD_DIM, axis=1)),
                    v=(vv, pltpu.roll(vv, HEAD_DIM, axis=1)))

    def scores(blk, heads, swapped):
        lhs = jnp.concatenate([blk["q_head"][h] for h in heads], axis=0)
        return lax.dot_general(lhs, blk["k"][swapped], nt, preferred_element_type=jnp.float32)

    def softmax_values(blk, heads, swapped, s_all, o_head):
        ps, inv_ls = [], []
        for r, h in enumerate(heads):
            s = s_all[r * BLOCK:(r + 1) * BLOCK, :] + bias_s[blk["variant"], h]
            sink = jnp.full((BLOCK, 1), sink_ref[h], jnp.float32) * LOG2E
            m = jnp.maximum(jnp.max(s, axis=-1, keepdims=True), sink)
            p = jnp.exp2(s - m)
            l = jnp.sum(p, axis=-1, keepdims=True) + jnp.exp2(sink - m)
            ps.append(p.astype(jnp.bfloat16))
            inv_ls.append(1.0 / l)
        r_all = dot(jnp.concatenate(ps, axis=0), blk["v"][swapped])
        for r, h in enumerate(heads):
            o_head[h] = r_all[r * BLOCK:(r + 1) * BLOCK, :] * inv_ls[r]

    stages = [(j, heads, int(swapped)) for j in range(blocks_per_tile) for heads, swapped in HEAD_GROUPS]
    yc_chunk = D_MODEL * 2 // len(stages)
    assert yc_chunk % MXU_COLS == 0 and len(stages) % 2 == 0
    yc_parts = []
    y = _layer_norm(dw_ref[...] + convb_ref[...], clng_ref[...], clnb_ref[...])
    yc = (y * jax.nn.sigmoid(y)).astype(jnp.bfloat16)
    blocks = {}
    o_heads = {}
    pending = None
    for c in range(len(stages) + 1):
        nxt = None
        if c < len(stages):
            j, heads, swapped = stages[c]
            if j not in blocks:
                blocks[j] = block_operands(j)
                o_heads[j] = {}
            nxt = (j, heads, swapped, scores(blocks[j], heads, swapped))
        if c % 2 == 1 and len(yc_parts) * yc_chunk < D_MODEL:
            c0 = len(yc_parts) * yc_chunk
            yc_parts.append(dot(yc, wco_ref[:, c0:c0 + yc_chunk]))
        if pending is not None:
            j, heads, swapped, s_all = pending
            softmax_values(blocks[j], heads, swapped, s_all, o_heads[j])
            if len(o_heads[j]) == N_HEADS:
                o_pairs = [jnp.where(low_half, o_heads[j][2 * p], o_heads[j][2 * p + 1])
                           for p in range(N_HEADS // 2)]
                o_s[j * BLOCK:(j + 1) * BLOCK, :] = jnp.concatenate(o_pairs, axis=1).astype(o_s.dtype)
        pending = nxt

    y_a = dot(o_s[...], wao_ref[...])
    y_c = jnp.concatenate(yc_parts, axis=1)
    merged = (gt_ref[:, :D_MODEL].astype(jnp.float32) * y_a
              + gt_ref[:, D_MODEL:].astype(jnp.float32) * y_c)
    mix_ref[...] = dot(merged.astype(jnp.bfloat16), wout_ref[...])


def _mix(rel_bias, sink, qkv, dw, gt, conv_b, cln_g, cln_b, w_ao, w_co, w_out):
    batch, seq, _ = gt.shape
    ts = TS_MIX
    assert seq % ts == 0 and ts % BLOCK == 0 and seq // BLOCK >= 2
    n_seq_blocks = seq // BLOCK
    buckets = jnp.asarray(_band_buckets())
    tile = lambda b, i: (b, i, 0)
    return pl.pallas_call(
        functools.partial(_mix_kernel, n_seq_blocks),
        grid=(batch, seq // ts),
        in_specs=[
            _smem_spec(),
            _smem_spec(),
            _const_spec((BLOCK, BAND)),
            pl.BlockSpec((None, ts, Q_COLS), tile),
            pl.BlockSpec((None, seq, 2 * KV_COLS), lambda b, i: (b, 0, Q_COLS // (2 * KV_COLS))),
            pl.BlockSpec((None, ts, CONV_CH), tile),
            pl.BlockSpec((None, ts, GATE_COLS), tile),
            _const_spec((1, CONV_CH)),
            _const_spec((1, CONV_CH)),
            _const_spec((1, CONV_CH)),
            _const_spec((Q_COLS, D_MODEL)),
            _const_spec((CONV_CH, D_MODEL)),
            _const_spec((D_MODEL, D_MODEL)),
        ],
        out_specs=pl.BlockSpec((None, ts, D_MODEL), tile),
        out_shape=jax.ShapeDtypeStruct((batch, seq, D_MODEL), jnp.float32),
        scratch_shapes=[
            pltpu.VMEM((3, N_HEADS, BLOCK, BAND), jnp.float32),
            pltpu.VMEM((ts, Q_COLS), jnp.bfloat16),
        ],
        compiler_params=pltpu.CompilerParams(
            dimension_semantics=(_ARB, _ARB), vmem_limit_bytes=VMEM_LIMIT_BYTES),
        name="token_mix",
    )(rel_bias, sink, buckets, qkv, qkv, dw, gt, conv_b, cln_g, cln_b, w_ao, w_co, w_out)


def _ffn_kernel(xn_ref, mix_ref, g1_ref, b1_ref, wgu_ref, wd_ref, g_ref, b_ref, o_ref):
    dot = functools.partial(jnp.dot, preferred_element_type=jnp.float32)
    n_sub = xn_ref.shape[0] // FFN_SUBTILE
    state = {}

    def up_stage(s):
        rows = slice(s * FFN_SUBTILE, (s + 1) * FFN_SUBTILE)
        h = _layer_norm(ALPHA * xn_ref[rows, :] + mix_ref[rows, :], g1_ref[...], b1_ref[...])
        gu = dot(h.astype(jnp.bfloat16), wgu_ref[...])
        hid = []
        for c0 in range(0, gu.shape[1], 2 * MXU_COLS):
            gate = gu[:, c0:c0 + MXU_COLS]
            up = gu[:, c0 + MXU_COLS:c0 + 2 * MXU_COLS]
            hid.append((gate * jax.nn.sigmoid(gate) * up).astype(jnp.bfloat16))
        state[s] = (h, jnp.concatenate(hid, axis=1))

    def down_stage(s):
        rows = slice(s * FFN_SUBTILE, (s + 1) * FFN_SUBTILE)
        h, hid = state.pop(s)
        o_ref[rows, :] = _layer_norm(ALPHA * h + dot(hid, wd_ref[...]), g_ref[...], b_ref[...])

    up_stage(0)
    for s in range(n_sub):
        if s + 1 < n_sub:
            up_stage(s + 1)
        down_stage(s)


def _ffn(xn2d, mix2d, ln1_g, ln1_b, w_gu, w_down, ln_g, ln_b):
    n_tok = xn2d.shape[0]
    assert n_tok % TM_FFN == 0 and mix2d.shape == xn2d.shape
    row = lambda i: (i, 0)
    return pl.pallas_call(
        _ffn_kernel,
        grid=(n_tok // TM_FFN,),
        in_specs=[
            pl.BlockSpec((TM_FFN, D_MODEL), row),
            pl.BlockSpec((TM_FFN, D_MODEL), row),
            _const_spec((1, D_MODEL)),
            _const_spec((1, D_MODEL)),
            _const_spec(w_gu.shape),
            _const_spec(w_down.shape),
            _const_spec((1, D_MODEL)),
            _const_spec((1, D_MODEL)),
        ],
        out_specs=pl.BlockSpec((TM_FFN, D_MODEL), row),
        out_shape=jax.ShapeDtypeStruct((n_tok, D_MODEL), jnp.float32),
        compiler_params=pltpu.CompilerParams(
            dimension_semantics=(_ARB,), vmem_limit_bytes=VMEM_LIMIT_BYTES),
        name="swiglu_ffn",
    )(xn2d, mix2d, ln1_g, ln1_b, w_gu, w_down, ln_g, ln_b)


def kernel(x, ln_in_g, ln_in_b, rel_bias, w_in, b_gate, conv_w, conv_b, conv_ln_g, conv_ln_b,
           w_conv_out, w_attn_out, sink, w_out, ln1_g, ln1_b, w_gate, w_up, w_down, ln2_g, ln2_b):
    batch, seq, d_model = x.shape
    assert d_model == D_MODEL and w_in.shape[0] == DEPTH
    row = lambda v: v.reshape(1, -1)
    n_tok = batch * seq

    (xn, qkv, gt, dw), (w_ao, w_co, w_o, w_gu, w_d) = _inproj(
        x, row(ln_in_g), row(ln_in_b), w_in[0], row(b_gate[0]), conv_w[0],
        (w_attn_out[0], w_conv_out[0], w_out[0], w_gate[0], w_up[0], w_down[0]))
    mix = _mix(rel_bias, sink[0], qkv, dw, gt,
               row(conv_b[0]), row(conv_ln_g[0]), row(conv_ln_b[0]), w_ao, w_co, w_o)
    out = _ffn(xn.reshape(n_tok, D_MODEL), mix.reshape(n_tok, D_MODEL),
               row(ln1_g[0]), row(ln1_b[0]), w_gu, w_d, row(ln2_g[0]), row(ln2_b[0]))
    return out.reshape(batch, seq, D_MODEL)
```

```python
import functools

import numpy as np
import jax
import jax.numpy as jnp
from jax import lax
from jax.experimental import pallas as pl
from jax.experimental.pallas import tpu as pltpu

D_MODEL = 1024
HEAD_DIM = 64
N_HEADS = 8
N_KV_HEADS = 2
WINDOW = 128
BLOCK = 128
NUM_BUCKETS = 32
MAX_DISTANCE = 128
CONV_CH = 512
CONV_WIDTH = 31
CONV_HALF = (CONV_WIDTH - 1) // 2
Q_COLS = N_HEADS * HEAD_DIM
KV_COLS = N_KV_HEADS * HEAD_DIM
CONV_COLS = 2 * CONV_CH
GATE_COLS = 2 * D_MODEL
QKV_COLS = Q_COLS + 2 * KV_COLS
DEPTH = 1
ALPHA = (2.0 * DEPTH) ** 0.25
LN_EPS = 1e-5
LOG2E = 1.4426950408889634
Q_SCALE = HEAD_DIM ** -0.5 * LOG2E
NEG_INF = -1e30
MASKED_BUCKET = NUM_BUCKETS

HEAD_GROUPS = (((0, 2, 5, 7), False), ((1, 3, 4, 6), True))
BAND = 3 * BLOCK
HALO = 16
CONV_ROWS = 64
CONV_GROUP = 4
SUBLANES = 8
BF16_SUBLANES = 16
LANES = 128
MXU_COLS = 256
CONV_SLABS = CONV_CH // LANES

VMEM_LIMIT_BYTES = 56 * 1024 * 1024
INPROJ_VMEM_LIMIT_BYTES = 60 * 1024 * 1024
WCAST_ROWS = 128

TS_IN = 512
TS_MIX = 512
TM_FFN = 1024
FFN_SUBTILE = 256

_ARB = "arbitrary"


def _layer_norm(x, g, b):
    mu = jnp.mean(x, axis=-1, keepdims=True)
    xc = x - mu
    var = jnp.mean(xc * xc, axis=-1, keepdims=True)
    return xc * lax.rsqrt(var + LN_EPS) * g + b


def _const_spec(shape):
    return pl.BlockSpec(shape, lambda *_: (0,) * len(shape), pipeline_mode=pl.Buffered(1))


def _smem_spec():
    return pl.BlockSpec(memory_space=pltpu.SMEM)


def _halo_specs(ts, seq, cols):
    per_tile = ts // HALO
    last = seq // HALO - 1
    prev = pl.BlockSpec((None, HALO, cols), lambda b, i: (b, jnp.maximum(i * per_tile - 1, 0), 0))
    nxt = pl.BlockSpec((None, HALO, cols), lambda b, i: (b, jnp.minimum((i + 1) * per_tile, last), 0))
    return prev, nxt


def _inproj_kernel(x_ref, xp_ref, xnx_ref, g_ref, b_ref, w32_ref, bgate_ref,
                   convw_ref,
                   wao_ref, wco_ref, wout_ref, wgate_ref, wup_ref, wdown_ref,
                   xn_ref, qkv_ref, gt_ref, dw_ref,
                   wao_o, wco_o, wout_o, wgu_o, wdown_o,
                   w_ref, xb_s, wb_s, ge_s, go_s):
    b_id = pl.program_id(0)
    i_id = pl.program_id(1)
    n_tiles = pl.num_programs(1)
    ts = x_ref.shape[0]
    dot = functools.partial(jnp.dot, preferred_element_type=jnp.float32)
    o1 = QKV_COLS
    o2 = o1 + CONV_COLS
    o3 = o2 + D_MODEL

    @pl.when((b_id == 0) & (i_id == 0))
    def _init():
        for k in range(CONV_WIDTH):
            wb_s[k] = jnp.broadcast_to(convw_ref[k:k + 1, :].astype(wb_s.dtype), (BF16_SUBLANES, CONV_CH))
        for r0 in range(0, w32_ref.shape[0], WCAST_ROWS):
            w_ref[r0:r0 + WCAST_ROWS, :] = w32_ref[r0:r0 + WCAST_ROWS, :].astype(w_ref.dtype)

    @pl.when(i_id == 0)
    def _cast_weights():
        for src, dst in ((wao_ref, wao_o), (wco_ref, wco_o), (wout_ref, wout_o), (wdown_ref, wdown_o)):
            dst[...] = src[...].astype(dst.dtype)
        for j in range(wgate_ref.shape[1] // MXU_COLS):
            src_cols = slice(j * MXU_COLS, (j + 1) * MXU_COLS)
            wgu_o[:, 2 * j * MXU_COLS:(2 * j + 1) * MXU_COLS] = wgate_ref[:, src_cols].astype(wgu_o.dtype)
            wgu_o[:, (2 * j + 1) * MXU_COLS:(2 * j + 2) * MXU_COLS] = wup_ref[:, src_cols].astype(wgu_o.dtype)

    g = g_ref[...]
    b = b_ref[...]
    xn = _layer_norm(x_ref[...], g, b)
    xn_ref[...] = xn
    xb_s[0:HALO, :] = _layer_norm(xp_ref[...], g, b).astype(xb_s.dtype)
    xb_s[HALO:HALO + ts, :] = xn.astype(xb_s.dtype)
    xb_s[HALO + ts:2 * HALO + ts, :] = _layer_norm(xnx_ref[...], g, b).astype(xb_s.dtype)

    u = dot(xb_s[...], w_ref[:, o1:o2])
    glu = u[:, :CONV_CH] * jax.nn.sigmoid(u[:, CONV_CH:])
    row = lax.broadcasted_iota(jnp.int32, (ts + 2 * HALO, 1), 0)
    pad = ((row < HALO) & (i_id == 0)) | ((row >= HALO + ts) & (i_id == n_tiles - 1))
    glu = jnp.where(pad, 0.0, glu)
    rows_ext = ts + 2 * HALO
    glu_up = pltpu.roll(glu, rows_ext - 1, axis=0)
    for c in range(CONV_SLABS):
        lanes = slice(c * LANES, (c + 1) * LANES)
        ge_s[c] = pltpu.bitcast(glu[:, lanes].astype(jnp.bfloat16), jnp.uint32)
        go_s[c] = pltpu.bitcast(glu_up[:, lanes].astype(jnp.bfloat16), jnp.uint32)

    xb = xb_s[HALO:HALO + ts, :]
    bgate = bgate_ref[...]
    gt_ref[:, :D_MODEL] = jax.nn.sigmoid(
        dot(xb, w_ref[:, o2:o3]) + bgate[:, :D_MODEL]).astype(gt_ref.dtype)
    gt_ref[:, D_MODEL:] = jax.nn.sigmoid(
        dot(xb, w_ref[:, o3:]) + bgate[:, D_MODEL:]).astype(gt_ref.dtype)
    qkv_ref[:, :Q_COLS] = (dot(xb, w_ref[:, :Q_COLS]) * Q_SCALE).astype(qkv_ref.dtype)
    qkv_ref[:, Q_COLS:] = dot(xb, w_ref[:, Q_COLS:o1]).astype(qkv_ref.dtype)

    first_tap = HALO - CONV_HALF

    def conv_chunk(chunk, carry):
        r0 = pl.multiple_of(chunk * CONV_ROWS, CONV_ROWS)
        w0 = pl.multiple_of(chunk * (CONV_ROWS // 2), CONV_ROWS // 2)
        n_groups = CONV_ROWS // BF16_SUBLANES
        for c in range(CONV_SLABS):
            lanes = slice(c * LANES, (c + 1) * LANES)
            accs = [jnp.zeros((BF16_SUBLANES, LANES), jnp.float32) for _ in range(n_groups)]
            for k0 in range(0, CONV_WIDTH, CONV_GROUP):
                parts = [None] * n_groups
                for k in range(k0, min(k0 + CONV_GROUP, CONV_WIDTH)):
                    w = wb_s[k, :, lanes]
                    for g in range(n_groups):
                        off = g * BF16_SUBLANES + first_tap + k
                        src = go_s if off % 2 else ge_s
                        words = src[c, pl.ds(w0 + off // 2, SUBLANES), :]
                        prod = pltpu.bitcast(words, jnp.bfloat16) * w
                        parts[g] = prod if parts[g] is None else parts[g] + prod
                for g in range(n_groups):
                    accs[g] = accs[g] + parts[g].astype(jnp.float32)
            for g in range(n_groups):
                dw_ref[pl.ds(r0 + g * BF16_SUBLANES, BF16_SUBLANES), lanes] = accs[g]
        return carry

    lax.fori_loop(0, ts // CONV_ROWS, conv_chunk, 0)


def _inproj(x, ln_g, ln_b, w_in, b_gate, conv_w, later_weights):
    batch, seq, _ = x.shape
    ts = TS_IN
    assert seq % ts == 0 and ts % CONV_ROWS == 0 and ts % HALO == 0 and HALO >= CONV_HALF
    tile = lambda b, i: (b, i, 0)
    prev_spec, next_spec = _halo_specs(ts, seq, D_MODEL)
    bf16 = jnp.bfloat16
    w_ao, w_co, w_out, w_gate, w_up, w_down = later_weights
    assert w_gate.shape == w_up.shape and w_gate.shape[1] % MXU_COLS == 0
    cast_out_shapes = [w_ao.shape, w_co.shape, w_out.shape,
                       (w_gate.shape[0], 2 * w_gate.shape[1]), w_down.shape]

    def slice_spec(shape):
        rows, cols = shape
        assert rows % (batch * BF16_SUBLANES) == 0
        return pl.BlockSpec((rows // batch, cols), lambda b, i: (b, 0))

    cast_in_specs = [slice_spec(w.shape) for w in later_weights]
    cast_out_specs = [slice_spec(s) for s in cast_out_shapes]
    outs = pl.pallas_call(
        _inproj_kernel,
        grid=(batch, seq // ts),
        in_specs=[
            pl.BlockSpec((None, ts, D_MODEL), tile),
            prev_spec,
            next_spec,
            _const_spec((1, D_MODEL)),
            _const_spec((1, D_MODEL)),
            _const_spec(w_in.shape),
            _const_spec((1, GATE_COLS)),
            _const_spec((CONV_WIDTH, CONV_CH)),
        ] + cast_in_specs,
        out_specs=[
            pl.BlockSpec((None, ts, D_MODEL), tile),
            pl.BlockSpec((None, ts, QKV_COLS), tile),
            pl.BlockSpec((None, ts, GATE_COLS), tile),
            pl.BlockSpec((None, ts, CONV_CH), tile),
        ] + cast_out_specs,
        out_shape=[
            jax.ShapeDtypeStruct((batch, seq, D_MODEL), jnp.float32),
            jax.ShapeDtypeStruct((batch, seq, QKV_COLS), bf16),
            jax.ShapeDtypeStruct((batch, seq, GATE_COLS), bf16),
            jax.ShapeDtypeStruct((batch, seq, CONV_CH), jnp.float32),
        ] + [jax.ShapeDtypeStruct(s, bf16) for s in cast_out_shapes],
        scratch_shapes=[
            pltpu.VMEM(w_in.shape, bf16),
            pltpu.VMEM((ts + 2 * HALO, D_MODEL), bf16),
            pltpu.VMEM((CONV_WIDTH, BF16_SUBLANES, CONV_CH), bf16),
            pltpu.VMEM((CONV_SLABS, ts // 2 + HALO, LANES), jnp.uint32),
            pltpu.VMEM((CONV_SLABS, ts // 2 + HALO, LANES), jnp.uint32),
        ],
        compiler_params=pltpu.CompilerParams(
            dimension_semantics=(_ARB, _ARB), vmem_limit_bytes=INPROJ_VMEM_LIMIT_BYTES),
        name="ln_inproj_conv",
    )(x, x, x, ln_g, ln_b, w_in, b_gate, conv_w, *later_weights)
    return outs[:4], outs[4:]


def _band_buckets():
    nb = NUM_BUCKETS // 2
    qi = np.arange(BLOCK)[:, None]
    kj = np.arange(BAND)[None, :]
    rel = kj - BLOCK - qi
    ret = (rel > 0).astype(np.int32) * nb
    n = np.abs(rel)
    max_exact = nb // 2
    large = max_exact + (np.log(np.maximum(n, 1) / max_exact)
                         / np.log(MAX_DISTANCE / max_exact) * (nb - max_exact)).astype(np.int32)
    large = np.minimum(large, nb - 1)
    buckets = (ret + np.where(n < max_exact, n, large)).astype(np.int32)
    return np.where(n <= WINDOW, buckets, MASKED_BUCKET).astype(np.int32)


def _mix_kernel(n_seq_blocks,
                relb_ref, sink_ref, bk_ref,
                q_ref, kv_ref, dw_ref, gt_ref, xn_ref,
                convb_ref, clng_ref, clnb_ref,
                wao_ref, wco_ref, wout_ref,
                r_ref,
                bias_s, o_s):
    b_id = pl.program_id(0)
    i_id = pl.program_id(1)
    ts = q_ref.shape[0]
    blocks_per_tile = ts // BLOCK
    dot = functools.partial(jnp.dot, preferred_element_type=jnp.float32)

    @pl.when((b_id == 0) & (i_id == 0))
    def _init():
        bk = bk_ref[...]
        col = lax.broadcasted_iota(jnp.int32, (BLOCK, BAND), 1)
        for h in range(N_HEADS):
            def body(b, acc):
                return jnp.where(bk == b, relb_ref[b, h], acc)
            mid = LOG2E * lax.fori_loop(0, NUM_BUCKETS, body,
                                        jnp.full((BLOCK, BAND), NEG_INF, jnp.float32))
            bias_s[0, h] = jnp.where(col < BLOCK, NEG_INF, mid)
            bias_s[1, h] = mid
            bias_s[2, h] = jnp.where(col >= 2 * BLOCK, NEG_INF, mid)

    lane = lax.broadcasted_iota(jnp.int32, (BLOCK, LANES), 1)
    low_half = lane < HEAD_DIM
    nt = (((1,), (1,)), ((), ()))

    def block_operands(j):
        n = i_id * blocks_per_tile + j
        variant = jnp.where(n == 0, 0, jnp.where(n == n_seq_blocks - 1, 2, 1))
        prev_r = pl.multiple_of(jnp.maximum(n - 1, 0) * BLOCK, BLOCK)
        own_r = pl.multiple_of(n * BLOCK, BLOCK)
        next_r = pl.multiple_of(jnp.minimum(n + 1, n_seq_blocks - 1) * BLOCK, BLOCK)
        kvb = jnp.concatenate([kv_ref[pl.ds(prev_r, BLOCK), :],
                               kv_ref[pl.ds(own_r, BLOCK), :],
                               kv_ref[pl.ds(next_r, BLOCK), :]], axis=0)
        kk = kvb[:, :LANES]
        vv = kvb[:, LANES:]
        q = q_ref[j * BLOCK:(j + 1) * BLOCK, :]
        zero = jnp.zeros((BLOCK, LANES), q.dtype)
        q_head = {}
        for p in range(N_HEADS // 2):
            q2 = q[:, p * LANES:(p + 1) * LANES]
            q_head[2 * p] = jnp.where(low_half, q2, zero)
            q_head[2 * p + 1] = jnp.where(low_half, zero, q2)
        return dict(variant=variant, q_head=q_head,
                    k=(kk, pltpu.roll(kk, HEAD_DIM, axis=1)),
                    v=(vv, pltpu.roll(vv, HEAD_DIM, axis=1)))

    def scores(blk, heads, swapped):
        lhs = jnp.concatenate([blk["q_head"][h] for h in heads], axis=0)
        return lax.dot_general(lhs, blk["k"][swapped], nt, preferred_element_type=jnp.float32)

    def softmax_values(blk, heads, swapped, s_all, o_head):
        ps, inv_ls = [], []
        for r, h in enumerate(heads):
            s = s_all[r * BLOCK:(r + 1) * BLOCK, :] + bias_s[blk["variant"], h]
            sink = jnp.full((BLOCK, 1), sink_ref[h], jnp.float32) * LOG2E
            m = jnp.maximum(jnp.max(s, axis=-1, keepdims=True), sink)
            p = jnp.exp2(s - m)
            l = jnp.sum(p, axis=-1, keepdims=True) + jnp.exp2(sink - m)
            ps.append(p.astype(jnp.bfloat16))
            inv_ls.append(1.0 / l)
        r_all = dot(jnp.concatenate(ps, axis=0), blk["v"][swapped])
        for r, h in enumerate(heads):
            o_head[h] = r_all[r * BLOCK:(r + 1) * BLOCK, :] * inv_ls[r]

    stages = [(j, heads, int(swapped)) for j in range(blocks_per_tile) for heads, swapped in HEAD_GROUPS]
    yc_chunk = D_MODEL * 2 // len(stages)
    assert yc_chunk % MXU_COLS == 0 and len(stages) % 2 == 0
    yc_parts = []
    y = _layer_norm(dw_ref[...] + convb_ref[...], clng_ref[...], clnb_ref[...])
    yc = (y * jax.nn.sigmoid(y)).astype(jnp.bfloat16)
    blocks = {}
    o_heads = {}
    pending = None
    for c in range(len(stages) + 1):
        nxt = None
        if c < len(stages):
            j, heads, swapped = stages[c]
            if j not in blocks:
                blocks[j] = block_operands(j)
                o_heads[j] = {}
            nxt = (j, heads, swapped, scores(blocks[j], heads, swapped))
        if c % 2 == 1 and len(yc_parts) * yc_chunk < D_MODEL:
            c0 = len(yc_parts) * yc_chunk
            yc_parts.append(dot(yc, wco_ref[:, c0:c0 + yc_chunk]))
        if pending is not None:
            j, heads, swapped, s_all = pending
            softmax_values(blocks[j], heads, swapped, s_all, o_heads[j])
            if len(o_heads[j]) == N_HEADS:
                o_pairs = [jnp.where(low_half, o_heads[j][2 * p], o_heads[j][2 * p + 1])
                           for p in range(N_HEADS // 2)]
                o_s[j * BLOCK:(j + 1) * BLOCK, :] = jnp.concatenate(o_pairs, axis=1).astype(o_s.dtype)
        pending = nxt

    y_a = dot(o_s[...], wao_ref[...])
    y_c = jnp.concatenate(yc_parts, axis=1)
    merged = (gt_ref[:, :D_MODEL].astype(jnp.float32) * y_a
              + gt_ref[:, D_MODEL:].astype(jnp.float32) * y_c)
    mix = dot(merged.astype(jnp.bfloat16), wout_ref[...])
    r_ref[...] = ALPHA * xn_ref[...] + mix


def _mix(rel_bias, sink, qkv, dw, gt, xn, conv_b, cln_g, cln_b, w_ao, w_co, w_out):
    batch, seq, _ = xn.shape
    ts = TS_MIX
    assert seq % ts == 0 and ts % BLOCK == 0 and seq // BLOCK >= 2
    n_seq_blocks = seq // BLOCK
    buckets = jnp.asarray(_band_buckets())
    tile = lambda b, i: (b, i, 0)
    return pl.pallas_call(
        functools.partial(_mix_kernel, n_seq_blocks),
        grid=(batch, seq // ts),
        in_specs=[
            _smem_spec(),
            _smem_spec(),
            _const_spec((BLOCK, BAND)),
            pl.BlockSpec((None, ts, Q_COLS), tile),
            pl.BlockSpec((None, seq, 2 * KV_COLS), lambda b, i: (b, 0, Q_COLS // (2 * KV_COLS))),
            pl.BlockSpec((None, ts, CONV_CH), tile),
            pl.BlockSpec((None, ts, GATE_COLS), tile),
            pl.BlockSpec((None, ts, D_MODEL), tile),
            _const_spec((1, CONV_CH)),
            _const_spec((1, CONV_CH)),
            _const_spec((1, CONV_CH)),
            _const_spec((Q_COLS, D_MODEL)),
            _const_spec((CONV_CH, D_MODEL)),
            _const_spec((D_MODEL, D_MODEL)),
        ],
        out_specs=pl.BlockSpec((None, ts, D_MODEL), tile),
        out_shape=jax.ShapeDtypeStruct((batch, seq, D_MODEL), jnp.float32),
        scratch_shapes=[
            pltpu.VMEM((3, N_HEADS, BLOCK, BAND), jnp.float32),
            pltpu.VMEM((ts, Q_COLS), jnp.bfloat16),
        ],
        compiler_params=pltpu.CompilerParams(
            dimension_semantics=(_ARB, _ARB), vmem_limit_bytes=VMEM_LIMIT_BYTES),
        name="token_mix",
    )(rel_bias, sink, buckets, qkv, qkv, dw, gt, xn, conv_b, cln_g, cln_b, w_ao, w_co, w_out)


def _ffn_kernel(r_ref, g1_ref, b1_ref, wgu_ref, wd_ref, g_ref, b_ref, o_ref):
    dot = functools.partial(jnp.dot, preferred_element_type=jnp.float32)
    n_sub = r_ref.shape[0] // FFN_SUBTILE
    state = {}

    def up_stage(s):
        rows = slice(s * FFN_SUBTILE, (s + 1) * FFN_SUBTILE)
        h = _layer_norm(r_ref[rows, :], g1_ref[...], b1_ref[...])
        gu = dot(h.astype(jnp.bfloat16), wgu_ref[...])
        hid = []
        for c0 in range(0, gu.shape[1], 2 * MXU_COLS):
            gate = gu[:, c0:c0 + MXU_COLS]
            up = gu[:, c0 + MXU_COLS:c0 + 2 * MXU_COLS]
            hid.append((gate * jax.nn.sigmoid(gate) * up).astype(jnp.bfloat16))
        state[s] = (h, jnp.concatenate(hid, axis=1))

    def down_stage(s):
        rows = slice(s * FFN_SUBTILE, (s + 1) * FFN_SUBTILE)
        h, hid = state.pop(s)
        o_ref[rows, :] = _layer_norm(ALPHA * h + dot(hid, wd_ref[...]), g_ref[...], b_ref[...])

    for s in range(n_sub):
        up_stage(s)
    for s in range(n_sub):
        down_stage(s)


def _ffn(r2d, ln1_g, ln1_b, w_gu, w_down, ln_g, ln_b):
    n_tok = r2d.shape[0]
    assert n_tok % TM_FFN == 0
    row = lambda i: (i, 0)
    return pl.pallas_call(
        _ffn_kernel,
        grid=(n_tok // TM_FFN,),
        in_specs=[
            pl.BlockSpec((TM_FFN, D_MODEL), row),
            _const_spec((1, D_MODEL)),
            _const_spec((1, D_MODEL)),
            _const_spec(w_gu.shape),
            _const_spec(w_down.shape),
            _const_spec((1, D_MODEL)),
            _const_spec((1, D_MODEL)),
        ],
        out_specs=pl.BlockSpec((TM_FFN, D_MODEL), row),
        out_shape=jax.ShapeDtypeStruct((n_tok, D_MODEL), jnp.float32),
        compiler_params=pltpu.CompilerParams(
            dimension_semantics=(_ARB,), vmem_limit_bytes=VMEM_LIMIT_BYTES),
        name="swiglu_ffn",
    )(r2d, ln1_g, ln1_b, w_gu, w_down, ln_g, ln_b)


def kernel(x, ln_in_g, ln_in_b, rel_bias, w_in, b_gate, conv_w, conv_b, conv_ln_g, conv_ln_b,
           w_conv_out, w_attn_out, sink, w_out, ln1_g, ln1_b, w_gate, w_up, w_down, ln2_g, ln2_b):
    batch, seq, d_model = x.shape
    assert d_model == D_MODEL and w_in.shape[0] == DEPTH
    row = lambda v: v.reshape(1, -1)
    n_tok = batch * seq

    (xn, qkv, gt, dw), (w_ao, w_co, w_o, w_gu, w_d) = _inproj(
        x, row(ln_in_g), row(ln_in_b), w_in[0], row(b_gate[0]), conv_w[0],
        (w_attn_out[0], w_conv_out[0], w_out[0], w_gate[0], w_up[0], w_down[0]))
    r = _mix(rel_bias, sink[0], qkv, dw, gt, xn,
             row(conv_b[0]), row(conv_ln_g[0]), row(conv_ln_b[0]), w_ao, w_co, w_o)
    out = _ffn(r.reshape(n_tok, D_MODEL), row(ln1_g[0]), row(ln1_b[0]), w_gu, w_d,
               row(ln2_g[0]), row(ln2_b[0]))
    return out.reshape(batch, seq, D_MODEL)
```

```python
import functools

import numpy as np
import jax
import jax.numpy as jnp
from jax import lax
from jax.experimental import pallas as pl
from jax.experimental.pallas import tpu as pltpu

D_MODEL = 1024
HEAD_DIM = 64
N_HEADS = 8
N_KV_HEADS = 2
WINDOW = 128
BLOCK = 128
NUM_BUCKETS = 32
MAX_DISTANCE = 128
CONV_CH = 512
CONV_WIDTH = 31
CONV_HALF = (CONV_WIDTH - 1) // 2
Q_COLS = N_HEADS * HEAD_DIM
KV_COLS = N_KV_HEADS * HEAD_DIM
CONV_COLS = 2 * CONV_CH
GATE_COLS = 2 * D_MODEL
QKV_COLS = Q_COLS + 2 * KV_COLS
DEPTH = 1
ALPHA = (2.0 * DEPTH) ** 0.25
LN_EPS = 1e-5
LOG2E = 1.4426950408889634
Q_SCALE = HEAD_DIM ** -0.5 * LOG2E
NEG_INF = -1e30
MASKED_BUCKET = NUM_BUCKETS

HEAD_GROUPS = (((0, 2, 5, 7), False), ((1, 3, 4, 6), True))
BAND = 3 * BLOCK
HALO = 16
CONV_ROWS = 64
CONV_GROUP = 4
SUBLANES = 8
BF16_SUBLANES = 16
LANES = 128
MXU_COLS = 256
CONV_SLABS = CONV_CH // LANES

VMEM_LIMIT_BYTES = 56 * 1024 * 1024
INPROJ_VMEM_LIMIT_BYTES = 60 * 1024 * 1024
WCAST_ROWS = 128

TS_IN = 512
TS_MIX = 512
TM_FFN = 1024
FFN_SUBTILE = 256
FFN_EDGE_SUBTILE = 128

_ARB = "arbitrary"


def _layer_norm(x, g, b):
    mu = jnp.mean(x, axis=-1, keepdims=True)
    xc = x - mu
    var = jnp.mean(xc * xc, axis=-1, keepdims=True)
    return xc * lax.rsqrt(var + LN_EPS) * g + b


def _const_spec(shape):
    return pl.BlockSpec(shape, lambda *_: (0,) * len(shape), pipeline_mode=pl.Buffered(1))


def _smem_spec():
    return pl.BlockSpec(memory_space=pltpu.SMEM)


def _halo_specs(ts, seq, cols):
    per_tile = ts // HALO
    last = seq // HALO - 1
    prev = pl.BlockSpec((None, HALO, cols), lambda b, i: (b, jnp.maximum(i * per_tile - 1, 0), 0))
    nxt = pl.BlockSpec((None, HALO, cols), lambda b, i: (b, jnp.minimum((i + 1) * per_tile, last), 0))
    return prev, nxt


def _inproj_kernel(x_ref, xp_ref, xnx_ref, g_ref, b_ref, w32_ref, bgate_ref,
                   convw_ref,
                   wao_ref, wco_ref, wout_ref, wgate_ref, wup_ref, wdown_ref,
                   xn_ref, qkv_ref, gt_ref, dw_ref,
                   wao_o, wco_o, wout_o, wgu_o, wdown_o,
                   w_ref, xb_s, wb_s, ge_s, go_s):
    b_id = pl.program_id(0)
    i_id = pl.program_id(1)
    n_tiles = pl.num_programs(1)
    ts = x_ref.shape[0]
    dot = functools.partial(jnp.dot, preferred_element_type=jnp.float32)
    o1 = QKV_COLS
    o2 = o1 + CONV_COLS
    o3 = o2 + D_MODEL

    @pl.when((b_id == 0) & (i_id == 0))
    def _init():
        for k in range(CONV_WIDTH):
            wb_s[k] = jnp.broadcast_to(convw_ref[k:k + 1, :].astype(wb_s.dtype), (BF16_SUBLANES, CONV_CH))
        for r0 in range(0, w32_ref.shape[0], WCAST_ROWS):
            w_ref[r0:r0 + WCAST_ROWS, :] = w32_ref[r0:r0 + WCAST_ROWS, :].astype(w_ref.dtype)

    @pl.when(i_id == 0)
    def _cast_weights():
        for src, dst in ((wao_ref, wao_o), (wco_ref, wco_o), (wout_ref, wout_o), (wdown_ref, wdown_o)):
            dst[...] = src[...].astype(dst.dtype)
        for j in range(wgate_ref.shape[1] // MXU_COLS):
            src_cols = slice(j * MXU_COLS, (j + 1) * MXU_COLS)
            wgu_o[:, 2 * j * MXU_COLS:(2 * j + 1) * MXU_COLS] = wgate_ref[:, src_cols].astype(wgu_o.dtype)
            wgu_o[:, (2 * j + 1) * MXU_COLS:(2 * j + 2) * MXU_COLS] = wup_ref[:, src_cols].astype(wgu_o.dtype)

    g = g_ref[...]
    b = b_ref[...]
    xn = _layer_norm(x_ref[...], g, b)
    xn_ref[...] = xn
    xb_s[0:HALO, :] = _layer_norm(xp_ref[...], g, b).astype(xb_s.dtype)
    xb_s[HALO:HALO + ts, :] = xn.astype(xb_s.dtype)
    xb_s[HALO + ts:2 * HALO + ts, :] = _layer_norm(xnx_ref[...], g, b).astype(xb_s.dtype)

    u = dot(xb_s[...], w_ref[:, o1:o2])
    glu = u[:, :CONV_CH] * jax.nn.sigmoid(u[:, CONV_CH:])
    row = lax.broadcasted_iota(jnp.int32, (ts + 2 * HALO, 1), 0)
    pad = ((row < HALO) & (i_id == 0)) | ((row >= HALO + ts) & (i_id == n_tiles - 1))
    glu = jnp.where(pad, 0.0, glu)
    rows_ext = ts + 2 * HALO
    glu_up = pltpu.roll(glu, rows_ext - 1, axis=0)
    for c in range(CONV_SLABS):
        lanes = slice(c * LANES, (c + 1) * LANES)
        ge_s[c] = pltpu.bitcast(glu[:, lanes].astype(jnp.bfloat16), jnp.uint32)
        go_s[c] = pltpu.bitcast(glu_up[:, lanes].astype(jnp.bfloat16), jnp.uint32)

    xb = xb_s[HALO:HALO + ts, :]
    bgate = bgate_ref[...]
    gt_ref[:, :D_MODEL] = jax.nn.sigmoid(
        dot(xb, w_ref[:, o2:o3]) + bgate[:, :D_MODEL]).astype(gt_ref.dtype)
    gt_ref[:, D_MODEL:] = jax.nn.sigmoid(
        dot(xb, w_ref[:, o3:]) + bgate[:, D_MODEL:]).astype(gt_ref.dtype)
    qkv_ref[:, :Q_COLS] = (dot(xb, w_ref[:, :Q_COLS]) * Q_SCALE).astype(qkv_ref.dtype)
    qkv_ref[:, Q_COLS:] = dot(xb, w_ref[:, Q_COLS:o1]).astype(qkv_ref.dtype)

    first_tap = HALO - CONV_HALF

    def conv_chunk(chunk, carry):
        r0 = pl.multiple_of(chunk * CONV_ROWS, CONV_ROWS)
        w0 = pl.multiple_of(chunk * (CONV_ROWS // 2), CONV_ROWS // 2)
        n_groups = CONV_ROWS // BF16_SUBLANES
        for c in range(CONV_SLABS):
            lanes = slice(c * LANES, (c + 1) * LANES)
            accs = [jnp.zeros((BF16_SUBLANES, LANES), jnp.float32) for _ in range(n_groups)]
            for k0 in range(0, CONV_WIDTH, CONV_GROUP):
                parts = [None] * n_groups
                for k in range(k0, min(k0 + CONV_GROUP, CONV_WIDTH)):
                    w = wb_s[k, :, lanes]
                    for g in range(n_groups):
                        off = g * BF16_SUBLANES + first_tap + k
                        src = go_s if off % 2 else ge_s
                        words = src[c, pl.ds(w0 + off // 2, SUBLANES), :]
                        prod = pltpu.bitcast(words, jnp.bfloat16) * w
                        parts[g] = prod if parts[g] is None else parts[g] + prod
                for g in range(n_groups):
                    accs[g] = accs[g] + parts[g].astype(jnp.float32)
            for g in range(n_groups):
                dw_ref[pl.ds(r0 + g * BF16_SUBLANES, BF16_SUBLANES), lanes] = accs[g]
        return carry

    lax.fori_loop(0, ts // CONV_ROWS, conv_chunk, 0)


def _inproj(x, ln_g, ln_b, w_in, b_gate, conv_w, later_weights):
    batch, seq, _ = x.shape
    ts = TS_IN
    assert seq % ts == 0 and ts % CONV_ROWS == 0 and ts % HALO == 0 and HALO >= CONV_HALF
    tile = lambda b, i: (b, i, 0)
    prev_spec, next_spec = _halo_specs(ts, seq, D_MODEL)
    bf16 = jnp.bfloat16
    w_ao, w_co, w_out, w_gate, w_up, w_down = later_weights
    assert w_gate.shape == w_up.shape and w_gate.shape[1] % MXU_COLS == 0
    cast_out_shapes = [w_ao.shape, w_co.shape, w_out.shape,
                       (w_gate.shape[0], 2 * w_gate.shape[1]), w_down.shape]

    def slice_spec(shape):
        rows, cols = shape
        assert rows % (batch * BF16_SUBLANES) == 0
        return pl.BlockSpec((rows // batch, cols), lambda b, i: (b, 0))

    cast_in_specs = [slice_spec(w.shape) for w in later_weights]
    cast_out_specs = [slice_spec(s) for s in cast_out_shapes]
    outs = pl.pallas_call(
        _inproj_kernel,
        grid=(batch, seq // ts),
        in_specs=[
            pl.BlockSpec((None, ts, D_MODEL), tile),
            prev_spec,
            next_spec,
            _const_spec((1, D_MODEL)),
            _const_spec((1, D_MODEL)),
            _const_spec(w_in.shape),
            _const_spec((1, GATE_COLS)),
            _const_spec((CONV_WIDTH, CONV_CH)),
        ] + cast_in_specs,
        out_specs=[
            pl.BlockSpec((None, ts, D_MODEL), tile),
            pl.BlockSpec((None, ts, QKV_COLS), tile),
            pl.BlockSpec((None, ts, GATE_COLS), tile),
            pl.BlockSpec((None, ts, CONV_CH), tile),
        ] + cast_out_specs,
        out_shape=[
            jax.ShapeDtypeStruct((batch, seq, D_MODEL), jnp.float32),
            jax.ShapeDtypeStruct((batch, seq, QKV_COLS), bf16),
            jax.ShapeDtypeStruct((batch, seq, GATE_COLS), bf16),
            jax.ShapeDtypeStruct((batch, seq, CONV_CH), jnp.float32),
        ] + [jax.ShapeDtypeStruct(s, bf16) for s in cast_out_shapes],
        scratch_shapes=[
            pltpu.VMEM(w_in.shape, bf16),
            pltpu.VMEM((ts + 2 * HALO, D_MODEL), bf16),
            pltpu.VMEM((CONV_WIDTH, BF16_SUBLANES, CONV_CH), bf16),
            pltpu.VMEM((CONV_SLABS, ts // 2 + HALO, LANES), jnp.uint32),
            pltpu.VMEM((CONV_SLABS, ts // 2 + HALO, LANES), jnp.uint32),
        ],
        compiler_params=pltpu.CompilerParams(
            dimension_semantics=(_ARB, _ARB), vmem_limit_bytes=INPROJ_VMEM_LIMIT_BYTES),
        name="ln_inproj_conv",
    )(x, x, x, ln_g, ln_b, w_in, b_gate, conv_w, *later_weights)
    return outs[:4], outs[4:]


def _band_buckets():
    nb = NUM_BUCKETS // 2
    qi = np.arange(BLOCK)[:, None]
    kj = np.arange(BAND)[None, :]
    rel = kj - BLOCK - qi
    ret = (rel > 0).astype(np.int32) * nb
    n = np.abs(rel)
    max_exact = nb // 2
    large = max_exact + (np.log(np.maximum(n, 1) / max_exact)
                         / np.log(MAX_DISTANCE / max_exact) * (nb - max_exact)).astype(np.int32)
    large = np.minimum(large, nb - 1)
    buckets = (ret + np.where(n < max_exact, n, large)).astype(np.int32)
    return np.where(n <= WINDOW, buckets, MASKED_BUCKET).astype(np.int32)


def _mix_kernel(n_seq_blocks,
                relb_ref, sink_ref, bk_ref,
                q_ref, kv_ref, dw_ref, gt_ref, xn_ref,
                convb_ref, clng_ref, clnb_ref,
                wao_ref, wco_ref, wout_ref,
                r_ref,
                bias_s, o_s):
    b_id = pl.program_id(0)
    i_id = pl.program_id(1)
    ts = q_ref.shape[0]
    blocks_per_tile = ts // BLOCK
    dot = functools.partial(jnp.dot, preferred_element_type=jnp.float32)

    @pl.when((b_id == 0) & (i_id == 0))
    def _init():
        bk = bk_ref[...]
        col = lax.broadcasted_iota(jnp.int32, (BLOCK, BAND), 1)
        for h in range(N_HEADS):
            def body(b, acc):
                return jnp.where(bk == b, relb_ref[b, h], acc)
            mid = LOG2E * lax.fori_loop(0, NUM_BUCKETS, body,
                                        jnp.full((BLOCK, BAND), NEG_INF, jnp.float32))
            bias_s[0, h] = jnp.where(col < BLOCK, NEG_INF, mid)
            bias_s[1, h] = mid
            bias_s[2, h] = jnp.where(col >= 2 * BLOCK, NEG_INF, mid)

    lane = lax.broadcasted_iota(jnp.int32, (BLOCK, LANES), 1)
    low_half = lane < HEAD_DIM
    nt = (((1,), (1,)), ((), ()))

    def block_operands(j):
        n = i_id * blocks_per_tile + j
        variant = jnp.where(n == 0, 0, jnp.where(n == n_seq_blocks - 1, 2, 1))
        prev_r = pl.multiple_of(jnp.maximum(n - 1, 0) * BLOCK, BLOCK)
        own_r = pl.multiple_of(n * BLOCK, BLOCK)
        next_r = pl.multiple_of(jnp.minimum(n + 1, n_seq_blocks - 1) * BLOCK, BLOCK)
        kvb = jnp.concatenate([kv_ref[pl.ds(prev_r, BLOCK), :],
                               kv_ref[pl.ds(own_r, BLOCK), :],
                               kv_ref[pl.ds(next_r, BLOCK), :]], axis=0)
        kk = kvb[:, :LANES]
        vv = kvb[:, LANES:]
        q = q_ref[j * BLOCK:(j + 1) * BLOCK, :]
        zero = jnp.zeros((BLOCK, LANES), q.dtype)
        q_head = {}
        for p in range(N_HEADS // 2):
            q2 = q[:, p * LANES:(p + 1) * LANES]
            q_head[2 * p] = jnp.where(low_half, q2, zero)
            q_head[2 * p + 1] = jnp.where(low_half, zero, q2)
        return dict(variant=variant, q_head=q_head,
                    k=(kk, pltpu.roll(kk, HEAD_DIM, axis=1)),
                    v=(vv, pltpu.roll(vv, HEAD_DIM, axis=1)))

    def scores(blk, heads, swapped):
        lhs = jnp.concatenate([blk["q_head"][h] for h in heads], axis=0)
        return lax.dot_general(lhs, blk["k"][swapped], nt, preferred_element_type=jnp.float32)

    def softmax_values(blk, heads, swapped, s_all, o_head):
        ps, inv_ls = [], []
        for r, h in enumerate(heads):
            s = s_all[r * BLOCK:(r + 1) * BLOCK, :] + bias_s[blk["variant"], h]
            sink = jnp.full((BLOCK, 1), sink_ref[h], jnp.float32) * LOG2E
            m = jnp.maximum(jnp.max(s, axis=-1, keepdims=True), sink)
            p = jnp.exp2(s - m)
            l = jnp.sum(p, axis=-1, keepdims=True) + jnp.exp2(sink - m)
            ps.append(p.astype(jnp.bfloat16))
            inv_ls.append(1.0 / l)
        r_all = dot(jnp.concatenate(ps, axis=0), blk["v"][swapped])
        for r, h in enumerate(heads):
            o_head[h] = r_all[r * BLOCK:(r + 1) * BLOCK, :] * inv_ls[r]

    stages = [(j, heads, int(swapped)) for j in range(blocks_per_tile) for heads, swapped in HEAD_GROUPS]
    yc_chunk = D_MODEL * 2 // len(stages)
    assert yc_chunk % MXU_COLS == 0 and len(stages) % 2 == 0
    yc_parts = []
    y = _layer_norm(dw_ref[...] + convb_ref[...], clng_ref[...], clnb_ref[...])
    yc = (y * jax.nn.sigmoid(y)).astype(jnp.bfloat16)
    blocks = {}
    o_heads = {}
    pending = None
    for c in range(len(stages) + 1):
        nxt = None
        if c < len(stages):
            j, heads, swapped = stages[c]
            if j not in blocks:
                blocks[j] = block_operands(j)
                o_heads[j] = {}
            nxt = (j, heads, swapped, scores(blocks[j], heads, swapped))
        if c % 2 == 1 and len(yc_parts) * yc_chunk < D_MODEL:
            c0 = len(yc_parts) * yc_chunk
            yc_parts.append(dot(yc, wco_ref[:, c0:c0 + yc_chunk]))
        if pending is not None:
            j, heads, swapped, s_all = pending
            softmax_values(blocks[j], heads, swapped, s_all, o_heads[j])
            if len(o_heads[j]) == N_HEADS:
                o_pairs = [jnp.where(low_half, o_heads[j][2 * p], o_heads[j][2 * p + 1])
                           for p in range(N_HEADS // 2)]
                o_s[j * BLOCK:(j + 1) * BLOCK, :] = jnp.concatenate(o_pairs, axis=1).astype(o_s.dtype)
        pending = nxt

    y_a = dot(o_s[...], wao_ref[...])
    y_c = jnp.concatenate(yc_parts, axis=1)
    merged = (gt_ref[:, :D_MODEL].astype(jnp.float32) * y_a
              + gt_ref[:, D_MODEL:].astype(jnp.float32) * y_c)
    mix = dot(merged.astype(jnp.bfloat16), wout_ref[...])
    r_ref[...] = ALPHA * xn_ref[...] + mix


def _mix(rel_bias, sink, qkv, dw, gt, xn, conv_b, cln_g, cln_b, w_ao, w_co, w_out):
    batch, seq, _ = xn.shape
    ts = TS_MIX
    assert seq % ts == 0 and ts % BLOCK == 0 and seq // BLOCK >= 2
    n_seq_blocks = seq // BLOCK
    buckets = jnp.asarray(_band_buckets())
    tile = lambda b, i: (b, i, 0)
    return pl.pallas_call(
        functools.partial(_mix_kernel, n_seq_blocks),
        grid=(batch, seq // ts),
        in_specs=[
            _smem_spec(),
            _smem_spec(),
            _const_spec((BLOCK, BAND)),
            pl.BlockSpec((None, ts, Q_COLS), tile),
            pl.BlockSpec((None, seq, 2 * KV_COLS), lambda b, i: (b, 0, Q_COLS // (2 * KV_COLS))),
            pl.BlockSpec((None, ts, CONV_CH), tile),
            pl.BlockSpec((None, ts, GATE_COLS), tile),
            pl.BlockSpec((None, ts, D_MODEL), tile),
            _const_spec((1, CONV_CH)),
            _const_spec((1, CONV_CH)),
            _const_spec((1, CONV_CH)),
            _const_spec((Q_COLS, D_MODEL)),
            _const_spec((CONV_CH, D_MODEL)),
            _const_spec((D_MODEL, D_MODEL)),
        ],
        out_specs=pl.BlockSpec((None, ts, D_MODEL), tile),
        out_shape=jax.ShapeDtypeStruct((batch, seq, D_MODEL), jnp.float32),
        scratch_shapes=[
            pltpu.VMEM((3, N_HEADS, BLOCK, BAND), jnp.float32),
            pltpu.VMEM((ts, Q_COLS), jnp.bfloat16),
        ],
        compiler_params=pltpu.CompilerParams(
            dimension_semantics=(_ARB, _ARB), vmem_limit_bytes=VMEM_LIMIT_BYTES),
        name="token_mix",
    )(rel_bias, sink, buckets, qkv, qkv, dw, gt, xn, conv_b, cln_g, cln_b, w_ao, w_co, w_out)


def _ffn_kernel(r_ref, g1_ref, b1_ref, wgu_ref, wd_ref, g_ref, b_ref, o_ref):
    dot = functools.partial(jnp.dot, preferred_element_type=jnp.float32)
    inner = r_ref.shape[0] - 2 * FFN_EDGE_SUBTILE
    assert inner >= 0 and inner % FFN_SUBTILE == 0
    sizes = [FFN_EDGE_SUBTILE] + [FFN_SUBTILE] * (inner // FFN_SUBTILE) + [FFN_EDGE_SUBTILE]
    starts = [sum(sizes[:s]) for s in range(len(sizes) + 1)]
    n_sub = len(sizes)
    state = {}

    def up_stage(s):
        rows = slice(starts[s], starts[s + 1])
        h = _layer_norm(r_ref[rows, :], g1_ref[...], b1_ref[...])
        gu = dot(h.astype(jnp.bfloat16), wgu_ref[...])
        hid = []
        for c0 in range(0, gu.shape[1], 2 * MXU_COLS):
            gate = gu[:, c0:c0 + MXU_COLS]
            up = gu[:, c0 + MXU_COLS:c0 + 2 * MXU_COLS]
            hid.append((gate * jax.nn.sigmoid(gate) * up).astype(jnp.bfloat16))
        state[s] = (h, jnp.concatenate(hid, axis=1))

    def down_stage(s):
        rows = slice(starts[s], starts[s + 1])
        h, hid = state.pop(s)
        o_ref[rows, :] = _layer_norm(ALPHA * h + dot(hid, wd_ref[...]), g_ref[...], b_ref[...])

    for s in range(n_sub):
        up_stage(s)
    for s in range(n_sub):
        down_stage(s)


def _ffn(r2d, ln1_g, ln1_b, w_gu, w_down, ln_g, ln_b):
    n_tok = r2d.shape[0]
    assert n_tok % TM_FFN == 0
    row = lambda i: (i, 0)
    return pl.pallas_call(
        _ffn_kernel,
        grid=(n_tok // TM_FFN,),
        in_specs=[
            pl.BlockSpec((TM_FFN, D_MODEL), row),
            _const_spec((1, D_MODEL)),
            _const_spec((1, D_MODEL)),
            _const_spec(w_gu.shape),
            _const_spec(w_down.shape),
            _const_spec((1, D_MODEL)),
            _const_spec((1, D_MODEL)),
        ],
        out_specs=pl.BlockSpec((TM_FFN, D_MODEL), row),
        out_shape=jax.ShapeDtypeStruct((n_tok, D_MODEL), jnp.float32),
        compiler_params=pltpu.CompilerParams(
            dimension_semantics=(_ARB,), vmem_limit_bytes=VMEM_LIMIT_BYTES),
        name="swiglu_ffn",
    )(r2d, ln1_g, ln1_b, w_gu, w_down, ln_g, ln_b)


def kernel(x, ln_in_g, ln_in_b, rel_bias, w_in, b_gate, conv_w, conv_b, conv_ln_g, conv_ln_b,
           w_conv_out, w_attn_out, sink, w_out, ln1_g, ln1_b, w_gate, w_up, w_down, ln2_g, ln2_b):
    batch, seq, d_model = x.shape
    assert d_model == D_MODEL and w_in.shape[0] == DEPTH
    row = lambda v: v.reshape(1, -1)
    n_tok = batch * seq

    (xn, qkv, gt, dw), (w_ao, w_co, w_o, w_gu, w_d) = _inproj(
        x, row(ln_in_g), row(ln_in_b), w_in[0], row(b_gate[0]), conv_w[0],
        (w_attn_out[0], w_conv_out[0], w_out[0], w_gate[0], w_up[0], w_down[0]))
    r = _mix(rel_bias, sink[0], qkv, dw, gt, xn,
             row(conv_b[0]), row(conv_ln_g[0]), row(conv_ln_b[0]), w_ao, w_co, w_o)
    out = _ffn(r.reshape(n_tok, D_MODEL), row(ln1_g[0]), row(ln1_b[0]), w_gu, w_d,
               row(ln2_g[0]), row(ln2_b[0]))
    return out.reshape(batch, seq, D_MODEL)
```

```python
import functools

import numpy as np
import jax
import jax.numpy as jnp
from jax import lax
from jax.experimental import pallas as pl
from jax.experimental.pallas import tpu as pltpu

D_MODEL = 1024
HEAD_DIM = 64
N_HEADS = 8
N_KV_HEADS = 2
WINDOW = 128
BLOCK = 128
NUM_BUCKETS = 32
MAX_DISTANCE = 128
CONV_CH = 512
CONV_WIDTH = 31
CONV_HALF = (CONV_WIDTH - 1) // 2
Q_COLS = N_HEADS * HEAD_DIM
KV_COLS = N_KV_HEADS * HEAD_DIM
CONV_COLS = 2 * CONV_CH
GATE_COLS = 2 * D_MODEL
QKV_COLS = Q_COLS + 2 * KV_COLS
DEPTH = 1
ALPHA = (2.0 * DEPTH) ** 0.25
LN_EPS = 1e-5
LOG2E = 1.4426950408889634
Q_SCALE = HEAD_DIM ** -0.5 * LOG2E
NEG_INF = -1e30
MASKED_BUCKET = NUM_BUCKETS

HEAD_GROUPS = (((0, 2, 5, 7), False), ((1, 3, 4, 6), True))
BAND = 3 * BLOCK
HALO = 16
CONV_ROWS = 64
CONV_GROUP = 4
SUBLANES = 8
BF16_SUBLANES = 16
LANES = 128
MXU_COLS = 256
CONV_SLABS = CONV_CH // LANES

VMEM_LIMIT_BYTES = 56 * 1024 * 1024
INPROJ_VMEM_LIMIT_BYTES = 60 * 1024 * 1024
WCAST_ROWS = 128

TS_IN = 512
INPROJ_FIRST_ROWS = HALO + 128
TS_MIX = 512
TM_FFN = 1024
FFN_SUBTILE = 256
FFN_EDGE_SUBTILE = 128

_ARB = "arbitrary"


def _layer_norm(x, g, b):
    mu = jnp.mean(x, axis=-1, keepdims=True)
    xc = x - mu
    var = jnp.mean(xc * xc, axis=-1, keepdims=True)
    return xc * lax.rsqrt(var + LN_EPS) * g + b


def _const_spec(shape):
    return pl.BlockSpec(shape, lambda *_: (0,) * len(shape), pipeline_mode=pl.Buffered(1))


def _smem_spec():
    return pl.BlockSpec(memory_space=pltpu.SMEM)


def _halo_specs(ts, seq, cols):
    per_tile = ts // HALO
    last = seq // HALO - 1
    prev = pl.BlockSpec((None, HALO, cols), lambda b, i: (b, jnp.maximum(i * per_tile - 1, 0), 0))
    nxt = pl.BlockSpec((None, HALO, cols), lambda b, i: (b, jnp.minimum((i + 1) * per_tile, last), 0))
    return prev, nxt


def _inproj_kernel(x_ref, xp_ref, xnx_ref, g_ref, b_ref, w32_ref, bgate_ref,
                   convw_ref,
                   wao_ref, wco_ref, wout_ref, wgate_ref, wup_ref, wdown_ref,
                   xn_ref, qkv_ref, gt_ref, dw_ref,
                   wao_o, wco_o, wout_o, wgu_o, wdown_o,
                   w_ref, xb_s, wb_s, ge_s, go_s):
    b_id = pl.program_id(0)
    i_id = pl.program_id(1)
    n_tiles = pl.num_programs(1)
    ts = x_ref.shape[0]
    dot = functools.partial(jnp.dot, preferred_element_type=jnp.float32)
    o1 = QKV_COLS
    o2 = o1 + CONV_COLS
    o3 = o2 + D_MODEL

    @pl.when((b_id == 0) & (i_id == 0))
    def _init():
        for k in range(CONV_WIDTH):
            wb_s[k] = jnp.broadcast_to(convw_ref[k:k + 1, :].astype(wb_s.dtype), (BF16_SUBLANES, CONV_CH))
        for r0 in range(0, w32_ref.shape[0], WCAST_ROWS):
            w_ref[r0:r0 + WCAST_ROWS, :] = w32_ref[r0:r0 + WCAST_ROWS, :].astype(w_ref.dtype)

    @pl.when(i_id == 0)
    def _cast_weights():
        for src, dst in ((wao_ref, wao_o), (wco_ref, wco_o), (wout_ref, wout_o), (wdown_ref, wdown_o)):
            dst[...] = src[...].astype(dst.dtype)
        for j in range(wgate_ref.shape[1] // MXU_COLS):
            src_cols = slice(j * MXU_COLS, (j + 1) * MXU_COLS)
            wgu_o[:, 2 * j * MXU_COLS:(2 * j + 1) * MXU_COLS] = wgate_ref[:, src_cols].astype(wgu_o.dtype)
            wgu_o[:, (2 * j + 1) * MXU_COLS:(2 * j + 2) * MXU_COLS] = wup_ref[:, src_cols].astype(wgu_o.dtype)

    g = g_ref[...]
    b = b_ref[...]
    xn = _layer_norm(x_ref[...], g, b)
    xn_ref[...] = xn
    xb_s[0:HALO, :] = _layer_norm(xp_ref[...], g, b).astype(xb_s.dtype)
    xb_s[HALO:HALO + ts, :] = xn.astype(xb_s.dtype)
    xb_s[HALO + ts:2 * HALO + ts, :] = _layer_norm(xnx_ref[...], g, b).astype(xb_s.dtype)

    u = jnp.concatenate([dot(xb_s[0:INPROJ_FIRST_ROWS, :], w_ref[:, o1:o2]),
                         dot(xb_s[INPROJ_FIRST_ROWS:, :], w_ref[:, o1:o2])], axis=0)
    glu = u[:, :CONV_CH] * jax.nn.sigmoid(u[:, CONV_CH:])
    row = lax.broadcasted_iota(jnp.int32, (ts + 2 * HALO, 1), 0)
    pad = ((row < HALO) & (i_id == 0)) | ((row >= HALO + ts) & (i_id == n_tiles - 1))
    glu = jnp.where(pad, 0.0, glu)
    rows_ext = ts + 2 * HALO
    glu_up = pltpu.roll(glu, rows_ext - 1, axis=0)
    for c in range(CONV_SLABS):
        lanes = slice(c * LANES, (c + 1) * LANES)
        ge_s[c] = pltpu.bitcast(glu[:, lanes].astype(jnp.bfloat16), jnp.uint32)
        go_s[c] = pltpu.bitcast(glu_up[:, lanes].astype(jnp.bfloat16), jnp.uint32)

    xb = xb_s[HALO:HALO + ts, :]
    bgate = bgate_ref[...]
    gt_ref[:, :D_MODEL] = jax.nn.sigmoid(
        dot(xb, w_ref[:, o2:o3]) + bgate[:, :D_MODEL]).astype(gt_ref.dtype)
    gt_ref[:, D_MODEL:] = jax.nn.sigmoid(
        dot(xb, w_ref[:, o3:]) + bgate[:, D_MODEL:]).astype(gt_ref.dtype)
    qkv_ref[:, :Q_COLS] = (dot(xb, w_ref[:, :Q_COLS]) * Q_SCALE).astype(qkv_ref.dtype)
    qkv_ref[:, Q_COLS:] = dot(xb, w_ref[:, Q_COLS:o1]).astype(qkv_ref.dtype)

    first_tap = HALO - CONV_HALF

    def conv_chunk(chunk, carry):
        r0 = pl.multiple_of(chunk * CONV_ROWS, CONV_ROWS)
        w0 = pl.multiple_of(chunk * (CONV_ROWS // 2), CONV_ROWS // 2)
        n_groups = CONV_ROWS // BF16_SUBLANES
        for c in range(CONV_SLABS):
            lanes = slice(c * LANES, (c + 1) * LANES)
            accs = [jnp.zeros((BF16_SUBLANES, LANES), jnp.float32) for _ in range(n_groups)]
            for k0 in range(0, CONV_WIDTH, CONV_GROUP):
                parts = [None] * n_groups
                for k in range(k0, min(k0 + CONV_GROUP, CONV_WIDTH)):
                    w = wb_s[k, :, lanes]
                    for g in range(n_groups):
                        off = g * BF16_SUBLANES + first_tap + k
                        src = go_s if off % 2 else ge_s
                        words = src[c, pl.ds(w0 + off // 2, SUBLANES), :]
                        prod = pltpu.bitcast(words, jnp.bfloat16) * w
                        parts[g] = prod if parts[g] is None else parts[g] + prod
                for g in range(n_groups):
                    accs[g] = accs[g] + parts[g].astype(jnp.float32)
            for g in range(n_groups):
                dw_ref[pl.ds(r0 + g * BF16_SUBLANES, BF16_SUBLANES), lanes] = accs[g]
        return carry

    lax.fori_loop(0, ts // CONV_ROWS, conv_chunk, 0)


def _inproj(x, ln_g, ln_b, w_in, b_gate, conv_w, later_weights):
    batch, seq, _ = x.shape
    ts = TS_IN
    assert seq % ts == 0 and ts % CONV_ROWS == 0 and ts % HALO == 0 and HALO >= CONV_HALF
    tile = lambda b, i: (b, i, 0)
    prev_spec, next_spec = _halo_specs(ts, seq, D_MODEL)
    bf16 = jnp.bfloat16
    w_ao, w_co, w_out, w_gate, w_up, w_down = later_weights
    assert w_gate.shape == w_up.shape and w_gate.shape[1] % MXU_COLS == 0
    cast_out_shapes = [w_ao.shape, w_co.shape, w_out.shape,
                       (w_gate.shape[0], 2 * w_gate.shape[1]), w_down.shape]

    def slice_spec(shape):
        rows, cols = shape
        assert rows % (batch * BF16_SUBLANES) == 0
        return pl.BlockSpec((rows // batch, cols), lambda b, i: (b, 0))

    cast_in_specs = [slice_spec(w.shape) for w in later_weights]
    cast_out_specs = [slice_spec(s) for s in cast_out_shapes]
    outs = pl.pallas_call(
        _inproj_kernel,
        grid=(batch, seq // ts),
        in_specs=[
            pl.BlockSpec((None, ts, D_MODEL), tile),
            prev_spec,
            next_spec,
            _const_spec((1, D_MODEL)),
            _const_spec((1, D_MODEL)),
            _const_spec(w_in.shape),
            _const_spec((1, GATE_COLS)),
            _const_spec((CONV_WIDTH, CONV_CH)),
        ] + cast_in_specs,
        out_specs=[
            pl.BlockSpec((None, ts, D_MODEL), tile),
            pl.BlockSpec((None, ts, QKV_COLS), tile),
            pl.BlockSpec((None, ts, GATE_COLS), tile),
            pl.BlockSpec((None, ts, CONV_CH), tile),
        ] + cast_out_specs,
        out_shape=[
            jax.ShapeDtypeStruct((batch, seq, D_MODEL), jnp.float32),
            jax.ShapeDtypeStruct((batch, seq, QKV_COLS), bf16),
            jax.ShapeDtypeStruct((batch, seq, GATE_COLS), bf16),
            jax.ShapeDtypeStruct((batch, seq, CONV_CH), jnp.float32),
        ] + [jax.ShapeDtypeStruct(s, bf16) for s in cast_out_shapes],
        scratch_shapes=[
            pltpu.VMEM(w_in.shape, bf16),
            pltpu.VMEM((ts + 2 * HALO, D_MODEL), bf16),
            pltpu.VMEM((CONV_WIDTH, BF16_SUBLANES, CONV_CH), bf16),
            pltpu.VMEM((CONV_SLABS, ts // 2 + HALO, LANES), jnp.uint32),
            pltpu.VMEM((CONV_SLABS, ts // 2 + HALO, LANES), jnp.uint32),
        ],
        compiler_params=pltpu.CompilerParams(
            dimension_semantics=(_ARB, _ARB), vmem_limit_bytes=INPROJ_VMEM_LIMIT_BYTES),
        name="ln_inproj_conv",
    )(x, x, x, ln_g, ln_b, w_in, b_gate, conv_w, *later_weights)
    return outs[:4], outs[4:]


def _band_buckets():
    nb = NUM_BUCKETS // 2
    qi = np.arange(BLOCK)[:, None]
    kj = np.arange(BAND)[None, :]
    rel = kj - BLOCK - qi
    ret = (rel > 0).astype(np.int32) * nb
    n = np.abs(rel)
    max_exact = nb // 2
    large = max_exact + (np.log(np.maximum(n, 1) / max_exact)
                         / np.log(MAX_DISTANCE / max_exact) * (nb - max_exact)).astype(np.int32)
    large = np.minimum(large, nb - 1)
    buckets = (ret + np.where(n < max_exact, n, large)).astype(np.int32)
    return np.where(n <= WINDOW, buckets, MASKED_BUCKET).astype(np.int32)


def _mix_kernel(n_seq_blocks,
                relb_ref, sink_ref, bk_ref,
                q_ref, kv_ref, dw_ref, gt_ref, xn_ref,
                convb_ref, clng_ref, clnb_ref,
                wao_ref, wco_ref, wout_ref,
                r_ref,
                bias_s, o_s):
    b_id = pl.program_id(0)
    i_id = pl.program_id(1)
    ts = q_ref.shape[0]
    blocks_per_tile = ts // BLOCK
    dot = functools.partial(jnp.dot, preferred_element_type=jnp.float32)

    @pl.when((b_id == 0) & (i_id == 0))
    def _init():
        bk = bk_ref[...]
        col = lax.broadcasted_iota(jnp.int32, (BLOCK, BAND), 1)
        for h in range(N_HEADS):
            def body(b, acc):
                return jnp.where(bk == b, relb_ref[b, h], acc)
            mid = LOG2E * lax.fori_loop(0, NUM_BUCKETS, body,
                                        jnp.full((BLOCK, BAND), NEG_INF, jnp.float32))
            bias_s[0, h] = jnp.where(col < BLOCK, NEG_INF, mid)
            bias_s[1, h] = mid
            bias_s[2, h] = jnp.where(col >= 2 * BLOCK, NEG_INF, mid)

    lane = lax.broadcasted_iota(jnp.int32, (BLOCK, LANES), 1)
    low_half = lane < HEAD_DIM
    nt = (((1,), (1,)), ((), ()))

    def block_operands(j):
        n = i_id * blocks_per_tile + j
        variant = jnp.where(n == 0, 0, jnp.where(n == n_seq_blocks - 1, 2, 1))
        prev_r = pl.multiple_of(jnp.maximum(n - 1, 0) * BLOCK, BLOCK)
        own_r = pl.multiple_of(n * BLOCK, BLOCK)
        next_r = pl.multiple_of(jnp.minimum(n + 1, n_seq_blocks - 1) * BLOCK, BLOCK)
        kvb = jnp.concatenate([kv_ref[pl.ds(prev_r, BLOCK), :],
                               kv_ref[pl.ds(own_r, BLOCK), :],
                               kv_ref[pl.ds(next_r, BLOCK), :]], axis=0)
        kk = kvb[:, :LANES]
        vv = kvb[:, LANES:]
        q = q_ref[j * BLOCK:(j + 1) * BLOCK, :]
        zero = jnp.zeros((BLOCK, LANES), q.dtype)
        q_head = {}
        for p in range(N_HEADS // 2):
            q2 = q[:, p * LANES:(p + 1) * LANES]
            q_head[2 * p] = jnp.where(low_half, q2, zero)
            q_head[2 * p + 1] = jnp.where(low_half, zero, q2)
        return dict(variant=variant, q_head=q_head,
                    k=(kk, pltpu.roll(kk, HEAD_DIM, axis=1)),
                    v=(vv, pltpu.roll(vv, HEAD_DIM, axis=1)))

    def scores(blk, heads, swapped):
        lhs = jnp.concatenate([blk["q_head"][h] for h in heads], axis=0)
        return lax.dot_general(lhs, blk["k"][swapped], nt, preferred_element_type=jnp.float32)

    def softmax_values(blk, heads, swapped, s_all, o_head):
        ps, inv_ls = [], []
        for r, h in enumerate(heads):
            s = s_all[r * BLOCK:(r + 1) * BLOCK, :] + bias_s[blk["variant"], h]
            sink = jnp.full((BLOCK, 1), sink_ref[h], jnp.float32) * LOG2E
            m = jnp.maximum(jnp.max(s, axis=-1, keepdims=True), sink)
            p = jnp.exp2(s - m)
            l = jnp.sum(p, axis=-1, keepdims=True) + jnp.exp2(sink - m)
            ps.append(p.astype(jnp.bfloat16))
            inv_ls.append(1.0 / l)
        r_all = dot(jnp.concatenate(ps, axis=0), blk["v"][swapped])
        for r, h in enumerate(heads):
            o_head[h] = r_all[r * BLOCK:(r + 1) * BLOCK, :] * inv_ls[r]

    stages = [(j, heads, int(swapped)) for j in range(blocks_per_tile) for heads, swapped in HEAD_GROUPS]
    yc_chunk = D_MODEL * 2 // len(stages)
    assert yc_chunk % MXU_COLS == 0 and len(stages) % 2 == 0
    yc_parts = []
    y = _layer_norm(dw_ref[...] + convb_ref[...], clng_ref[...], clnb_ref[...])
    yc = (y * jax.nn.sigmoid(y)).astype(jnp.bfloat16)
    blocks = {}
    o_heads = {}
    pending = None
    for c in range(len(stages) + 1):
        nxt = None
        if c < len(stages):
            j, heads, swapped = stages[c]
            if j not in blocks:
                blocks[j] = block_operands(j)
                o_heads[j] = {}
            nxt = (j, heads, swapped, scores(blocks[j], heads, swapped))
        if c % 2 == 1 and len(yc_parts) * yc_chunk < D_MODEL:
            c0 = len(yc_parts) * yc_chunk
            yc_parts.append(dot(yc, wco_ref[:, c0:c0 + yc_chunk]))
        if pending is not None:
            j, heads, swapped, s_all = pending
            softmax_values(blocks[j], heads, swapped, s_all, o_heads[j])
            if len(o_heads[j]) == N_HEADS:
                o_pairs = [jnp.where(low_half, o_heads[j][2 * p], o_heads[j][2 * p + 1])
                           for p in range(N_HEADS // 2)]
                o_s[j * BLOCK:(j + 1) * BLOCK, :] = jnp.concatenate(o_pairs, axis=1).astype(o_s.dtype)
        pending = nxt

    y_a = dot(o_s[...], wao_ref[...])
    y_c = jnp.concatenate(yc_parts, axis=1)
    merged = (gt_ref[:, :D_MODEL].astype(jnp.float32) * y_a
              + gt_ref[:, D_MODEL:].astype(jnp.float32) * y_c)
    mix = dot(merged.astype(jnp.bfloat16), wout_ref[...])
    r_ref[...] = ALPHA * xn_ref[...] + mix


def _mix(rel_bias, sink, qkv, dw, gt, xn, conv_b, cln_g, cln_b, w_ao, w_co, w_out):
    batch, seq, _ = xn.shape
    ts = TS_MIX
    assert seq % ts == 0 and ts % BLOCK == 0 and seq // BLOCK >= 2
    n_seq_blocks = seq // BLOCK
    buckets = jnp.asarray(_band_buckets())
    tile = lambda b, i: (b, i, 0)
    return pl.pallas_call(
        functools.partial(_mix_kernel, n_seq_blocks),
        grid=(batch, seq // ts),
        in_specs=[
            _smem_spec(),
            _smem_spec(),
            _const_spec((BLOCK, BAND)),
            pl.BlockSpec((None, ts, Q_COLS), tile),
            pl.BlockSpec((None, seq, 2 * KV_COLS), lambda b, i: (b, 0, Q_COLS // (2 * KV_COLS))),
            pl.BlockSpec((None, ts, CONV_CH), tile),
            pl.BlockSpec((None, ts, GATE_COLS), tile),
            pl.BlockSpec((None, ts, D_MODEL), tile),
            _const_spec((1, CONV_CH)),
            _const_spec((1, CONV_CH)),
            _const_spec((1, CONV_CH)),
            _const_spec((Q_COLS, D_MODEL)),
            _const_spec((CONV_CH, D_MODEL)),
            _const_spec((D_MODEL, D_MODEL)),
        ],
        out_specs=pl.BlockSpec((None, ts, D_MODEL), tile),
        out_shape=jax.ShapeDtypeStruct((batch, seq, D_MODEL), jnp.float32),
        scratch_shapes=[
            pltpu.VMEM((3, N_HEADS, BLOCK, BAND), jnp.float32),
            pltpu.VMEM((ts, Q_COLS), jnp.bfloat16),
        ],
        compiler_params=pltpu.CompilerParams(
            dimension_semantics=(_ARB, _ARB), vmem_limit_bytes=VMEM_LIMIT_BYTES),
        name="token_mix",
    )(rel_bias, sink, buckets, qkv, qkv, dw, gt, xn, conv_b, cln_g, cln_b, w_ao, w_co, w_out)


def _ffn_kernel(r_ref, g1_ref, b1_ref, wgu_ref, wd_ref, g_ref, b_ref, o_ref):
    dot = functools.partial(jnp.dot, preferred_element_type=jnp.float32)
    inner = r_ref.shape[0] - 2 * FFN_EDGE_SUBTILE
    assert inner >= 0 and inner % FFN_SUBTILE == 0
    sizes = [FFN_EDGE_SUBTILE] + [FFN_SUBTILE] * (inner // FFN_SUBTILE) + [FFN_EDGE_SUBTILE]
    starts = [sum(sizes[:s]) for s in range(len(sizes) + 1)]
    n_sub = len(sizes)
    state = {}

    def up_stage(s):
        rows = slice(starts[s], starts[s + 1])
        h = _layer_norm(r_ref[rows, :], g1_ref[...], b1_ref[...])
        gu = dot(h.astype(jnp.bfloat16), wgu_ref[...])
        hid = []
        for c0 in range(0, gu.shape[1], 2 * MXU_COLS):
            gate = gu[:, c0:c0 + MXU_COLS]
            up = gu[:, c0 + MXU_COLS:c0 + 2 * MXU_COLS]
            hid.append((gate * jax.nn.sigmoid(gate) * up).astype(jnp.bfloat16))
        state[s] = (h, jnp.concatenate(hid, axis=1))

    def down_stage(s):
        rows = slice(starts[s], starts[s + 1])
        h, hid = state.pop(s)
        o_ref[rows, :] = _layer_norm(ALPHA * h + dot(hid, wd_ref[...]), g_ref[...], b_ref[...])

    for s in range(n_sub):
        up_stage(s)
    for s in range(n_sub):
        down_stage(s)


def _ffn(r2d, ln1_g, ln1_b, w_gu, w_down, ln_g, ln_b):
    n_tok = r2d.shape[0]
    assert n_tok % TM_FFN == 0
    row = lambda i: (i, 0)
    return pl.pallas_call(
        _ffn_kernel,
        grid=(n_tok // TM_FFN,),
        in_specs=[
            pl.BlockSpec((TM_FFN, D_MODEL), row),
            _const_spec((1, D_MODEL)),
            _const_spec((1, D_MODEL)),
            _const_spec(w_gu.shape),
            _const_spec(w_down.shape),
            _const_spec((1, D_MODEL)),
            _const_spec((1, D_MODEL)),
        ],
        out_specs=pl.BlockSpec((TM_FFN, D_MODEL), row),
        out_shape=jax.ShapeDtypeStruct((n_tok, D_MODEL), jnp.float32),
        compiler_params=pltpu.CompilerParams(
            dimension_semantics=(_ARB,), vmem_limit_bytes=VMEM_LIMIT_BYTES),
        name="swiglu_ffn",
    )(r2d, ln1_g, ln1_b, w_gu, w_down, ln_g, ln_b)


def kernel(x, ln_in_g, ln_in_b, rel_bias, w_in, b_gate, conv_w, conv_b, conv_ln_g, conv_ln_b,
           w_conv_out, w_attn_out, sink, w_out, ln1_g, ln1_b, w_gate, w_up, w_down, ln2_g, ln2_b):
    batch, seq, d_model = x.shape
    assert d_model == D_MODEL and w_in.shape[0] == DEPTH
    row = lambda v: v.reshape(1, -1)
    n_tok = batch * seq

    (xn, qkv, gt, dw), (w_ao, w_co, w_o, w_gu, w_d) = _inproj(
        x, row(ln_in_g), row(ln_in_b), w_in[0], row(b_gate[0]), conv_w[0],
        (w_attn_out[0], w_conv_out[0], w_out[0], w_gate[0], w_up[0], w_down[0]))
    r = _mix(rel_bias, sink[0], qkv, dw, gt, xn,
             row(conv_b[0]), row(conv_ln_g[0]), row(conv_ln_b[0]), w_ao, w_co, w_o)
    out = _ffn(r.reshape(n_tok, D_MODEL), row(ln1_g[0]), row(ln1_b[0]), w_gu, w_d,
               row(ln2_g[0]), row(ln2_b[0]))
    return out.reshape(batch, seq, D_MODEL)
```

```python
import functools

import numpy as np
import jax
import jax.numpy as jnp
from jax import lax
from jax.experimental import pallas as pl
from jax.experimental.pallas import tpu as pltpu

D_MODEL = 1024
HEAD_DIM = 64
N_HEADS = 8
N_KV_HEADS = 2
WINDOW = 128
BLOCK = 128
NUM_BUCKETS = 32
MAX_DISTANCE = 128
CONV_CH = 512
CONV_WIDTH = 31
CONV_HALF = (CONV_WIDTH - 1) // 2
Q_COLS = N_HEADS * HEAD_DIM
KV_COLS = N_KV_HEADS * HEAD_DIM
CONV_COLS = 2 * CONV_CH
GATE_COLS = 2 * D_MODEL
QKV_COLS = Q_COLS + 2 * KV_COLS
DEPTH = 1
ALPHA = (2.0 * DEPTH) ** 0.25
LN_EPS = 1e-5
LOG2E = 1.4426950408889634
Q_SCALE = HEAD_DIM ** -0.5 * LOG2E
NEG_INF = -1e30
MASKED_BUCKET = NUM_BUCKETS

HEAD_GROUPS = (((0, 2, 5, 7), False), ((1, 3, 4, 6), True))
BAND = 3 * BLOCK
HALO = 16
CONV_ROWS = 64
CONV_GROUP = 4
SUBLANES = 8
BF16_SUBLANES = 16
LANES = 128
MXU_COLS = 256
CONV_SLABS = CONV_CH // LANES

VMEM_LIMIT_BYTES = 56 * 1024 * 1024
INPROJ_VMEM_LIMIT_BYTES = 60 * 1024 * 1024
WCAST_ROWS = 128

TS_IN = 512
TS_MIX = 512
TM_FFN = 1024
FFN_SUBTILE = 256
FFN_EDGE_SUBTILE = 128

_ARB = "arbitrary"


def _layer_norm(x, g, b):
    mu = jnp.mean(x, axis=-1, keepdims=True)
    xc = x - mu
    var = jnp.mean(xc * xc, axis=-1, keepdims=True)
    return xc * lax.rsqrt(var + LN_EPS) * g + b


def _const_spec(shape):
    return pl.BlockSpec(shape, lambda *_: (0,) * len(shape), pipeline_mode=pl.Buffered(1))


def _smem_spec():
    return pl.BlockSpec(memory_space=pltpu.SMEM)


def _halo_specs(ts, seq, cols):
    per_tile = ts // HALO
    last = seq // HALO - 1
    prev = pl.BlockSpec((None, HALO, cols), lambda b, i: (b, jnp.maximum(i * per_tile - 1, 0), 0))
    nxt = pl.BlockSpec((None, HALO, cols), lambda b, i: (b, jnp.minimum((i + 1) * per_tile, last), 0))
    return prev, nxt


def _inproj_kernel(x_ref, xp_ref, xnx_ref, g_ref, b_ref, w32_ref, bgate_ref,
                   convw_ref,
                   wao_ref, wco_ref, wout_ref, wgate_ref, wup_ref, wdown_ref,
                   xn_ref, qkv_ref, gt_ref, dw_ref,
                   wao_o, wco_o, wout_o, wgu_o, wdown_o,
                   w_ref, xb_s, wb_s, ge_s, go_s):
    b_id = pl.program_id(0)
    i_id = pl.program_id(1)
    n_tiles = pl.num_programs(1)
    ts = x_ref.shape[0]
    dot = functools.partial(jnp.dot, preferred_element_type=jnp.float32)
    o1 = QKV_COLS
    o2 = o1 + CONV_COLS
    o3 = o2 + D_MODEL

    @pl.when((b_id == 0) & (i_id == 0))
    def _init():
        for k in range(CONV_WIDTH):
            wb_s[k] = jnp.broadcast_to(convw_ref[k:k + 1, :].astype(wb_s.dtype), (BF16_SUBLANES, CONV_CH))
        for r0 in range(0, w32_ref.shape[0], WCAST_ROWS):
            w_ref[r0:r0 + WCAST_ROWS, :] = w32_ref[r0:r0 + WCAST_ROWS, :].astype(w_ref.dtype)

    @pl.when(i_id == 0)
    def _cast_weights():
        for src, dst in ((wao_ref, wao_o), (wco_ref, wco_o), (wout_ref, wout_o), (wdown_ref, wdown_o)):
            dst[...] = src[...].astype(dst.dtype)
        for j in range(wgate_ref.shape[1] // MXU_COLS):
            src_cols = slice(j * MXU_COLS, (j + 1) * MXU_COLS)
            wgu_o[:, 2 * j * MXU_COLS:(2 * j + 1) * MXU_COLS] = wgate_ref[:, src_cols].astype(wgu_o.dtype)
            wgu_o[:, (2 * j + 1) * MXU_COLS:(2 * j + 2) * MXU_COLS] = wup_ref[:, src_cols].astype(wgu_o.dtype)

    g = g_ref[...]
    b = b_ref[...]
    xn = _layer_norm(x_ref[...], g, b)
    xn_ref[...] = xn
    xb_s[0:HALO, :] = _layer_norm(xp_ref[...], g, b).astype(xb_s.dtype)
    xb_s[HALO:HALO + ts, :] = xn.astype(xb_s.dtype)
    xb_s[HALO + ts:2 * HALO + ts, :] = _layer_norm(xnx_ref[...], g, b).astype(xb_s.dtype)

    u = dot(xb_s[...], w_ref[:, o1:o2])
    glu = u[:, :CONV_CH] * jax.nn.sigmoid(u[:, CONV_CH:])
    row = lax.broadcasted_iota(jnp.int32, (ts + 2 * HALO, 1), 0)
    pad = ((row < HALO) & (i_id == 0)) | ((row >= HALO + ts) & (i_id == n_tiles - 1))
    glu = jnp.where(pad, 0.0, glu)
    rows_ext = ts + 2 * HALO
    glu_up = pltpu.roll(glu, rows_ext - 1, axis=0)
    for c in range(CONV_SLABS):
        lanes = slice(c * LANES, (c + 1) * LANES)
        ge_s[c] = pltpu.bitcast(glu[:, lanes].astype(jnp.bfloat16), jnp.uint32)
        go_s[c] = pltpu.bitcast(glu_up[:, lanes].astype(jnp.bfloat16), jnp.uint32)

    xb = xb_s[HALO:HALO + ts, :]
    bgate = bgate_ref[...]
    gt_ref[:, :D_MODEL] = jax.nn.sigmoid(
        dot(xb, w_ref[:, o2:o3]) + bgate[:, :D_MODEL]).astype(gt_ref.dtype)
    gt_ref[:, D_MODEL:] = jax.nn.sigmoid(
        dot(xb, w_ref[:, o3:]) + bgate[:, D_MODEL:]).astype(gt_ref.dtype)
    qkv_ref[:, :Q_COLS] = (dot(xb, w_ref[:, :Q_COLS]) * Q_SCALE).astype(qkv_ref.dtype)
    qkv_ref[:, Q_COLS:] = dot(xb, w_ref[:, Q_COLS:o1]).astype(qkv_ref.dtype)

    first_tap = HALO - CONV_HALF

    def conv_chunk(chunk, carry):
        r0 = pl.multiple_of(chunk * CONV_ROWS, CONV_ROWS)
        w0 = pl.multiple_of(chunk * (CONV_ROWS // 2), CONV_ROWS // 2)
        n_groups = CONV_ROWS // BF16_SUBLANES
        for c in range(CONV_SLABS):
            lanes = slice(c * LANES, (c + 1) * LANES)
            accs = [jnp.zeros((BF16_SUBLANES, LANES), jnp.float32) for _ in range(n_groups)]
            for k0 in range(0, CONV_WIDTH, CONV_GROUP):
                parts = [None] * n_groups
                for k in range(k0, min(k0 + CONV_GROUP, CONV_WIDTH)):
                    w = wb_s[k, :, lanes]
                    for g in range(n_groups):
                        off = g * BF16_SUBLANES + first_tap + k
                        src = go_s if off % 2 else ge_s
                        words = src[c, pl.ds(w0 + off // 2, SUBLANES), :]
                        prod = pltpu.bitcast(words, jnp.bfloat16) * w
                        parts[g] = prod if parts[g] is None else parts[g] + prod
                for g in range(n_groups):
                    accs[g] = accs[g] + parts[g].astype(jnp.float32)
            for g in range(n_groups):
                dw_ref[pl.ds(r0 + g * BF16_SUBLANES, BF16_SUBLANES), lanes] = accs[g]
        return carry

    lax.fori_loop(0, ts // CONV_ROWS, conv_chunk, 0)


def _inproj(x, ln_g, ln_b, w_in, b_gate, conv_w, later_weights):
    batch, seq, _ = x.shape
    ts = TS_IN
    assert seq % ts == 0 and ts % CONV_ROWS == 0 and ts % HALO == 0 and HALO >= CONV_HALF
    tile = lambda b, i: (b, i, 0)
    prev_spec, next_spec = _halo_specs(ts, seq, D_MODEL)
    bf16 = jnp.bfloat16
    w_ao, w_co, w_out, w_gate, w_up, w_down = later_weights
    assert w_gate.shape == w_up.shape and w_gate.shape[1] % MXU_COLS == 0
    cast_out_shapes = [w_ao.shape, w_co.shape, w_out.shape,
                       (w_gate.shape[0], 2 * w_gate.shape[1]), w_down.shape]

    def slice_spec(shape):
        rows, cols = shape
        assert rows % (batch * BF16_SUBLANES) == 0
        return pl.BlockSpec((rows // batch, cols), lambda b, i: (b, 0))

    cast_in_specs = [slice_spec(w.shape) for w in later_weights]
    cast_out_specs = [slice_spec(s) for s in cast_out_shapes]
    outs = pl.pallas_call(
        _inproj_kernel,
        grid=(batch, seq // ts),
        in_specs=[
            pl.BlockSpec((None, ts, D_MODEL), tile),
            prev_spec,
            next_spec,
            _const_spec((1, D_MODEL)),
            _const_spec((1, D_MODEL)),
            _const_spec(w_in.shape),
            _const_spec((1, GATE_COLS)),
            _const_spec((CONV_WIDTH, CONV_CH)),
        ] + cast_in_specs,
        out_specs=[
            pl.BlockSpec((None, ts, D_MODEL), tile),
            pl.BlockSpec((None, ts, QKV_COLS), tile),
            pl.BlockSpec((None, ts, GATE_COLS), tile),
            pl.BlockSpec((None, ts, CONV_CH), tile),
        ] + cast_out_specs,
        out_shape=[
            jax.ShapeDtypeStruct((batch, seq, D_MODEL), jnp.float32),
            jax.ShapeDtypeStruct((batch, seq, QKV_COLS), bf16),
            jax.ShapeDtypeStruct((batch, seq, GATE_COLS), bf16),
            jax.ShapeDtypeStruct((batch, seq, CONV_CH), jnp.float32),
        ] + [jax.ShapeDtypeStruct(s, bf16) for s in cast_out_shapes],
        scratch_shapes=[
            pltpu.VMEM(w_in.shape, bf16),
            pltpu.VMEM((ts + 2 * HALO, D_MODEL), bf16),
            pltpu.VMEM((CONV_WIDTH, BF16_SUBLANES, CONV_CH), bf16),
            pltpu.VMEM((CONV_SLABS, ts // 2 + HALO, LANES), jnp.uint32),
            pltpu.VMEM((CONV_SLABS, ts // 2 + HALO, LANES), jnp.uint32),
        ],
        compiler_params=pltpu.CompilerParams(
            dimension_semantics=(_ARB, _ARB), vmem_limit_bytes=INPROJ_VMEM_LIMIT_BYTES),
        name="ln_inproj_conv",
    )(x, x, x, ln_g, ln_b, w_in, b_gate, conv_w, *later_weights)
    return outs[:4], outs[4:]


def _band_buckets():
    nb = NUM_BUCKETS // 2
    qi = np.arange(BLOCK)[:, None]
    kj = np.arange(BAND)[None, :]
    rel = kj - BLOCK - qi
    ret = (rel > 0).astype(np.int32) * nb
    n = np.abs(rel)
    max_exact = nb // 2
    large = max_exact + (np.log(np.maximum(n, 1) / max_exact)
                         / np.log(MAX_DISTANCE / max_exact) * (nb - max_exact)).astype(np.int32)
    large = np.minimum(large, nb - 1)
    buckets = (ret + np.where(n < max_exact, n, large)).astype(np.int32)
    return np.where(n <= WINDOW, buckets, MASKED_BUCKET).astype(np.int32)


def _mix_kernel(n_seq_blocks,
                relb_ref, sink_ref, bk_ref,
                q_ref, kv_ref, dw_ref, gt_ref, xn_ref,
                convb_ref, clng_ref, clnb_ref,
                wao_ref, wco_ref, wout_ref,
                r_ref,
                bias_s, o_s):
    b_id = pl.program_id(0)
    i_id = pl.program_id(1)
    ts = q_ref.shape[0]
    blocks_per_tile = ts // BLOCK
    dot = functools.partial(jnp.dot, preferred_element_type=jnp.float32)

    @pl.when((b_id == 0) & (i_id == 0))
    def _init():
        bk = bk_ref[...]
        col = lax.broadcasted_iota(jnp.int32, (BLOCK, BAND), 1)
        for h in range(N_HEADS):
            def body(b, acc):
                return jnp.where(bk == b, relb_ref[b, h], acc)
            mid = LOG2E * lax.fori_loop(0, NUM_BUCKETS, body,
                                        jnp.full((BLOCK, BAND), NEG_INF, jnp.float32))
            bias_s[0, h] = jnp.where(col < BLOCK, NEG_INF, mid)
            bias_s[1, h] = mid
            bias_s[2, h] = jnp.where(col >= 2 * BLOCK, NEG_INF, mid)

    lane = lax.broadcasted_iota(jnp.int32, (BLOCK, LANES), 1)
    low_half = lane < HEAD_DIM
    nt = (((1,), (1,)), ((), ()))

    def block_operands(j):
        n = i_id * blocks_per_tile + j
        variant = jnp.where(n == 0, 0, jnp.where(n == n_seq_blocks - 1, 2, 1))
        prev_r = pl.multiple_of(jnp.maximum(n - 1, 0) * BLOCK, BLOCK)
        own_r = pl.multiple_of(n * BLOCK, BLOCK)
        next_r = pl.multiple_of(jnp.minimum(n + 1, n_seq_blocks - 1) * BLOCK, BLOCK)
        kvb = jnp.concatenate([kv_ref[pl.ds(prev_r, BLOCK), :],
                               kv_ref[pl.ds(own_r, BLOCK), :],
                               kv_ref[pl.ds(next_r, BLOCK), :]], axis=0)
        kk = kvb[:, :LANES]
        vv = kvb[:, LANES:]
        q = q_ref[j * BLOCK:(j + 1) * BLOCK, :]
        zero = jnp.zeros((BLOCK, LANES), q.dtype)
        q_head = {}
        for p in range(N_HEADS // 2):
            q2 = q[:, p * LANES:(p + 1) * LANES]
            q_head[2 * p] = jnp.where(low_half, q2, zero)
            q_head[2 * p + 1] = jnp.where(low_half, zero, q2)
        return dict(variant=variant, q_head=q_head,
                    k=(kk, pltpu.roll(kk, HEAD_DIM, axis=1)),
                    v=(vv, pltpu.roll(vv, HEAD_DIM, axis=1)))

    def scores(blk, heads, swapped):
        lhs = jnp.concatenate([blk["q_head"][h] for h in heads], axis=0)
        return lax.dot_general(lhs, blk["k"][swapped], nt, preferred_element_type=jnp.float32)

    def softmax_values(blk, heads, swapped, s_all, o_head):
        ps, inv_ls = [], []
        for r, h in enumerate(heads):
            s = s_all[r * BLOCK:(r + 1) * BLOCK, :] + bias_s[blk["variant"], h]
            sink = jnp.full((BLOCK, 1), sink_ref[h], jnp.float32) * LOG2E
            m = jnp.maximum(jnp.max(s, axis=-1, keepdims=True), sink)
            p = jnp.exp2(s - m)
            l = jnp.sum(p, axis=-1, keepdims=True) + jnp.exp2(sink - m)
            ps.append(p.astype(jnp.bfloat16))
            inv_ls.append(1.0 / l)
        r_all = dot(jnp.concatenate(ps, axis=0), blk["v"][swapped])
        for r, h in enumerate(heads):
            o_head[h] = r_all[r * BLOCK:(r + 1) * BLOCK, :] * inv_ls[r]

    stages = [(j, heads, int(swapped)) for j in range(blocks_per_tile) for heads, swapped in HEAD_GROUPS]
    yc_chunk = D_MODEL * 2 // len(stages)
    assert yc_chunk % MXU_COLS == 0 and len(stages) % 2 == 0
    yc_parts = []
    y = _layer_norm(dw_ref[...] + convb_ref[...], clng_ref[...], clnb_ref[...])
    yc = (y * jax.nn.sigmoid(y)).astype(jnp.bfloat16)
    blocks = {}
    o_heads = {}
    pending = None
    for c in range(len(stages) + 1):
        nxt = None
        if c < len(stages):
            j, heads, swapped = stages[c]
            if j not in blocks:
                blocks[j] = block_operands(j)
                o_heads[j] = {}
            nxt = (j, heads, swapped, scores(blocks[j], heads, swapped))
        if c % 2 == 1 and len(yc_parts) * yc_chunk < D_MODEL:
            c0 = len(yc_parts) * yc_chunk
            yc_parts.append(dot(yc, wco_ref[:, c0:c0 + yc_chunk]))
        if pending is not None:
            j, heads, swapped, s_all = pending
            softmax_values(blocks[j], heads, swapped, s_all, o_heads[j])
            if len(o_heads[j]) == N_HEADS:
                o_pairs = [jnp.where(low_half, o_heads[j][2 * p], o_heads[j][2 * p + 1])
                           for p in range(N_HEADS // 2)]
                o_s[j * BLOCK:(j + 1) * BLOCK, :] = jnp.concatenate(o_pairs, axis=1).astype(o_s.dtype)
        pending = nxt

    y_a = dot(o_s[...], wao_ref[...])
    y_c = jnp.concatenate(yc_parts, axis=1)
    merged = (gt_ref[:, :D_MODEL].astype(jnp.float32) * y_a
              + gt_ref[:, D_MODEL:].astype(jnp.float32) * y_c)
    mix = dot(merged.astype(jnp.bfloat16), wout_ref[...])
    r_ref[...] = ALPHA * xn_ref[...] + mix


def _mix(rel_bias, sink, qkv, dw, gt, xn, conv_b, cln_g, cln_b, w_ao, w_co, w_out):
    batch, seq, _ = xn.shape
    ts = TS_MIX
    assert seq % ts == 0 and ts % BLOCK == 0 and seq // BLOCK >= 2
    n_seq_blocks = seq // BLOCK
    buckets = jnp.asarray(_band_buckets())
    tile = lambda b, i: (b, i, 0)
    return pl.pallas_call(
        functools.partial(_mix_kernel, n_seq_blocks),
        grid=(batch, seq // ts),
        in_specs=[
            _smem_spec(),
            _smem_spec(),
            _const_spec((BLOCK, BAND)),
            pl.BlockSpec((None, ts, Q_COLS), tile),
            pl.BlockSpec((None, seq, 2 * KV_COLS), lambda b, i: (b, 0, Q_COLS // (2 * KV_COLS))),
            pl.BlockSpec((None, ts, CONV_CH), tile),
            pl.BlockSpec((None, ts, GATE_COLS), tile),
            pl.BlockSpec((None, ts, D_MODEL), tile),
            _const_spec((1, CONV_CH)),
            _const_spec((1, CONV_CH)),
            _const_spec((1, CONV_CH)),
            _const_spec((Q_COLS, D_MODEL)),
            _const_spec((CONV_CH, D_MODEL)),
            _const_spec((D_MODEL, D_MODEL)),
        ],
        out_specs=pl.BlockSpec((None, ts, D_MODEL), tile),
        out_shape=jax.ShapeDtypeStruct((batch, seq, D_MODEL), jnp.float32),
        scratch_shapes=[
            pltpu.VMEM((3, N_HEADS, BLOCK, BAND), jnp.float32),
            pltpu.VMEM((ts, Q_COLS), jnp.bfloat16),
        ],
        compiler_params=pltpu.CompilerParams(
            dimension_semantics=(_ARB, _ARB), vmem_limit_bytes=VMEM_LIMIT_BYTES),
        name="token_mix",
    )(rel_bias, sink, buckets, qkv, qkv, dw, gt, xn, conv_b, cln_g, cln_b, w_ao, w_co, w_out)


def _ffn_kernel(r_ref, g1_ref, b1_ref, wgu_ref, wd_ref, g_ref, b_ref, o_ref):
    dot = functools.partial(jnp.dot, preferred_element_type=jnp.float32)
    inner = r_ref.shape[0] - 2 * FFN_EDGE_SUBTILE
    assert inner >= 0 and inner % FFN_SUBTILE == 0
    sizes = [FFN_EDGE_SUBTILE] + [FFN_SUBTILE] * (inner // FFN_SUBTILE) + [FFN_EDGE_SUBTILE]
    starts = [sum(sizes[:s]) for s in range(len(sizes) + 1)]
    n_sub = len(sizes)
    state = {}

    def up_stage(s):
        rows = slice(starts[s], starts[s + 1])
        h = _layer_norm(r_ref[rows, :], g1_ref[...], b1_ref[...])
        gu = dot(h.astype(jnp.bfloat16), wgu_ref[...])
        hid = []
        for c0 in range(0, gu.shape[1], 2 * MXU_COLS):
            gate = gu[:, c0:c0 + MXU_COLS]
            up = gu[:, c0 + MXU_COLS:c0 + 2 * MXU_COLS]
            hid.append((gate * jax.nn.sigmoid(gate) * up).astype(jnp.bfloat16))
        state[s] = (h, jnp.concatenate(hid, axis=1))

    def down_stage(s):
        rows = slice(starts[s], starts[s + 1])
        h, hid = state.pop(s)
        o_ref[rows, :] = _layer_norm(ALPHA * h + dot(hid, wd_ref[...]), g_ref[...], b_ref[...])

    for s in range(n_sub):
        up_stage(s)
    for s in range(n_sub):
        down_stage(s)


def _ffn(r2d, ln1_g, ln1_b, w_gu, w_down, ln_g, ln_b):
    n_tok = r2d.shape[0]
    assert n_tok % TM_FFN == 0
    row = lambda i: (i, 0)
    return pl.pallas_call(
        _ffn_kernel,
        grid=(n_tok // TM_FFN,),
        in_specs=[
            pl.BlockSpec((TM_FFN, D_MODEL), row),
            _const_spec((1, D_MODEL)),
            _const_spec((1, D_MODEL)),
            _const_spec(w_gu.shape),
            _const_spec(w_down.shape),
            _const_spec((1, D_MODEL)),
            _const_spec((1, D_MODEL)),
        ],
        out_specs=pl.BlockSpec((TM_FFN, D_MODEL), row),
        out_shape=jax.ShapeDtypeStruct((n_tok, D_MODEL), jnp.float32),
        compiler_params=pltpu.CompilerParams(
            dimension_semantics=(_ARB,), vmem_limit_bytes=VMEM_LIMIT_BYTES),
        name="swiglu_ffn",
    )(r2d, ln1_g, ln1_b, w_gu, w_down, ln_g, ln_b)


def kernel(x, ln_in_g, ln_in_b, rel_bias, w_in, b_gate, conv_w, conv_b, conv_ln_g, conv_ln_b,
           w_conv_out, w_attn_out, sink, w_out, ln1_g, ln1_b, w_gate, w_up, w_down, ln2_g, ln2_b):
    batch, seq, d_model = x.shape
    assert d_model == D_MODEL and w_in.shape[0] == DEPTH
    row = lambda v: v.reshape(1, -1)
    n_tok = batch * seq

    (xn, qkv, gt, dw), (w_ao, w_co, w_o, w_gu, w_d) = _inproj(
        x, row(ln_in_g), row(ln_in_b), w_in[0], row(b_gate[0]), conv_w[0],
        (w_attn_out[0], w_conv_out[0], w_out[0], w_gate[0], w_up[0], w_down[0]))
    r = _mix(rel_bias, sink[0], qkv, dw, gt, xn,
             row(conv_b[0]), row(conv_ln_g[0]), row(conv_ln_b[0]), w_ao, w_co, w_o)
    out = _ffn(r.reshape(n_tok, D_MODEL), row(ln1_g[0]), row(ln1_b[0]), w_gu, w_d,
               row(ln2_g[0]), row(ln2_b[0]))
    return out.reshape(batch, seq, D_MODEL)
```

```python
import functools

import numpy as np
import jax
import jax.numpy as jnp
from jax import lax
from jax.experimental import pallas as pl
from jax.experimental.pallas import tpu as pltpu

D_MODEL = 1024
HEAD_DIM = 64
N_HEADS = 8
N_KV_HEADS = 2
WINDOW = 128
BLOCK = 128
NUM_BUCKETS = 32
MAX_DISTANCE = 128
CONV_CH = 512
CONV_WIDTH = 31
CONV_HALF = (CONV_WIDTH - 1) // 2
Q_COLS = N_HEADS * HEAD_DIM
KV_COLS = N_KV_HEADS * HEAD_DIM
CONV_COLS = 2 * CONV_CH
GATE_COLS = 2 * D_MODEL
QKV_COLS = Q_COLS + 2 * KV_COLS
DEPTH = 1
ALPHA = (2.0 * DEPTH) ** 0.25
LN_EPS = 1e-5
LOG2E = 1.4426950408889634
Q_SCALE = HEAD_DIM ** -0.5 * LOG2E
NEG_INF = -1e30
MASKED_BUCKET = NUM_BUCKETS

HEAD_GROUPS = (((0, 2, 5, 7), False), ((1, 3, 4, 6), True))
BAND = 3 * BLOCK
HALO = 16
CONV_ROWS = 64
CONV_GROUP = 4
SUBLANES = 8
BF16_SUBLANES = 16
LANES = 128
MXU_COLS = 256
CONV_SLABS = CONV_CH // LANES

VMEM_LIMIT_BYTES = 56 * 1024 * 1024
INPROJ_VMEM_LIMIT_BYTES = 60 * 1024 * 1024
WCAST_ROWS = 128

TS_IN = 512
TS_MIX = 512
TM_FFN = 1024
FFN_SUBTILE = 256
FFN_EDGE_SUBTILE = 128

_ARB = "arbitrary"


def _layer_norm(x, g, b):
    mu = jnp.mean(x, axis=-1, keepdims=True)
    xc = x - mu
    var = jnp.mean(xc * xc, axis=-1, keepdims=True)
    return xc * lax.rsqrt(var + LN_EPS) * g + b


def _const_spec(shape):
    return pl.BlockSpec(shape, lambda *_: (0,) * len(shape), pipeline_mode=pl.Buffered(1))


def _smem_spec():
    return pl.BlockSpec(memory_space=pltpu.SMEM)


def _halo_specs(ts, seq, cols):
    per_tile = ts // HALO
    last = seq // HALO - 1
    prev = pl.BlockSpec((None, HALO, cols), lambda b, i: (b, jnp.maximum(i * per_tile - 1, 0), 0))
    nxt = pl.BlockSpec((None, HALO, cols), lambda b, i: (b, jnp.minimum((i + 1) * per_tile, last), 0))
    return prev, nxt


def _inproj_kernel(x_ref, xp_ref, xnx_ref, g_ref, b_ref, w32_ref, bgate_ref,
                   convw_ref,
                   wao_ref, wco_ref, wout_ref, wgate_ref, wup_ref, wdown_ref,
                   xn_ref, qkv_ref, gt_ref, dw_ref,
                   wao_o, wco_o, wout_o, wgu_o, wdown_o,
                   w_ref, xb_s, wb_s, ge_s, go_s):
    b_id = pl.program_id(0)
    i_id = pl.program_id(1)
    n_tiles = pl.num_programs(1)
    ts = x_ref.shape[0]
    dot = functools.partial(jnp.dot, preferred_element_type=jnp.float32)
    o1 = QKV_COLS
    o2 = o1 + CONV_COLS
    o3 = o2 + D_MODEL

    @pl.when((b_id == 0) & (i_id == 0))
    def _init():
        for k in range(CONV_WIDTH):
            wb_s[k] = jnp.broadcast_to(convw_ref[k:k + 1, :].astype(wb_s.dtype), (BF16_SUBLANES, CONV_CH))
        for r0 in range(0, w32_ref.shape[0], WCAST_ROWS):
            w_ref[r0:r0 + WCAST_ROWS, :] = w32_ref[r0:r0 + WCAST_ROWS, :].astype(w_ref.dtype)

    @pl.when(i_id == 0)
    def _cast_weights():
        for src, dst in ((wao_ref, wao_o), (wco_ref, wco_o), (wout_ref, wout_o), (wdown_ref, wdown_o)):
            dst[...] = src[...].astype(dst.dtype)
        for j in range(wgate_ref.shape[1] // MXU_COLS):
            src_cols = slice(j * MXU_COLS, (j + 1) * MXU_COLS)
            wgu_o[:, 2 * j * MXU_COLS:(2 * j + 1) * MXU_COLS] = wgate_ref[:, src_cols].astype(wgu_o.dtype)
            wgu_o[:, (2 * j + 1) * MXU_COLS:(2 * j + 2) * MXU_COLS] = wup_ref[:, src_cols].astype(wgu_o.dtype)

    g = g_ref[...]
    b = b_ref[...]
    xn = _layer_norm(x_ref[...], g, b)
    xn_ref[...] = xn
    xb_s[0:HALO, :] = _layer_norm(xp_ref[...], g, b).astype(xb_s.dtype)
    xb_s[HALO:HALO + ts, :] = xn.astype(xb_s.dtype)
    xb_s[HALO + ts:2 * HALO + ts, :] = _layer_norm(xnx_ref[...], g, b).astype(xb_s.dtype)

    u = dot(xb_s[...], w_ref[:, o1:o2])
    glu = u[:, :CONV_CH] * jax.nn.sigmoid(u[:, CONV_CH:])
    row = lax.broadcasted_iota(jnp.int32, (ts + 2 * HALO, 1), 0)
    pad = ((row < HALO) & (i_id == 0)) | ((row >= HALO + ts) & (i_id == n_tiles - 1))
    glu = jnp.where(pad, 0.0, glu)
    rows_ext = ts + 2 * HALO
    glu_up = pltpu.roll(glu, rows_ext - 1, axis=0)
    for c in range(CONV_SLABS):
        lanes = slice(c * LANES, (c + 1) * LANES)
        ge_s[c] = pltpu.bitcast(glu[:, lanes].astype(jnp.bfloat16), jnp.uint32)
        go_s[c] = pltpu.bitcast(glu_up[:, lanes].astype(jnp.bfloat16), jnp.uint32)

    xb = xb_s[HALO:HALO + ts, :]
    bgate = bgate_ref[...]
    gt_ref[:, :D_MODEL] = jax.nn.sigmoid(
        dot(xb, w_ref[:, o2:o3]) + bgate[:, :D_MODEL]).astype(gt_ref.dtype)
    gt_ref[:, D_MODEL:] = jax.nn.sigmoid(
        dot(xb, w_ref[:, o3:]) + bgate[:, D_MODEL:]).astype(gt_ref.dtype)
    qkv_ref[:, :Q_COLS] = (dot(xb, w_ref[:, :Q_COLS]) * Q_SCALE).astype(qkv_ref.dtype)
    qkv_ref[:, Q_COLS:] = dot(xb, w_ref[:, Q_COLS:o1]).astype(qkv_ref.dtype)

    first_tap = HALO - CONV_HALF

    def conv_chunk(chunk, carry):
        r0 = pl.multiple_of(chunk * CONV_ROWS, CONV_ROWS)
        w0 = pl.multiple_of(chunk * (CONV_ROWS // 2), CONV_ROWS // 2)
        n_groups = CONV_ROWS // BF16_SUBLANES
        for c in range(CONV_SLABS):
            lanes = slice(c * LANES, (c + 1) * LANES)
            accs = [jnp.zeros((BF16_SUBLANES, LANES), jnp.float32) for _ in range(n_groups)]
            for k0 in range(0, CONV_WIDTH, CONV_GROUP):
                parts = [None] * n_groups
                for k in range(k0, min(k0 + CONV_GROUP, CONV_WIDTH)):
                    w = wb_s[k, :, lanes]
                    for g in range(n_groups):
                        off = g * BF16_SUBLANES + first_tap + k
                        src = go_s if off % 2 else ge_s
                        words = src[c, pl.ds(w0 + off // 2, SUBLANES), :]
                        prod = pltpu.bitcast(words, jnp.bfloat16) * w
                        parts[g] = prod if parts[g] is None else parts[g] + prod
                for g in range(n_groups):
                    accs[g] = accs[g] + parts[g].astype(jnp.float32)
            for g in range(n_groups):
                dw_ref[pl.ds(r0 + g * BF16_SUBLANES, BF16_SUBLANES), lanes] = accs[g]
        return carry

    lax.fori_loop(0, ts // CONV_ROWS, conv_chunk, 0)


def _inproj(x, ln_g, ln_b, w_in, b_gate, conv_w, later_weights):
    batch, seq, _ = x.shape
    ts = TS_IN
    assert seq % ts == 0 and ts % CONV_ROWS == 0 and ts % HALO == 0 and HALO >= CONV_HALF
    tile = lambda b, i: (b, i, 0)
    prev_spec, next_spec = _halo_specs(ts, seq, D_MODEL)
    bf16 = jnp.bfloat16
    w_ao, w_co, w_out, w_gate, w_up, w_down = later_weights
    assert w_gate.shape == w_up.shape and w_gate.shape[1] % MXU_COLS == 0
    cast_out_shapes = [w_ao.shape, w_co.shape, w_out.shape,
                       (w_gate.shape[0], 2 * w_gate.shape[1]), w_down.shape]

    def slice_spec(shape):
        rows, cols = shape
        assert rows % (batch * BF16_SUBLANES) == 0
        return pl.BlockSpec((rows // batch, cols), lambda b, i: (b, 0))

    cast_in_specs = [slice_spec(w.shape) for w in later_weights]
    cast_out_specs = [slice_spec(s) for s in cast_out_shapes]
    outs = pl.pallas_call(
        _inproj_kernel,
        grid=(batch, seq // ts),
        in_specs=[
            pl.BlockSpec((None, ts, D_MODEL), tile),
            prev_spec,
            next_spec,
            _const_spec((1, D_MODEL)),
            _const_spec((1, D_MODEL)),
            _const_spec(w_in.shape),
            _const_spec((1, GATE_COLS)),
            _const_spec((CONV_WIDTH, CONV_CH)),
        ] + cast_in_specs,
        out_specs=[
            pl.BlockSpec((None, ts, D_MODEL), tile),
            pl.BlockSpec((None, ts, QKV_COLS), tile),
            pl.BlockSpec((None, ts, GATE_COLS), tile),
            pl.BlockSpec((None, ts, CONV_CH), tile),
        ] + cast_out_specs,
        out_shape=[
            jax.ShapeDtypeStruct((batch, seq, D_MODEL), jnp.float32),
            jax.ShapeDtypeStruct((batch, seq, QKV_COLS), bf16),
            jax.ShapeDtypeStruct((batch, seq, GATE_COLS), bf16),
            jax.ShapeDtypeStruct((batch, seq, CONV_CH), jnp.float32),
        ] + [jax.ShapeDtypeStruct(s, bf16) for s in cast_out_shapes],
        scratch_shapes=[
            pltpu.VMEM(w_in.shape, bf16),
            pltpu.VMEM((ts + 2 * HALO, D_MODEL), bf16),
            pltpu.VMEM((CONV_WIDTH, BF16_SUBLANES, CONV_CH), bf16),
            pltpu.VMEM((CONV_SLABS, ts // 2 + HALO, LANES), jnp.uint32),
            pltpu.VMEM((CONV_SLABS, ts // 2 + HALO, LANES), jnp.uint32),
        ],
        compiler_params=pltpu.CompilerParams(
            dimension_semantics=(_ARB, _ARB), vmem_limit_bytes=INPROJ_VMEM_LIMIT_BYTES),
        name="ln_inproj_conv",
    )(x, x, x, ln_g, ln_b, w_in, b_gate, conv_w, *later_weights)
    return outs[:4], outs[4:]


def _band_buckets():
    nb = NUM_BUCKETS // 2
    qi = np.arange(BLOCK)[:, None]
    kj = np.arange(BAND)[None, :]
    rel = kj - BLOCK - qi
    ret = (rel > 0).astype(np.int32) * nb
    n = np.abs(rel)
    max_exact = nb // 2
    large = max_exact + (np.log(np.maximum(n, 1) / max_exact)
                         / np.log(MAX_DISTANCE / max_exact) * (nb - max_exact)).astype(np.int32)
    large = np.minimum(large, nb - 1)
    buckets = (ret + np.where(n < max_exact, n, large)).astype(np.int32)
    return np.where(n <= WINDOW, buckets, MASKED_BUCKET).astype(np.int32)


def _mix_kernel(n_seq_blocks,
                relb_ref, sink_ref, bk_ref,
                q_ref, kv_ref, dw_ref, gt_ref, xn_ref,
                convb_ref, clng_ref, clnb_ref,
                wao_ref, wco_ref, wout_ref,
                r_ref,
                bias_s, o_s):
    b_id = pl.program_id(0)
    i_id = pl.program_id(1)
    ts = q_ref.shape[0]
    blocks_per_tile = ts // BLOCK
    dot = functools.partial(jnp.dot, preferred_element_type=jnp.float32)

    @pl.when((b_id == 0) & (i_id == 0))
    def _init():
        bk = bk_ref[...]
        col = lax.broadcasted_iota(jnp.int32, (BLOCK, BAND), 1)
        for h in range(N_HEADS):
            def body(b, acc):
                return jnp.where(bk == b, relb_ref[b, h], acc)
            mid = LOG2E * lax.fori_loop(0, NUM_BUCKETS, body,
                                        jnp.full((BLOCK, BAND), NEG_INF, jnp.float32))
            bias_s[0, h] = jnp.where(col < BLOCK, NEG_INF, mid)
            bias_s[1, h] = mid
            bias_s[2, h] = jnp.where(col >= 2 * BLOCK, NEG_INF, mid)

    lane = lax.broadcasted_iota(jnp.int32, (BLOCK, LANES), 1)
    low_half = lane < HEAD_DIM
    nt = (((1,), (1,)), ((), ()))

    def block_operands(j):
        n = i_id * blocks_per_tile + j
        variant = jnp.where(n == 0, 0, jnp.where(n == n_seq_blocks - 1, 2, 1))
        prev_r = pl.multiple_of(jnp.maximum(n - 1, 0) * BLOCK, BLOCK)
        own_r = pl.multiple_of(n * BLOCK, BLOCK)
        next_r = pl.multiple_of(jnp.minimum(n + 1, n_seq_blocks - 1) * BLOCK, BLOCK)
        kvb = jnp.concatenate([kv_ref[pl.ds(prev_r, BLOCK), :],
                               kv_ref[pl.ds(own_r, BLOCK), :],
                               kv_ref[pl.ds(next_r, BLOCK), :]], axis=0)
        kk = kvb[:, :LANES]
        vv = kvb[:, LANES:]
        q = q_ref[j * BLOCK:(j + 1) * BLOCK, :]
        zero = jnp.zeros((BLOCK, LANES), q.dtype)
        q_head = {}
        for p in range(N_HEADS // 2):
            q2 = q[:, p * LANES:(p + 1) * LANES]
            q_head[2 * p] = jnp.where(low_half, q2, zero)
            q_head[2 * p + 1] = jnp.where(low_half, zero, q2)
        return dict(variant=variant, q_head=q_head,
                    k=(kk, pltpu.roll(kk, HEAD_DIM, axis=1)),
                    v=(vv, pltpu.roll(vv, HEAD_DIM, axis=1)))

    def scores(blk, heads, swapped):
        lhs = jnp.concatenate([blk["q_head"][h] for h in heads], axis=0)
        return lax.dot_general(lhs, blk["k"][swapped], nt, preferred_element_type=jnp.float32)

    def softmax_values(blk, heads, swapped, s_all, o_head):
        ps, inv_ls = [], []
        for r, h in enumerate(heads):
            s = s_all[r * BLOCK:(r + 1) * BLOCK, :] + bias_s[blk["variant"], h]
            sink = jnp.full((BLOCK, 1), sink_ref[h], jnp.float32) * LOG2E
            m = jnp.maximum(jnp.max(s, axis=-1, keepdims=True), sink)
            p = jnp.exp2(s - m)
            l = jnp.sum(p, axis=-1, keepdims=True) + jnp.exp2(sink - m)
            ps.append(p.astype(jnp.bfloat16))
            inv_ls.append(1.0 / l)
        r_all = dot(jnp.concatenate(ps, axis=0), blk["v"][swapped])
        for r, h in enumerate(heads):
            o_head[h] = r_all[r * BLOCK:(r + 1) * BLOCK, :] * inv_ls[r]

    stages = [(j, heads, int(swapped)) for j in range(blocks_per_tile) for heads, swapped in HEAD_GROUPS]
    yc_chunk = D_MODEL * 2 // len(stages)
    assert yc_chunk % MXU_COLS == 0 and len(stages) % 2 == 0
    yc_parts = []
    y = _layer_norm(dw_ref[...] + convb_ref[...], clng_ref[...], clnb_ref[...])
    yc = (y * jax.nn.sigmoid(y)).astype(jnp.bfloat16)
    blocks = {}
    o_heads = {}
    pending = None
    for c in range(len(stages) + 1):
        nxt = None
        if c < len(stages):
            j, heads, swapped = stages[c]
            if j not in blocks:
                blocks[j] = block_operands(j)
                o_heads[j] = {}
            nxt = (j, heads, swapped, scores(blocks[j], heads, swapped))
        if c > len(stages) - D_MODEL // yc_chunk:
            c0 = len(yc_parts) * yc_chunk
            yc_parts.append(dot(yc, wco_ref[:, c0:c0 + yc_chunk]))
        if pending is not None:
            j, heads, swapped, s_all = pending
            softmax_values(blocks[j], heads, swapped, s_all, o_heads[j])
            if len(o_heads[j]) == N_HEADS:
                o_pairs = [jnp.where(low_half, o_heads[j][2 * p], o_heads[j][2 * p + 1])
                           for p in range(N_HEADS // 2)]
                o_s[j * BLOCK:(j + 1) * BLOCK, :] = jnp.concatenate(o_pairs, axis=1).astype(o_s.dtype)
        pending = nxt

    y_a = dot(o_s[...], wao_ref[...])
    y_c = jnp.concatenate(yc_parts, axis=1)
    merged = (gt_ref[:, :D_MODEL].astype(jnp.float32) * y_a
              + gt_ref[:, D_MODEL:].astype(jnp.float32) * y_c)
    mix = dot(merged.astype(jnp.bfloat16), wout_ref[...])
    r_ref[...] = ALPHA * xn_ref[...] + mix


def _mix(rel_bias, sink, qkv, dw, gt, xn, conv_b, cln_g, cln_b, w_ao, w_co, w_out):
    batch, seq, _ = xn.shape
    ts = TS_MIX
    assert seq % ts == 0 and ts % BLOCK == 0 and seq // BLOCK >= 2
    n_seq_blocks = seq // BLOCK
    buckets = jnp.asarray(_band_buckets())
    tile = lambda b, i: (b, i, 0)
    return pl.pallas_call(
        functools.partial(_mix_kernel, n_seq_blocks),
        grid=(batch, seq // ts),
        in_specs=[
            _smem_spec(),
            _smem_spec(),
            _const_spec((BLOCK, BAND)),
            pl.BlockSpec((None, ts, Q_COLS), tile),
            pl.BlockSpec((None, seq, 2 * KV_COLS), lambda b, i: (b, 0, Q_COLS // (2 * KV_COLS))),
            pl.BlockSpec((None, ts, CONV_CH), tile),
            pl.BlockSpec((None, ts, GATE_COLS), tile),
            pl.BlockSpec((None, ts, D_MODEL), tile),
            _const_spec((1, CONV_CH)),
            _const_spec((1, CONV_CH)),
            _const_spec((1, CONV_CH)),
            _const_spec((Q_COLS, D_MODEL)),
            _const_spec((CONV_CH, D_MODEL)),
            _const_spec((D_MODEL, D_MODEL)),
        ],
        out_specs=pl.BlockSpec((None, ts, D_MODEL), tile),
        out_shape=jax.ShapeDtypeStruct((batch, seq, D_MODEL), jnp.float32),
        scratch_shapes=[
            pltpu.VMEM((3, N_HEADS, BLOCK, BAND), jnp.float32),
            pltpu.VMEM((ts, Q_COLS), jnp.bfloat16),
        ],
        compiler_params=pltpu.CompilerParams(
            dimension_semantics=(_ARB, _ARB), vmem_limit_bytes=VMEM_LIMIT_BYTES),
        name="token_mix",
    )(rel_bias, sink, buckets, qkv, qkv, dw, gt, xn, conv_b, cln_g, cln_b, w_ao, w_co, w_out)


def _ffn_kernel(r_ref, g1_ref, b1_ref, wgu_ref, wd_ref, g_ref, b_ref, o_ref):
    dot = functools.partial(jnp.dot, preferred_element_type=jnp.float32)
    inner = r_ref.shape[0] - 2 * FFN_EDGE_SUBTILE
    assert inner >= 0 and inner % FFN_SUBTILE == 0
    sizes = [FFN_EDGE_SUBTILE] + [FFN_SUBTILE] * (inner // FFN_SUBTILE) + [FFN_EDGE_SUBTILE]
    starts = [sum(sizes[:s]) for s in range(len(sizes) + 1)]
    n_sub = len(sizes)
    state = {}

    def up_stage(s):
        rows = slice(starts[s], starts[s + 1])
        h = _layer_norm(r_ref[rows, :], g1_ref[...], b1_ref[...])
        gu = dot(h.astype(jnp.bfloat16), wgu_ref[...])
        hid = []
        for c0 in range(0, gu.shape[1], 2 * MXU_COLS):
            gate = gu[:, c0:c0 + MXU_COLS]
            up = gu[:, c0 + MXU_COLS:c0 + 2 * MXU_COLS]
            hid.append((gate * jax.nn.sigmoid(gate) * up).astype(jnp.bfloat16))
        state[s] = (h, jnp.concatenate(hid, axis=1))

    def down_stage(s):
        rows = slice(starts[s], starts[s + 1])
        h, hid = state.pop(s)
        o_ref[rows, :] = _layer_norm(ALPHA * h + dot(hid, wd_ref[...]), g_ref[...], b_ref[...])

    for s in range(n_sub):
        up_stage(s)
    for s in range(n_sub):
        down_stage(s)


def _ffn(r2d, ln1_g, ln1_b, w_gu, w_down, ln_g, ln_b):
    n_tok = r2d.shape[0]
    assert n_tok % TM_FFN == 0
    row = lambda i: (i, 0)
    return pl.pallas_call(
        _ffn_kernel,
        grid=(n_tok // TM_FFN,),
        in_specs=[
            pl.BlockSpec((TM_FFN, D_MODEL), row),
            _const_spec((1, D_MODEL)),
            _const_spec((1, D_MODEL)),
            _const_spec(w_gu.shape),
            _const_spec(w_down.shape),
            _const_spec((1, D_MODEL)),
            _const_spec((1, D_MODEL)),
        ],
        out_specs=pl.BlockSpec((TM_FFN, D_MODEL), row),
        out_shape=jax.ShapeDtypeStruct((n_tok, D_MODEL), jnp.float32),
        compiler_params=pltpu.CompilerParams(
            dimension_semantics=(_ARB,), vmem_limit_bytes=VMEM_LIMIT_BYTES),
        name="swiglu_ffn",
    )(r2d, ln1_g, ln1_b, w_gu, w_down, ln_g, ln_b)


def kernel(x, ln_in_g, ln_in_b, rel_bias, w_in, b_gate, conv_w, conv_b, conv_ln_g, conv_ln_b,
           w_conv_out, w_attn_out, sink, w_out, ln1_g, ln1_b, w_gate, w_up, w_down, ln2_g, ln2_b):
    batch, seq, d_model = x.shape
    assert d_model == D_MODEL and w_in.shape[0] == DEPTH
    row = lambda v: v.reshape(1, -1)
    n_tok = batch * seq

    (xn, qkv, gt, dw), (w_ao, w_co, w_o, w_gu, w_d) = _inproj(
        x, row(ln_in_g), row(ln_in_b), w_in[0], row(b_gate[0]), conv_w[0],
        (w_attn_out[0], w_conv_out[0], w_out[0], w_gate[0], w_up[0], w_down[0]))
    r = _mix(rel_bias, sink[0], qkv, dw, gt, xn,
             row(conv_b[0]), row(conv_ln_g[0]), row(conv_ln_b[0]), w_ao, w_co, w_o)
    out = _ffn(r.reshape(n_tok, D_MODEL), row(ln1_g[0]), row(ln1_b[0]), w_gu, w_d,
               row(ln2_g[0]), row(ln2_b[0]))
    return out.reshape(batch, seq, D_MODEL)
```

```python
import functools

import numpy as np
import jax
import jax.numpy as jnp
from jax import lax
from jax.experimental import pallas as pl
from jax.experimental.pallas import tpu as pltpu

D_MODEL = 1024
HEAD_DIM = 64
N_HEADS = 8
N_KV_HEADS = 2
WINDOW = 128
BLOCK = 128
NUM_BUCKETS = 32
MAX_DISTANCE = 128
CONV_CH = 512
CONV_WIDTH = 31
CONV_HALF = (CONV_WIDTH - 1) // 2
Q_COLS = N_HEADS * HEAD_DIM
KV_COLS = N_KV_HEADS * HEAD_DIM
CONV_COLS = 2 * CONV_CH
GATE_COLS = 2 * D_MODEL
QKV_COLS = Q_COLS + 2 * KV_COLS
DEPTH = 1
ALPHA = (2.0 * DEPTH) ** 0.25
LN_EPS = 1e-5
LOG2E = 1.4426950408889634
Q_SCALE = HEAD_DIM ** -0.5 * LOG2E
NEG_INF = -1e30
MASKED_BUCKET = NUM_BUCKETS

HEAD_GROUPS = (((0, 2, 5, 7), False), ((1, 3, 4, 6), True))
BAND = 3 * BLOCK
HALO = 16
CONV_ROWS = 64
CONV_GROUP = 4
SUBLANES = 8
BF16_SUBLANES = 16
LANES = 128
MXU_COLS = 256
CONV_SLABS = CONV_CH // LANES

VMEM_LIMIT_BYTES = 56 * 1024 * 1024
INPROJ_VMEM_LIMIT_BYTES = 60 * 1024 * 1024
WCAST_ROWS = 128

TS_IN = 512
TS_MIX = 512
TM_FFN = 1024
FFN_SUBTILE = 256
FFN_EDGE_SUBTILE = 128

_ARB = "arbitrary"


def _layer_norm(x, g, b):
    mu = jnp.mean(x, axis=-1, keepdims=True)
    xc = x - mu
    var = jnp.mean(xc * xc, axis=-1, keepdims=True)
    return xc * lax.rsqrt(var + LN_EPS) * g + b


def _const_spec(shape):
    return pl.BlockSpec(shape, lambda *_: (0,) * len(shape), pipeline_mode=pl.Buffered(1))


def _smem_spec():
    return pl.BlockSpec(memory_space=pltpu.SMEM)


def _halo_specs(ts, seq, cols):
    per_tile = ts // HALO
    last = seq // HALO - 1
    prev = pl.BlockSpec((None, HALO, cols), lambda b, i: (b, jnp.maximum(i * per_tile - 1, 0), 0))
    nxt = pl.BlockSpec((None, HALO, cols), lambda b, i: (b, jnp.minimum((i + 1) * per_tile, last), 0))
    return prev, nxt


def _inproj_kernel(x_ref, xp_ref, xnx_ref, g_ref, b_ref, w32_ref, bgate_ref,
                   convw_ref,
                   wao_ref, wco_ref, wout_ref, wgate_ref, wup_ref, wdown_ref,
                   xn_ref, qkv_ref, gt_ref, dw_ref,
                   wao_o, wco_o, wout_o, wgu_o, wdown_o,
                   w_ref, xb_s, wb_s, ge_s, go_s):
    b_id = pl.program_id(0)
    i_id = pl.program_id(1)
    n_tiles = pl.num_programs(1)
    ts = x_ref.shape[0]
    dot = functools.partial(jnp.dot, preferred_element_type=jnp.float32)
    o1 = QKV_COLS
    o2 = o1 + CONV_COLS
    o3 = o2 + D_MODEL

    @pl.when((b_id == 0) & (i_id == 0))
    def _init():
        for k in range(CONV_WIDTH):
            wb_s[k] = jnp.broadcast_to(convw_ref[k:k + 1, :].astype(wb_s.dtype), (BF16_SUBLANES, CONV_CH))
        for r0 in range(0, w32_ref.shape[0], WCAST_ROWS):
            w_ref[r0:r0 + WCAST_ROWS, :] = w32_ref[r0:r0 + WCAST_ROWS, :].astype(w_ref.dtype)

    @pl.when(i_id == 0)
    def _cast_weights():
        for src, dst in ((wao_ref, wao_o), (wco_ref, wco_o), (wout_ref, wout_o), (wdown_ref, wdown_o)):
            dst[...] = src[...].astype(dst.dtype)
        for j in range(wgate_ref.shape[1] // MXU_COLS):
            src_cols = slice(j * MXU_COLS, (j + 1) * MXU_COLS)
            wgu_o[:, 2 * j * MXU_COLS:(2 * j + 1) * MXU_COLS] = wgate_ref[:, src_cols].astype(wgu_o.dtype)
            wgu_o[:, (2 * j + 1) * MXU_COLS:(2 * j + 2) * MXU_COLS] = wup_ref[:, src_cols].astype(wgu_o.dtype)

    g = g_ref[...]
    b = b_ref[...]
    xn = _layer_norm(x_ref[...], g, b)
    xn_ref[...] = xn
    xb_s[0:HALO, :] = _layer_norm(xp_ref[...], g, b).astype(xb_s.dtype)
    xb_s[HALO:HALO + ts, :] = xn.astype(xb_s.dtype)
    xb_s[HALO + ts:2 * HALO + ts, :] = _layer_norm(xnx_ref[...], g, b).astype(xb_s.dtype)

    u = dot(xb_s[...], w_ref[:, o1:o2])
    glu = u[:, :CONV_CH] * jax.nn.sigmoid(u[:, CONV_CH:])
    row = lax.broadcasted_iota(jnp.int32, (ts + 2 * HALO, 1), 0)
    pad = ((row < HALO) & (i_id == 0)) | ((row >= HALO + ts) & (i_id == n_tiles - 1))
    glu = jnp.where(pad, 0.0, glu)
    rows_ext = ts + 2 * HALO
    glu_up = pltpu.roll(glu, rows_ext - 1, axis=0)
    for c in range(CONV_SLABS):
        lanes = slice(c * LANES, (c + 1) * LANES)
        ge_s[c] = pltpu.bitcast(glu[:, lanes].astype(jnp.bfloat16), jnp.uint32)
        go_s[c] = pltpu.bitcast(glu_up[:, lanes].astype(jnp.bfloat16), jnp.uint32)

    xb = xb_s[HALO:HALO + ts, :]
    bgate = bgate_ref[...]
    gt_ref[:, :D_MODEL] = jax.nn.sigmoid(
        dot(xb, w_ref[:, o2:o3]) + bgate[:, :D_MODEL]).astype(gt_ref.dtype)
    gt_ref[:, D_MODEL:] = jax.nn.sigmoid(
        dot(xb, w_ref[:, o3:]) + bgate[:, D_MODEL:]).astype(gt_ref.dtype)
    qkv_ref[:, :Q_COLS] = (dot(xb, w_ref[:, :Q_COLS]) * Q_SCALE).astype(qkv_ref.dtype)
    qkv_ref[:, Q_COLS:] = dot(xb, w_ref[:, Q_COLS:o1]).astype(qkv_ref.dtype)

    first_tap = HALO - CONV_HALF

    def conv_chunk(chunk, carry):
        r0 = pl.multiple_of(chunk * CONV_ROWS, CONV_ROWS)
        w0 = pl.multiple_of(chunk * (CONV_ROWS // 2), CONV_ROWS // 2)
        n_groups = CONV_ROWS // BF16_SUBLANES
        for c in range(CONV_SLABS):
            lanes = slice(c * LANES, (c + 1) * LANES)
            accs = [jnp.zeros((BF16_SUBLANES, LANES), jnp.float32) for _ in range(n_groups)]
            for k0 in range(0, CONV_WIDTH, CONV_GROUP):
                parts = [None] * n_groups
                for k in range(k0, min(k0 + CONV_GROUP, CONV_WIDTH)):
                    w = wb_s[k, :, lanes]
                    for g in range(n_groups):
                        off = g * BF16_SUBLANES + first_tap + k
                        src = go_s if off % 2 else ge_s
                        words = src[c, pl.ds(w0 + off // 2, SUBLANES), :]
                        prod = pltpu.bitcast(words, jnp.bfloat16) * w
                        parts[g] = prod if parts[g] is None else parts[g] + prod
                for g in range(n_groups):
                    accs[g] = accs[g] + parts[g].astype(jnp.float32)
            for g in range(n_groups):
                dw_ref[pl.ds(r0 + g * BF16_SUBLANES, BF16_SUBLANES), lanes] = accs[g]
        return carry

    lax.fori_loop(0, ts // CONV_ROWS, conv_chunk, 0)


def _inproj(x, ln_g, ln_b, w_in, b_gate, conv_w, later_weights):
    batch, seq, _ = x.shape
    ts = TS_IN
    assert seq % ts == 0 and ts % CONV_ROWS == 0 and ts % HALO == 0 and HALO >= CONV_HALF
    tile = lambda b, i: (b, i, 0)
    prev_spec, next_spec = _halo_specs(ts, seq, D_MODEL)
    bf16 = jnp.bfloat16
    w_ao, w_co, w_out, w_gate, w_up, w_down = later_weights
    assert w_gate.shape == w_up.shape and w_gate.shape[1] % MXU_COLS == 0
    cast_out_shapes = [w_ao.shape, w_co.shape, w_out.shape,
                       (w_gate.shape[0], 2 * w_gate.shape[1]), w_down.shape]

    def slice_spec(shape):
        rows, cols = shape
        assert rows % (batch * BF16_SUBLANES) == 0
        return pl.BlockSpec((rows // batch, cols), lambda b, i: (b, 0))

    cast_in_specs = [slice_spec(w.shape) for w in later_weights]
    cast_out_specs = [slice_spec(s) for s in cast_out_shapes]
    outs = pl.pallas_call(
        _inproj_kernel,
        grid=(batch, seq // ts),
        in_specs=[
            pl.BlockSpec((None, ts, D_MODEL), tile),
            prev_spec,
            next_spec,
            _const_spec((1, D_MODEL)),
            _const_spec((1, D_MODEL)),
            _const_spec(w_in.shape),
            _const_spec((1, GATE_COLS)),
            _const_spec((CONV_WIDTH, CONV_CH)),
        ] + cast_in_specs,
        out_specs=[
            pl.BlockSpec((None, ts, D_MODEL), tile),
            pl.BlockSpec((None, ts, QKV_COLS), tile),
            pl.BlockSpec((None, ts, GATE_COLS), tile),
            pl.BlockSpec((None, ts, CONV_CH), tile),
        ] + cast_out_specs,
        out_shape=[
            jax.ShapeDtypeStruct((batch, seq, D_MODEL), jnp.float32),
            jax.ShapeDtypeStruct((batch, seq, QKV_COLS), bf16),
            jax.ShapeDtypeStruct((batch, seq, GATE_COLS), bf16),
            jax.ShapeDtypeStruct((batch, seq, CONV_CH), jnp.float32),
        ] + [jax.ShapeDtypeStruct(s, bf16) for s in cast_out_shapes],
        scratch_shapes=[
            pltpu.VMEM(w_in.shape, bf16),
            pltpu.VMEM((ts + 2 * HALO, D_MODEL), bf16),
            pltpu.VMEM((CONV_WIDTH, BF16_SUBLANES, CONV_CH), bf16),
            pltpu.VMEM((CONV_SLABS, ts // 2 + HALO, LANES), jnp.uint32),
            pltpu.VMEM((CONV_SLABS, ts // 2 + HALO, LANES), jnp.uint32),
        ],
        compiler_params=pltpu.CompilerParams(
            dimension_semantics=(_ARB, _ARB), vmem_limit_bytes=INPROJ_VMEM_LIMIT_BYTES),
        name="ln_inproj_conv",
    )(x, x, x, ln_g, ln_b, w_in, b_gate, conv_w, *later_weights)
    return outs[:4], outs[4:]


def _band_buckets():
    nb = NUM_BUCKETS // 2
    qi = np.arange(BLOCK)[:, None]
    kj = np.arange(BAND)[None, :]
    rel = kj - BLOCK - qi
    ret = (rel > 0).astype(np.int32) * nb
    n = np.abs(rel)
    max_exact = nb // 2
    large = max_exact + (np.log(np.maximum(n, 1) / max_exact)
                         / np.log(MAX_DISTANCE / max_exact) * (nb - max_exact)).astype(np.int32)
    large = np.minimum(large, nb - 1)
    buckets = (ret + np.where(n < max_exact, n, large)).astype(np.int32)
    return np.where(n <= WINDOW, buckets, MASKED_BUCKET).astype(np.int32)


def _mix_kernel(n_seq_blocks,
                relb_ref, sink_ref, bk_ref,
                q_ref, kv_ref, dw_ref, gt_ref, xn_ref,
                convb_ref, clng_ref, clnb_ref,
                wao_ref, wco_ref, wout_ref,
                r_ref,
                bias_s, o_s):
    b_id = pl.program_id(0)
    i_id = pl.program_id(1)
    ts = q_ref.shape[0]
    blocks_per_tile = ts // BLOCK
    dot = functools.partial(jnp.dot, preferred_element_type=jnp.float32)

    @pl.when((b_id == 0) & (i_id == 0))
    def _init():
        bk = bk_ref[...]
        col = lax.broadcasted_iota(jnp.int32, (BLOCK, BAND), 1)
        for h in range(N_HEADS):
            def body(b, acc):
                return jnp.where(bk == b, relb_ref[b, h], acc)
            mid = LOG2E * lax.fori_loop(0, NUM_BUCKETS, body,
                                        jnp.full((BLOCK, BAND), NEG_INF, jnp.float32))
            bias_s[0, h] = jnp.where(col < BLOCK, NEG_INF, mid)
            bias_s[1, h] = mid
            bias_s[2, h] = jnp.where(col >= 2 * BLOCK, NEG_INF, mid)

    lane = lax.broadcasted_iota(jnp.int32, (BLOCK, LANES), 1)
    low_half = lane < HEAD_DIM
    nt = (((1,), (1,)), ((), ()))

    def block_operands(j):
        n = i_id * blocks_per_tile + j
        variant = jnp.where(n == 0, 0, jnp.where(n == n_seq_blocks - 1, 2, 1))
        prev_r = pl.multiple_of(jnp.maximum(n - 1, 0) * BLOCK, BLOCK)
        own_r = pl.multiple_of(n * BLOCK, BLOCK)
        next_r = pl.multiple_of(jnp.minimum(n + 1, n_seq_blocks - 1) * BLOCK, BLOCK)
        kvb = jnp.concatenate([kv_ref[pl.ds(prev_r, BLOCK), :],
                               kv_ref[pl.ds(own_r, BLOCK), :],
                               kv_ref[pl.ds(next_r, BLOCK), :]], axis=0)
        kk = kvb[:, :LANES]
        vv = kvb[:, LANES:]
        q = q_ref[j * BLOCK:(j + 1) * BLOCK, :]
        zero = jnp.zeros((BLOCK, LANES), q.dtype)
        q_head = {}
        for p in range(N_HEADS // 2):
            q2 = q[:, p * LANES:(p + 1) * LANES]
            q_head[2 * p] = jnp.where(low_half, q2, zero)
            q_head[2 * p + 1] = jnp.where(low_half, zero, q2)
        return dict(variant=variant, q_head=q_head,
                    k=(kk, pltpu.roll(kk, HEAD_DIM, axis=1)),
                    v=(vv, pltpu.roll(vv, HEAD_DIM, axis=1)))

    def scores(blk, heads, swapped):
        lhs = jnp.concatenate([blk["q_head"][h] for h in heads], axis=0)
        return lax.dot_general(lhs, blk["k"][swapped], nt, preferred_element_type=jnp.float32)

    def softmax_values(blk, heads, swapped, s_all, o_head):
        ps, inv_ls = [], []
        for r, h in enumerate(heads):
            s = s_all[r * BLOCK:(r + 1) * BLOCK, :] + bias_s[blk["variant"], h]
            sink = jnp.full((BLOCK, 1), sink_ref[h], jnp.float32) * LOG2E
            m = jnp.maximum(jnp.max(s, axis=-1, keepdims=True), sink)
            p = jnp.exp2(s - m)
            l = jnp.sum(p, axis=-1, keepdims=True) + jnp.exp2(sink - m)
            ps.append(p.astype(jnp.bfloat16))
            inv_ls.append(1.0 / l)
        r_all = dot(jnp.concatenate(ps, axis=0), blk["v"][swapped])
        for r, h in enumerate(heads):
            o_head[h] = r_all[r * BLOCK:(r + 1) * BLOCK, :] * inv_ls[r]

    stages = [(j, heads, int(swapped)) for j in range(blocks_per_tile) for heads, swapped in HEAD_GROUPS]
    yc_rows = ts * 2 // len(stages)
    assert yc_rows % BF16_SUBLANES == 0 and len(stages) % 2 == 0
    yc_parts = []
    blocks = {}
    o_heads = {}
    pending = None
    for c in range(len(stages) + 1):
        nxt = None
        if c < len(stages):
            j, heads, swapped = stages[c]
            if j not in blocks:
                blocks[j] = block_operands(j)
                o_heads[j] = {}
            nxt = (j, heads, swapped, scores(blocks[j], heads, swapped))
        if c % 2 == 1 and len(yc_parts) * yc_rows < ts:
            rows = slice(len(yc_parts) * yc_rows, (len(yc_parts) + 1) * yc_rows)
            y = _layer_norm(dw_ref[rows, :] + convb_ref[...], clng_ref[...], clnb_ref[...])
            yc_parts.append(dot((y * jax.nn.sigmoid(y)).astype(jnp.bfloat16), wco_ref[...]))
        if pending is not None:
            j, heads, swapped, s_all = pending
            softmax_values(blocks[j], heads, swapped, s_all, o_heads[j])
            if len(o_heads[j]) == N_HEADS:
                o_pairs = [jnp.where(low_half, o_heads[j][2 * p], o_heads[j][2 * p + 1])
                           for p in range(N_HEADS // 2)]
                o_s[j * BLOCK:(j + 1) * BLOCK, :] = jnp.concatenate(o_pairs, axis=1).astype(o_s.dtype)
        pending = nxt

    y_a = dot(o_s[...], wao_ref[...])
    y_c = jnp.concatenate(yc_parts, axis=0)
    merged = (gt_ref[:, :D_MODEL].astype(jnp.float32) * y_a
              + gt_ref[:, D_MODEL:].astype(jnp.float32) * y_c)
    mix = dot(merged.astype(jnp.bfloat16), wout_ref[...])
    r_ref[...] = ALPHA * xn_ref[...] + mix


def _mix(rel_bias, sink, qkv, dw, gt, xn, conv_b, cln_g, cln_b, w_ao, w_co, w_out):
    batch, seq, _ = xn.shape
    ts = TS_MIX
    assert seq % ts == 0 and ts % BLOCK == 0 and seq // BLOCK >= 2
    n_seq_blocks = seq // BLOCK
    buckets = jnp.asarray(_band_buckets())
    tile = lambda b, i: (b, i, 0)
    return pl.pallas_call(
        functools.partial(_mix_kernel, n_seq_blocks),
        grid=(batch, seq // ts),
        in_specs=[
            _smem_spec(),
            _smem_spec(),
            _const_spec((BLOCK, BAND)),
            pl.BlockSpec((None, ts, Q_COLS), tile),
            pl.BlockSpec((None, seq, 2 * KV_COLS), lambda b, i: (b, 0, Q_COLS // (2 * KV_COLS))),
            pl.BlockSpec((None, ts, CONV_CH), tile),
            pl.BlockSpec((None, ts, GATE_COLS), tile),
            pl.BlockSpec((None, ts, D_MODEL), tile),
            _const_spec((1, CONV_CH)),
            _const_spec((1, CONV_CH)),
            _const_spec((1, CONV_CH)),
            _const_spec((Q_COLS, D_MODEL)),
            _const_spec((CONV_CH, D_MODEL)),
            _const_spec((D_MODEL, D_MODEL)),
        ],
        out_specs=pl.BlockSpec((None, ts, D_MODEL), tile),
        out_shape=jax.ShapeDtypeStruct((batch, seq, D_MODEL), jnp.float32),
        scratch_shapes=[
            pltpu.VMEM((3, N_HEADS, BLOCK, BAND), jnp.float32),
            pltpu.VMEM((ts, Q_COLS), jnp.bfloat16),
        ],
        compiler_params=pltpu.CompilerParams(
            dimension_semantics=(_ARB, _ARB), vmem_limit_bytes=VMEM_LIMIT_BYTES),
        name="token_mix",
    )(rel_bias, sink, buckets, qkv, qkv, dw, gt, xn, conv_b, cln_g, cln_b, w_ao, w_co, w_out)


def _ffn_kernel(r_ref, g1_ref, b1_ref, wgu_ref, wd_ref, g_ref, b_ref, o_ref):
    dot = functools.partial(jnp.dot, preferred_element_type=jnp.float32)
    inner = r_ref.shape[0] - 2 * FFN_EDGE_SUBTILE
    assert inner >= 0 and inner % FFN_SUBTILE == 0
    sizes = [FFN_EDGE_SUBTILE] + [FFN_SUBTILE] * (inner // FFN_SUBTILE) + [FFN_EDGE_SUBTILE]
    starts = [sum(sizes[:s]) for s in range(len(sizes) + 1)]
    n_sub = len(sizes)
    state = {}

    def up_stage(s):
        rows = slice(starts[s], starts[s + 1])
        h = _layer_norm(r_ref[rows, :], g1_ref[...], b1_ref[...])
        gu = dot(h.astype(jnp.bfloat16), wgu_ref[...])
        hid = []
        for c0 in range(0, gu.shape[1], 2 * MXU_COLS):
            gate = gu[:, c0:c0 + MXU_COLS]
            up = gu[:, c0 + MXU_COLS:c0 + 2 * MXU_COLS]
            hid.append((gate * jax.nn.sigmoid(gate) * up).astype(jnp.bfloat16))
        state[s] = (h, jnp.concatenate(hid, axis=1))

    def down_stage(s):
        rows = slice(starts[s], starts[s + 1])
        h, hid = state.pop(s)
        o_ref[rows, :] = _layer_norm(ALPHA * h + dot(hid, wd_ref[...]), g_ref[...], b_ref[...])

    for s in range(n_sub):
        up_stage(s)
    for s in range(n_sub):
        down_stage(s)


def _ffn(r2d, ln1_g, ln1_b, w_gu, w_down, ln_g, ln_b):
    n_tok = r2d.shape[0]
    assert n_tok % TM_FFN == 0
    row = lambda i: (i, 0)
    return pl.pallas_call(
        _ffn_kernel,
        grid=(n_tok // TM_FFN,),
        in_specs=[
            pl.BlockSpec((TM_FFN, D_MODEL), row),
            _const_spec((1, D_MODEL)),
            _const_spec((1, D_MODEL)),
            _const_spec(w_gu.shape),
            _const_spec(w_down.shape),
            _const_spec((1, D_MODEL)),
            _const_spec((1, D_MODEL)),
        ],
        out_specs=pl.BlockSpec((TM_FFN, D_MODEL), row),
        out_shape=jax.ShapeDtypeStruct((n_tok, D_MODEL), jnp.float32),
        compiler_params=pltpu.CompilerParams(
            dimension_semantics=(_ARB,), vmem_limit_bytes=VMEM_LIMIT_BYTES),
        name="swiglu_ffn",
    )(r2d, ln1_g, ln1_b, w_gu, w_down, ln_g, ln_b)


def kernel(x, ln_in_g, ln_in_b, rel_bias, w_in, b_gate, conv_w, conv_b, conv_ln_g, conv_ln_b,
           w_conv_out, w_attn_out, sink, w_out, ln1_g, ln1_b, w_gate, w_up, w_down, ln2_g, ln2_b):
    batch, seq, d_model = x.shape
    assert d_model == D_MODEL and w_in.shape[0] == DEPTH
    row = lambda v: v.reshape(1, -1)
    n_tok = batch * seq

    (xn, qkv, gt, dw), (w_ao, w_co, w_o, w_gu, w_d) = _inproj(
        x, row(ln_in_g), row(ln_in_b), w_in[0], row(b_gate[0]), conv_w[0],
        (w_attn_out[0], w_conv_out[0], w_out[0], w_gate[0], w_up[0], w_down[0]))
    r = _mix(rel_bias, sink[0], qkv, dw, gt, xn,
             row(conv_b[0]), row(conv_ln_g[0]), row(conv_ln_b[0]), w_ao, w_co, w_o)
    out = _ffn(r.reshape(n_tok, D_MODEL), row(ln1_g[0]), row(ln1_b[0]), w_gu, w_d,
               row(ln2_g[0]), row(ln2_b[0]))
    return out.reshape(batch, seq, D_MODEL)
```

```python
import functools

import numpy as np
import jax
import jax.numpy as jnp
from jax import lax
from jax.experimental import pallas as pl
from jax.experimental.pallas import tpu as pltpu

D_MODEL = 1024
HEAD_DIM = 64
N_HEADS = 8
N_KV_HEADS = 2
WINDOW = 128
BLOCK = 128
NUM_BUCKETS = 32
MAX_DISTANCE = 128
CONV_CH = 512
CONV_WIDTH = 31
CONV_HALF = (CONV_WIDTH - 1) // 2
Q_COLS = N_HEADS * HEAD_DIM
KV_COLS = N_KV_HEADS * HEAD_DIM
CONV_COLS = 2 * CONV_CH
GATE_COLS = 2 * D_MODEL
QKV_COLS = Q_COLS + 2 * KV_COLS
DEPTH = 1
ALPHA = (2.0 * DEPTH) ** 0.25
LN_EPS = 1e-5
LOG2E = 1.4426950408889634
Q_SCALE = HEAD_DIM ** -0.5 * LOG2E
NEG_INF = -1e30
MASKED_BUCKET = NUM_BUCKETS

HEAD_GROUPS = (((0, 2, 5, 7), False), ((1, 3, 4, 6), True))
BAND = 3 * BLOCK
HALO = 16
CONV_ROWS = 64
CONV_GROUP = 4
SUBLANES = 8
BF16_SUBLANES = 16
LANES = 128
MXU_COLS = 256
CONV_SLABS = CONV_CH // LANES

VMEM_LIMIT_BYTES = 56 * 1024 * 1024
INPROJ_VMEM_LIMIT_BYTES = 60 * 1024 * 1024
WCAST_ROWS = 128

TS_IN = 512
TS_MIX = 512
TM_FFN = 1024
FFN_SUBTILE = 256
FFN_EDGE_SUBTILE = 128

_ARB = "arbitrary"


def _layer_norm(x, g, b):
    mu = jnp.mean(x, axis=-1, keepdims=True)
    xc = x - mu
    var = jnp.mean(xc * xc, axis=-1, keepdims=True)
    return xc * lax.rsqrt(var + LN_EPS) * g + b


def _const_spec(shape):
    return pl.BlockSpec(shape, lambda *_: (0,) * len(shape), pipeline_mode=pl.Buffered(1))


def _smem_spec():
    return pl.BlockSpec(memory_space=pltpu.SMEM)


def _halo_specs(ts, seq, cols):
    per_tile = ts // HALO
    last = seq // HALO - 1
    prev = pl.BlockSpec((None, HALO, cols), lambda b, i: (b, jnp.maximum(i * per_tile - 1, 0), 0))
    nxt = pl.BlockSpec((None, HALO, cols), lambda b, i: (b, jnp.minimum((i + 1) * per_tile, last), 0))
    return prev, nxt


def _inproj_kernel(x_ref, xp_ref, xnx_ref, g_ref, b_ref, w32_ref, bgate_ref,
                   convw_ref,
                   wao_ref, wco_ref, wout_ref, wgate_ref, wup_ref, wdown_ref,
                   xn_ref, qkv_ref, gt_ref, dw_ref,
                   wao_o, wco_o, wout_o, wgu_o, wdown_o,
                   w_ref, xb_s, wb_s, ge_s, go_s):
    b_id = pl.program_id(0)
    i_id = pl.program_id(1)
    n_tiles = pl.num_programs(1)
    ts = x_ref.shape[0]
    dot = functools.partial(jnp.dot, preferred_element_type=jnp.float32)
    o1 = QKV_COLS
    o2 = o1 + CONV_COLS
    o3 = o2 + D_MODEL

    @pl.when((b_id == 0) & (i_id == 0))
    def _init():
        for k in range(CONV_WIDTH):
            wb_s[k] = jnp.broadcast_to(convw_ref[k:k + 1, :].astype(wb_s.dtype), (BF16_SUBLANES, CONV_CH))
        for r0 in range(0, w32_ref.shape[0], WCAST_ROWS):
            w_ref[r0:r0 + WCAST_ROWS, :] = w32_ref[r0:r0 + WCAST_ROWS, :].astype(w_ref.dtype)

    @pl.when(i_id == 0)
    def _cast_weights():
        for src, dst in ((wao_ref, wao_o), (wco_ref, wco_o), (wout_ref, wout_o), (wdown_ref, wdown_o)):
            dst[...] = src[...].astype(dst.dtype)
        for j in range(wgate_ref.shape[1] // MXU_COLS):
            src_cols = slice(j * MXU_COLS, (j + 1) * MXU_COLS)
            wgu_o[:, 2 * j * MXU_COLS:(2 * j + 1) * MXU_COLS] = wgate_ref[:, src_cols].astype(wgu_o.dtype)
            wgu_o[:, (2 * j + 1) * MXU_COLS:(2 * j + 2) * MXU_COLS] = wup_ref[:, src_cols].astype(wgu_o.dtype)

    g = g_ref[...]
    b = b_ref[...]
    xn = _layer_norm(x_ref[...], g, b)
    xn_ref[...] = xn
    xb_s[0:HALO, :] = _layer_norm(xp_ref[...], g, b).astype(xb_s.dtype)
    xb_s[HALO:HALO + ts, :] = xn.astype(xb_s.dtype)
    xb_s[HALO + ts:2 * HALO + ts, :] = _layer_norm(xnx_ref[...], g, b).astype(xb_s.dtype)

    u = dot(xb_s[...], w_ref[:, o1:o2])
    glu = u[:, :CONV_CH] * jax.nn.sigmoid(u[:, CONV_CH:])
    row = lax.broadcasted_iota(jnp.int32, (ts + 2 * HALO, 1), 0)
    pad = ((row < HALO) & (i_id == 0)) | ((row >= HALO + ts) & (i_id == n_tiles - 1))
    glu = jnp.where(pad, 0.0, glu)
    rows_ext = ts + 2 * HALO
    glu_up = pltpu.roll(glu, rows_ext - 1, axis=0)
    for c in range(CONV_SLABS):
        lanes = slice(c * LANES, (c + 1) * LANES)
        ge_s[c] = pltpu.bitcast(glu[:, lanes].astype(jnp.bfloat16), jnp.uint32)
        go_s[c] = pltpu.bitcast(glu_up[:, lanes].astype(jnp.bfloat16), jnp.uint32)

    xb = xb_s[HALO:HALO + ts, :]
    bgate = bgate_ref[...]
    gt_ref[:, :D_MODEL] = jax.nn.sigmoid(
        dot(xb, w_ref[:, o2:o3]) + bgate[:, :D_MODEL]).astype(gt_ref.dtype)
    gt_ref[:, D_MODEL:] = jax.nn.sigmoid(
        dot(xb, w_ref[:, o3:]) + bgate[:, D_MODEL:]).astype(gt_ref.dtype)
    qkv_ref[:, :Q_COLS] = (dot(xb, w_ref[:, :Q_COLS]) * Q_SCALE).astype(qkv_ref.dtype)
    qkv_ref[:, Q_COLS:] = dot(xb, w_ref[:, Q_COLS:o1]).astype(qkv_ref.dtype)

    first_tap = HALO - CONV_HALF

    def conv_chunk(chunk, carry):
        r0 = pl.multiple_of(chunk * CONV_ROWS, CONV_ROWS)
        w0 = pl.multiple_of(chunk * (CONV_ROWS // 2), CONV_ROWS // 2)
        n_groups = CONV_ROWS // BF16_SUBLANES
        for c in range(CONV_SLABS):
            lanes = slice(c * LANES, (c + 1) * LANES)
            accs = [jnp.zeros((BF16_SUBLANES, LANES), jnp.float32) for _ in range(n_groups)]
            for k0 in range(0, CONV_WIDTH, CONV_GROUP):
                parts = [None] * n_groups
                for k in range(k0, min(k0 + CONV_GROUP, CONV_WIDTH)):
                    w = wb_s[k, :, lanes]
                    for g in range(n_groups):
                        off = g * BF16_SUBLANES + first_tap + k
                        src = go_s if off % 2 else ge_s
                        words = src[c, pl.ds(w0 + off // 2, SUBLANES), :]
                        prod = pltpu.bitcast(words, jnp.bfloat16) * w
                        parts[g] = prod if parts[g] is None else parts[g] + prod
                for g in range(n_groups):
                    accs[g] = accs[g] + parts[g].astype(jnp.float32)
            for g in range(n_groups):
                dw_ref[pl.ds(r0 + g * BF16_SUBLANES, BF16_SUBLANES), lanes] = accs[g]
        return carry

    lax.fori_loop(0, ts // CONV_ROWS, conv_chunk, 0)


def _inproj(x, ln_g, ln_b, w_in, b_gate, conv_w, later_weights):
    batch, seq, _ = x.shape
    ts = TS_IN
    assert seq % ts == 0 and ts % CONV_ROWS == 0 and ts % HALO == 0 and HALO >= CONV_HALF
    tile = lambda b, i: (b, i, 0)
    prev_spec, next_spec = _halo_specs(ts, seq, D_MODEL)
    bf16 = jnp.bfloat16
    w_ao, w_co, w_out, w_gate, w_up, w_down = later_weights
    assert w_gate.shape == w_up.shape and w_gate.shape[1] % MXU_COLS == 0
    cast_out_shapes = [w_ao.shape, w_co.shape, w_out.shape,
                       (w_gate.shape[0], 2 * w_gate.shape[1]), w_down.shape]

    def slice_spec(shape):
        rows, cols = shape
        assert rows % (batch * BF16_SUBLANES) == 0
        return pl.BlockSpec((rows // batch, cols), lambda b, i: (b, 0))

    cast_in_specs = [slice_spec(w.shape) for w in later_weights]
    cast_out_specs = [slice_spec(s) for s in cast_out_shapes]
    outs = pl.pallas_call(
        _inproj_kernel,
        grid=(batch, seq // ts),
        in_specs=[
            pl.BlockSpec((None, ts, D_MODEL), tile),
            prev_spec,
            next_spec,
            _const_spec((1, D_MODEL)),
            _const_spec((1, D_MODEL)),
            _const_spec(w_in.shape),
            _const_spec((1, GATE_COLS)),
            _const_spec((CONV_WIDTH, CONV_CH)),
        ] + cast_in_specs,
        out_specs=[
            pl.BlockSpec((None, ts, D_MODEL), tile),
            pl.BlockSpec((None, ts, QKV_COLS), tile),
            pl.BlockSpec((None, ts, GATE_COLS), tile),
            pl.BlockSpec((None, ts, CONV_CH), tile),
        ] + cast_out_specs,
        out_shape=[
            jax.ShapeDtypeStruct((batch, seq, D_MODEL), jnp.float32),
            jax.ShapeDtypeStruct((batch, seq, QKV_COLS), bf16),
            jax.ShapeDtypeStruct((batch, seq, GATE_COLS), bf16),
            jax.ShapeDtypeStruct((batch, seq, CONV_CH), jnp.float32),
        ] + [jax.ShapeDtypeStruct(s, bf16) for s in cast_out_shapes],
        scratch_shapes=[
            pltpu.VMEM(w_in.shape, bf16),
            pltpu.VMEM((ts + 2 * HALO, D_MODEL), bf16),
            pltpu.VMEM((CONV_WIDTH, BF16_SUBLANES, CONV_CH), bf16),
            pltpu.VMEM((CONV_SLABS, ts // 2 + HALO, LANES), jnp.uint32),
            pltpu.VMEM((CONV_SLABS, ts // 2 + HALO, LANES), jnp.uint32),
        ],
        compiler_params=pltpu.CompilerParams(
            dimension_semantics=(_ARB, _ARB), vmem_limit_bytes=INPROJ_VMEM_LIMIT_BYTES),
        name="ln_inproj_conv",
    )(x, x, x, ln_g, ln_b, w_in, b_gate, conv_w, *later_weights)
    return outs[:4], outs[4:]


def _band_buckets():
    nb = NUM_BUCKETS // 2
    qi = np.arange(BLOCK)[:, None]
    kj = np.arange(BAND)[None, :]
    rel = kj - BLOCK - qi
    ret = (rel > 0).astype(np.int32) * nb
    n = np.abs(rel)
    max_exact = nb // 2
    large = max_exact + (np.log(np.maximum(n, 1) / max_exact)
                         / np.log(MAX_DISTANCE / max_exact) * (nb - max_exact)).astype(np.int32)
    large = np.minimum(large, nb - 1)
    buckets = (ret + np.where(n < max_exact, n, large)).astype(np.int32)
    return np.where(n <= WINDOW, buckets, MASKED_BUCKET).astype(np.int32)


def _mix_kernel(n_seq_blocks,
                relb_ref, sink_ref, bk_ref,
                q_ref, kv_ref, dw_ref, gt_ref, xn_ref,
                convb_ref, clng_ref, clnb_ref,
                wao_ref, wco_ref, wout_ref,
                r_ref,
                bias_s, o_s):
    b_id = pl.program_id(0)
    i_id = pl.program_id(1)
    ts = q_ref.shape[0]
    blocks_per_tile = ts // BLOCK
    dot = functools.partial(jnp.dot, preferred_element_type=jnp.float32)

    @pl.when((b_id == 0) & (i_id == 0))
    def _init():
        bk = bk_ref[...]
        col = lax.broadcasted_iota(jnp.int32, (BLOCK, BAND), 1)
        for h in range(N_HEADS):
            def body(b, acc):
                return jnp.where(bk == b, relb_ref[b, h], acc)
            mid = LOG2E * lax.fori_loop(0, NUM_BUCKETS, body,
                                        jnp.full((BLOCK, BAND), NEG_INF, jnp.float32))
            bias_s[0, h] = jnp.where(col < BLOCK, NEG_INF, mid)
            bias_s[1, h] = mid
            bias_s[2, h] = jnp.where(col >= 2 * BLOCK, NEG_INF, mid)

    lane = lax.broadcasted_iota(jnp.int32, (BLOCK, LANES), 1)
    low_half = lane < HEAD_DIM
    nt = (((1,), (1,)), ((), ()))

    def block_operands(j):
        n = i_id * blocks_per_tile + j
        variant = jnp.where(n == 0, 0, jnp.where(n == n_seq_blocks - 1, 2, 1))
        prev_r = pl.multiple_of(jnp.maximum(n - 1, 0) * BLOCK, BLOCK)
        own_r = pl.multiple_of(n * BLOCK, BLOCK)
        next_r = pl.multiple_of(jnp.minimum(n + 1, n_seq_blocks - 1) * BLOCK, BLOCK)
        kvb = jnp.concatenate([kv_ref[pl.ds(prev_r, BLOCK), :],
                               kv_ref[pl.ds(own_r, BLOCK), :],
                               kv_ref[pl.ds(next_r, BLOCK), :]], axis=0)
        kk = kvb[:, :LANES]
        vv = kvb[:, LANES:]
        q = q_ref[j * BLOCK:(j + 1) * BLOCK, :]
        zero = jnp.zeros((BLOCK, LANES), q.dtype)
        q_head = {}
        for p in range(N_HEADS // 2):
            q2 = q[:, p * LANES:(p + 1) * LANES]
            q_head[2 * p] = jnp.where(low_half, q2, zero)
            q_head[2 * p + 1] = jnp.where(low_half, zero, q2)
        return dict(variant=variant, q_head=q_head,
                    k=(kk, pltpu.roll(kk, HEAD_DIM, axis=1)),
                    v=(vv, pltpu.roll(vv, HEAD_DIM, axis=1)))

    def scores(blk, heads, swapped):
        lhs = jnp.concatenate([blk["q_head"][h] for h in heads], axis=0)
        return lax.dot_general(lhs, blk["k"][swapped], nt, preferred_element_type=jnp.float32)

    def softmax_values(blk, heads, swapped, s_all, o_head):
        ps, inv_ls = [], []
        for r, h in enumerate(heads):
            s = s_all[r * BLOCK:(r + 1) * BLOCK, :] + bias_s[blk["variant"], h]
            sink = jnp.full((BLOCK, 1), sink_ref[h], jnp.float32) * LOG2E
            m = jnp.maximum(jnp.max(s, axis=-1, keepdims=True), sink)
            p = jnp.exp2(s - m)
            l = jnp.sum(p, axis=-1, keepdims=True) + jnp.exp2(sink - m)
            ps.append(p.astype(jnp.bfloat16))
            inv_ls.append(1.0 / l)
        r_all = dot(jnp.concatenate(ps, axis=0), blk["v"][swapped])
        for r, h in enumerate(heads):
            o_head[h] = r_all[r * BLOCK:(r + 1) * BLOCK, :] * inv_ls[r]

    stages = [(j, heads, int(swapped)) for j in range(blocks_per_tile) for heads, swapped in HEAD_GROUPS]
    yc_chunk = D_MODEL * 2 // len(stages)
    assert yc_chunk % MXU_COLS == 0 and len(stages) % 2 == 0
    yc_parts = []
    y = _layer_norm(dw_ref[...] + convb_ref[...], clng_ref[...], clnb_ref[...])
    yc = (y * jax.nn.sigmoid(y)).astype(jnp.bfloat16)
    blocks = {}
    o_heads = {}
    pending = None
    for c in range(len(stages) + 1):
        nxt = None
        if c < len(stages):
            j, heads, swapped = stages[c]
            if j not in blocks:
                blocks[j] = block_operands(j)
                o_heads[j] = {}
            nxt = (j, heads, swapped, scores(blocks[j], heads, swapped))
        if c % 2 == 1 and len(yc_parts) * yc_chunk < D_MODEL:
            c0 = len(yc_parts) * yc_chunk
            yc_parts.append(dot(yc, wco_ref[:, c0:c0 + yc_chunk]))
        if pending is not None:
            j, heads, swapped, s_all = pending
            softmax_values(blocks[j], heads, swapped, s_all, o_heads[j])
            if len(o_heads[j]) == N_HEADS:
                o_pairs = [jnp.where(low_half, o_heads[j][2 * p], o_heads[j][2 * p + 1])
                           for p in range(N_HEADS // 2)]
                o_s[j * BLOCK:(j + 1) * BLOCK, :] = jnp.concatenate(o_pairs, axis=1).astype(o_s.dtype)
        pending = nxt

    y_a = dot(o_s[...], wao_ref[...])
    y_c = jnp.concatenate(yc_parts, axis=1)
    merged = (gt_ref[:, :D_MODEL].astype(jnp.float32) * y_a
              + gt_ref[:, D_MODEL:].astype(jnp.float32) * y_c)
    mix = dot(merged.astype(jnp.bfloat16), wout_ref[...])
    r_ref[...] = ALPHA * xn_ref[...] + mix


def _mix(rel_bias, sink, qkv, dw, gt, xn, conv_b, cln_g, cln_b, w_ao, w_co, w_out):
    batch, seq, _ = xn.shape
    ts = TS_MIX
    assert seq % ts == 0 and ts % BLOCK == 0 and seq // BLOCK >= 2
    n_seq_blocks = seq // BLOCK
    buckets = jnp.asarray(_band_buckets())
    tile = lambda b, i: (b, i, 0)
    return pl.pallas_call(
        functools.partial(_mix_kernel, n_seq_blocks),
        grid=(batch, seq // ts),
        in_specs=[
            _smem_spec(),
            _smem_spec(),
            _const_spec((BLOCK, BAND)),
            pl.BlockSpec((None, ts, Q_COLS), tile),
            pl.BlockSpec((None, seq, 2 * KV_COLS), lambda b, i: (b, 0, Q_COLS // (2 * KV_COLS))),
            pl.BlockSpec((None, ts, CONV_CH), tile),
            pl.BlockSpec((None, ts, GATE_COLS), tile),
            pl.BlockSpec((None, ts, D_MODEL), tile),
            _const_spec((1, CONV_CH)),
            _const_spec((1, CONV_CH)),
            _const_spec((1, CONV_CH)),
            _const_spec((Q_COLS, D_MODEL)),
            _const_spec((CONV_CH, D_MODEL)),
            _const_spec((D_MODEL, D_MODEL)),
        ],
        out_specs=pl.BlockSpec((None, ts, D_MODEL), tile),
        out_shape=jax.ShapeDtypeStruct((batch, seq, D_MODEL), jnp.float32),
        scratch_shapes=[
            pltpu.VMEM((3, N_HEADS, BLOCK, BAND), jnp.float32),
            pltpu.VMEM((ts, Q_COLS), jnp.bfloat16),
        ],
        compiler_params=pltpu.CompilerParams(
            dimension_semantics=(_ARB, _ARB), vmem_limit_bytes=VMEM_LIMIT_BYTES),
        name="token_mix",
    )(rel_bias, sink, buckets, qkv, qkv, dw, gt, xn, conv_b, cln_g, cln_b, w_ao, w_co, w_out)


def _ffn_kernel(r_ref, g1_ref, b1_ref, wgu_ref, wd_ref, g_ref, b_ref, o_ref):
    dot = functools.partial(jnp.dot, preferred_element_type=jnp.float32)
    inner = r_ref.shape[0] - 2 * FFN_EDGE_SUBTILE
    assert inner >= 0 and inner % FFN_SUBTILE == 0
    sizes = ([FFN_EDGE_SUBTILE] + [FFN_SUBTILE] * (inner // FFN_SUBTILE - 1)
             + [FFN_EDGE_SUBTILE] * (1 + FFN_SUBTILE // FFN_EDGE_SUBTILE))
    starts = [sum(sizes[:s]) for s in range(len(sizes) + 1)]
    n_sub = len(sizes)
    state = {}

    def up_stage(s):
        rows = slice(starts[s], starts[s + 1])
        h = _layer_norm(r_ref[rows, :], g1_ref[...], b1_ref[...])
        gu = dot(h.astype(jnp.bfloat16), wgu_ref[...])
        hid = []
        for c0 in range(0, gu.shape[1], 2 * MXU_COLS):
            gate = gu[:, c0:c0 + MXU_COLS]
            up = gu[:, c0 + MXU_COLS:c0 + 2 * MXU_COLS]
            hid.append((gate * jax.nn.sigmoid(gate) * up).astype(jnp.bfloat16))
        state[s] = (h, jnp.concatenate(hid, axis=1))

    def down_stage(s):
        rows = slice(starts[s], starts[s + 1])
        h, hid = state.pop(s)
        o_ref[rows, :] = _layer_norm(ALPHA * h + dot(hid, wd_ref[...]), g_ref[...], b_ref[...])

    for s in range(n_sub):
        up_stage(s)
    for s in range(n_sub):
        down_stage(s)


def _ffn(r2d, ln1_g, ln1_b, w_gu, w_down, ln_g, ln_b):
    n_tok = r2d.shape[0]
    assert n_tok % TM_FFN == 0
    row = lambda i: (i, 0)
    return pl.pallas_call(
        _ffn_kernel,
        grid=(n_tok // TM_FFN,),
        in_specs=[
            pl.BlockSpec((TM_FFN, D_MODEL), row),
            _const_spec((1, D_MODEL)),
            _const_spec((1, D_MODEL)),
            _const_spec(w_gu.shape),
            _const_spec(w_down.shape),
            _const_spec((1, D_MODEL)),
            _const_spec((1, D_MODEL)),
        ],
        out_specs=pl.BlockSpec((TM_FFN, D_MODEL), row),
        out_shape=jax.ShapeDtypeStruct((n_tok, D_MODEL), jnp.float32),
        compiler_params=pltpu.CompilerParams(
            dimension_semantics=(_ARB,), vmem_limit_bytes=VMEM_LIMIT_BYTES),
        name="swiglu_ffn",
    )(r2d, ln1_g, ln1_b, w_gu, w_down, ln_g, ln_b)


def kernel(x, ln_in_g, ln_in_b, rel_bias, w_in, b_gate, conv_w, conv_b, conv_ln_g, conv_ln_b,
           w_conv_out, w_attn_out, sink, w_out, ln1_g, ln1_b, w_gate, w_up, w_down, ln2_g, ln2_b):
    batch, seq, d_model = x.shape
    assert d_model == D_MODEL and w_in.shape[0] == DEPTH
    row = lambda v: v.reshape(1, -1)
    n_tok = batch * seq

    (xn, qkv, gt, dw), (w_ao, w_co, w_o, w_gu, w_d) = _inproj(
        x, row(ln_in_g), row(ln_in_b), w_in[0], row(b_gate[0]), conv_w[0],
        (w_attn_out[0], w_conv_out[0], w_out[0], w_gate[0], w_up[0], w_down[0]))
    r = _mix(rel_bias, sink[0], qkv, dw, gt, xn,
             row(conv_b[0]), row(conv_ln_g[0]), row(conv_ln_b[0]), w_ao, w_co, w_o)
    out = _ffn(r.reshape(n_tok, D_MODEL), row(ln1_g[0]), row(ln1_b[0]), w_gu, w_d,
               row(ln2_g[0]), row(ln2_b[0]))
    return out.reshape(batch, seq, D_MODEL)
```

```python
import functools

import numpy as np
import jax
import jax.numpy as jnp
from jax import lax
from jax.experimental import pallas as pl
from jax.experimental.pallas import tpu as pltpu

D_MODEL = 1024
HEAD_DIM = 64
N_HEADS = 8
N_KV_HEADS = 2
WINDOW = 128
BLOCK = 128
NUM_BUCKETS = 32
MAX_DISTANCE = 128
CONV_CH = 512
CONV_WIDTH = 31
CONV_HALF = (CONV_WIDTH - 1) // 2
Q_COLS = N_HEADS * HEAD_DIM
KV_COLS = N_KV_HEADS * HEAD_DIM
CONV_COLS = 2 * CONV_CH
GATE_COLS = 2 * D_MODEL
QKV_COLS = Q_COLS + 2 * KV_COLS
DEPTH = 1
ALPHA = (2.0 * DEPTH) ** 0.25
LN_EPS = 1e-5
LOG2E = 1.4426950408889634
Q_SCALE = HEAD_DIM ** -0.5 * LOG2E
NEG_INF = -1e30
MASKED_BUCKET = NUM_BUCKETS

HEAD_GROUPS = (((0, 2, 5, 7), False), ((1, 3, 4, 6), True))
BAND = 3 * BLOCK
HALO = 16
CONV_ROWS = 64
CONV_GROUP = 4
SUBLANES = 8
BF16_SUBLANES = 16
LANES = 128
MXU_COLS = 256
CONV_SLABS = CONV_CH // LANES

VMEM_LIMIT_BYTES = 56 * 1024 * 1024
INPROJ_VMEM_LIMIT_BYTES = 60 * 1024 * 1024
WCAST_ROWS = 128

TS_IN = 512
TS_MIX = 512
TM_FFN = 1024
FFN_SUBTILE = 384
FFN_EDGE_SUBTILE = 128

_ARB = "arbitrary"


def _layer_norm(x, g, b):
    mu = jnp.mean(x, axis=-1, keepdims=True)
    xc = x - mu
    var = jnp.mean(xc * xc, axis=-1, keepdims=True)
    return xc * lax.rsqrt(var + LN_EPS) * g + b


def _const_spec(shape):
    return pl.BlockSpec(shape, lambda *_: (0,) * len(shape), pipeline_mode=pl.Buffered(1))


def _smem_spec():
    return pl.BlockSpec(memory_space=pltpu.SMEM)


def _halo_specs(ts, seq, cols):
    per_tile = ts // HALO
    last = seq // HALO - 1
    prev = pl.BlockSpec((None, HALO, cols), lambda b, i: (b, jnp.maximum(i * per_tile - 1, 0), 0))
    nxt = pl.BlockSpec((None, HALO, cols), lambda b, i: (b, jnp.minimum((i + 1) * per_tile, last), 0))
    return prev, nxt


def _inproj_kernel(x_ref, xp_ref, xnx_ref, g_ref, b_ref, w32_ref, bgate_ref,
                   convw_ref,
                   wao_ref, wco_ref, wout_ref, wgate_ref, wup_ref, wdown_ref,
                   xn_ref, qkv_ref, gt_ref, dw_ref,
                   wao_o, wco_o, wout_o, wgu_o, wdown_o,
                   w_ref, xb_s, wb_s, ge_s, go_s):
    b_id = pl.program_id(0)
    i_id = pl.program_id(1)
    n_tiles = pl.num_programs(1)
    ts = x_ref.shape[0]
    dot = functools.partial(jnp.dot, preferred_element_type=jnp.float32)
    o1 = QKV_COLS
    o2 = o1 + CONV_COLS
    o3 = o2 + D_MODEL

    @pl.when((b_id == 0) & (i_id == 0))
    def _init():
        for k in range(CONV_WIDTH):
            wb_s[k] = jnp.broadcast_to(convw_ref[k:k + 1, :].astype(wb_s.dtype), (BF16_SUBLANES, CONV_CH))
        for r0 in range(0, w32_ref.shape[0], WCAST_ROWS):
            w_ref[r0:r0 + WCAST_ROWS, :] = w32_ref[r0:r0 + WCAST_ROWS, :].astype(w_ref.dtype)

    @pl.when(i_id == 0)
    def _cast_weights():
        for src, dst in ((wao_ref, wao_o), (wco_ref, wco_o), (wout_ref, wout_o), (wdown_ref, wdown_o)):
            dst[...] = src[...].astype(dst.dtype)
        for j in range(wgate_ref.shape[1] // MXU_COLS):
            src_cols = slice(j * MXU_COLS, (j + 1) * MXU_COLS)
            wgu_o[:, 2 * j * MXU_COLS:(2 * j + 1) * MXU_COLS] = wgate_ref[:, src_cols].astype(wgu_o.dtype)
            wgu_o[:, (2 * j + 1) * MXU_COLS:(2 * j + 2) * MXU_COLS] = wup_ref[:, src_cols].astype(wgu_o.dtype)

    g = g_ref[...]
    b = b_ref[...]
    xn = _layer_norm(x_ref[...], g, b)
    xn_ref[...] = xn
    xb_s[0:HALO, :] = _layer_norm(xp_ref[...], g, b).astype(xb_s.dtype)
    xb_s[HALO:HALO + ts, :] = xn.astype(xb_s.dtype)
    xb_s[HALO + ts:2 * HALO + ts, :] = _layer_norm(xnx_ref[...], g, b).astype(xb_s.dtype)

    u = dot(xb_s[...], w_ref[:, o1:o2])
    glu = u[:, :CONV_CH] * jax.nn.sigmoid(u[:, CONV_CH:])
    row = lax.broadcasted_iota(jnp.int32, (ts + 2 * HALO, 1), 0)
    pad = ((row < HALO) & (i_id == 0)) | ((row >= HALO + ts) & (i_id == n_tiles - 1))
    glu = jnp.where(pad, 0.0, glu)
    rows_ext = ts + 2 * HALO
    glu_up = pltpu.roll(glu, rows_ext - 1, axis=0)
    for c in range(CONV_SLABS):
        lanes = slice(c * LANES, (c + 1) * LANES)
        ge_s[c] = pltpu.bitcast(glu[:, lanes].astype(jnp.bfloat16), jnp.uint32)
        go_s[c] = pltpu.bitcast(glu_up[:, lanes].astype(jnp.bfloat16), jnp.uint32)

    xb = xb_s[HALO:HALO + ts, :]
    bgate = bgate_ref[...]
    gt_ref[:, :D_MODEL] = jax.nn.sigmoid(
        dot(xb, w_ref[:, o2:o3]) + bgate[:, :D_MODEL]).astype(gt_ref.dtype)
    gt_ref[:, D_MODEL:] = jax.nn.sigmoid(
        dot(xb, w_ref[:, o3:]) + bgate[:, D_MODEL:]).astype(gt_ref.dtype)
    qkv_ref[:, :Q_COLS] = (dot(xb, w_ref[:, :Q_COLS]) * Q_SCALE).astype(qkv_ref.dtype)
    qkv_ref[:, Q_COLS:] = dot(xb, w_ref[:, Q_COLS:o1]).astype(qkv_ref.dtype)

    first_tap = HALO - CONV_HALF

    def conv_chunk(chunk, carry):
        r0 = pl.multiple_of(chunk * CONV_ROWS, CONV_ROWS)
        w0 = pl.multiple_of(chunk * (CONV_ROWS // 2), CONV_ROWS // 2)
        n_groups = CONV_ROWS // BF16_SUBLANES
        for c in range(CONV_SLABS):
            lanes = slice(c * LANES, (c + 1) * LANES)
            accs = [jnp.zeros((BF16_SUBLANES, LANES), jnp.float32) for _ in range(n_groups)]
            for k0 in range(0, CONV_WIDTH, CONV_GROUP):
                parts = [None] * n_groups
                for k in range(k0, min(k0 + CONV_GROUP, CONV_WIDTH)):
                    w = wb_s[k, :, lanes]
                    for g in range(n_groups):
                        off = g * BF16_SUBLANES + first_tap + k
                        src = go_s if off % 2 else ge_s
                        words = src[c, pl.ds(w0 + off // 2, SUBLANES), :]
                        prod = pltpu.bitcast(words, jnp.bfloat16) * w
                        parts[g] = prod if parts[g] is None else parts[g] + prod
                for g in range(n_groups):
                    accs[g] = accs[g] + parts[g].astype(jnp.float32)
            for g in range(n_groups):
                dw_ref[pl.ds(r0 + g * BF16_SUBLANES, BF16_SUBLANES), lanes] = accs[g]
        return carry

    lax.fori_loop(0, ts // CONV_ROWS, conv_chunk, 0)


def _inproj(x, ln_g, ln_b, w_in, b_gate, conv_w, later_weights):
    batch, seq, _ = x.shape
    ts = TS_IN
    assert seq % ts == 0 and ts % CONV_ROWS == 0 and ts % HALO == 0 and HALO >= CONV_HALF
    tile = lambda b, i: (b, i, 0)
    prev_spec, next_spec = _halo_specs(ts, seq, D_MODEL)
    bf16 = jnp.bfloat16
    w_ao, w_co, w_out, w_gate, w_up, w_down = later_weights
    assert w_gate.shape == w_up.shape and w_gate.shape[1] % MXU_COLS == 0
    cast_out_shapes = [w_ao.shape, w_co.shape, w_out.shape,
                       (w_gate.shape[0], 2 * w_gate.shape[1]), w_down.shape]

    def slice_spec(shape):
        rows, cols = shape
        assert rows % (batch * BF16_SUBLANES) == 0
        return pl.BlockSpec((rows // batch, cols), lambda b, i: (b, 0))

    cast_in_specs = [slice_spec(w.shape) for w in later_weights]
    cast_out_specs = [slice_spec(s) for s in cast_out_shapes]
    outs = pl.pallas_call(
        _inproj_kernel,
        grid=(batch, seq // ts),
        in_specs=[
            pl.BlockSpec((None, ts, D_MODEL), tile),
            prev_spec,
            next_spec,
            _const_spec((1, D_MODEL)),
            _const_spec((1, D_MODEL)),
            _const_spec(w_in.shape),
            _const_spec((1, GATE_COLS)),
            _const_spec((CONV_WIDTH, CONV_CH)),
        ] + cast_in_specs,
        out_specs=[
            pl.BlockSpec((None, ts, D_MODEL), tile),
            pl.BlockSpec((None, ts, QKV_COLS), tile),
            pl.BlockSpec((None, ts, GATE_COLS), tile),
            pl.BlockSpec((None, ts, CONV_CH), tile),
        ] + cast_out_specs,
        out_shape=[
            jax.ShapeDtypeStruct((batch, seq, D_MODEL), jnp.float32),
            jax.ShapeDtypeStruct((batch, seq, QKV_COLS), bf16),
            jax.ShapeDtypeStruct((batch, seq, GATE_COLS), bf16),
            jax.ShapeDtypeStruct((batch, seq, CONV_CH), jnp.float32),
        ] + [jax.ShapeDtypeStruct(s, bf16) for s in cast_out_shapes],
        scratch_shapes=[
            pltpu.VMEM(w_in.shape, bf16),
            pltpu.VMEM((ts + 2 * HALO, D_MODEL), bf16),
            pltpu.VMEM((CONV_WIDTH, BF16_SUBLANES, CONV_CH), bf16),
            pltpu.VMEM((CONV_SLABS, ts // 2 + HALO, LANES), jnp.uint32),
            pltpu.VMEM((CONV_SLABS, ts // 2 + HALO, LANES), jnp.uint32),
        ],
        compiler_params=pltpu.CompilerParams(
            dimension_semantics=(_ARB, _ARB), vmem_limit_bytes=INPROJ_VMEM_LIMIT_BYTES),
        name="ln_inproj_conv",
    )(x, x, x, ln_g, ln_b, w_in, b_gate, conv_w, *later_weights)
    return outs[:4], outs[4:]


def _band_buckets():
    nb = NUM_BUCKETS // 2
    qi = np.arange(BLOCK)[:, None]
    kj = np.arange(BAND)[None, :]
    rel = kj - BLOCK - qi
    ret = (rel > 0).astype(np.int32) * nb
    n = np.abs(rel)
    max_exact = nb // 2
    large = max_exact + (np.log(np.maximum(n, 1) / max_exact)
                         / np.log(MAX_DISTANCE / max_exact) * (nb - max_exact)).astype(np.int32)
    large = np.minimum(large, nb - 1)
    buckets = (ret + np.where(n < max_exact, n, large)).astype(np.int32)
    return np.where(n <= WINDOW, buckets, MASKED_BUCKET).astype(np.int32)


def _mix_kernel(n_seq_blocks,
                relb_ref, sink_ref, bk_ref,
                q_ref, kv_ref, dw_ref, gt_ref, xn_ref,
                convb_ref, clng_ref, clnb_ref,
                wao_ref, wco_ref, wout_ref,
                r_ref,
                bias_s, o_s):
    b_id = pl.program_id(0)
    i_id = pl.program_id(1)
    ts = q_ref.shape[0]
    blocks_per_tile = ts // BLOCK
    dot = functools.partial(jnp.dot, preferred_element_type=jnp.float32)

    @pl.when((b_id == 0) & (i_id == 0))
    def _init():
        bk = bk_ref[...]
        col = lax.broadcasted_iota(jnp.int32, (BLOCK, BAND), 1)
        for h in range(N_HEADS):
            def body(b, acc):
                return jnp.where(bk == b, relb_ref[b, h], acc)
            mid = LOG2E * lax.fori_loop(0, NUM_BUCKETS, body,
                                        jnp.full((BLOCK, BAND), NEG_INF, jnp.float32))
            bias_s[0, h] = jnp.where(col < BLOCK, NEG_INF, mid)
            bias_s[1, h] = mid
            bias_s[2, h] = jnp.where(col >= 2 * BLOCK, NEG_INF, mid)

    lane = lax.broadcasted_iota(jnp.int32, (BLOCK, LANES), 1)
    low_half = lane < HEAD_DIM
    nt = (((1,), (1,)), ((), ()))

    def block_operands(j):
        n = i_id * blocks_per_tile + j
        variant = jnp.where(n == 0, 0, jnp.where(n == n_seq_blocks - 1, 2, 1))
        prev_r = pl.multiple_of(jnp.maximum(n - 1, 0) * BLOCK, BLOCK)
        own_r = pl.multiple_of(n * BLOCK, BLOCK)
        next_r = pl.multiple_of(jnp.minimum(n + 1, n_seq_blocks - 1) * BLOCK, BLOCK)
        kvb = jnp.concatenate([kv_ref[pl.ds(prev_r, BLOCK), :],
                               kv_ref[pl.ds(own_r, BLOCK), :],
                               kv_ref[pl.ds(next_r, BLOCK), :]], axis=0)
        kk = kvb[:, :LANES]
        vv = kvb[:, LANES:]
        q = q_ref[j * BLOCK:(j + 1) * BLOCK, :]
        zero = jnp.zeros((BLOCK, LANES), q.dtype)
        q_head = {}
        for p in range(N_HEADS // 2):
            q2 = q[:, p * LANES:(p + 1) * LANES]
            q_head[2 * p] = jnp.where(low_half, q2, zero)
            q_head[2 * p + 1] = jnp.where(low_half, zero, q2)
        return dict(variant=variant, q_head=q_head,
                    k=(kk, pltpu.roll(kk, HEAD_DIM, axis=1)),
                    v=(vv, pltpu.roll(vv, HEAD_DIM, axis=1)))

    def scores(blk, heads, swapped):
        lhs = jnp.concatenate([blk["q_head"][h] for h in heads], axis=0)
        return lax.dot_general(lhs, blk["k"][swapped], nt, preferred_element_type=jnp.float32)

    def softmax_values(blk, heads, swapped, s_all, o_head):
        ps, inv_ls = [], []
        for r, h in enumerate(heads):
            s = s_all[r * BLOCK:(r + 1) * BLOCK, :] + bias_s[blk["variant"], h]
            sink = jnp.full((BLOCK, 1), sink_ref[h], jnp.float32) * LOG2E
            m = jnp.maximum(jnp.max(s, axis=-1, keepdims=True), sink)
            p = jnp.exp2(s - m)
            l = jnp.sum(p, axis=-1, keepdims=True) + jnp.exp2(sink - m)
            ps.append(p.astype(jnp.bfloat16))
            inv_ls.append(1.0 / l)
        r_all = dot(jnp.concatenate(ps, axis=0), blk["v"][swapped])
        for r, h in enumerate(heads):
            o_head[h] = r_all[r * BLOCK:(r + 1) * BLOCK, :] * inv_ls[r]

    stages = [(j, heads, int(swapped)) for j in range(blocks_per_tile) for heads, swapped in HEAD_GROUPS]
    yc_chunk = D_MODEL * 2 // len(stages)
    assert yc_chunk % MXU_COLS == 0 and len(stages) % 2 == 0
    yc_parts = []
    y = _layer_norm(dw_ref[...] + convb_ref[...], clng_ref[...], clnb_ref[...])
    yc = (y * jax.nn.sigmoid(y)).astype(jnp.bfloat16)
    blocks = {}
    o_heads = {}
    pending = None
    for c in range(len(stages) + 1):
        nxt = None
        if c < len(stages):
            j, heads, swapped = stages[c]
            if j not in blocks:
                blocks[j] = block_operands(j)
                o_heads[j] = {}
            nxt = (j, heads, swapped, scores(blocks[j], heads, swapped))
        if c % 2 == 1 and len(yc_parts) * yc_chunk < D_MODEL:
            c0 = len(yc_parts) * yc_chunk
            yc_parts.append(dot(yc, wco_ref[:, c0:c0 + yc_chunk]))
        if pending is not None:
            j, heads, swapped, s_all = pending
            softmax_values(blocks[j], heads, swapped, s_all, o_heads[j])
            if len(o_heads[j]) == N_HEADS:
                o_pairs = [jnp.where(low_half, o_heads[j][2 * p], o_heads[j][2 * p + 1])
                           for p in range(N_HEADS // 2)]
                o_s[j * BLOCK:(j + 1) * BLOCK, :] = jnp.concatenate(o_pairs, axis=1).astype(o_s.dtype)
        pending = nxt

    y_a = dot(o_s[...], wao_ref[...])
    y_c = jnp.concatenate(yc_parts, axis=1)
    merged = (gt_ref[:, :D_MODEL].astype(jnp.float32) * y_a
              + gt_ref[:, D_MODEL:].astype(jnp.float32) * y_c)
    mix = dot(merged.astype(jnp.bfloat16), wout_ref[...])
    r_ref[...] = ALPHA * xn_ref[...] + mix


def _mix(rel_bias, sink, qkv, dw, gt, xn, conv_b, cln_g, cln_b, w_ao, w_co, w_out):
    batch, seq, _ = xn.shape
    ts = TS_MIX
    assert seq % ts == 0 and ts % BLOCK == 0 and seq // BLOCK >= 2
    n_seq_blocks = seq // BLOCK
    buckets = jnp.asarray(_band_buckets())
    tile = lambda b, i: (b, i, 0)
    return pl.pallas_call(
        functools.partial(_mix_kernel, n_seq_blocks),
        grid=(batch, seq // ts),
        in_specs=[
            _smem_spec(),
            _smem_spec(),
            _const_spec((BLOCK, BAND)),
            pl.BlockSpec((None, ts, Q_COLS), tile),
            pl.BlockSpec((None, seq, 2 * KV_COLS), lambda b, i: (b, 0, Q_COLS // (2 * KV_COLS))),
            pl.BlockSpec((None, ts, CONV_CH), tile),
            pl.BlockSpec((None, ts, GATE_COLS), tile),
            pl.BlockSpec((None, ts, D_MODEL), tile),
            _const_spec((1, CONV_CH)),
            _const_spec((1, CONV_CH)),
            _const_spec((1, CONV_CH)),
            _const_spec((Q_COLS, D_MODEL)),
            _const_spec((CONV_CH, D_MODEL)),
            _const_spec((D_MODEL, D_MODEL)),
        ],
        out_specs=pl.BlockSpec((None, ts, D_MODEL), tile),
        out_shape=jax.ShapeDtypeStruct((batch, seq, D_MODEL), jnp.float32),
        scratch_shapes=[
            pltpu.VMEM((3, N_HEADS, BLOCK, BAND), jnp.float32),
            pltpu.VMEM((ts, Q_COLS), jnp.bfloat16),
        ],
        compiler_params=pltpu.CompilerParams(
            dimension_semantics=(_ARB, _ARB), vmem_limit_bytes=VMEM_LIMIT_BYTES),
        name="token_mix",
    )(rel_bias, sink, buckets, qkv, qkv, dw, gt, xn, conv_b, cln_g, cln_b, w_ao, w_co, w_out)


def _ffn_kernel(r_ref, g1_ref, b1_ref, wgu_ref, wd_ref, g_ref, b_ref, o_ref):
    dot = functools.partial(jnp.dot, preferred_element_type=jnp.float32)
    inner = r_ref.shape[0] - 2 * FFN_EDGE_SUBTILE
    assert inner >= 0 and inner % FFN_SUBTILE == 0
    sizes = [FFN_EDGE_SUBTILE] + [FFN_SUBTILE] * (inner // FFN_SUBTILE) + [FFN_EDGE_SUBTILE]
    starts = [sum(sizes[:s]) for s in range(len(sizes) + 1)]
    n_sub = len(sizes)
    state = {}

    def up_stage(s):
        rows = slice(starts[s], starts[s + 1])
        h = _layer_norm(r_ref[rows, :], g1_ref[...], b1_ref[...])
        gu = dot(h.astype(jnp.bfloat16), wgu_ref[...])
        hid = []
        for c0 in range(0, gu.shape[1], 2 * MXU_COLS):
            gate = gu[:, c0:c0 + MXU_COLS]
            up = gu[:, c0 + MXU_COLS:c0 + 2 * MXU_COLS]
            hid.append((gate * jax.nn.sigmoid(gate) * up).astype(jnp.bfloat16))
        state[s] = (h, jnp.concatenate(hid, axis=1))

    def down_stage(s):
        rows = slice(starts[s], starts[s + 1])
        h, hid = state.pop(s)
        o_ref[rows, :] = _layer_norm(ALPHA * h + dot(hid, wd_ref[...]), g_ref[...], b_ref[...])

    for s in range(n_sub):
        up_stage(s)
    for s in range(n_sub):
        down_stage(s)


def _ffn(r2d, ln1_g, ln1_b, w_gu, w_down, ln_g, ln_b):
    n_tok = r2d.shape[0]
    assert n_tok % TM_FFN == 0
    row = lambda i: (i, 0)
    return pl.pallas_call(
        _ffn_kernel,
        grid=(n_tok // TM_FFN,),
        in_specs=[
            pl.BlockSpec((TM_FFN, D_MODEL), row),
            _const_spec((1, D_MODEL)),
            _const_spec((1, D_MODEL)),
            _const_spec(w_gu.shape),
            _const_spec(w_down.shape),
            _const_spec((1, D_MODEL)),
            _const_spec((1, D_MODEL)),
        ],
        out_specs=pl.BlockSpec((TM_FFN, D_MODEL), row),
        out_shape=jax.ShapeDtypeStruct((n_tok, D_MODEL), jnp.float32),
        compiler_params=pltpu.CompilerParams(
            dimension_semantics=(_ARB,), vmem_limit_bytes=VMEM_LIMIT_BYTES),
        name="swiglu_ffn",
    )(r2d, ln1_g, ln1_b, w_gu, w_down, ln_g, ln_b)


def kernel(x, ln_in_g, ln_in_b, rel_bias, w_in, b_gate, conv_w, conv_b, conv_ln_g, conv_ln_b,
           w_conv_out, w_attn_out, sink, w_out, ln1_g, ln1_b, w_gate, w_up, w_down, ln2_g, ln2_b):
    batch, seq, d_model = x.shape
    assert d_model == D_MODEL and w_in.shape[0] == DEPTH
    row = lambda v: v.reshape(1, -1)
    n_tok = batch * seq

    (xn, qkv, gt, dw), (w_ao, w_co, w_o, w_gu, w_d) = _inproj(
        x, row(ln_in_g), row(ln_in_b), w_in[0], row(b_gate[0]), conv_w[0],
        (w_attn_out[0], w_conv_out[0], w_out[0], w_gate[0], w_up[0], w_down[0]))
    r = _mix(rel_bias, sink[0], qkv, dw, gt, xn,
             row(conv_b[0]), row(conv_ln_g[0]), row(conv_ln_b[0]), w_ao, w_co, w_o)
    out = _ffn(r.reshape(n_tok, D_MODEL), row(ln1_g[0]), row(ln1_b[0]), w_gu, w_d,
               row(ln2_g[0]), row(ln2_b[0]))
    return out.reshape(batch, seq, D_MODEL)
```

```python
import functools

import numpy as np
import jax
import jax.numpy as jnp
from jax import lax
from jax.experimental import pallas as pl
from jax.experimental.pallas import tpu as pltpu

D_MODEL = 1024
HEAD_DIM = 64
N_HEADS = 8
N_KV_HEADS = 2
WINDOW = 128
BLOCK = 128
NUM_BUCKETS = 32
MAX_DISTANCE = 128
CONV_CH = 512
CONV_WIDTH = 31
CONV_HALF = (CONV_WIDTH - 1) // 2
Q_COLS = N_HEADS * HEAD_DIM
KV_COLS = N_KV_HEADS * HEAD_DIM
CONV_COLS = 2 * CONV_CH
GATE_COLS = 2 * D_MODEL
QKV_COLS = Q_COLS + 2 * KV_COLS
DEPTH = 1
ALPHA = (2.0 * DEPTH) ** 0.25
LN_EPS = 1e-5
LOG2E = 1.4426950408889634
Q_SCALE = HEAD_DIM ** -0.5 * LOG2E
NEG_INF = -1e30
MASKED_BUCKET = NUM_BUCKETS

HEAD_GROUPS = (((0, 2, 5, 7), False), ((1, 3, 4, 6), True))
BAND = 3 * BLOCK
HALO = 16
CONV_ROWS = 64
CONV_GROUP = 4
SUBLANES = 8
BF16_SUBLANES = 16
LANES = 128
MXU_COLS = 256
CONV_SLABS = CONV_CH // LANES

VMEM_LIMIT_BYTES = 56 * 1024 * 1024
INPROJ_VMEM_LIMIT_BYTES = 60 * 1024 * 1024
WCAST_ROWS = 128

TS_IN = 512
TS_MIX = 512
MIX_RING_SLOTS = 3
TM_FFN = 1024
FFN_SUBTILE = 256
FFN_EDGE_SUBTILE = 128

_ARB = "arbitrary"


def _layer_norm(x, g, b):
    mu = jnp.mean(x, axis=-1, keepdims=True)
    xc = x - mu
    var = jnp.mean(xc * xc, axis=-1, keepdims=True)
    return xc * lax.rsqrt(var + LN_EPS) * g + b


def _const_spec(shape):
    return pl.BlockSpec(shape, lambda *_: (0,) * len(shape), pipeline_mode=pl.Buffered(1))


def _smem_spec():
    return pl.BlockSpec(memory_space=pltpu.SMEM)


def _halo_specs(ts, seq, cols):
    per_tile = ts // HALO
    last = seq // HALO - 1
    prev = pl.BlockSpec((None, HALO, cols), lambda b, i: (b, jnp.maximum(i * per_tile - 1, 0), 0))
    nxt = pl.BlockSpec((None, HALO, cols), lambda b, i: (b, jnp.minimum((i + 1) * per_tile, last), 0))
    return prev, nxt


def _inproj_kernel(x_ref, xp_ref, xnx_ref, g_ref, b_ref, w32_ref, bgate_ref,
                   convw_ref,
                   wao_ref, wco_ref, wout_ref, wgate_ref, wup_ref, wdown_ref,
                   xn_ref, qkv_ref, gt_ref, dw_ref,
                   wao_o, wco_o, wout_o, wgu_o, wdown_o,
                   w_ref, xb_s, wb_s, ge_s, go_s):
    b_id = pl.program_id(0)
    i_id = pl.program_id(1)
    n_tiles = pl.num_programs(1)
    ts = x_ref.shape[0]
    dot = functools.partial(jnp.dot, preferred_element_type=jnp.float32)
    o1 = QKV_COLS
    o2 = o1 + CONV_COLS
    o3 = o2 + D_MODEL

    @pl.when((b_id == 0) & (i_id == 0))
    def _init():
        for k in range(CONV_WIDTH):
            wb_s[k] = jnp.broadcast_to(convw_ref[k:k + 1, :].astype(wb_s.dtype), (BF16_SUBLANES, CONV_CH))
        for r0 in range(0, w32_ref.shape[0], WCAST_ROWS):
            w_ref[r0:r0 + WCAST_ROWS, :] = w32_ref[r0:r0 + WCAST_ROWS, :].astype(w_ref.dtype)

    @pl.when(i_id == 0)
    def _cast_weights():
        for src, dst in ((wao_ref, wao_o), (wco_ref, wco_o), (wout_ref, wout_o), (wdown_ref, wdown_o)):
            dst[...] = src[...].astype(dst.dtype)
        for j in range(wgate_ref.shape[1] // MXU_COLS):
            src_cols = slice(j * MXU_COLS, (j + 1) * MXU_COLS)
            wgu_o[:, 2 * j * MXU_COLS:(2 * j + 1) * MXU_COLS] = wgate_ref[:, src_cols].astype(wgu_o.dtype)
            wgu_o[:, (2 * j + 1) * MXU_COLS:(2 * j + 2) * MXU_COLS] = wup_ref[:, src_cols].astype(wgu_o.dtype)

    g = g_ref[...]
    b = b_ref[...]
    xn = _layer_norm(x_ref[...], g, b)
    xn_ref[...] = xn
    xb_s[0:HALO, :] = _layer_norm(xp_ref[...], g, b).astype(xb_s.dtype)
    xb_s[HALO:HALO + ts, :] = xn.astype(xb_s.dtype)
    xb_s[HALO + ts:2 * HALO + ts, :] = _layer_norm(xnx_ref[...], g, b).astype(xb_s.dtype)

    u = dot(xb_s[...], w_ref[:, o1:o2])
    glu = u[:, :CONV_CH] * jax.nn.sigmoid(u[:, CONV_CH:])
    row = lax.broadcasted_iota(jnp.int32, (ts + 2 * HALO, 1), 0)
    pad = ((row < HALO) & (i_id == 0)) | ((row >= HALO + ts) & (i_id == n_tiles - 1))
    glu = jnp.where(pad, 0.0, glu)
    rows_ext = ts + 2 * HALO
    glu_up = pltpu.roll(glu, rows_ext - 1, axis=0)
    for c in range(CONV_SLABS):
        lanes = slice(c * LANES, (c + 1) * LANES)
        ge_s[c] = pltpu.bitcast(glu[:, lanes].astype(jnp.bfloat16), jnp.uint32)
        go_s[c] = pltpu.bitcast(glu_up[:, lanes].astype(jnp.bfloat16), jnp.uint32)

    xb = xb_s[HALO:HALO + ts, :]
    bgate = bgate_ref[...]
    gt_ref[:, :D_MODEL] = jax.nn.sigmoid(
        dot(xb, w_ref[:, o2:o3]) + bgate[:, :D_MODEL]).astype(gt_ref.dtype)
    gt_ref[:, D_MODEL:] = jax.nn.sigmoid(
        dot(xb, w_ref[:, o3:]) + bgate[:, D_MODEL:]).astype(gt_ref.dtype)
    qkv_ref[:, :Q_COLS] = (dot(xb, w_ref[:, :Q_COLS]) * Q_SCALE).astype(qkv_ref.dtype)
    qkv_ref[:, Q_COLS:] = dot(xb, w_ref[:, Q_COLS:o1]).astype(qkv_ref.dtype)

    first_tap = HALO - CONV_HALF

    def conv_chunk(chunk, carry):
        r0 = pl.multiple_of(chunk * CONV_ROWS, CONV_ROWS)
        w0 = pl.multiple_of(chunk * (CONV_ROWS // 2), CONV_ROWS // 2)
        n_groups = CONV_ROWS // BF16_SUBLANES
        for c in range(CONV_SLABS):
            lanes = slice(c * LANES, (c + 1) * LANES)
            accs = [jnp.zeros((BF16_SUBLANES, LANES), jnp.float32) for _ in range(n_groups)]
            for k0 in range(0, CONV_WIDTH, CONV_GROUP):
                parts = [None] * n_groups
                for k in range(k0, min(k0 + CONV_GROUP, CONV_WIDTH)):
                    w = wb_s[k, :, lanes]
                    for g in range(n_groups):
                        off = g * BF16_SUBLANES + first_tap + k
                        src = go_s if off % 2 else ge_s
                        words = src[c, pl.ds(w0 + off // 2, SUBLANES), :]
                        prod = pltpu.bitcast(words, jnp.bfloat16) * w
                        parts[g] = prod if parts[g] is None else parts[g] + prod
                for g in range(n_groups):
                    accs[g] = accs[g] + parts[g].astype(jnp.float32)
            for g in range(n_groups):
                dw_ref[pl.ds(r0 + g * BF16_SUBLANES, BF16_SUBLANES), lanes] = accs[g]
        return carry

    lax.fori_loop(0, ts // CONV_ROWS, conv_chunk, 0)


def _inproj(x, ln_g, ln_b, w_in, b_gate, conv_w, later_weights):
    batch, seq, _ = x.shape
    ts = TS_IN
    assert seq % ts == 0 and ts % CONV_ROWS == 0 and ts % HALO == 0 and HALO >= CONV_HALF
    tile = lambda b, i: (b, i, 0)
    prev_spec, next_spec = _halo_specs(ts, seq, D_MODEL)
    bf16 = jnp.bfloat16
    w_ao, w_co, w_out, w_gate, w_up, w_down = later_weights
    assert w_gate.shape == w_up.shape and w_gate.shape[1] % MXU_COLS == 0
    cast_out_shapes = [w_ao.shape, w_co.shape, w_out.shape,
                       (w_gate.shape[0], 2 * w_gate.shape[1]), w_down.shape]

    def slice_spec(shape):
        rows, cols = shape
        assert rows % (batch * BF16_SUBLANES) == 0
        return pl.BlockSpec((rows // batch, cols), lambda b, i: (b, 0))

    cast_in_specs = [slice_spec(w.shape) for w in later_weights]
    cast_out_specs = [slice_spec(s) for s in cast_out_shapes]
    outs = pl.pallas_call(
        _inproj_kernel,
        grid=(batch, seq // ts),
        in_specs=[
            pl.BlockSpec((None, ts, D_MODEL), tile),
            prev_spec,
            next_spec,
            _const_spec((1, D_MODEL)),
            _const_spec((1, D_MODEL)),
            _const_spec(w_in.shape),
            _const_spec((1, GATE_COLS)),
            _const_spec((CONV_WIDTH, CONV_CH)),
        ] + cast_in_specs,
        out_specs=[
            pl.BlockSpec((None, ts, D_MODEL), tile),
            pl.BlockSpec((None, ts, QKV_COLS), tile),
            pl.BlockSpec((None, ts, GATE_COLS), tile),
            pl.BlockSpec((None, ts, CONV_CH), tile),
        ] + cast_out_specs,
        out_shape=[
            jax.ShapeDtypeStruct((batch, seq, D_MODEL), jnp.float32),
            jax.ShapeDtypeStruct((batch, seq, QKV_COLS), bf16),
            jax.ShapeDtypeStruct((batch, seq, GATE_COLS), bf16),
            jax.ShapeDtypeStruct((batch, seq, CONV_CH), jnp.float32),
        ] + [jax.ShapeDtypeStruct(s, bf16) for s in cast_out_shapes],
        scratch_shapes=[
            pltpu.VMEM(w_in.shape, bf16),
            pltpu.VMEM((ts + 2 * HALO, D_MODEL), bf16),
            pltpu.VMEM((CONV_WIDTH, BF16_SUBLANES, CONV_CH), bf16),
            pltpu.VMEM((CONV_SLABS, ts // 2 + HALO, LANES), jnp.uint32),
            pltpu.VMEM((CONV_SLABS, ts // 2 + HALO, LANES), jnp.uint32),
        ],
        compiler_params=pltpu.CompilerParams(
            dimension_semantics=(_ARB, _ARB), vmem_limit_bytes=INPROJ_VMEM_LIMIT_BYTES),
        name="ln_inproj_conv",
    )(x, x, x, ln_g, ln_b, w_in, b_gate, conv_w, *later_weights)
    return outs[:4], outs[4:]


def _band_buckets():
    nb = NUM_BUCKETS // 2
    qi = np.arange(BLOCK)[:, None]
    kj = np.arange(BAND)[None, :]
    rel = kj - BLOCK - qi
    ret = (rel > 0).astype(np.int32) * nb
    n = np.abs(rel)
    max_exact = nb // 2
    large = max_exact + (np.log(np.maximum(n, 1) / max_exact)
                         / np.log(MAX_DISTANCE / max_exact) * (nb - max_exact)).astype(np.int32)
    large = np.minimum(large, nb - 1)
    buckets = (ret + np.where(n < max_exact, n, large)).astype(np.int32)
    return np.where(n <= WINDOW, buckets, MASKED_BUCKET).astype(np.int32)


def _mix_kernel(n_seq_blocks,
                relb_ref, sink_ref, bk_ref,
                q_ref, kv_ref, dw_ref, gt_ref, xn_ref,
                convb_ref, clng_ref, clnb_ref,
                wao_ref, wco_ref, wout_ref,
                r_ref,
                bias_s, o_s, gt_buf, xn_buf, dma_sem):
    b_id = pl.program_id(0)
    i_id = pl.program_id(1)
    ts = q_ref.shape[0]
    blocks_per_tile = ts // BLOCK
    dot = functools.partial(jnp.dot, preferred_element_type=jnp.float32)

    n_tiles = pl.num_programs(1)
    step = b_id * n_tiles + i_id
    n_steps = pl.num_programs(0) * n_tiles

    def ring_copies(t, slot):
        bb = t // n_tiles
        r0 = pl.multiple_of((t % n_tiles) * ts, ts)
        return (pltpu.make_async_copy(gt_ref.at[bb, pl.ds(r0, ts), :], gt_buf.at[slot], dma_sem.at[0, slot]),
                pltpu.make_async_copy(xn_ref.at[bb, pl.ds(r0, ts), :], xn_buf.at[slot], dma_sem.at[1, slot]))

    @pl.when(step == 0)
    def _prime():
        for t in range(MIX_RING_SLOTS - 1):
            for cp in ring_copies(t, t):
                cp.start()

    ahead = step + (MIX_RING_SLOTS - 1)

    @pl.when(ahead < n_steps)
    def _prefetch():
        for cp in ring_copies(ahead, ahead % MIX_RING_SLOTS):
            cp.start()

    slot = step % MIX_RING_SLOTS
    for cp in ring_copies(step, slot):
        cp.wait()

    @pl.when((b_id == 0) & (i_id == 0))
    def _init():
        bk = bk_ref[...]
        col = lax.broadcasted_iota(jnp.int32, (BLOCK, BAND), 1)
        for h in range(N_HEADS):
            def body(b, acc):
                return jnp.where(bk == b, relb_ref[b, h], acc)
            mid = LOG2E * lax.fori_loop(0, NUM_BUCKETS, body,
                                        jnp.full((BLOCK, BAND), NEG_INF, jnp.float32))
            bias_s[0, h] = jnp.where(col < BLOCK, NEG_INF, mid)
            bias_s[1, h] = mid
            bias_s[2, h] = jnp.where(col >= 2 * BLOCK, NEG_INF, mid)

    lane = lax.broadcasted_iota(jnp.int32, (BLOCK, LANES), 1)
    low_half = lane < HEAD_DIM
    nt = (((1,), (1,)), ((), ()))

    def block_operands(j):
        n = i_id * blocks_per_tile + j
        variant = jnp.where(n == 0, 0, jnp.where(n == n_seq_blocks - 1, 2, 1))
        prev_r = pl.multiple_of(jnp.maximum(n - 1, 0) * BLOCK, BLOCK)
        own_r = pl.multiple_of(n * BLOCK, BLOCK)
        next_r = pl.multiple_of(jnp.minimum(n + 1, n_seq_blocks - 1) * BLOCK, BLOCK)
        kvb = jnp.concatenate([kv_ref[pl.ds(prev_r, BLOCK), :],
                               kv_ref[pl.ds(own_r, BLOCK), :],
                               kv_ref[pl.ds(next_r, BLOCK), :]], axis=0)
        kk = kvb[:, :LANES]
        vv = kvb[:, LANES:]
        q = q_ref[j * BLOCK:(j + 1) * BLOCK, :]
        zero = jnp.zeros((BLOCK, LANES), q.dtype)
        q_head = {}
        for p in range(N_HEADS // 2):
            q2 = q[:, p * LANES:(p + 1) * LANES]
            q_head[2 * p] = jnp.where(low_half, q2, zero)
            q_head[2 * p + 1] = jnp.where(low_half, zero, q2)
        return dict(variant=variant, q_head=q_head,
                    k=(kk, pltpu.roll(kk, HEAD_DIM, axis=1)),
                    v=(vv, pltpu.roll(vv, HEAD_DIM, axis=1)))

    def scores(blk, heads, swapped):
        lhs = jnp.concatenate([blk["q_head"][h] for h in heads], axis=0)
        return lax.dot_general(lhs, blk["k"][swapped], nt, preferred_element_type=jnp.float32)

    def softmax_values(blk, heads, swapped, s_all, o_head):
        ps, inv_ls = [], []
        for r, h in enumerate(heads):
            s = s_all[r * BLOCK:(r + 1) * BLOCK, :] + bias_s[blk["variant"], h]
            sink = jnp.full((BLOCK, 1), sink_ref[h], jnp.float32) * LOG2E
            m = jnp.maximum(jnp.max(s, axis=-1, keepdims=True), sink)
            p = jnp.exp2(s - m)
            l = jnp.sum(p, axis=-1, keepdims=True) + jnp.exp2(sink - m)
            ps.append(p.astype(jnp.bfloat16))
            inv_ls.append(1.0 / l)
        r_all = dot(jnp.concatenate(ps, axis=0), blk["v"][swapped])
        for r, h in enumerate(heads):
            o_head[h] = r_all[r * BLOCK:(r + 1) * BLOCK, :] * inv_ls[r]

    stages = [(j, heads, int(swapped)) for j in range(blocks_per_tile) for heads, swapped in HEAD_GROUPS]
    yc_chunk = D_MODEL * 2 // len(stages)
    assert yc_chunk % MXU_COLS == 0 and len(stages) % 2 == 0
    yc_parts = []
    y = _layer_norm(dw_ref[...] + convb_ref[...], clng_ref[...], clnb_ref[...])
    yc = (y * jax.nn.sigmoid(y)).astype(jnp.bfloat16)
    blocks = {}
    o_heads = {}
    pending = None
    for c in range(len(stages) + 1):
        nxt = None
        if c < len(stages):
            j, heads, swapped = stages[c]
            if j not in blocks:
                blocks[j] = block_operands(j)
                o_heads[j] = {}
            nxt = (j, heads, swapped, scores(blocks[j], heads, swapped))
        if c % 2 == 1 and len(yc_parts) * yc_chunk < D_MODEL:
            c0 = len(yc_parts) * yc_chunk
            yc_parts.append(dot(yc, wco_ref[:, c0:c0 + yc_chunk]))
        if pending is not None:
            j, heads, swapped, s_all = pending
            softmax_values(blocks[j], heads, swapped, s_all, o_heads[j])
            if len(o_heads[j]) == N_HEADS:
                o_pairs = [jnp.where(low_half, o_heads[j][2 * p], o_heads[j][2 * p + 1])
                           for p in range(N_HEADS // 2)]
                o_s[j * BLOCK:(j + 1) * BLOCK, :] = jnp.concatenate(o_pairs, axis=1).astype(o_s.dtype)
        pending = nxt

    y_a = dot(o_s[...], wao_ref[...])
    y_c = jnp.concatenate(yc_parts, axis=1)
    merged = (gt_buf[slot, :, :D_MODEL].astype(jnp.float32) * y_a
              + gt_buf[slot, :, D_MODEL:].astype(jnp.float32) * y_c)
    mix = dot(merged.astype(jnp.bfloat16), wout_ref[...])
    r_ref[...] = ALPHA * xn_buf[slot] + mix


def _mix(rel_bias, sink, qkv, dw, gt, xn, conv_b, cln_g, cln_b, w_ao, w_co, w_out):
    batch, seq, _ = xn.shape
    ts = TS_MIX
    assert seq % ts == 0 and ts % BLOCK == 0 and seq // BLOCK >= 2
    assert batch * (seq // ts) >= MIX_RING_SLOTS - 1
    n_seq_blocks = seq // BLOCK
    buckets = jnp.asarray(_band_buckets())
    tile = lambda b, i: (b, i, 0)
    return pl.pallas_call(
        functools.partial(_mix_kernel, n_seq_blocks),
        grid=(batch, seq // ts),
        in_specs=[
            _smem_spec(),
            _smem_spec(),
            _const_spec((BLOCK, BAND)),
            pl.BlockSpec((None, ts, Q_COLS), tile),
            pl.BlockSpec((None, seq, 2 * KV_COLS), lambda b, i: (b, 0, Q_COLS // (2 * KV_COLS))),
            pl.BlockSpec((None, ts, CONV_CH), tile),
            pl.BlockSpec(memory_space=pl.ANY),
            pl.BlockSpec(memory_space=pl.ANY),
            _const_spec((1, CONV_CH)),
            _const_spec((1, CONV_CH)),
            _const_spec((1, CONV_CH)),
            _const_spec((Q_COLS, D_MODEL)),
            _const_spec((CONV_CH, D_MODEL)),
            _const_spec((D_MODEL, D_MODEL)),
        ],
        out_specs=pl.BlockSpec((None, ts, D_MODEL), tile),
        out_shape=jax.ShapeDtypeStruct((batch, seq, D_MODEL), jnp.float32),
        scratch_shapes=[
            pltpu.VMEM((3, N_HEADS, BLOCK, BAND), jnp.float32),
            pltpu.VMEM((ts, Q_COLS), jnp.bfloat16),
            pltpu.VMEM((MIX_RING_SLOTS, ts, GATE_COLS), gt.dtype),
            pltpu.VMEM((MIX_RING_SLOTS, ts, D_MODEL), xn.dtype),
            pltpu.SemaphoreType.DMA((2, MIX_RING_SLOTS)),
        ],
        compiler_params=pltpu.CompilerParams(
            dimension_semantics=(_ARB, _ARB), vmem_limit_bytes=VMEM_LIMIT_BYTES),
        name="token_mix",
    )(rel_bias, sink, buckets, qkv, qkv, dw, gt, xn, conv_b, cln_g, cln_b, w_ao, w_co, w_out)


def _ffn_kernel(r_ref, g1_ref, b1_ref, wgu_ref, wd_ref, g_ref, b_ref, o_ref):
    dot = functools.partial(jnp.dot, preferred_element_type=jnp.float32)
    inner = r_ref.shape[0] - 2 * FFN_EDGE_SUBTILE
    assert inner >= 0 and inner % FFN_SUBTILE == 0
    sizes = [FFN_EDGE_SUBTILE] + [FFN_SUBTILE] * (inner // FFN_SUBTILE) + [FFN_EDGE_SUBTILE]
    starts = [sum(sizes[:s]) for s in range(len(sizes) + 1)]
    n_sub = len(sizes)
    state = {}

    def up_stage(s):
        rows = slice(starts[s], starts[s + 1])
        h = _layer_norm(r_ref[rows, :], g1_ref[...], b1_ref[...])
        gu = dot(h.astype(jnp.bfloat16), wgu_ref[...])
        hid = []
        for c0 in range(0, gu.shape[1], 2 * MXU_COLS):
            gate = gu[:, c0:c0 + MXU_COLS]
            up = gu[:, c0 + MXU_COLS:c0 + 2 * MXU_COLS]
            hid.append((gate * jax.nn.sigmoid(gate) * up).astype(jnp.bfloat16))
        state[s] = (h, jnp.concatenate(hid, axis=1))

    def down_stage(s):
        rows = slice(starts[s], starts[s + 1])
        h, hid = state.pop(s)
        o_ref[rows, :] = _layer_norm(ALPHA * h + dot(hid, wd_ref[...]), g_ref[...], b_ref[...])

    for s in range(n_sub):
        up_stage(s)
    for s in range(n_sub):
        down_stage(s)


def _ffn(r2d, ln1_g, ln1_b, w_gu, w_down, ln_g, ln_b):
    n_tok = r2d.shape[0]
    assert n_tok % TM_FFN == 0
    row = lambda i: (i, 0)
    return pl.pallas_call(
        _ffn_kernel,
        grid=(n_tok // TM_FFN,),
        in_specs=[
            pl.BlockSpec((TM_FFN, D_MODEL), row),
            _const_spec((1, D_MODEL)),
            _const_spec((1, D_MODEL)),
            _const_spec(w_gu.shape),
            _const_spec(w_down.shape),
            _const_spec((1, D_MODEL)),
            _const_spec((1, D_MODEL)),
        ],
        out_specs=pl.BlockSpec((TM_FFN, D_MODEL), row),
        out_shape=jax.ShapeDtypeStruct((n_tok, D_MODEL), jnp.float32),
        compiler_params=pltpu.CompilerParams(
            dimension_semantics=(_ARB,), vmem_limit_bytes=VMEM_LIMIT_BYTES),
        name="swiglu_ffn",
    )(r2d, ln1_g, ln1_b, w_gu, w_down, ln_g, ln_b)


def kernel(x, ln_in_g, ln_in_b, rel_bias, w_in, b_gate, conv_w, conv_b, conv_ln_g, conv_ln_b,
           w_conv_out, w_attn_out, sink, w_out, ln1_g, ln1_b, w_gate, w_up, w_down, ln2_g, ln2_b):
    batch, seq, d_model = x.shape
    assert d_model == D_MODEL and w_in.shape[0] == DEPTH
    row = lambda v: v.reshape(1, -1)
    n_tok = batch * seq

    (xn, qkv, gt, dw), (w_ao, w_co, w_o, w_gu, w_d) = _inproj(
        x, row(ln_in_g), row(ln_in_b), w_in[0], row(b_gate[0]), conv_w[0],
        (w_attn_out[0], w_conv_out[0], w_out[0], w_gate[0], w_up[0], w_down[0]))
    r = _mix(rel_bias, sink[0], qkv, dw, gt, xn,
             row(conv_b[0]), row(conv_ln_g[0]), row(conv_ln_b[0]), w_ao, w_co, w_o)
    out = _ffn(r.reshape(n_tok, D_MODEL), row(ln1_g[0]), row(ln1_b[0]), w_gu, w_d,
               row(ln2_g[0]), row(ln2_b[0]))
    return out.reshape(batch, seq, D_MODEL)
```
